```python
import math
import jax, jax.numpy as jnp
from jax import lax
import numpy as np

D_MODEL = 2048
BATCH = 2
SEQ = 4096
DEPTH = 2
DEC_BATCH = 8
DEC_SEQ = 4
PAST_LEN = 16384
PAGE_SIZE = 128

HEAD_DIM = 128
ATTN_WIDTH = D_MODEL // 2
HEADS_PER_GROUP = ATTN_WIDTH // HEAD_DIM
WINDOWS = (128, 512, 2048)
DILATIONS = (1, 4, 16)
N_DIL = len(WINDOWS)
QKV_COLS = N_DIL * HEADS_PER_GROUP * HEAD_DIM
SSM_WIDTH = D_MODEL // 2
SSM_GROUP_CH = 16
SSM_GROUPS = SSM_WIDTH // SSM_GROUP_CH
SSM_STATE = 64
IN_WIDTHS = (QKV_COLS, QKV_COLS, QKV_COLS, ATTN_WIDTH, SSM_WIDTH, SSM_WIDTH, D_MODEL, D_MODEL)
IN_COLS = 3 * QKV_COLS + ATTN_WIDTH + 2 * SSM_WIDTH + 2 * D_MODEL
ROPE_THETA = 10000.0
NORM_EPS = 1e-6
DT_MIN = 1e-3
DT_MAX = 1e-1

kernel_name = "dilated_swa_s5_gated_hybrid_step"


def _rmsnorm(x, w):
    xf = x.astype(jnp.float32)
    y = xf * lax.rsqrt(jnp.mean(xf * xf, axis=-1, keepdims=True) + NORM_EPS)
    return (y * w.astype(jnp.float32)).astype(x.dtype)


def _rope(x, pos):
    half = HEAD_DIM // 2
    inv_freq = jnp.power(ROPE_THETA, -jnp.arange(half, dtype=jnp.float32) * (2.0 / HEAD_DIM))
    ang = pos[:, None] * inv_freq[None, :]
    cos = jnp.cos(ang)[None, :, None, :]
    sin = jnp.sin(ang)[None, :, None, :]
    xf = x.astype(jnp.float32)
    x1, x2 = xf[..., :half], xf[..., half:]
    return jnp.concatenate([x1 * cos - x2 * sin, x2 * cos + x1 * sin], axis=-1).astype(x.dtype)


def _softmax_stats(s, mask):
    s = jnp.where(mask, s, -jnp.inf)
    m = jnp.max(s, axis=-1, keepdims=True)
    p = jnp.exp(s - m)
    l = jnp.sum(p, axis=-1, keepdims=True)
    return p / l, (m + jnp.log(l))[..., 0]


def _dilated_attn_prompt(q, k, v, window, dilation):
    nb, S, H, Dh = q.shape
    span = window // dilation
    L = S // dilation
    n_blk = -(-L // span)
    Lp = n_blk * span

    def blocks(t):
        t = t.reshape(nb, L, dilation, H, Dh).transpose(0, 2, 1, 3, 4)
        t = jnp.pad(t, ((0, 0), (0, 0), (0, Lp - L), (0, 0), (0, 0)))
        return t.reshape(nb, dilation, n_blk, span, H, Dh)

    qb = blocks(q).astype(jnp.float32)
    kb = blocks(k)
    vb = blocks(v)
    pad_prev = ((0, 0), (0, 0), (1, 0), (0, 0), (0, 0), (0, 0))
    kk = jnp.concatenate([jnp.pad(kb, pad_prev)[:, :, :-1], kb], axis=3).astype(jnp.float32)
    vv = jnp.concatenate([jnp.pad(vb, pad_prev)[:, :, :-1], vb], axis=3).astype(jnp.float32)
    s = jnp.einsum('brnqhd,brnkhd->brnqhk', qb, kk) * (Dh ** -0.5)
    a = jnp.arange(span)[:, None]
    c = jnp.arange(2 * span)[None, :]
    blk = jnp.arange(n_blk)[:, None, None]
    mask = (c >= a) & (c <= a + span) & ((blk > 0) | (c >= span))
    p, lse = _softmax_stats(s, mask[:, :, None, :])
    o = jnp.einsum('brnqhk,brnkhd->brnqhd', p, vv)
    o = o.reshape(nb, dilation, Lp, H, Dh)[:, :, :L].transpose(0, 2, 1, 3, 4).reshape(nb, S, H, Dh)
    lse = lse.reshape(nb, dilation, Lp, H)[:, :, :L].transpose(0, 2, 1, 3).reshape(nb, S, H)
    return o, lse


def _dilated_attn_sample(q, k, v, kv_cache, window, dilation):
    nb, T, H, Dh = q.shape
    buf = kv_cache.shape[1]
    ext = jnp.concatenate([kv_cache.astype(k.dtype), jnp.stack([k, v], axis=2)], axis=1)
    span = window // dilation
    idx = buf + jnp.arange(T)[:, None] - dilation * jnp.arange(span + 1)[None, :]
    valid = idx >= 0
    kvg = ext[:, jnp.maximum(idx, 0)].astype(jnp.float32)
    s = jnp.einsum('bthd,btjhd->bthj', q.astype(jnp.float32), kvg[:, :, :, 0]) * (Dh ** -0.5)
    p, lse = _softmax_stats(s, valid[None, :, None, :])
    o = jnp.einsum('bthj,btjhd->bthd', p, kvg[:, :, :, 1])
    return o, lse, ext[:, T:]


def _linear_recurrence_combine(left, right):
    a_l, b_l = left
    a_r, b_r = right
    return a_r * a_l, a_r * b_l + b_r


def _s5_scan(u, h0, lam_re, lam_im, log_dt, b_re, b_im, c_re, c_im, ssm_d):
    nb, T, _ = u.shape
    f32 = jnp.float32
    lam = lax.complex(lam_re.astype(f32), lam_im.astype(f32))
    dt = jnp.exp(log_dt.astype(f32))[:, None]
    a_bar = jnp.exp(lam * dt)
    b_bar = ((a_bar - 1.0) / lam)[..., None] * lax.complex(b_re.astype(f32), b_im.astype(f32))
    c_mat = lax.complex(c_re.astype(f32), c_im.astype(f32))
    uf = u.astype(f32)
    ug = uf.reshape(nb, T, SSM_GROUPS, SSM_GROUP_CH).astype(jnp.complex64)
    bu = jnp.einsum('gpc,btgc->btgp', b_bar, ug)
    if h0 is not None:
        bu = bu.at[:, 0].add(a_bar * h0)
    a = jnp.broadcast_to(a_bar, bu.shape)
    _, h = lax.associative_scan(_linear_recurrence_combine, (a, bu), axis=1)
    y = jnp.einsum('gcp,btgp->btgc', c_mat, h).real.reshape(nb, T, SSM_WIDTH)
    return y + ssm_d.astype(f32) * uf, h[:, -1]


def _decoder_layer(x, pos, kv_caches, ssm_h0, norm_w, w_in, q_norm_w, k_norm_w,
                   lam_re, lam_im, log_dt, b_re, b_im, c_re, c_im, ssm_d,
                   w_glu, b_glu, w_br_attn, w_br_ssm, w_out):
    nb, T, _ = x.shape
    xn = _rmsnorm(x, norm_w)
    proj = xn @ w_in
    split_at = np.cumsum(IN_WIDTHS)[:-1].tolist()
    q, k, v, g_attn, u, g_ssm, m_attn, m_ssm = jnp.split(proj, split_at, axis=-1)
    n_heads = N_DIL * HEADS_PER_GROUP
    q = _rope(_rmsnorm(q.reshape(nb, T, n_heads, HEAD_DIM), q_norm_w), pos)
    k = _rope(_rmsnorm(k.reshape(nb, T, n_heads, HEAD_DIM), k_norm_w), pos)
    v = v.reshape(nb, T, n_heads, HEAD_DIM)

    outs, lses, new_kv = [], [], []
    for g in range(N_DIL):
        hs = slice(g * HEADS_PER_GROUP, (g + 1) * HEADS_PER_GROUP)
        qg, kg, vg = q[:, :, hs], k[:, :, hs], v[:, :, hs]
        if kv_caches is None:
            o, lse = _dilated_attn_prompt(qg, kg, vg, WINDOWS[g], DILATIONS[g])
            keep = min(WINDOWS[g], T)
            new_kv.append(jnp.stack([kg[:, T - keep:], vg[:, T - keep:]], axis=2))
        else:
            o, lse, kv = _dilated_attn_sample(qg, kg, vg, kv_caches[g], WINDOWS[g], DILATIONS[g])
            new_kv.append(kv)
        outs.append(o)
        lses.append(lse)
    w_grp = jax.nn.softmax(jnp.stack(lses, axis=0), axis=0)
    attn = jnp.sum(w_grp[..., None] * jnp.stack(outs, axis=0), axis=0)
    attn = attn.reshape(nb, T, ATTN_WIDTH).astype(x.dtype) * jax.nn.silu(g_attn)

    y_ssm, h_last = _s5_scan(u, ssm_h0, lam_re, lam_im, log_dt, b_re, b_im, c_re, c_im, ssm_d)
    s = jax.nn.gelu(y_ssm)
    y_ssm = (s * jax.nn.sigmoid(s @ w_glu.astype(jnp.float32) + b_glu.astype(jnp.float32))).astype(x.dtype)
    y_ssm = y_ssm * jax.nn.silu(g_ssm)

    merged = jax.nn.sigmoid(m_attn) * (attn @ w_br_attn) + jax.nn.sigmoid(m_ssm) * (y_ssm @ w_br_ssm)
    x_out = x + merged @ w_out
    ssm_state = jnp.stack([h_last.real, h_last.imag], axis=-1)
    return x_out, new_kv, ssm_state


def setup_inputs(seed: int = 0) -> dict:
    key = jax.random.key(seed)
    ks = jax.random.split(key, 24)
    f32 = jnp.float32

    def nrm(k, shape, scale):
        return jax.random.normal(k, shape, f32) * scale

    H = HEADS_PER_GROUP
    lam_im_base = jnp.pi * jnp.arange(SSM_STATE, dtype=f32)
    return {
        'x_prompt': nrm(ks[0], (BATCH, SEQ, D_MODEL), 1.0),
        'x_sample': nrm(ks[1], (DEC_BATCH, DEC_SEQ, D_MODEL), 1.0),
        'cache_kv_d1': nrm(ks[2], (DEPTH, DEC_BATCH, min(WINDOWS[0], PAST_LEN), 2, H, HEAD_DIM), 1.0),
        'cache_kv_d4': nrm(ks[3], (DEPTH, DEC_BATCH, min(WINDOWS[1], PAST_LEN), 2, H, HEAD_DIM), 1.0),
        'cache_kv_d16': nrm(ks[4], (DEPTH, DEC_BATCH, min(WINDOWS[2], PAST_LEN), 2, H, HEAD_DIM), 1.0),
        'state_ssm': nrm(ks[5], (DEPTH, DEC_BATCH, SSM_GROUPS, SSM_STATE, 2), 0.1),
        'norm_w': 1.0 + nrm(ks[6], (DEPTH, D_MODEL), 0.01),
        'w_in': nrm(ks[7], (DEPTH, D_MODEL, IN_COLS), D_MODEL ** -0.5),
        'q_norm_w': 1.0 + nrm(ks[8], (DEPTH, HEAD_DIM), 0.01),
        'k_norm_w': 1.0 + nrm(ks[9], (DEPTH, HEAD_DIM), 0.01),
        'ssm_lambda_re': -0.5 + nrm(ks[10], (DEPTH, SSM_GROUPS, SSM_STATE), 0.01),
        'ssm_lambda_im': lam_im_base + nrm(ks[11], (DEPTH, SSM_GROUPS, SSM_STATE), 0.01),
        'ssm_log_dt': jax.random.uniform(ks[12], (DEPTH, SSM_GROUPS), f32, math.log(DT_MIN), math.log(DT_MAX)),
        'ssm_b_re': nrm(ks[13], (DEPTH, SSM_GROUPS, SSM_STATE, SSM_GROUP_CH), (2 * SSM_GROUP_CH) ** -0.5),
        'ssm_b_im': nrm(ks[14], (DEPTH, SSM_GROUPS, SSM_STATE, SSM_GROUP_CH), (2 * SSM_GROUP_CH) ** -0.5),
        'ssm_c_re': nrm(ks[15], (DEPTH, SSM_GROUPS, SSM_GROUP_CH, SSM_STATE), (2 * SSM_STATE) ** -0.5),
        'ssm_c_im': nrm(ks[16], (DEPTH, SSM_GROUPS, SSM_GROUP_CH, SSM_STATE), (2 * SSM_STATE) ** -0.5),
        'ssm_d': nrm(ks[17], (DEPTH, SSM_WIDTH), 1.0),
        'w_glu': nrm(ks[18], (DEPTH, SSM_WIDTH, SSM_WIDTH), SSM_WIDTH ** -0.5),
        'b_glu': nrm(ks[19], (DEPTH, SSM_WIDTH), 0.01),
        'w_br_attn': nrm(ks[20], (DEPTH, ATTN_WIDTH, D_MODEL), ATTN_WIDTH ** -0.5),
        'w_br_ssm': nrm(ks[21], (DEPTH, SSM_WIDTH, D_MODEL), SSM_WIDTH ** -0.5),
        'w_out': nrm(ks[22], (DEPTH, D_MODEL, D_MODEL), D_MODEL ** -0.5),
    }


def reference(x_prompt, x_sample, cache_kv_d1, cache_kv_d4, cache_kv_d16, state_ssm,
              norm_w, w_in, q_norm_w, k_norm_w, ssm_lambda_re, ssm_lambda_im, ssm_log_dt,
              ssm_b_re, ssm_b_im, ssm_c_re, ssm_c_im, ssm_d, w_glu, b_glu,
              w_br_attn, w_br_ssm, w_out):
    f32 = jnp.float32
    layer_params = (norm_w, w_in, q_norm_w, k_norm_w, ssm_lambda_re, ssm_lambda_im, ssm_log_dt,
                    ssm_b_re, ssm_b_im, ssm_c_re, ssm_c_im, ssm_d, w_glu, b_glu,
                    w_br_attn, w_br_ssm, w_out)
    pos_prompt = jnp.arange(x_prompt.shape[1], dtype=f32)
    pos_sample = PAST_LEN + jnp.arange(x_sample.shape[1], dtype=f32)

    hp, hs = x_prompt, x_sample
    kv_p = [[] for _ in range(N_DIL)]
    kv_s = [[] for _ in range(N_DIL)]
    ssm_p, ssm_s = [], []
    for l in range(DEPTH):
        lw = [w[l] for w in layer_params]
        hp, new_kv, st = _decoder_layer(hp, pos_prompt, None, None, *lw)
        for g in range(N_DIL):
            kv_p[g].append(new_kv[g])
        ssm_p.append(st)

        caches = (cache_kv_d1[l], cache_kv_d4[l], cache_kv_d16[l])
        h0 = lax.complex(state_ssm[l, ..., 0].astype(f32), state_ssm[l, ..., 1].astype(f32))
        hs, new_kv_s, st_s = _decoder_layer(hs, pos_sample, caches, h0, *lw)
        for g in range(N_DIL):
            kv_s[g].append(new_kv_s[g])
        ssm_s.append(st_s)

    kv_d1_prompt = jnp.stack(kv_p[0])
    kv_d4_prompt = jnp.stack(kv_p[1])
    kv_d16_prompt = jnp.stack(kv_p[2])
    ssm_prompt = jnp.stack(ssm_p)
    kv_d1_sample = jnp.stack(kv_s[0])
    kv_d4_sample = jnp.stack(kv_s[1])
    kv_d16_sample = jnp.stack(kv_s[2])
    ssm_sample = jnp.stack(ssm_s)
    return (hp, hs, kv_d1_prompt, kv_d4_prompt, kv_d16_prompt, ssm_prompt,
            kv_d1_sample, kv_d4_sample, kv_d16_sample, ssm_sample)
```

```python
import functools
import math

import jax
import jax.numpy as jnp
from jax import lax
from jax.experimental import pallas as pl
from jax.experimental.pallas import tpu as pltpu

F32 = jnp.float32
BF16 = jnp.bfloat16

HEAD_DIM = 128
WINDOWS = (128, 512, 2048)
DILATIONS = (1, 4, 16)
N_DIL = 3
SPAN = 128
SSM_GROUP_CH = 16
SSM_STATE = 64
GROUPS_PER_CHUNK = 8
ROPE_THETA = 10000.0
NORM_EPS = 1e-6
PAST_LEN = 16384
NEG_BIG = -1e30
VMEM_LIMIT = 48 * 1024 * 1024
ROW_TILE = 8


def _params(*sem):
    return pltpu.CompilerParams(dimension_semantics=sem, vmem_limit_bytes=VMEM_LIMIT)


def _sigmoid(x):
    return 1.0 / (1.0 + jnp.exp(-x))


def _row_tile(m, cap):
    t = min(m, cap)
    assert m % t == 0, (m, t)
    return t


def _rmsnorm_kernel(x_ref, w_ref, o_ref):
    x = x_ref[...]
    y = x * lax.rsqrt(jnp.mean(x * x, axis=-1, keepdims=True) + NORM_EPS)
    o_ref[...] = (y * w_ref[...]).astype(o_ref.dtype)


def _rmsnorm(x, w):
    m, d = x.shape
    tm = _row_tile(m, 512)
    return pl.pallas_call(
        _rmsnorm_kernel,
        grid=(m // tm,),
        in_specs=[pl.BlockSpec((tm, d), lambda i: (i, 0)),
                  pl.BlockSpec((1, d), lambda i: (0, 0))],
        out_specs=pl.BlockSpec((tm, d), lambda i: (i, 0)),
        out_shape=jax.ShapeDtypeStruct((m, d), BF16),
        compiler_params=_params("parallel"),
        name="rmsnorm",
    )(x, w.reshape(1, d))


def _proj_kernel(x_ref, w_ref, *rest, epilogue):
    acc = jnp.dot(x_ref[...], w_ref[...], preferred_element_type=F32)
    if epilogue == "qk":
        nw_ref, cos_ref, sin_ref, o_ref = rest
        for h in range(acc.shape[1] // HEAD_DIM):
            sl = slice(h * HEAD_DIM, (h + 1) * HEAD_DIM)
            a = acc[:, sl]
            y = a * lax.rsqrt(jnp.mean(a * a, axis=-1, keepdims=True) + NORM_EPS) * nw_ref[...]
            swapped = pltpu.roll(y, HEAD_DIM // 2, axis=1)
            o_ref[:, sl] = (y * cos_ref[...] + swapped * sin_ref[...]).astype(o_ref.dtype)
    else:
        (o_ref,) = rest
        if epilogue == "silu":
            acc = acc * _sigmoid(acc)
        elif epilogue == "sigmoid":
            acc = _sigmoid(acc)
        o_ref[...] = acc.astype(o_ref.dtype)


def _proj(xn, w, col0, ncols, epilogue, out_dtype, norm_w=None, cos=None, sin=None):
    m, d = xn.shape
    tm = _row_tile(m, 1024)
    tn = min(ncols, 1024)
    assert ncols % tn == 0 and col0 % tn == 0
    jb = col0 // tn
    in_specs = [pl.BlockSpec((tm, d), lambda i, j: (i, 0)),
                pl.BlockSpec((d, tn), lambda i, j: (0, jb + j))]
    args = [xn, w]
    if epilogue == "qk":
        cyc = cos.shape[0] // tm
        in_specs += [pl.BlockSpec((1, HEAD_DIM), lambda i, j: (0, 0)),
                     pl.BlockSpec((tm, HEAD_DIM), lambda i, j: (i % cyc, 0)),
                     pl.BlockSpec((tm, HEAD_DIM), lambda i, j: (i % cyc, 0))]
        args += [norm_w.reshape(1, HEAD_DIM), cos, sin]
    return pl.pallas_call(
        functools.partial(_proj_kernel, epilogue=epilogue),
        grid=(m // tm, ncols // tn),
        in_specs=in_specs,
        out_specs=pl.BlockSpec((tm, tn), lambda i, j: (i, j)),
        out_shape=jax.ShapeDtypeStruct((m, ncols), out_dtype),
        compiler_params=_params("parallel", "arbitrary"),
        name="proj_" + epilogue,
    )(*args)


def _rope_tables(pos):
    half = HEAD_DIM // 2
    inv_freq = jnp.power(ROPE_THETA, -jnp.arange(half, dtype=F32) * (2.0 / HEAD_DIM))
    ang = pos[:, None] * inv_freq[None, :]
    cos, sin = jnp.cos(ang), jnp.sin(ang)
    return jnp.concatenate([cos, cos], axis=-1), jnp.concatenate([-sin, sin], axis=-1)


def _attn_prompt_kernel(q_ref, kp_ref, kc_ref, vp_ref, vc_ref, o_ref, lse_ref, *, n_blk):
    n = lax.rem(pl.program_id(1), n_blk)
    a = lax.broadcasted_iota(jnp.int32, (SPAN, 2 * SPAN), 0)
    c = lax.broadcasted_iota(jnp.int32, (SPAN, 2 * SPAN), 1)
    mask = (c >= a) & (c <= a + SPAN) & ((c >= SPAN) | (n > 0))
    scale = HEAD_DIM ** -0.5
    for h in range(q_ref.shape[2] // HEAD_DIM):
        sl = slice(h * HEAD_DIM, (h + 1) * HEAD_DIM)
        q = q_ref[0, :, sl]
        k = jnp.concatenate([kp_ref[0, :, sl], kc_ref[0, :, sl]], axis=0).astype(BF16)
        v = jnp.concatenate([vp_ref[0, :, sl], vc_ref[0, :, sl]], axis=0).astype(BF16)
        s = lax.dot_general(q, k, (((1,), (1,)), ((), ())), preferred_element_type=F32) * scale
        s = jnp.where(mask, s, NEG_BIG)
        m = jnp.max(s, axis=-1, keepdims=True)
        p = jnp.exp(s - m)
        l = jnp.sum(p, axis=-1, keepdims=True)
        o = jnp.dot(p.astype(BF16), v, preferred_element_type=F32) / l
        o_ref[0, :, sl] = o
        lse_ref[0, :, sl] = jnp.broadcast_to(m + jnp.log(l), (SPAN, HEAD_DIM))


def _attn_prompt(q, k, v, g, nb, seq):
    aw = q.shape[1] // N_DIL
    dil = DILATIONS[g]
    length = seq // dil
    n_blk = length // SPAN
    assert length % SPAN == 0
    view = (nb, length, dil * N_DIL * aw)
    qv, kv, vv = q.reshape(view), k.reshape(view), v.reshape(view)

    def cur(b, s):
        return (b, lax.rem(s, n_blk), lax.div(s, n_blk) * N_DIL + g)

    def prev(b, s):
        return (b, jnp.maximum(lax.rem(s, n_blk) - 1, 0), lax.div(s, n_blk) * N_DIL + g)

    def out_map(b, s):
        return (b, lax.rem(s, n_blk), lax.div(s, n_blk))

    blk = (1, SPAN, aw)
    o, lse = pl.pallas_call(
        functools.partial(_attn_prompt_kernel, n_blk=n_blk),
        grid=(nb, dil * n_blk),
        in_specs=[pl.BlockSpec(blk, cur), pl.BlockSpec(blk, prev), pl.BlockSpec(blk, cur),
                  pl.BlockSpec(blk, prev), pl.BlockSpec(blk, cur)],
        out_specs=[pl.BlockSpec(blk, out_map), pl.BlockSpec(blk, out_map)],
        out_shape=[jax.ShapeDtypeStruct((nb, length, dil * aw), F32)] * 2,
        compiler_params=_params("parallel", "arbitrary"),
        name="attn_prompt_d%d" % dil,
    )(qv, kv, kv, vv, vv)
    return o.reshape(nb * seq, aw), lse.reshape(nb * seq, aw)


def _attn_sample_kernel(q_ref, kn_ref, vn_ref, c1_ref, c4_ref, c16_ref, o_ref, lse_ref):
    n_new, n_heads, _ = q_ref.shape
    hg = n_heads // N_DIL
    scale = HEAD_DIM ** -0.5
    caches = (c1_ref, c4_ref, c16_ref)
    for g in range(N_DIL):
        hs = slice(g * hg, (g + 1) * hg)
        for t in range(n_new):
            q = q_ref[t, hs, :]
            if DILATIONS[g] == 1:
                kc, vc = caches[g][:, 0], caches[g][:, 1]
                rows = lax.broadcasted_iota(jnp.int32, (kc.shape[0], hg, 1), 0)
                s = jnp.sum(kc * q[None], axis=-1, keepdims=True) * scale
                s = jnp.where(rows >= t, s, NEG_BIG)
                new = range(t + 1)
            else:
                kc, vc = caches[g][:, t, 0], caches[g][:, t, 1]
                s = jnp.sum(kc * q[None], axis=-1, keepdims=True) * scale
                new = (t,)
            s_new = [jnp.sum(kn_ref[u, hs, :] * q, axis=-1, keepdims=True) * scale for u in new]
            m = jnp.max(s, axis=0)
            for sn in s_new:
                m = jnp.maximum(m, sn)
            p = jnp.exp(s - m[None])
            l = jnp.sum(p, axis=0)
            acc = jnp.sum(p * vc, axis=0)
            for u, sn in zip(new, s_new):
                pn = jnp.exp(sn - m)
                l = l + pn
                acc = acc + pn * vn_ref[u, hs, :]
            o_ref[t, hs, :] = acc / l
            lse_ref[t, hs, :] = jnp.broadcast_to(m + jnp.log(l), (hg, HEAD_DIM))


def _attn_sample(q, kn, vn, caches, layer):
    nb, n_new, n_heads, _ = q.shape
    hg = n_heads // N_DIL
    views, specs = [], []
    for g in range(N_DIL):
        dil = DILATIONS[g]
        depth, _, buf = caches[g].shape[:3]
        assert buf == WINDOWS[g] and (dil == 1 or dil % n_new == 0)
        if dil == 1:
            views.append(caches[g])
            specs.append(pl.BlockSpec((None, None, buf, 2, hg, HEAD_DIM),
                                      lambda b: (layer, b, 0, 0, 0, 0)))
        else:
            views.append(caches[g].reshape(depth, nb, SPAN, dil, 2, hg, HEAD_DIM))
            specs.append(pl.BlockSpec((None, None, SPAN, n_new, 2, hg, HEAD_DIM),
                                      lambda b: (layer, b, 0, 0, 0, 0, 0)))
    tok = pl.BlockSpec((None, n_new, n_heads, HEAD_DIM), lambda b: (b, 0, 0, 0))
    return pl.pallas_call(
        _attn_sample_kernel,
        grid=(nb,),
        in_specs=[tok, tok, tok] + specs,
        out_specs=[tok, tok],
        out_shape=[jax.ShapeDtypeStruct(q.shape, F32)] * 2,
        compiler_params=_params("parallel"),
        name="attn_sample",
    )(q, kn, vn, *views)


def _s5_params_kernel(lre_ref, lim_ref, ldt_ref, bre_ref, bim_ref, cst_ref, bbre_ref, bbim_ref):
    dt = jnp.exp(ldt_ref[...])
    lre, lim = lre_ref[...], lim_ref[...]
    xr, xi = lre * dt, lim * dt
    shape = (ROW_TILE, xr.shape[1])
    row = lax.broadcasted_iota(jnp.int32, shape, 0)

    def a_pow(k):
        mag = jnp.exp(xr * k)
        return mag * jnp.cos(xi * k), mag * jnp.sin(xi * k)

    for idx, k in enumerate((1, 2, 4)):
        pr, pi = a_pow(float(k))
        cst_ref[2 * idx] = jnp.where(row >= k, jnp.broadcast_to(pr, shape), 0.0)
        cst_ref[2 * idx + 1] = jnp.where(row >= k, jnp.broadcast_to(pi, shape), 0.0)
    cr, ci = a_pow((row + 1).astype(F32))
    cst_ref[6] = cr
    cst_ref[7] = ci
    ar, ai = a_pow(1.0)
    nr, ni = ar - 1.0, ai
    den = lre * lre + lim * lim
    fr = (nr * lre + ni * lim) / den
    fi = (ni * lre - nr * lim) / den
    bre, bim = bre_ref[...], bim_ref[...]
    bbre_ref[...] = fr * bre - fi * bim
    bbim_ref[...] = fr * bim + fi * bre


def _s5_params(lam_re, lam_im, log_dt, b_re, b_im, c_re, c_im):
    n_grp, n_st = lam_re.shape
    n_ch = b_re.shape[2]
    n = n_grp * n_st
    gc = GROUPS_PER_CHUNK
    n_chunk = n_grp // gc
    to_lanes = lambda b: b.transpose(2, 0, 1).reshape(n_ch, n)
    full = lambda *shape: pl.BlockSpec(shape, lambda: (0,) * len(shape))
    cst, bb_re, bb_im = pl.pallas_call(
        _s5_params_kernel,
        in_specs=[full(1, n)] * 3 + [full(n_ch, n)] * 2,
        out_specs=[full(8, ROW_TILE, n), full(n_ch, n), full(n_ch, n)],
        out_shape=[jax.ShapeDtypeStruct((8, ROW_TILE, n), F32),
                   jax.ShapeDtypeStruct((n_ch, n), F32), jax.ShapeDtypeStruct((n_ch, n), F32)],
        name="s5_params",
    )(lam_re.reshape(1, n), lam_im.reshape(1, n),
      jnp.broadcast_to(log_dt[:, None], (n_grp, n_st)).reshape(1, n), to_lanes(b_re), to_lanes(b_im))

    eye = jnp.eye(gc, dtype=F32)

    def b_blocks(bb):
        bb = bb.reshape(n_ch, n_chunk, gc, n_st)
        blk = jnp.einsum("cjgp,gh->jgchp", bb, eye)
        return blk.reshape(n_chunk, gc * n_ch, gc * n_st).astype(BF16)

    def c_blocks(cc):
        cc = cc.reshape(n_chunk, gc, n_ch, n_st)
        blk = jnp.einsum("jgcp,gh->jgphc", cc, eye)
        return blk.reshape(n_chunk, gc * n_st, gc * n_ch).astype(BF16)

    return cst, b_blocks(bb_re), b_blocks(bb_im), c_blocks(c_re), c_blocks(c_im)


def _s5_scan_kernel(u_ref, bre_ref, bim_ref, cst_ref, cre_ref, cim_ref, d_ref,
                    y_ref, hre_ref, him_ref, sre, sim, car_re, car_im):
    @pl.when(pl.program_id(2) == 0)
    def _():
        car_re[...] = jnp.zeros_like(car_re)
        car_im[...] = jnp.zeros_like(car_im)

    u = u_ref[0]
    ub = u.astype(BF16)
    sre[...] = jnp.dot(ub, bre_ref[0], preferred_element_type=F32)
    sim[...] = jnp.dot(ub, bim_ref[0], preferred_element_type=F32)
    steps = [(cst_ref[2 * i], cst_ref[2 * i + 1], 1 << i) for i in range(3)]
    acr, aci = cst_ref[6], cst_ref[7]
    n_lane = sre.shape[1]

    def tile(i, carry):
        cr, ci = carry
        r0 = pl.multiple_of(i * ROW_TILE, ROW_TILE)
        xr, xi = sre[pl.ds(r0, ROW_TILE), :], sim[pl.ds(r0, ROW_TILE), :]
        for ar, ai, sh in steps:
            pr, pi = pltpu.roll(xr, sh, axis=0), pltpu.roll(xi, sh, axis=0)
            xr, xi = xr + ar * pr - ai * pi, xi + ar * pi + ai * pr
        hr = xr + acr * cr - aci * ci
        hi = xi + acr * ci + aci * cr
        sre[pl.ds(r0, ROW_TILE), :] = hr
        sim[pl.ds(r0, ROW_TILE), :] = hi
        last = ROW_TILE - 1
        return (jnp.broadcast_to(hr[last:, :], (ROW_TILE, n_lane)),
                jnp.broadcast_to(hi[last:, :], (ROW_TILE, n_lane)))

    cr, ci = lax.fori_loop(0, sre.shape[0] // ROW_TILE, tile, (car_re[...], car_im[...]))
    car_re[...] = cr
    car_im[...] = ci
    y = (jnp.dot(sre[...].astype(BF16), cre_ref[0], preferred_element_type=F32)
         - jnp.dot(sim[...].astype(BF16), cim_ref[0], preferred_element_type=F32))
    y_ref[0] = y + d_ref[...] * u
    hre_ref[0] = cr
    him_ref[0] = ci


def _s5_prompt(u, prm, ssm_d, nb, seq):
    cst, bre, bim, cre, cim = prm
    n_chunk, cw, sw = bre.shape
    width = u.shape[1]
    tc = min(seq, 512)
    assert seq % tc == 0
    y, hre, him = pl.pallas_call(
        _s5_scan_kernel,
        grid=(nb, n_chunk, seq // tc),
        in_specs=[pl.BlockSpec((1, tc, cw), lambda b, j, t: (b, t, j)),
                  pl.BlockSpec((1, cw, sw), lambda b, j, t: (j, 0, 0)),
                  pl.BlockSpec((1, cw, sw), lambda b, j, t: (j, 0, 0)),
                  pl.BlockSpec((8, ROW_TILE, sw), lambda b, j, t: (0, 0, j)),
                  pl.BlockSpec((1, sw, cw), lambda b, j, t: (j, 0, 0)),
                  pl.BlockSpec((1, sw, cw), lambda b, j, t: (j, 0, 0)),
                  pl.BlockSpec((1, cw), lambda b, j, t: (0, j))],
        out_specs=[pl.BlockSpec((1, tc, cw), lambda b, j, t: (b, t, j)),
                   pl.BlockSpec((1, ROW_TILE, sw), lambda b, j, t: (b, 0, j)),
                   pl.BlockSpec((1, ROW_TILE, sw), lambda b, j, t: (b, 0, j))],
        out_shape=[jax.ShapeDtypeStruct((nb, seq, width), F32),
                   jax.ShapeDtypeStruct((nb, ROW_TILE, n_chunk * sw), F32),
                   jax.ShapeDtypeStruct((nb, ROW_TILE, n_chunk * sw), F32)],
        scratch_shapes=[pltpu.VMEM((tc, sw), F32), pltpu.VMEM((tc, sw), F32),
                        pltpu.VMEM((ROW_TILE, sw), F32), pltpu.VMEM((ROW_TILE, sw), F32)],
        compiler_params=_params("parallel", "parallel", "arbitrary"),
        name="s5_scan",
    )(u.reshape(nb, seq, width), bre, bim, cst, cre, cim, ssm_d.reshape(1, width))
    return y.reshape(nb * seq, width), hre[:, 0], him[:, 0]


def _s5_sample_kernel(u_ref, h0re_ref, h0im_ref, bre_ref, bim_ref, cst_ref, cre_ref, cim_ref, d_ref,
                      y_ref, hre_ref, him_ref):
    ar, ai = cst_ref[6, 0:1, :], cst_ref[7, 0:1, :]
    hr, hi = h0re_ref[...], h0im_ref[...]
    for t in range(u_ref.shape[0]):
        u = u_ref[t]
        ub = u.astype(BF16)
        hr, hi = (ar * hr - ai * hi + jnp.dot(ub, bre_ref[0], preferred_element_type=F32),
                  ar * hi + ai * hr + jnp.dot(ub, bim_ref[0], preferred_element_type=F32))
        y = (jnp.dot(hr.astype(BF16), cre_ref[0], preferred_element_type=F32)
             - jnp.dot(hi.astype(BF16), cim_ref[0], preferred_element_type=F32))
        y_ref[t] = y + d_ref[...] * u
    hre_ref[...] = hr
    him_ref[...] = hi


def _s5_sample(u, h0re, h0im, prm, ssm_d):
    cst, bre, bim, cre, cim = prm
    n_chunk, cw, sw = bre.shape
    n_new, nb, width = u.shape
    return pl.pallas_call(
        _s5_sample_kernel,
        grid=(n_chunk,),
        in_specs=[pl.BlockSpec((n_new, nb, cw), lambda j: (0, 0, j)),
                  pl.BlockSpec((nb, sw), lambda j: (0, j)),
                  pl.BlockSpec((nb, sw), lambda j: (0, j)),
                  pl.BlockSpec((1, cw, sw), lambda j: (j, 0, 0)),
                  pl.BlockSpec((1, cw, sw), lambda j: (j, 0, 0)),
                  pl.BlockSpec((8, ROW_TILE, sw), lambda j: (0, 0, j)),
                  pl.BlockSpec((1, sw, cw), lambda j: (j, 0, 0)),
                  pl.BlockSpec((1, sw, cw), lambda j: (j, 0, 0)),
                  pl.BlockSpec((1, cw), lambda j: (0, j))],
        out_specs=[pl.BlockSpec((n_new, nb, cw), lambda j: (0, 0, j)),
                   pl.BlockSpec((nb, sw), lambda j: (0, j)),
                   pl.BlockSpec((nb, sw), lambda j: (0, j))],
        out_shape=[jax.ShapeDtypeStruct((n_new, nb, width), F32),
                   jax.ShapeDtypeStruct((nb, n_chunk * sw), F32),
                   jax.ShapeDtypeStruct((nb, n_chunk * sw), F32)],
        compiler_params=_params("parallel"),
        name="s5_sample",
    )(u, h0re, h0im, bre, bim, cst, cre, cim, ssm_d.reshape(1, width))


def _glu_kernel(y_ref, g_ref, w_ref, b_ref, o_ref):
    y = y_ref[...]
    s = 0.5 * y * (1.0 + jnp.tanh(math.sqrt(2.0 / math.pi) * (y + 0.044715 * (y * y * y))))
    z = jnp.dot(s.astype(BF16), w_ref[...], preferred_element_type=F32) + b_ref[...]
    o_ref[...] = (s * _sigmoid(z) * g_ref[...].astype(F32)).astype(o_ref.dtype)


def _glu(y, gate, w, b):
    m, width = y.shape
    tm = _row_tile(m, 512)
    row = pl.BlockSpec((tm, width), lambda i: (i, 0))
    return pl.pallas_call(
        _glu_kernel,
        grid=(m // tm,),
        in_specs=[row, row, pl.BlockSpec((width, width), lambda i: (0, 0)),
                  pl.BlockSpec((1, width), lambda i: (0, 0))],
        out_specs=row,
        out_shape=jax.ShapeDtypeStruct((m, width), BF16),
        compiler_params=_params("parallel"),
        name="glu",
    )(y, gate, w, b.reshape(1, width))


def _merge_kernel(o0_ref, o1_ref, o2_ref, l0_ref, l1_ref, l2_ref, ga_ref, ys_ref, ma_ref, ms_ref,
                  wa_ref, ws_ref, out_ref):
    l0, l1, l2 = l0_ref[...], l1_ref[...], l2_ref[...]
    mx = jnp.maximum(jnp.maximum(l0, l1), l2)
    e0, e1, e2 = jnp.exp(l0 - mx), jnp.exp(l1 - mx), jnp.exp(l2 - mx)
    attn = (e0 * o0_ref[...] + e1 * o1_ref[...] + e2 * o2_ref[...]) / (e0 + e1 + e2)
    a = (attn * ga_ref[...].astype(F32)).astype(BF16)
    ya = jnp.dot(a, wa_ref[...], preferred_element_type=F32)
    yb = jnp.dot(ys_ref[...], ws_ref[...], preferred_element_type=F32)
    out_ref[...] = (ma_ref[...].astype(F32) * ya + ms_ref[...].astype(F32) * yb).astype(out_ref.dtype)


def _merge(outs, lses, g_attn, y_ssm, gates, w_a, w_s):
    m, aw = g_attn.shape
    d = w_a.shape[1]
    tm = _row_tile(m, 256)
    row = pl.BlockSpec((tm, aw), lambda i: (i, 0))
    return pl.pallas_call(
        _merge_kernel,
        grid=(m // tm,),
        in_specs=[row] * 8 + [pl.BlockSpec((tm, d), lambda i: (i, 0)),
                              pl.BlockSpec((tm, d), lambda i: (i, 1)),
                              pl.BlockSpec((aw, d), lambda i: (0, 0)),
                              pl.BlockSpec((aw, d), lambda i: (0, 0))],
        out_specs=pl.BlockSpec((tm, d), lambda i: (i, 0)),
        out_shape=jax.ShapeDtypeStruct((m, d), BF16),
        compiler_params=_params("parallel"),
        name="merge",
    )(*outs, *lses, g_attn, y_ssm, gates, gates, w_a, w_s)


def _out_kernel(x_ref, m_ref, w_ref, o_ref):
    o_ref[...] = x_ref[...] + jnp.dot(m_ref[...], w_ref[...], preferred_element_type=F32)


def _out_proj(x, merged, w):
    m, d = x.shape
    tm = _row_tile(m, 1024)
    tn = min(d, 1024)
    return pl.pallas_call(
        _out_kernel,
        grid=(m // tm, d // tn),
        in_specs=[pl.BlockSpec((tm, tn), lambda i, j: (i, j)),
                  pl.BlockSpec((tm, d), lambda i, j: (i, 0)),
                  pl.BlockSpec((d, tn), lambda i, j: (0, j))],
        out_specs=pl.BlockSpec((tm, tn), lambda i, j: (i, j)),
        out_shape=jax.ShapeDtypeStruct((m, d), F32),
        compiler_params=_params("parallel", "arbitrary"),
        name="out_proj",
    )(x, merged, w)


def _kv_shift_kernel(*refs):
    n = (len(refs) - 1) // 3
    caches, news, outs, sem = refs[:n], refs[n:2 * n], refs[2 * n:3 * n], refs[3 * n]
    copies = []
    for g in range(n):
        buf, n_new = caches[g].shape[2], news[g].shape[2]
        keep = buf - n_new
        copies.append(pltpu.make_async_copy(caches[g].at[:, :, pl.ds(n_new, keep)],
                                            outs[g].at[:, :, pl.ds(0, keep)], sem.at[2 * g]))
        copies.append(pltpu.make_async_copy(news[g], outs[g].at[:, :, pl.ds(keep, n_new)],
                                            sem.at[2 * g + 1]))
    for cp in copies:
        cp.start()
    for cp in copies:
        cp.wait()


def _kv_shift(caches, news):
    n = len(caches)
    for c, w in zip(caches, news):
        assert c.shape[2] >= w.shape[2]
    hbm = pl.BlockSpec(memory_space=pl.ANY)
    return pl.pallas_call(
        _kv_shift_kernel,
        in_specs=[hbm] * (2 * n),
        out_specs=[hbm] * n,
        out_shape=[jax.ShapeDtypeStruct(c.shape, c.dtype) for c in caches],
        scratch_shapes=[pltpu.SemaphoreType.DMA((2 * n,))],
        name="kv_shift",
    )(*caches, *news)


def _layer(x, cos, sin, lw, nb, seq, sample):
    (norm_w, w_in, q_norm_w, k_norm_w, w_glu, b_glu, w_br_attn, w_br_ssm, w_out, ssm_d, s5prm) = lw
    m, d = x.shape
    aw = d // 2
    qkv = N_DIL * aw
    hg = aw // HEAD_DIM
    xn = _rmsnorm(x, norm_w)
    q = _proj(xn, w_in, 0, qkv, "qk", F32 if sample else BF16, q_norm_w, cos, sin)
    k = _proj(xn, w_in, qkv, qkv, "qk", F32, k_norm_w, cos, sin)
    v = _proj(xn, w_in, 2 * qkv, qkv, "plain", F32)
    g_attn = _proj(xn, w_in, 3 * qkv, aw, "silu", BF16)
    u = _proj(xn, w_in, 3 * qkv + aw, aw, "plain", F32)
    g_ssm = _proj(xn, w_in, 3 * qkv + 2 * aw, aw, "silu", BF16)
    gates = _proj(xn, w_in, 3 * qkv + 3 * aw, 2 * d, "sigmoid", BF16)

    if sample is None:
        res = [_attn_prompt(q, k, v, g, nb, seq) for g in range(N_DIL)]
        outs, lses = [r[0] for r in res], [r[1] for r in res]
        y, hre, him = _s5_prompt(u, s5prm, ssm_d, nb, seq)
    else:
        caches, layer, h0re, h0im = sample
        tok = (nb, seq, N_DIL * hg, HEAD_DIM)
        o, lse = _attn_sample(q.reshape(tok), k.reshape(tok), v.reshape(tok), caches, layer)
        o, lse = o.reshape(m, qkv), lse.reshape(m, qkv)
        outs = [o[:, g * aw:(g + 1) * aw] for g in range(N_DIL)]
        lses = [lse[:, g * aw:(g + 1) * aw] for g in range(N_DIL)]
        ut = u.reshape(nb, seq, aw).transpose(1, 0, 2)
        yt, hre, him = _s5_sample(ut, h0re, h0im, s5prm, ssm_d)
        y = yt.transpose(1, 0, 2).reshape(m, aw)

    y_ssm = _glu(y, g_ssm, w_glu, b_glu)
    merged = _merge(outs, lses, g_attn, y_ssm, gates, w_br_attn, w_br_ssm)
    x_out = _out_proj(x, merged, w_out)

    n_grp = aw // SSM_GROUP_CH
    state = jnp.stack([hre, him], axis=-1).reshape(nb, n_grp, SSM_STATE, 2)
    k4 = k.reshape(nb, seq, N_DIL, hg, HEAD_DIM)
    v4 = v.reshape(nb, seq, N_DIL, hg, HEAD_DIM)
    new_kv = []
    for g in range(N_DIL):
        keep = min(WINDOWS[g], seq)
        new_kv.append(jnp.stack([k4[:, seq - keep:, g], v4[:, seq - keep:, g]], axis=2))
    return x_out, new_kv, state


def kernel(x_prompt, x_sample, cache_kv_d1, cache_kv_d4, cache_kv_d16, state_ssm, norm_w, w_in, q_norm_w, k_norm_w, ssm_lambda_re, ssm_lambda_im, ssm_log_dt, ssm_b_re, ssm_b_im, ssm_c_re, ssm_c_im, ssm_d, w_glu, b_glu, w_br_attn, w_br_ssm, w_out):
    nb_p, seq_p, d = x_prompt.shape
    nb_s, seq_s, _ = x_sample.shape
    depth = norm_w.shape[0]
    caches = (cache_kv_d1, cache_kv_d4, cache_kv_d16)

    cos_p, sin_p = _rope_tables(jnp.arange(seq_p, dtype=F32))
    cos_s, sin_s = _rope_tables(PAST_LEN + jnp.arange(seq_s, dtype=F32))
    cos_s, sin_s = jnp.tile(cos_s, (nb_s, 1)), jnp.tile(sin_s, (nb_s, 1))

    hp = x_prompt.reshape(nb_p * seq_p, d)
    hs = x_sample.reshape(nb_s * seq_s, d)
    kv_p = [[] for _ in range(N_DIL)]
    kv_s_new = [[] for _ in range(N_DIL)]
    ssm_p, ssm_s = [], []
    for l in range(depth):
        s5prm = _s5_params(ssm_lambda_re[l], ssm_lambda_im[l], ssm_log_dt[l], ssm_b_re[l], ssm_b_im[l],
                           ssm_c_re[l], ssm_c_im[l])
        lw = (norm_w[l], w_in[l].astype(BF16), q_norm_w[l], k_norm_w[l], w_glu[l].astype(BF16), b_glu[l],
              w_br_attn[l].astype(BF16), w_br_ssm[l].astype(BF16), w_out[l].astype(BF16), ssm_d[l], s5prm)
        hp, new_kv, st = _layer(hp, cos_p, sin_p, lw, nb_p, seq_p, None)
        for g in range(N_DIL):
            kv_p[g].append(new_kv[g])
        ssm_p.append(st)

        n_state = state_ssm.shape[2] * state_ssm.shape[3]
        h0re = state_ssm[l, ..., 0].reshape(nb_s, n_state)
        h0im = state_ssm[l, ..., 1].reshape(nb_s, n_state)
        hs, new_kv_s, st_s = _layer(hs, cos_s, sin_s, lw, nb_s, seq_s, (caches, l, h0re, h0im))
        for g in range(N_DIL):
            kv_s_new[g].append(new_kv_s[g])
        ssm_s.append(st_s)

    kv_s = _kv_shift(caches, [jnp.stack(n) for n in kv_s_new])
    return (hp.reshape(x_prompt.shape), hs.reshape(x_sample.shape),
            jnp.stack(kv_p[0]), jnp.stack(kv_p[1]), jnp.stack(kv_p[2]), jnp.stack(ssm_p),
            kv_s[0], kv_s[1], kv_s[2], jnp.stack(ssm_s))
```

```python
import functools
import math

import jax
import jax.numpy as jnp
from jax import lax
from jax.experimental import pallas as pl
from jax.experimental.pallas import tpu as pltpu

F32 = jnp.float32
BF16 = jnp.bfloat16

HEAD_DIM = 128
WINDOWS = (128, 512, 2048)
DILATIONS = (1, 4, 16)
N_DIL = 3
SPAN = 128
SSM_GROUP_CH = 16
SSM_STATE = 64
GROUPS_PER_CHUNK = 8
ROPE_THETA = 10000.0
NORM_EPS = 1e-6
PAST_LEN = 16384
NEG_BIG = -1e30
VMEM_LIMIT = 48 * 1024 * 1024
ROW_TILE = 8
QK_ROW_CHUNK = 64
KV_COPY_ROWS = 512
KV_COPY_SLOTS = 4


def _params(*sem):
    return pltpu.CompilerParams(dimension_semantics=sem, vmem_limit_bytes=VMEM_LIMIT)


def _sigmoid(x):
    return 1.0 / (1.0 + jnp.exp(-x))


def _row_tile(m, cap):
    t = min(m, cap)
    assert m % t == 0, (m, t)
    return t


def _rmsnorm_kernel(x_ref, w_ref, o_ref):
    x = x_ref[...]
    y = x * lax.rsqrt(jnp.mean(x * x, axis=-1, keepdims=True) + NORM_EPS)
    o_ref[...] = (y * w_ref[...]).astype(o_ref.dtype)


def _rmsnorm(x, w):
    m, d = x.shape
    tm = _row_tile(m, 512)
    return pl.pallas_call(
        _rmsnorm_kernel,
        grid=(m // tm,),
        in_specs=[pl.BlockSpec((tm, d), lambda i: (i, 0)),
                  pl.BlockSpec((1, d), lambda i: (0, 0))],
        out_specs=pl.BlockSpec((tm, d), lambda i: (i, 0)),
        out_shape=jax.ShapeDtypeStruct((m, d), BF16),
        compiler_params=_params("parallel"),
        name="rmsnorm",
    )(x, w.reshape(1, d))


def _norm_rope_inplace(t_ref, nw_ref, cos_ref, sin_ref):
    head_major = len(t_ref.shape) == 3
    rows = t_ref.shape[1] if head_major else t_ref.shape[0]
    n_heads = t_ref.shape[0] if head_major else t_ref.shape[1] // HEAD_DIM
    rc = min(rows, QK_ROW_CHUNK)
    half = HEAD_DIM // 2
    nw = nw_ref[...]
    nw_swapped = pltpu.roll(nw, half, axis=1)

    def chunk(c, carry):
        r0 = pl.multiple_of(c * rc, rc)
        cs = cos_ref[pl.ds(r0, rc), :] * nw
        ss = sin_ref[pl.ds(r0, rc), :] * nw_swapped
        for h in range(n_heads):
            idx = (h, pl.ds(r0, rc), slice(None)) if head_major else (
                pl.ds(r0, rc), slice(h * HEAD_DIM, (h + 1) * HEAD_DIM))
            a = t_ref[idx]
            r = lax.rsqrt(jnp.mean(a * a, axis=-1, keepdims=True) + NORM_EPS)
            t_ref[idx] = (a * cs + pltpu.roll(a, half, axis=1) * ss) * r
        return carry

    lax.fori_loop(0, rows // rc, chunk, 0)


def _proj_kernel(x_ref, w_ref, *rest, epilogue):
    acc = jnp.dot(x_ref[...], w_ref[...], preferred_element_type=F32)
    if epilogue == "qk":
        nw_ref, cos_ref, sin_ref, o_ref = rest
        o_ref[...] = acc
        _norm_rope_inplace(o_ref, nw_ref, cos_ref, sin_ref)
    else:
        (o_ref,) = rest
        if epilogue == "silu":
            acc = acc * _sigmoid(acc)
        elif epilogue == "sigmoid":
            acc = _sigmoid(acc)
        o_ref[...] = acc.astype(o_ref.dtype)


def _proj_dil_kernel(x_ref, w_ref, *rest, epilogue, dil, want_tok):
    if epilogue == "qk":
        nw_ref, cos_ref, sin_ref = rest[:3]
        rest = rest[3:]
    dil_ref = rest[0]
    tok_ref = rest[1] if want_tok else None
    hm_ref = rest[-1]
    acc = jnp.dot(x_ref[...], w_ref[...], preferred_element_type=F32)
    n_heads, tm, _ = hm_ref.shape
    for h in range(n_heads):
        hm_ref[h] = acc[:, h * HEAD_DIM:(h + 1) * HEAD_DIM]
    if epilogue == "qk":
        _norm_rope_inplace(hm_ref, nw_ref, cos_ref, sin_ref)
    per = tm // dil
    for h in range(n_heads):
        sl = slice(h * HEAD_DIM, (h + 1) * HEAD_DIM)
        if want_tok:
            tok_ref[:, sl] = hm_ref[h]
        for r in range(dil):
            rows = pl.ds(r, per, stride=dil) if dil > 1 else pl.ds(0, per)
            dil_ref[0, r, :, sl] = hm_ref[h, rows, :].astype(dil_ref.dtype)


def _proj_dil(xn, w, col0, g, epilogue, want_tok, nb, seq, norm_w=None, cos=None, sin=None):
    m, d = xn.shape
    aw = d // 2
    dil = DILATIONS[g]
    tm = _row_tile(seq, 1024)
    per = tm // dil
    assert per % 16 == 0 and col0 % aw == 0
    jb = col0 // aw
    bps = seq // tm
    in_specs = [pl.BlockSpec((tm, d), lambda i: (i, 0)),
                pl.BlockSpec((d, aw), lambda i: (0, jb))]
    args = [xn, w]
    if epilogue == "qk":
        in_specs += [pl.BlockSpec((1, HEAD_DIM), lambda i: (0, 0)),
                     pl.BlockSpec((tm, HEAD_DIM), lambda i: (lax.rem(i, bps), 0)),
                     pl.BlockSpec((tm, HEAD_DIM), lambda i: (lax.rem(i, bps), 0))]
        args += [norm_w.reshape(1, HEAD_DIM), cos, sin]
    dil_spec = pl.BlockSpec((1, dil, per, aw), lambda i: (lax.div(i, bps), 0, lax.rem(i, bps), 0))
    dil_shape = jax.ShapeDtypeStruct((nb, dil, seq // dil, aw), BF16)
    if want_tok:
        out_specs = [dil_spec, pl.BlockSpec((tm, aw), lambda i: (i, 0))]
        out_shape = [dil_shape, jax.ShapeDtypeStruct((m, aw), F32)]
    else:
        out_specs, out_shape = [dil_spec], [dil_shape]
    scratch = [pltpu.VMEM((aw // HEAD_DIM, tm, HEAD_DIM), F32)]
    res = pl.pallas_call(
        functools.partial(_proj_dil_kernel, epilogue=epilogue, dil=dil, want_tok=want_tok),
        grid=(m // tm,),
        in_specs=in_specs,
        out_specs=out_specs,
        out_shape=out_shape,
        scratch_shapes=scratch,
        compiler_params=_params("parallel"),
        name="proj_dil_%s_d%d" % (epilogue, dil),
    )(*args)
    return (res[0], res[1]) if want_tok else (res[0], None)


def _proj(xn, w, col0, ncols, epilogue, out_dtype, norm_w=None, cos=None, sin=None):
    m, d = xn.shape
    tm = _row_tile(m, 1024)
    tn = min(ncols, 1024)
    assert ncols % tn == 0 and col0 % tn == 0
    jb = col0 // tn
    in_specs = [pl.BlockSpec((tm, d), lambda i, j: (i, 0)),
                pl.BlockSpec((d, tn), lambda i, j: (0, jb + j))]
    args = [xn, w]
    if epilogue == "qk":
        cyc = cos.shape[0] // tm
        in_specs += [pl.BlockSpec((1, HEAD_DIM), lambda i, j: (0, 0)),
                     pl.BlockSpec((tm, HEAD_DIM), lambda i, j: (i % cyc, 0)),
                     pl.BlockSpec((tm, HEAD_DIM), lambda i, j: (i % cyc, 0))]
        args += [norm_w.reshape(1, HEAD_DIM), cos, sin]
    return pl.pallas_call(
        functools.partial(_proj_kernel, epilogue=epilogue),
        grid=(m // tm, ncols // tn),
        in_specs=in_specs,
        out_specs=pl.BlockSpec((tm, tn), lambda i, j: (i, j)),
        out_shape=jax.ShapeDtypeStruct((m, ncols), out_dtype),
        compiler_params=_params("parallel", "arbitrary"),
        name="proj_" + epilogue,
    )(*args)


def _rope_tables(pos):
    half = HEAD_DIM // 2
    inv_freq = jnp.power(ROPE_THETA, -jnp.arange(half, dtype=F32) * (2.0 / HEAD_DIM))
    ang = pos[:, None] * inv_freq[None, :]
    cos, sin = jnp.cos(ang), jnp.sin(ang)
    return jnp.concatenate([cos, cos], axis=-1), jnp.concatenate([-sin, sin], axis=-1)


def _attn_prompt_kernel(q_ref, kp_ref, kc_ref, vp_ref, vc_ref, o_ref, lse_ref):
    n = pl.program_id(2)
    a = lax.broadcasted_iota(jnp.int32, (SPAN, 2 * SPAN), 0)
    c = lax.broadcasted_iota(jnp.int32, (SPAN, 2 * SPAN), 1)
    mask = (c >= a) & (c <= a + SPAN) & ((c >= SPAN) | (n > 0))
    scale = HEAD_DIM ** -0.5
    for h in range(q_ref.shape[3] // HEAD_DIM):
        sl = slice(h * HEAD_DIM, (h + 1) * HEAD_DIM)
        q = q_ref[0, 0, :, sl]
        k = jnp.concatenate([kp_ref[0, 0, :, sl], kc_ref[0, 0, :, sl]], axis=0)
        v = jnp.concatenate([vp_ref[0, 0, :, sl], vc_ref[0, 0, :, sl]], axis=0)
        s = lax.dot_general(q, k, (((1,), (1,)), ((), ())), preferred_element_type=F32) * scale
        s = jnp.where(mask, s, NEG_BIG)
        m = jnp.max(s, axis=-1, keepdims=True)
        p = jnp.exp(s - m)
        l = jnp.sum(p, axis=-1, keepdims=True)
        o = jnp.dot(p.astype(BF16), v, preferred_element_type=F32) / l
        o_ref[0, 0, :, sl] = o
        lse_ref[0, 0, :, sl] = jnp.broadcast_to(m + jnp.log(l), (SPAN, HEAD_DIM))


def _attn_prompt(q, k, v):
    nb, dil, length, aw = q.shape
    assert length % SPAN == 0
    blk = (1, 1, SPAN, aw)
    cur = pl.BlockSpec(blk, lambda b, r, n: (b, r, n, 0))
    prev = pl.BlockSpec(blk, lambda b, r, n: (b, r, jnp.maximum(n - 1, 0), 0))
    return pl.pallas_call(
        _attn_prompt_kernel,
        grid=(nb, dil, length // SPAN),
        in_specs=[cur, prev, cur, prev, cur],
        out_specs=[cur, cur],
        out_shape=[jax.ShapeDtypeStruct(q.shape, F32)] * 2,
        compiler_params=_params("parallel", "parallel", "arbitrary"),
        name="attn_prompt_d%d" % dil,
    )(q, k, k, v, v)


def _attn_sample_kernel(q_ref, kn_ref, vn_ref, c1_ref, c4_ref, c16_ref, o_ref, lse_ref):
    n_new, n_heads, _ = q_ref.shape
    hg = n_heads // N_DIL
    scale = HEAD_DIM ** -0.5
    caches = (c1_ref, c4_ref, c16_ref)
    for g in range(N_DIL):
        hs = slice(g * hg, (g + 1) * hg)
        for t in range(n_new):
            q = q_ref[t, hs, :]
            if DILATIONS[g] == 1:
                kc, vc = caches[g][:, 0], caches[g][:, 1]
                rows = lax.broadcasted_iota(jnp.int32, (kc.shape[0], hg, 1), 0)
                s = jnp.sum(kc * q[None], axis=-1, keepdims=True) * scale
                s = jnp.where(rows >= t, s, NEG_BIG)
                new = range(t + 1)
            else:
                kc, vc = caches[g][:, t, 0], caches[g][:, t, 1]
                s = jnp.sum(kc * q[None], axis=-1, keepdims=True) * scale
                new = (t,)
            s_new = [jnp.sum(kn_ref[u, hs, :] * q, axis=-1, keepdims=True) * scale for u in new]
            m = jnp.max(s, axis=0)
            for sn in s_new:
                m = jnp.maximum(m, sn)
            p = jnp.exp(s - m[None])
            l = jnp.sum(p, axis=0)
            acc = jnp.sum(p * vc, axis=0)
            for u, sn in zip(new, s_new):
                pn = jnp.exp(sn - m)
                l = l + pn
                acc = acc + pn * vn_ref[u, hs, :]
            o_ref[t, hs, :] = acc / l
            lse_ref[t, hs, :] = jnp.broadcast_to(m + jnp.log(l), (hg, HEAD_DIM))


def _attn_sample(q, kn, vn, caches, layer):
    nb, n_new, n_heads, _ = q.shape
    hg = n_heads // N_DIL
    views, specs = [], []
    for g in range(N_DIL):
        dil = DILATIONS[g]
        depth, _, buf = caches[g].shape[:3]
        assert buf == WINDOWS[g] and (dil == 1 or dil % n_new == 0)
        if dil == 1:
            views.append(caches[g])
            specs.append(pl.BlockSpec((None, None, buf, 2, hg, HEAD_DIM),
                                      lambda b: (layer, b, 0, 0, 0, 0)))
        else:
            views.append(caches[g].reshape(depth, nb, SPAN, dil, 2, hg, HEAD_DIM))
            specs.append(pl.BlockSpec((None, None, SPAN, n_new, 2, hg, HEAD_DIM),
                                      lambda b: (layer, b, 0, 0, 0, 0, 0)))
    tok = pl.BlockSpec((None, n_new, n_heads, HEAD_DIM), lambda b: (b, 0, 0, 0))
    return pl.pallas_call(
        _attn_sample_kernel,
        grid=(nb,),
        in_specs=[tok, tok, tok] + specs,
        out_specs=[tok, tok],
        out_shape=[jax.ShapeDtypeStruct(q.shape, F32)] * 2,
        compiler_params=_params("parallel"),
        name="attn_sample",
    )(q, kn, vn, *views)


def _s5_params_kernel(lre_ref, lim_ref, ldt_ref, bre_ref, bim_ref, cst_ref, bbre_ref, bbim_ref):
    dt = jnp.exp(ldt_ref[...])
    lre, lim = lre_ref[...], lim_ref[...]
    xr, xi = lre * dt, lim * dt
    shape = (ROW_TILE, xr.shape[1])
    row = lax.broadcasted_iota(jnp.int32, shape, 0)

    def a_pow(k):
        mag = jnp.exp(xr * k)
        return mag * jnp.cos(xi * k), mag * jnp.sin(xi * k)

    for idx, k in enumerate((1, 2, 4)):
        pr, pi = a_pow(float(k))
        cst_ref[2 * idx] = jnp.where(row >= k, jnp.broadcast_to(pr, shape), 0.0)
        cst_ref[2 * idx + 1] = jnp.where(row >= k, jnp.broadcast_to(pi, shape), 0.0)
    cr, ci = a_pow((row + 1).astype(F32))
    cst_ref[6] = cr
    cst_ref[7] = ci
    ar, ai = a_pow(1.0)
    nr, ni = ar - 1.0, ai
    den = lre * lre + lim * lim
    fr = (nr * lre + ni * lim) / den
    fi = (ni * lre - nr * lim) / den
    bre, bim = bre_ref[...], bim_ref[...]
    bbre_ref[...] = fr * bre - fi * bim
    bbim_ref[...] = fr * bim + fi * bre


def _s5_params(lam_re, lam_im, log_dt, b_re, b_im, c_re, c_im):
    n_grp, n_st = lam_re.shape
    n_ch = b_re.shape[2]
    n = n_grp * n_st
    gc = GROUPS_PER_CHUNK
    n_chunk = n_grp // gc
    to_lanes = lambda b: b.transpose(2, 0, 1).reshape(n_ch, n)
    full = lambda *shape: pl.BlockSpec(shape, lambda: (0,) * len(shape))
    cst, bb_re, bb_im = pl.pallas_call(
        _s5_params_kernel,
        in_specs=[full(1, n)] * 3 + [full(n_ch, n)] * 2,
        out_specs=[full(8, ROW_TILE, n), full(n_ch, n), full(n_ch, n)],
        out_shape=[jax.ShapeDtypeStruct((8, ROW_TILE, n), F32),
                   jax.ShapeDtypeStruct((n_ch, n), F32), jax.ShapeDtypeStruct((n_ch, n), F32)],
        name="s5_params",
    )(lam_re.reshape(1, n), lam_im.reshape(1, n),
      jnp.broadcast_to(log_dt[:, None], (n_grp, n_st)).reshape(1, n), to_lanes(b_re), to_lanes(b_im))

    eye = jnp.eye(gc, dtype=F32)

    def b_blocks(bb):
        bb = bb.reshape(n_ch, n_chunk, gc, n_st)
        blk = jnp.einsum("cjgp,gh->jgchp", bb, eye)
        return blk.reshape(n_chunk, gc * n_ch, gc * n_st).astype(BF16)

    def c_blocks(cc):
        cc = cc.reshape(n_chunk, gc, n_ch, n_st)
        blk = jnp.einsum("jgcp,gh->jgphc", cc, eye)
        return blk.reshape(n_chunk, gc * n_st, gc * n_ch).astype(BF16)

    return cst, b_blocks(bb_re), b_blocks(bb_im), c_blocks(c_re), c_blocks(c_im)


def _s5_scan_kernel(u_ref, bre_ref, bim_ref, cst_ref, cre_ref, cim_ref, d_ref,
                    y_ref, hre_ref, him_ref, sre, sim, car_re, car_im):
    @pl.when(pl.program_id(2) == 0)
    def _():
        car_re[...] = jnp.zeros_like(car_re)
        car_im[...] = jnp.zeros_like(car_im)

    u = u_ref[0]
    ub = u.astype(BF16)
    sre[...] = jnp.dot(ub, bre_ref[0], preferred_element_type=F32)
    sim[...] = jnp.dot(ub, bim_ref[0], preferred_element_type=F32)
    steps = [(cst_ref[2 * i], cst_ref[2 * i + 1], 1 << i) for i in range(3)]
    acr, aci = cst_ref[6], cst_ref[7]
    n_lane = sre.shape[1]

    def tile(i, carry):
        cr, ci = carry
        r0 = pl.multiple_of(i * ROW_TILE, ROW_TILE)
        xr, xi = sre[pl.ds(r0, ROW_TILE), :], sim[pl.ds(r0, ROW_TILE), :]
        for ar, ai, sh in steps:
            pr, pi = pltpu.roll(xr, sh, axis=0), pltpu.roll(xi, sh, axis=0)
            xr, xi = xr + ar * pr - ai * pi, xi + ar * pi + ai * pr
        hr = xr + acr * cr - aci * ci
        hi = xi + acr * ci + aci * cr
        sre[pl.ds(r0, ROW_TILE), :] = hr
        sim[pl.ds(r0, ROW_TILE), :] = hi
        last = ROW_TILE - 1
        return (jnp.broadcast_to(hr[last:, :], (ROW_TILE, n_lane)),
                jnp.broadcast_to(hi[last:, :], (ROW_TILE, n_lane)))

    cr, ci = lax.fori_loop(0, sre.shape[0] // ROW_TILE, tile, (car_re[...], car_im[...]), unroll=4)
    car_re[...] = cr
    car_im[...] = ci
    y = (jnp.dot(sre[...].astype(BF16), cre_ref[0], preferred_element_type=F32)
         - jnp.dot(sim[...].astype(BF16), cim_ref[0], preferred_element_type=F32))
    y_ref[0] = y + d_ref[...] * u
    hre_ref[0] = cr
    him_ref[0] = ci


def _s5_prompt(u, prm, ssm_d, nb, seq):
    cst, bre, bim, cre, cim = prm
    n_chunk, cw, sw = bre.shape
    width = u.shape[1]
    tc = min(seq, 512)
    assert seq % tc == 0
    y, hre, him = pl.pallas_call(
        _s5_scan_kernel,
        grid=(nb, n_chunk, seq // tc),
        in_specs=[pl.BlockSpec((1, tc, cw), lambda b, j, t: (b, t, j)),
                  pl.BlockSpec((1, cw, sw), lambda b, j, t: (j, 0, 0)),
                  pl.BlockSpec((1, cw, sw), lambda b, j, t: (j, 0, 0)),
                  pl.BlockSpec((8, ROW_TILE, sw), lambda b, j, t: (0, 0, j)),
                  pl.BlockSpec((1, sw, cw), lambda b, j, t: (j, 0, 0)),
                  pl.BlockSpec((1, sw, cw), lambda b, j, t: (j, 0, 0)),
                  pl.BlockSpec((1, cw), lambda b, j, t: (0, j))],
        out_specs=[pl.BlockSpec((1, tc, cw), lambda b, j, t: (b, t, j)),
                   pl.BlockSpec((1, ROW_TILE, sw), lambda b, j, t: (b, 0, j)),
                   pl.BlockSpec((1, ROW_TILE, sw), lambda b, j, t: (b, 0, j))],
        out_shape=[jax.ShapeDtypeStruct((nb, seq, width), F32),
                   jax.ShapeDtypeStruct((nb, ROW_TILE, n_chunk * sw), F32),
                   jax.ShapeDtypeStruct((nb, ROW_TILE, n_chunk * sw), F32)],
        scratch_shapes=[pltpu.VMEM((tc, sw), F32), pltpu.VMEM((tc, sw), F32),
                        pltpu.VMEM((ROW_TILE, sw), F32), pltpu.VMEM((ROW_TILE, sw), F32)],
        compiler_params=_params("parallel", "parallel", "arbitrary"),
        name="s5_scan",
    )(u.reshape(nb, seq, width), bre, bim, cst, cre, cim, ssm_d.reshape(1, width))
    return y.reshape(nb * seq, width), hre[:, 0], him[:, 0]


def _s5_sample_kernel(u_ref, h0re_ref, h0im_ref, bre_ref, bim_ref, cst_ref, cre_ref, cim_ref, d_ref,
                      y_ref, hre_ref, him_ref):
    ar, ai = cst_ref[6, 0:1, :], cst_ref[7, 0:1, :]
    hr, hi = h0re_ref[...], h0im_ref[...]
    for t in range(u_ref.shape[0]):
        u = u_ref[t]
        ub = u.astype(BF16)
        hr, hi = (ar * hr - ai * hi + jnp.dot(ub, bre_ref[0], preferred_element_type=F32),
                  ar * hi + ai * hr + jnp.dot(ub, bim_ref[0], preferred_element_type=F32))
        y = (jnp.dot(hr.astype(BF16), cre_ref[0], preferred_element_type=F32)
             - jnp.dot(hi.astype(BF16), cim_ref[0], preferred_element_type=F32))
        y_ref[t] = y + d_ref[...] * u
    hre_ref[...] = hr
    him_ref[...] = hi


def _s5_sample(u, h0re, h0im, prm, ssm_d):
    cst, bre, bim, cre, cim = prm
    n_chunk, cw, sw = bre.shape
    n_new, nb, width = u.shape
    return pl.pallas_call(
        _s5_sample_kernel,
        grid=(n_chunk,),
        in_specs=[pl.BlockSpec((n_new, nb, cw), lambda j: (0, 0, j)),
                  pl.BlockSpec((nb, sw), lambda j: (0, j)),
                  pl.BlockSpec((nb, sw), lambda j: (0, j)),
                  pl.BlockSpec((1, cw, sw), lambda j: (j, 0, 0)),
                  pl.BlockSpec((1, cw, sw), lambda j: (j, 0, 0)),
                  pl.BlockSpec((8, ROW_TILE, sw), lambda j: (0, 0, j)),
                  pl.BlockSpec((1, sw, cw), lambda j: (j, 0, 0)),
                  pl.BlockSpec((1, sw, cw), lambda j: (j, 0, 0)),
                  pl.BlockSpec((1, cw), lambda j: (0, j))],
        out_specs=[pl.BlockSpec((n_new, nb, cw), lambda j: (0, 0, j)),
                   pl.BlockSpec((nb, sw), lambda j: (0, j)),
                   pl.BlockSpec((nb, sw), lambda j: (0, j))],
        out_shape=[jax.ShapeDtypeStruct((n_new, nb, width), F32),
                   jax.ShapeDtypeStruct((nb, n_chunk * sw), F32),
                   jax.ShapeDtypeStruct((nb, n_chunk * sw), F32)],
        compiler_params=_params("parallel"),
        name="s5_sample",
    )(u, h0re, h0im, bre, bim, cst, cre, cim, ssm_d.reshape(1, width))


def _glu_kernel(y_ref, g_ref, w_ref, b_ref, o_ref):
    y = y_ref[...]
    s = 0.5 * y * (1.0 + jnp.tanh(math.sqrt(2.0 / math.pi) * (y + 0.044715 * (y * y * y))))
    z = jnp.dot(s.astype(BF16), w_ref[...], preferred_element_type=F32) + b_ref[...]
    o_ref[...] = (s * _sigmoid(z) * g_ref[...].astype(F32)).astype(o_ref.dtype)


def _glu(y, gate, w, b):
    m, width = y.shape
    tm = _row_tile(m, 512)
    row = pl.BlockSpec((tm, width), lambda i: (i, 0))
    return pl.pallas_call(
        _glu_kernel,
        grid=(m // tm,),
        in_specs=[row, row, pl.BlockSpec((width, width), lambda i: (0, 0)),
                  pl.BlockSpec((1, width), lambda i: (0, 0))],
        out_specs=row,
        out_shape=jax.ShapeDtypeStruct((m, width), BF16),
        compiler_params=_params("parallel"),
        name="glu",
    )(y, gate, w, b.reshape(1, width))


def _merge_kernel(*refs):
    ol_refs = refs[:2 * N_DIL]
    ga_ref, ys_ref, ma_ref, ms_ref, wa_ref, ws_ref, out_ref = refs[2 * N_DIL:2 * N_DIL + 7]
    a_ref, tok_ref = refs[2 * N_DIL + 7:]
    for h in range(a_ref.shape[1] // HEAD_DIM):
        sl = slice(h * HEAD_DIM, (h + 1) * HEAD_DIM)
        vals = []
        for idx, ref in enumerate(ol_refs):
            dil, per = ref.shape[1], ref.shape[2]
            if dil == 1:
                vals.append(ref[0, 0, :, sl])
            else:
                for r in range(dil):
                    tok_ref[idx, pl.ds(r, per, stride=dil), :] = ref[0, r, :, sl]
                vals.append(tok_ref[idx])
        o0, o1, o2, l0, l1, l2 = vals
        mx = jnp.maximum(jnp.maximum(l0, l1), l2)
        e0, e1, e2 = jnp.exp(l0 - mx), jnp.exp(l1 - mx), jnp.exp(l2 - mx)
        attn = (e0 * o0 + e1 * o1 + e2 * o2) / (e0 + e1 + e2)
        a_ref[:, sl] = (attn * ga_ref[:, sl].astype(F32)).astype(BF16)
    ya = jnp.dot(a_ref[...], wa_ref[...], preferred_element_type=F32)
    yb = jnp.dot(ys_ref[...], ws_ref[...], preferred_element_type=F32)
    out_ref[...] = (ma_ref[...].astype(F32) * ya + ms_ref[...].astype(F32) * yb).astype(out_ref.dtype)


def _merge(outs, lses, g_attn, y_ssm, gates, w_a, w_s):
    m, aw = g_attn.shape
    d = w_a.shape[1]
    nb = outs[0].shape[0]
    seq = m // nb
    tm = _row_tile(seq, 256)
    bps = seq // tm
    row = pl.BlockSpec((tm, aw), lambda i: (i, 0))
    ol_specs = []
    for arr in list(outs) + list(lses):
        dil = arr.shape[1]
        assert tm % (dil * ROW_TILE) == 0
        ol_specs.append(pl.BlockSpec((1, dil, tm // dil, aw),
                                     lambda i: (lax.div(i, bps), 0, lax.rem(i, bps), 0)))
    scratch = [pltpu.VMEM((tm, aw), BF16), pltpu.VMEM((2 * N_DIL, tm, HEAD_DIM), F32)]
    return pl.pallas_call(
        _merge_kernel,
        grid=(m // tm,),
        in_specs=ol_specs + [row, row, pl.BlockSpec((tm, d), lambda i: (i, 0)),
                             pl.BlockSpec((tm, d), lambda i: (i, 1)),
                             pl.BlockSpec((aw, d), lambda i: (0, 0)),
                             pl.BlockSpec((aw, d), lambda i: (0, 0))],
        out_specs=pl.BlockSpec((tm, d), lambda i: (i, 0)),
        out_shape=jax.ShapeDtypeStruct((m, d), BF16),
        scratch_shapes=scratch,
        compiler_params=_params("parallel"),
        name="merge",
    )(*outs, *lses, g_attn, y_ssm, gates, gates, w_a, w_s)


def _out_kernel(x_ref, m_ref, w_ref, o_ref):
    o_ref[...] = x_ref[...] + jnp.dot(m_ref[...], w_ref[...], preferred_element_type=F32)


def _out_proj(x, merged, w):
    m, d = x.shape
    tm = _row_tile(m, 1024)
    tn = min(d, 1024)
    return pl.pallas_call(
        _out_kernel,
        grid=(m // tm, d // tn),
        in_specs=[pl.BlockSpec((tm, tn), lambda i, j: (i, j)),
                  pl.BlockSpec((tm, d), lambda i, j: (i, 0)),
                  pl.BlockSpec((d, tn), lambda i, j: (0, j))],
        out_specs=pl.BlockSpec((tm, tn), lambda i, j: (i, j)),
        out_shape=jax.ShapeDtypeStruct((m, d), F32),
        compiler_params=_params("parallel", "arbitrary"),
        name="out_proj",
    )(x, merged, w)


def _kv_shift_kernel(*refs):
    n = N_DIL
    caches, news, outs = refs[:n], refs[n:2 * n], refs[2 * n:3 * n]
    ring, in_sem, out_sem = refs[3 * n:3 * n + 3]
    new_bufs, new_sem = refs[3 * n + 3:3 * n + 3 + n], refs[3 * n + 3 + n]

    new_in = [pltpu.make_async_copy(news[g], new_bufs[g], new_sem.at[g]) for g in range(n)]
    for cp in new_in:
        cp.start()

    jobs = []
    for g in range(n):
        buf, n_new = caches[g].shape[1], news[g].shape[1]
        keep = buf - n_new
        for i in range(caches[g].shape[0]):
            for r0 in range(0, keep, KV_COPY_ROWS):
                rows = min(KV_COPY_ROWS, keep - r0)
                jobs.append((caches[g].at[i, pl.ds(n_new + r0, rows)], outs[g].at[i, pl.ds(r0, rows)], rows))

    def load(j):
        src, _, rows = jobs[j]
        slot = j % KV_COPY_SLOTS
        return pltpu.make_async_copy(src, ring.at[slot, pl.ds(0, rows)], in_sem.at[slot])

    def store(j):
        _, dst, rows = jobs[j]
        slot = j % KV_COPY_SLOTS
        return pltpu.make_async_copy(ring.at[slot, pl.ds(0, rows)], dst, out_sem.at[slot])

    ahead = KV_COPY_SLOTS // 2
    for j in range(min(ahead, len(jobs))):
        load(j).start()
    for j in range(len(jobs)):
        load(j).wait()
        store(j).start()
        nxt = j + ahead
        if nxt < len(jobs):
            if nxt >= KV_COPY_SLOTS:
                store(nxt - KV_COPY_SLOTS).wait()
            load(nxt).start()
    for j in range(max(0, len(jobs) - KV_COPY_SLOTS), len(jobs)):
        store(j).wait()

    new_out = []
    for g in range(n):
        keep = caches[g].shape[1] - news[g].shape[1]
        new_in[g].wait()
        new_out.append(pltpu.make_async_copy(new_bufs[g], outs[g].at[:, pl.ds(keep, news[g].shape[1])],
                                             new_sem.at[n + g]))
        new_out[g].start()
    for cp in new_out:
        cp.wait()


def _kv_shift(caches, news):
    flat = lambda a: a.reshape(a.shape[0] * a.shape[1], a.shape[2], a.shape[3] * a.shape[4], a.shape[5])
    cf, nf = [flat(c) for c in caches], [flat(w) for w in news]
    for c, w in zip(cf, nf):
        assert c.shape[1] > w.shape[1]
    tile = cf[0].shape[2:]
    hbm = pl.BlockSpec(memory_space=pl.ANY)
    outs = pl.pallas_call(
        _kv_shift_kernel,
        in_specs=[hbm] * (2 * N_DIL),
        out_specs=[hbm] * N_DIL,
        out_shape=[jax.ShapeDtypeStruct(c.shape, c.dtype) for c in cf],
        scratch_shapes=[pltpu.VMEM((KV_COPY_SLOTS, KV_COPY_ROWS) + tile, F32),
                        pltpu.SemaphoreType.DMA((KV_COPY_SLOTS,)),
                        pltpu.SemaphoreType.DMA((KV_COPY_SLOTS,))]
                       + [pltpu.VMEM(w.shape, F32) for w in nf]
                       + [pltpu.SemaphoreType.DMA((2 * N_DIL,))],
        compiler_params=pltpu.CompilerParams(vmem_limit_bytes=VMEM_LIMIT),
        name="kv_shift",
    )(*cf, *nf)
    return [o.reshape(c.shape) for o, c in zip(outs, caches)]


def _layer(x, cos, sin, lw, nb, seq, sample):
    (norm_w, w_in, q_norm_w, k_norm_w, w_glu, b_glu, w_br_attn, w_br_ssm, w_out, ssm_d, s5prm) = lw
    m, d = x.shape
    aw = d // 2
    qkv = N_DIL * aw
    hg = aw // HEAD_DIM
    xn = _rmsnorm(x, norm_w)
    g_attn = _proj(xn, w_in, 3 * qkv, aw, "silu", BF16)
    u = _proj(xn, w_in, 3 * qkv + aw, aw, "plain", F32)
    g_ssm = _proj(xn, w_in, 3 * qkv + 2 * aw, aw, "silu", BF16)
    gates = _proj(xn, w_in, 3 * qkv + 3 * aw, 2 * d, "sigmoid", BF16)

    new_kv = []
    if sample is None:
        outs, lses = [], []
        for g in range(N_DIL):
            qd, _ = _proj_dil(xn, w_in, g * aw, g, "qk", False, nb, seq, q_norm_w, cos, sin)
            kd, kt = _proj_dil(xn, w_in, qkv + g * aw, g, "qk", True, nb, seq, k_norm_w, cos, sin)
            vd, vt = _proj_dil(xn, w_in, 2 * qkv + g * aw, g, "plain", True, nb, seq)
            o, lse = _attn_prompt(qd, kd, vd)
            outs.append(o)
            lses.append(lse)
            keep = min(WINDOWS[g], seq)
            kt, vt = kt.reshape(nb, seq, hg, HEAD_DIM), vt.reshape(nb, seq, hg, HEAD_DIM)
            new_kv.append(jnp.stack([kt[:, seq - keep:], vt[:, seq - keep:]], axis=2))
        y, hre, him = _s5_prompt(u, s5prm, ssm_d, nb, seq)
    else:
        caches, layer, h0re, h0im = sample
        q = _proj(xn, w_in, 0, qkv, "qk", F32, q_norm_w, cos, sin)
        k = _proj(xn, w_in, qkv, qkv, "qk", F32, k_norm_w, cos, sin)
        v = _proj(xn, w_in, 2 * qkv, qkv, "plain", F32)
        tok = (nb, seq, N_DIL * hg, HEAD_DIM)
        o, lse = _attn_sample(q.reshape(tok), k.reshape(tok), v.reshape(tok), caches, layer)
        o, lse = o.reshape(m, qkv), lse.reshape(m, qkv)
        outs = [o[:, g * aw:(g + 1) * aw].reshape(1, 1, m, aw) for g in range(N_DIL)]
        lses = [lse[:, g * aw:(g + 1) * aw].reshape(1, 1, m, aw) for g in range(N_DIL)]
        ut = u.reshape(nb, seq, aw).transpose(1, 0, 2)
        yt, hre, him = _s5_sample(ut, h0re, h0im, s5prm, ssm_d)
        y = yt.transpose(1, 0, 2).reshape(m, aw)
        k5, v5 = k.reshape(nb, seq, N_DIL, hg, HEAD_DIM), v.reshape(nb, seq, N_DIL, hg, HEAD_DIM)
        new_kv = [jnp.stack([k5[:, :, g], v5[:, :, g]], axis=2) for g in range(N_DIL)]

    y_ssm = _glu(y, g_ssm, w_glu, b_glu)
    merged = _merge(outs, lses, g_attn, y_ssm, gates, w_br_attn, w_br_ssm)
    x_out = _out_proj(x, merged, w_out)

    n_grp = aw // SSM_GROUP_CH
    state = jnp.stack([hre, him], axis=-1).reshape(nb, n_grp, SSM_STATE, 2)
    return x_out, new_kv, state


def kernel(x_prompt, x_sample, cache_kv_d1, cache_kv_d4, cache_kv_d16, state_ssm, norm_w, w_in, q_norm_w, k_norm_w, ssm_lambda_re, ssm_lambda_im, ssm_log_dt, ssm_b_re, ssm_b_im, ssm_c_re, ssm_c_im, ssm_d, w_glu, b_glu, w_br_attn, w_br_ssm, w_out):
    nb_p, seq_p, d = x_prompt.shape
    nb_s, seq_s, _ = x_sample.shape
    depth = norm_w.shape[0]
    caches = (cache_kv_d1, cache_kv_d4, cache_kv_d16)

    cos_p, sin_p = _rope_tables(jnp.arange(seq_p, dtype=F32))
    cos_s, sin_s = _rope_tables(PAST_LEN + jnp.arange(seq_s, dtype=F32))
    cos_s, sin_s = jnp.tile(cos_s, (nb_s, 1)), jnp.tile(sin_s, (nb_s, 1))

    hp = x_prompt.reshape(nb_p * seq_p, d)
    hs = x_sample.reshape(nb_s * seq_s, d)
    kv_p = [[] for _ in range(N_DIL)]
    kv_s_new = [[] for _ in range(N_DIL)]
    ssm_p, ssm_s = [], []
    for l in range(depth):
        s5prm = _s5_params(ssm_lambda_re[l], ssm_lambda_im[l], ssm_log_dt[l], ssm_b_re[l], ssm_b_im[l],
                           ssm_c_re[l], ssm_c_im[l])
        lw = (norm_w[l], w_in[l].astype(BF16), q_norm_w[l], k_norm_w[l], w_glu[l].astype(BF16), b_glu[l],
              w_br_attn[l].astype(BF16), w_br_ssm[l].astype(BF16), w_out[l].astype(BF16), ssm_d[l], s5prm)
        hp, new_kv, st = _layer(hp, cos_p, sin_p, lw, nb_p, seq_p, None)
        for g in range(N_DIL):
            kv_p[g].append(new_kv[g])
        ssm_p.append(st)

        n_state = state_ssm.shape[2] * state_ssm.shape[3]
        h0re = state_ssm[l, ..., 0].reshape(nb_s, n_state)
        h0im = state_ssm[l, ..., 1].reshape(nb_s, n_state)
        hs, new_kv_s, st_s = _layer(hs, cos_s, sin_s, lw, nb_s, seq_s, (caches, l, h0re, h0im))
        for g in range(N_DIL):
            kv_s_new[g].append(new_kv_s[g])
        ssm_s.append(st_s)

    kv_s = _kv_shift(caches, [jnp.stack(n) for n in kv_s_new])
    return (hp.reshape(x_prompt.shape), hs.reshape(x_sample.shape),
            jnp.stack(kv_p[0]), jnp.stack(kv_p[1]), jnp.stack(kv_p[2]), jnp.stack(ssm_p),
            kv_s[0], kv_s[1], kv_s[2], jnp.stack(ssm_s))
```

```python
import functools
import math

import jax
import jax.numpy as jnp
from jax import lax
from jax.experimental import pallas as pl
from jax.experimental.pallas import tpu as pltpu

F32 = jnp.float32
BF16 = jnp.bfloat16

HEAD_DIM = 128
WINDOWS = (128, 512, 2048)
DILATIONS = (1, 4, 16)
N_DIL = 3
SPAN = 128
SSM_GROUP_CH = 16
SSM_STATE = 64
GROUPS_PER_CHUNK = 8
ROPE_THETA = 10000.0
NORM_EPS = 1e-6
PAST_LEN = 16384
NEG_BIG = -1e30
VMEM_LIMIT = 48 * 1024 * 1024
ROW_TILE = 8
QK_ROW_CHUNK = 64
KV_COPY_ROWS = 512
KV_COPY_SLOTS = 4


def _params(*sem):
    return pltpu.CompilerParams(dimension_semantics=sem, vmem_limit_bytes=VMEM_LIMIT)


def _sigmoid(x):
    return 1.0 / (1.0 + jnp.exp(-x))


def _row_tile(m, cap):
    t = min(m, cap)
    assert m % t == 0, (m, t)
    return t


def _rmsnorm_kernel(x_ref, w_ref, o_ref):
    x = x_ref[...]
    y = x * lax.rsqrt(jnp.mean(x * x, axis=-1, keepdims=True) + NORM_EPS)
    o_ref[...] = (y * w_ref[...]).astype(o_ref.dtype)


def _rmsnorm(x, w):
    m, d = x.shape
    tm = _row_tile(m, 512)
    return pl.pallas_call(
        _rmsnorm_kernel,
        grid=(m // tm,),
        in_specs=[pl.BlockSpec((tm, d), lambda i: (i, 0)),
                  pl.BlockSpec((1, d), lambda i: (0, 0))],
        out_specs=pl.BlockSpec((tm, d), lambda i: (i, 0)),
        out_shape=jax.ShapeDtypeStruct((m, d), BF16),
        compiler_params=_params("parallel"),
        name="rmsnorm",
    )(x, w.reshape(1, d))


def _norm_rope_inplace(t_ref, nw_ref, cos_ref, sin_ref, head_major=False):
    rows = cos_ref.shape[0]
    n_heads = t_ref.shape[0] // rows if head_major else t_ref.shape[1] // HEAD_DIM
    rc = min(rows, QK_ROW_CHUNK)
    half = HEAD_DIM // 2
    nw = nw_ref[...]
    nw_swapped = pltpu.roll(nw, half, axis=1)

    def chunk(c, carry):
        r0 = pl.multiple_of(c * rc, rc)
        cs = cos_ref[pl.ds(r0, rc), :] * nw
        ss = sin_ref[pl.ds(r0, rc), :] * nw_swapped
        for h in range(n_heads):
            idx = (pl.ds(h * rows + r0, rc), slice(None)) if head_major else (
                pl.ds(r0, rc), slice(h * HEAD_DIM, (h + 1) * HEAD_DIM))
            a = t_ref[idx]
            r = lax.rsqrt(jnp.mean(a * a, axis=-1, keepdims=True) + NORM_EPS)
            t_ref[idx] = (a * cs + pltpu.roll(a, half, axis=1) * ss) * r
        return carry

    lax.fori_loop(0, rows // rc, chunk, 0)


def _proj_kernel(x_ref, w_ref, *rest, epilogue):
    acc = jnp.dot(x_ref[...], w_ref[...], preferred_element_type=F32)
    if epilogue == "qk":
        nw_ref, cos_ref, sin_ref, o_ref = rest
        o_ref[...] = acc
        _norm_rope_inplace(o_ref, nw_ref, cos_ref, sin_ref)
    else:
        (o_ref,) = rest
        if epilogue == "silu":
            acc = acc * _sigmoid(acc)
        elif epilogue == "sigmoid":
            acc = _sigmoid(acc)
        o_ref[...] = acc.astype(o_ref.dtype)


def _project_dilated(x, w_ref, nw_ref, cos_ref, sin_ref, hm_ref, d_ref, dil):
    tm = x.shape[0]
    n_heads = w_ref.shape[1] // HEAD_DIM
    per = tm // dil
    acc = jnp.dot(x, w_ref[...], preferred_element_type=F32)
    for h in range(n_heads):
        hm_ref[h * tm:(h + 1) * tm, :] = acc[:, h * HEAD_DIM:(h + 1) * HEAD_DIM]
    if nw_ref is not None:
        _norm_rope_inplace(hm_ref, nw_ref, cos_ref, sin_ref, head_major=True)
    for h in range(n_heads):
        for r in range(dil):
            rows = pl.ds(h * tm + r, per, stride=dil) if dil > 1 else pl.ds(h * tm, per)
            d_ref[0, r, :, h * HEAD_DIM:(h + 1) * HEAD_DIM] = hm_ref[rows, :].astype(d_ref.dtype)


def _q_kernel(x_ref, w_ref, nw_ref, cos_ref, sin_ref, qd_ref, hm_ref, *, dil):
    _project_dilated(x_ref[...], w_ref, nw_ref, cos_ref, sin_ref, hm_ref, qd_ref, dil)


def _kv_kernel(x_ref, wk_ref, wv_ref, nw_ref, cos_ref, sin_ref, kd_ref, vd_ref, kv_ref, hk_ref, hv_ref,
               *, dil, first_kept):
    half_idx = pl.program_id(2)
    tm = x_ref.shape[0]
    n_heads = wk_ref.shape[1] // HEAD_DIM

    @pl.when(half_idx == 0)
    def _():
        x = x_ref[...]
        _project_dilated(x, wk_ref, nw_ref, cos_ref, sin_ref, hk_ref, kd_ref, dil)
        _project_dilated(x, wv_ref, None, None, None, hv_ref, vd_ref, dil)

    kb = kv_ref.shape[1]
    half = tm // 2

    @pl.when(2 * pl.program_id(1) + half_idx >= first_kept)
    def _():
        base = (half_idx + 1) * half - kb

        def token(t, carry):
            rows = pl.ds(base + t, n_heads, stride=tm)
            kv_ref[0, t, 0] = hk_ref[rows, :]
            kv_ref[0, t, 1] = hv_ref[rows, :]
            return carry

        lax.fori_loop(0, kb, token, 0, unroll=8)


def _qkv(xn, w, g, nb, seq, q_norm_w, k_norm_w, cos, sin):
    m, d = xn.shape
    aw = d // 2
    hg = aw // HEAD_DIM
    dil = DILATIONS[g]
    keep = min(WINDOWS[g], seq)
    tm = _row_tile(seq, 1024)
    per = tm // dil
    half = tm // 2
    kb = min(keep, half)
    bps = seq // tm
    n_half = 2 * bps
    first_kept = (seq - keep) // half if keep >= half else n_half - 1
    assert per % 16 == 0 and keep % kb == 0 and (seq - keep) % kb == 0
    once = pl.Buffered(1)
    nw = lambda a: a.reshape(1, HEAD_DIM)
    dshape = jax.ShapeDtypeStruct((nb, dil, seq // dil, aw), BF16)
    staging = pltpu.VMEM((hg * tm, HEAD_DIM), F32)

    qd = pl.pallas_call(
        functools.partial(_q_kernel, dil=dil),
        grid=(nb, bps),
        in_specs=[pl.BlockSpec((tm, d), lambda b, i: (b * bps + i, 0)),
                  pl.BlockSpec((d, aw), lambda b, i: (0, g), pipeline_mode=once),
                  pl.BlockSpec((1, HEAD_DIM), lambda b, i: (0, 0)),
                  pl.BlockSpec((tm, HEAD_DIM), lambda b, i: (i, 0)),
                  pl.BlockSpec((tm, HEAD_DIM), lambda b, i: (i, 0))],
        out_specs=pl.BlockSpec((1, dil, per, aw), lambda b, i: (b, 0, i, 0)),
        out_shape=dshape,
        scratch_shapes=[staging],
        compiler_params=_params("parallel", "parallel"),
        name="q_d%d" % dil,
    )(xn, w, nw(q_norm_w), cos, sin)

    wspec = lambda col: pl.BlockSpec((d, aw), lambda b, i, h: (0, col), pipeline_mode=once)
    tspec = pl.BlockSpec((tm, HEAD_DIM), lambda b, i, h: (i, 0))
    dspec = pl.BlockSpec((1, dil, per, aw), lambda b, i, h: (b, 0, i, 0))
    kd, vd, kv = pl.pallas_call(
        functools.partial(_kv_kernel, dil=dil, first_kept=first_kept),
        grid=(nb, bps, 2),
        in_specs=[pl.BlockSpec((tm, d), lambda b, i, h: (b * bps + i, 0)),
                  wspec(N_DIL + g), wspec(2 * N_DIL + g),
                  pl.BlockSpec((1, HEAD_DIM), lambda b, i, h: (0, 0)), tspec, tspec],
        out_specs=[dspec, dspec,
                   pl.BlockSpec((1, kb, 2, hg, HEAD_DIM),
                                lambda b, i, h: (b, jnp.maximum(2 * i + h - first_kept, 0), 0, 0, 0))],
        out_shape=[dshape, dshape, jax.ShapeDtypeStruct((nb, keep, 2, hg, HEAD_DIM), F32)],
        scratch_shapes=[staging, staging],
        compiler_params=_params("parallel", "arbitrary", "arbitrary"),
        name="kv_d%d" % dil,
    )(xn, w, w, nw(k_norm_w), cos, sin)
    return qd, kd, vd, kv


def _proj(xn, w, col0, ncols, epilogue, out_dtype, norm_w=None, cos=None, sin=None):
    m, d = xn.shape
    tm = _row_tile(m, 1024)
    tn = min(ncols, 1024)
    assert ncols % tn == 0 and col0 % tn == 0
    jb = col0 // tn
    in_specs = [pl.BlockSpec((tm, d), lambda i, j: (i, 0)),
                pl.BlockSpec((d, tn), lambda i, j: (0, jb + j))]
    args = [xn, w]
    if epilogue == "qk":
        cyc = cos.shape[0] // tm
        in_specs += [pl.BlockSpec((1, HEAD_DIM), lambda i, j: (0, 0)),
                     pl.BlockSpec((tm, HEAD_DIM), lambda i, j: (i % cyc, 0)),
                     pl.BlockSpec((tm, HEAD_DIM), lambda i, j: (i % cyc, 0))]
        args += [norm_w.reshape(1, HEAD_DIM), cos, sin]
    return pl.pallas_call(
        functools.partial(_proj_kernel, epilogue=epilogue),
        grid=(m // tm, ncols // tn),
        in_specs=in_specs,
        out_specs=pl.BlockSpec((tm, tn), lambda i, j: (i, j)),
        out_shape=jax.ShapeDtypeStruct((m, ncols), out_dtype),
        compiler_params=_params("parallel", "arbitrary"),
        name="proj_" + epilogue,
    )(*args)


def _rope_tables(pos):
    half = HEAD_DIM // 2
    inv_freq = jnp.power(ROPE_THETA, -jnp.arange(half, dtype=F32) * (2.0 / HEAD_DIM))
    ang = pos[:, None] * inv_freq[None, :]
    cos, sin = jnp.cos(ang), jnp.sin(ang)
    return jnp.concatenate([cos, cos], axis=-1), jnp.concatenate([-sin, sin], axis=-1)


def _attn_prompt_kernel(q_ref, kp_ref, kc_ref, vp_ref, vc_ref, o_ref, lse_ref):
    n = pl.program_id(2)
    a = lax.broadcasted_iota(jnp.int32, (SPAN, 2 * SPAN), 0)
    c = lax.broadcasted_iota(jnp.int32, (SPAN, 2 * SPAN), 1)
    mask = (c >= a) & (c <= a + SPAN) & ((c >= SPAN) | (n > 0))
    scale = HEAD_DIM ** -0.5
    for h in range(q_ref.shape[3] // HEAD_DIM):
        sl = slice(h * HEAD_DIM, (h + 1) * HEAD_DIM)
        q = q_ref[0, 0, :, sl]
        k = jnp.concatenate([kp_ref[0, 0, :, sl], kc_ref[0, 0, :, sl]], axis=0)
        v = jnp.concatenate([vp_ref[0, 0, :, sl], vc_ref[0, 0, :, sl]], axis=0)
        s = lax.dot_general(q, k, (((1,), (1,)), ((), ())), preferred_element_type=F32) * scale
        s = jnp.where(mask, s, NEG_BIG)
        m = jnp.max(s, axis=-1, keepdims=True)
        p = jnp.exp(s - m)
        l = jnp.sum(p, axis=-1, keepdims=True)
        o = jnp.dot(p.astype(BF16), v, preferred_element_type=F32) / l
        o_ref[0, 0, :, sl] = o
        lse_ref[0, 0, :, sl] = jnp.broadcast_to(m + jnp.log(l), (SPAN, HEAD_DIM))


def _attn_prompt(q, k, v):
    nb, dil, length, aw = q.shape
    assert length % SPAN == 0
    blk = (1, 1, SPAN, aw)
    cur = pl.BlockSpec(blk, lambda b, r, n: (b, r, n, 0))
    prev = pl.BlockSpec(blk, lambda b, r, n: (b, r, jnp.maximum(n - 1, 0), 0))
    return pl.pallas_call(
        _attn_prompt_kernel,
        grid=(nb, dil, length // SPAN),
        in_specs=[cur, prev, cur, prev, cur],
        out_specs=[cur, cur],
        out_shape=[jax.ShapeDtypeStruct(q.shape, F32)] * 2,
        compiler_params=_params("parallel", "parallel", "arbitrary"),
        name="attn_prompt_d%d" % dil,
    )(q, k, k, v, v)


def _attn_sample_kernel(q_ref, kn_ref, vn_ref, c1_ref, c4_ref, c16_ref, o_ref, lse_ref):
    n_new, n_heads, _ = q_ref.shape
    hg = n_heads // N_DIL
    scale = HEAD_DIM ** -0.5
    caches = (c1_ref, c4_ref, c16_ref)
    for g in range(N_DIL):
        hs = slice(g * hg, (g + 1) * hg)
        for t in range(n_new):
            q = q_ref[t, hs, :]
            if DILATIONS[g] == 1:
                kc, vc = caches[g][:, 0], caches[g][:, 1]
                rows = lax.broadcasted_iota(jnp.int32, (kc.shape[0], hg, 1), 0)
                s = jnp.sum(kc * q[None], axis=-1, keepdims=True) * scale
                s = jnp.where(rows >= t, s, NEG_BIG)
                new = range(t + 1)
            else:
                kc, vc = caches[g][:, t, 0], caches[g][:, t, 1]
                s = jnp.sum(kc * q[None], axis=-1, keepdims=True) * scale
                new = (t,)
            s_new = [jnp.sum(kn_ref[u, hs, :] * q, axis=-1, keepdims=True) * scale for u in new]
            m = jnp.max(s, axis=0)
            for sn in s_new:
                m = jnp.maximum(m, sn)
            p = jnp.exp(s - m[None])
            l = jnp.sum(p, axis=0)
            acc = jnp.sum(p * vc, axis=0)
            for u, sn in zip(new, s_new):
                pn = jnp.exp(sn - m)
                l = l + pn
                acc = acc + pn * vn_ref[u, hs, :]
            o_ref[t, hs, :] = acc / l
            lse_ref[t, hs, :] = jnp.broadcast_to(m + jnp.log(l), (hg, HEAD_DIM))


def _attn_sample(q, kn, vn, caches, layer):
    nb, n_new, n_heads, _ = q.shape
    hg = n_heads // N_DIL
    views, specs = [], []
    for g in range(N_DIL):
        dil = DILATIONS[g]
        depth, _, buf = caches[g].shape[:3]
        assert buf == WINDOWS[g] and (dil == 1 or dil % n_new == 0)
        if dil == 1:
            views.append(caches[g])
            specs.append(pl.BlockSpec((None, None, buf, 2, hg, HEAD_DIM),
                                      lambda b: (layer, b, 0, 0, 0, 0)))
        else:
            views.append(caches[g].reshape(depth, nb, SPAN, dil, 2, hg, HEAD_DIM))
            specs.append(pl.BlockSpec((None, None, SPAN, n_new, 2, hg, HEAD_DIM),
                                      lambda b: (layer, b, 0, 0, 0, 0, 0)))
    tok = pl.BlockSpec((None, n_new, n_heads, HEAD_DIM), lambda b: (b, 0, 0, 0))
    return pl.pallas_call(
        _attn_sample_kernel,
        grid=(nb,),
        in_specs=[tok, tok, tok] + specs,
        out_specs=[tok, tok],
        out_shape=[jax.ShapeDtypeStruct(q.shape, F32)] * 2,
        compiler_params=_params("parallel"),
        name="attn_sample",
    )(q, kn, vn, *views)


def _s5_params_kernel(lre_ref, lim_ref, ldt_ref, bre_ref, bim_ref, ptab_ref, bbre_ref, bbim_ref):
    dt = jnp.exp(ldt_ref[...])
    lre, lim = lre_ref[...], lim_ref[...]
    xr, xi = lre * dt, lim * dt
    mag = jnp.exp(xr)
    ar, ai = mag * jnp.cos(xi), mag * jnp.sin(xi)
    shape = (ROW_TILE, xr.shape[1])
    abr, abi = jnp.broadcast_to(ar, shape), jnp.broadcast_to(ai, shape)
    ptab_ref[0, 0] = abr
    ptab_ref[1, 0] = abi

    def next_power(i, carry):
        pr, pi = carry
        pr, pi = pr * abr - pi * abi, pr * abi + pi * abr
        ptab_ref[0, i] = pr
        ptab_ref[1, i] = pi
        return pr, pi

    lax.fori_loop(1, ptab_ref.shape[1], next_power, (abr, abi))
    nr, ni = ar - 1.0, ai
    den = lre * lre + lim * lim
    fr = (nr * lre + ni * lim) / den
    fi = (ni * lre - nr * lim) / den
    bre, bim = bre_ref[...], bim_ref[...]
    bbre_ref[...] = fr * bre - fi * bim
    bbim_ref[...] = fr * bim + fi * bre


def _s5_params(lam_re, lam_im, log_dt, b_re, b_im, c_re, c_im, seg_len):
    n_grp, n_st = lam_re.shape
    n_ch = b_re.shape[2]
    n = n_grp * n_st
    gc = GROUPS_PER_CHUNK
    n_chunk = n_grp // gc
    sw = gc * n_st
    to_lanes = lambda b: b.transpose(2, 0, 1).reshape(n_ch, n)
    lane = lambda rows: pl.BlockSpec((rows, sw), lambda j: (0, j))
    ptab, bb_re, bb_im = pl.pallas_call(
        _s5_params_kernel,
        grid=(n_chunk,),
        in_specs=[lane(1)] * 3 + [lane(n_ch)] * 2,
        out_specs=[pl.BlockSpec((2, seg_len, ROW_TILE, sw), lambda j: (0, 0, 0, j)), lane(n_ch), lane(n_ch)],
        out_shape=[jax.ShapeDtypeStruct((2, seg_len, ROW_TILE, n), F32),
                   jax.ShapeDtypeStruct((n_ch, n), F32), jax.ShapeDtypeStruct((n_ch, n), F32)],
        compiler_params=_params("parallel"),
        name="s5_params",
    )(lam_re.reshape(1, n), lam_im.reshape(1, n),
      jnp.broadcast_to(log_dt[:, None], (n_grp, n_st)).reshape(1, n), to_lanes(b_re), to_lanes(b_im))

    eye = jnp.eye(gc, dtype=F32)

    def b_blocks(bb):
        bb = bb.reshape(n_ch, n_chunk, gc, n_st)
        blk = jnp.einsum("cjgp,gh->jgchp", bb, eye)
        return blk.reshape(n_chunk, gc * n_ch, gc * n_st).astype(BF16)

    def c_blocks(cc):
        cc = cc.reshape(n_chunk, gc, n_ch, n_st)
        blk = jnp.einsum("jgcp,gh->jgphc", cc, eye)
        return blk.reshape(n_chunk, gc * n_st, gc * n_ch).astype(BF16)

    b_cat = jnp.concatenate([b_blocks(bb_re), b_blocks(bb_im)], axis=2)
    c_cat = jnp.concatenate([c_blocks(c_re), c_blocks(c_im)], axis=1)
    return ptab, b_cat, c_cat


def _cmul_add(xr, xi, ar, ai, br, bi):
    return xr + ar * br - ai * bi, xi + ar * bi + ai * br


def _s5_scan_kernel(u_ref, b_ref, ptab_ref, c_ref, d_ref, y_ref, hfin_ref, up_ref, s_ref, hb_ref, end_ref):
    tc, sw2 = s_ref.shape
    sw = sw2 // 2
    seg = tc // ROW_TILE
    re, im = slice(0, sw), slice(sw, sw2)

    @pl.when(pl.program_id(2) == 0)
    def _():
        end_ref[...] = jnp.zeros_like(end_ref)

    for i in range(seg):
        up_ref[i * ROW_TILE:(i + 1) * ROW_TILE, :] = u_ref[0, pl.ds(i, ROW_TILE, stride=seg), :]
    s_ref[...] = jnp.dot(up_ref[...].astype(BF16), b_ref[0], preferred_element_type=F32)

    ar, ai = ptab_ref[0, 0], ptab_ref[1, 0]

    def local_step(i, carry):
        hr, hi = carry
        r0 = pl.multiple_of(i * ROW_TILE, ROW_TILE)
        hr, hi = _cmul_add(s_ref[pl.ds(r0, ROW_TILE), re], s_ref[pl.ds(r0, ROW_TILE), im], ar, ai, hr, hi)
        s_ref[pl.ds(r0, ROW_TILE), re] = hr
        s_ref[pl.ds(r0, ROW_TILE), im] = hi
        return hr, hi

    zero = jnp.zeros((ROW_TILE, sw), F32)
    er, ei = lax.fori_loop(0, seg, local_step, (zero, zero), unroll=8)

    row = lax.broadcasted_iota(jnp.int32, (ROW_TILE, sw), 0)
    cr = jnp.where(row == 0, pltpu.roll(end_ref[0], 1, axis=0), pltpu.roll(er, 1, axis=0))
    ci = jnp.where(row == 0, pltpu.roll(end_ref[1], 1, axis=0), pltpu.roll(ei, 1, axis=0))
    mr, mi = ptab_ref[0, seg - 1], ptab_ref[1, seg - 1]
    wr, wi = mr, mi
    for sh in (1, 2, 4):
        gr, gi = jnp.where(row >= sh, wr, 0.0), jnp.where(row >= sh, wi, 0.0)
        cr, ci = _cmul_add(cr, ci, gr, gi, pltpu.roll(cr, sh, axis=0), pltpu.roll(ci, sh, axis=0))
        wr, wi = wr * wr - wi * wi, 2.0 * wr * wi
    fr, fi = _cmul_add(er, ei, mr, mi, cr, ci)
    end_ref[0] = fr
    end_ref[1] = fi
    hfin_ref[0, 0] = fr
    hfin_ref[0, 1] = fi

    def fix_up(k, carry):
        r0 = pl.multiple_of(k * 2 * ROW_TILE, 2 * ROW_TILE)
        hs, ns = [], []
        for half in range(2):
            i = 2 * k + half
            rows = pl.ds(r0 + half * ROW_TILE, ROW_TILE)
            hr, hi = _cmul_add(s_ref[rows, re], s_ref[rows, im], ptab_ref[0, i], ptab_ref[1, i], cr, ci)
            hs.append(hr)
            ns.append(-hi)
        hb_ref[pl.ds(r0, 2 * ROW_TILE), re] = jnp.concatenate(hs, axis=0).astype(BF16)
        hb_ref[pl.ds(r0, 2 * ROW_TILE), im] = jnp.concatenate(ns, axis=0).astype(BF16)
        return carry

    lax.fori_loop(0, seg // 2, fix_up, 0, unroll=4)
    yp = jnp.dot(hb_ref[...], c_ref[0], preferred_element_type=F32) + d_ref[...] * up_ref[...]
    for i in range(seg):
        y_ref[0, pl.ds(i, ROW_TILE, stride=seg), :] = yp[i * ROW_TILE:(i + 1) * ROW_TILE, :]


def _s5_chunk(seq):
    return min(seq, 1024)


def _s5_prompt(u, prm, ssm_d, nb, seq):
    ptab, b_cat, c_cat = prm
    n_chunk, cw, sw2 = b_cat.shape
    sw = sw2 // 2
    width = u.shape[1]
    tc = _s5_chunk(seq)
    seg = tc // ROW_TILE
    assert seq % tc == 0 and seg % 2 == 0 and ptab.shape[1] == seg
    y, hfin = pl.pallas_call(
        _s5_scan_kernel,
        grid=(nb, n_chunk, seq // tc),
        in_specs=[pl.BlockSpec((1, tc, cw), lambda b, j, t: (b, t, j)),
                  pl.BlockSpec((1, cw, sw2), lambda b, j, t: (j, 0, 0)),
                  pl.BlockSpec((2, seg, ROW_TILE, sw), lambda b, j, t: (0, 0, 0, j)),
                  pl.BlockSpec((1, sw2, cw), lambda b, j, t: (j, 0, 0)),
                  pl.BlockSpec((1, cw), lambda b, j, t: (0, j))],
        out_specs=[pl.BlockSpec((1, tc, cw), lambda b, j, t: (b, t, j)),
                   pl.BlockSpec((1, 2, ROW_TILE, sw), lambda b, j, t: (b, 0, 0, j))],
        out_shape=[jax.ShapeDtypeStruct((nb, seq, width), F32),
                   jax.ShapeDtypeStruct((nb, 2, ROW_TILE, n_chunk * sw), F32)],
        scratch_shapes=[pltpu.VMEM((tc, cw), F32), pltpu.VMEM((tc, sw2), F32),
                        pltpu.VMEM((tc, sw2), BF16), pltpu.VMEM((2, ROW_TILE, sw), F32)],
        compiler_params=_params("parallel", "parallel", "arbitrary"),
        name="s5_scan",
    )(u.reshape(nb, seq, width), b_cat, ptab, c_cat, ssm_d.reshape(1, width))
    last = ROW_TILE - 1
    return y.reshape(nb * seq, width), hfin[:, 0, last], hfin[:, 1, last]


def _s5_sample_kernel(u_ref, h0re_ref, h0im_ref, b_ref, ptab_ref, c_ref, d_ref, y_ref, hre_ref, him_ref):
    sw = h0re_ref.shape[1]
    ar, ai = ptab_ref[0, 0, 0:1, :], ptab_ref[1, 0, 0:1, :]
    hr, hi = h0re_ref[...], h0im_ref[...]
    for t in range(u_ref.shape[0]):
        u = u_ref[t]
        bu = jnp.dot(u.astype(BF16), b_ref[0], preferred_element_type=F32)
        hr, hi = _cmul_add(bu[:, :sw], bu[:, sw:], ar, ai, hr, hi)
        hb = jnp.concatenate([hr, -hi], axis=1).astype(BF16)
        y_ref[t] = jnp.dot(hb, c_ref[0], preferred_element_type=F32) + d_ref[...] * u
    hre_ref[...] = hr
    him_ref[...] = hi


def _s5_sample(u, h0re, h0im, prm, ssm_d):
    ptab, b_cat, c_cat = prm
    n_chunk, cw, sw2 = b_cat.shape
    sw = sw2 // 2
    n_new, nb, width = u.shape
    return pl.pallas_call(
        _s5_sample_kernel,
        grid=(n_chunk,),
        in_specs=[pl.BlockSpec((n_new, nb, cw), lambda j: (0, 0, j)),
                  pl.BlockSpec((nb, sw), lambda j: (0, j)),
                  pl.BlockSpec((nb, sw), lambda j: (0, j)),
                  pl.BlockSpec((1, cw, sw2), lambda j: (j, 0, 0)),
                  pl.BlockSpec((2, 1, ROW_TILE, sw), lambda j: (0, 0, 0, j)),
                  pl.BlockSpec((1, sw2, cw), lambda j: (j, 0, 0)),
                  pl.BlockSpec((1, cw), lambda j: (0, j))],
        out_specs=[pl.BlockSpec((n_new, nb, cw), lambda j: (0, 0, j)),
                   pl.BlockSpec((nb, sw), lambda j: (0, j)),
                   pl.BlockSpec((nb, sw), lambda j: (0, j))],
        out_shape=[jax.ShapeDtypeStruct((n_new, nb, width), F32),
                   jax.ShapeDtypeStruct((nb, n_chunk * sw), F32),
                   jax.ShapeDtypeStruct((nb, n_chunk * sw), F32)],
        compiler_params=_params("parallel"),
        name="s5_sample",
    )(u, h0re, h0im, b_cat, ptab, c_cat, ssm_d.reshape(1, width))


def _glu_kernel(y_ref, g_ref, w_ref, b_ref, o_ref):
    y = y_ref[...]
    s = 0.5 * y * (1.0 + jnp.tanh(math.sqrt(2.0 / math.pi) * (y + 0.044715 * (y * y * y))))
    z = jnp.dot(s.astype(BF16), w_ref[...], preferred_element_type=F32) + b_ref[...]
    o_ref[...] = (s * _sigmoid(z) * g_ref[...].astype(F32)).astype(o_ref.dtype)


def _glu(y, gate, w, b):
    m, width = y.shape
    tm = _row_tile(m, 512)
    row = pl.BlockSpec((tm, width), lambda i: (i, 0))
    return pl.pallas_call(
        _glu_kernel,
        grid=(m // tm,),
        in_specs=[row, row, pl.BlockSpec((width, width), lambda i: (0, 0)),
                  pl.BlockSpec((1, width), lambda i: (0, 0))],
        out_specs=row,
        out_shape=jax.ShapeDtypeStruct((m, width), BF16),
        compiler_params=_params("parallel"),
        name="glu",
    )(y, gate, w, b.reshape(1, width))


def _merge_kernel(*refs):
    ol_refs = refs[:2 * N_DIL]
    ga_ref, ys_ref, ma_ref, ms_ref, wa_ref, ws_ref, out_ref = refs[2 * N_DIL:2 * N_DIL + 7]
    a_ref, tok_ref = refs[2 * N_DIL + 7:]
    for h in range(a_ref.shape[1] // HEAD_DIM):
        sl = slice(h * HEAD_DIM, (h + 1) * HEAD_DIM)
        vals = []
        for idx, ref in enumerate(ol_refs):
            dil, per = ref.shape[1], ref.shape[2]
            if dil == 1:
                vals.append(ref[0, 0, :, sl])
            else:
                for r in range(dil):
                    tok_ref[idx, pl.ds(r, per, stride=dil), :] = ref[0, r, :, sl]
                vals.append(tok_ref[idx])
        o0, o1, o2, l0, l1, l2 = vals
        mx = jnp.maximum(jnp.maximum(l0, l1), l2)
        e0, e1, e2 = jnp.exp(l0 - mx), jnp.exp(l1 - mx), jnp.exp(l2 - mx)
        attn = (e0 * o0 + e1 * o1 + e2 * o2) / (e0 + e1 + e2)
        a_ref[:, sl] = (attn * ga_ref[:, sl].astype(F32)).astype(BF16)
    ya = jnp.dot(a_ref[...], wa_ref[...], preferred_element_type=F32)
    yb = jnp.dot(ys_ref[...], ws_ref[...], preferred_element_type=F32)
    out_ref[...] = (ma_ref[...].astype(F32) * ya + ms_ref[...].astype(F32) * yb).astype(out_ref.dtype)


def _merge(outs, lses, g_attn, y_ssm, gates, w_a, w_s):
    m, aw = g_attn.shape
    d = w_a.shape[1]
    nb = outs[0].shape[0]
    seq = m // nb
    tm = _row_tile(seq, 256)
    bps = seq // tm
    row = pl.BlockSpec((tm, aw), lambda i: (i, 0))
    ol_specs = []
    for arr in list(outs) + list(lses):
        dil = arr.shape[1]
        assert tm % (dil * ROW_TILE) == 0
        ol_specs.append(pl.BlockSpec((1, dil, tm // dil, aw),
                                     lambda i: (lax.div(i, bps), 0, lax.rem(i, bps), 0)))
    scratch = [pltpu.VMEM((tm, aw), BF16), pltpu.VMEM((2 * N_DIL, tm, HEAD_DIM), F32)]
    return pl.pallas_call(
        _merge_kernel,
        grid=(m // tm,),
        in_specs=ol_specs + [row, row, pl.BlockSpec((tm, d), lambda i: (i, 0)),
                             pl.BlockSpec((tm, d), lambda i: (i, 1)),
                             pl.BlockSpec((aw, d), lambda i: (0, 0)),
                             pl.BlockSpec((aw, d), lambda i: (0, 0))],
        out_specs=pl.BlockSpec((tm, d), lambda i: (i, 0)),
        out_shape=jax.ShapeDtypeStruct((m, d), BF16),
        scratch_shapes=scratch,
        compiler_params=_params("parallel"),
        name="merge",
    )(*outs, *lses, g_attn, y_ssm, gates, gates, w_a, w_s)


def _out_kernel(x_ref, m_ref, w_ref, o_ref):
    o_ref[...] = x_ref[...] + jnp.dot(m_ref[...], w_ref[...], preferred_element_type=F32)


def _out_proj(x, merged, w):
    m, d = x.shape
    tm = _row_tile(m, 1024)
    tn = min(d, 1024)
    return pl.pallas_call(
        _out_kernel,
        grid=(m // tm, d // tn),
        in_specs=[pl.BlockSpec((tm, tn), lambda i, j: (i, j)),
                  pl.BlockSpec((tm, d), lambda i, j: (i, 0)),
                  pl.BlockSpec((d, tn), lambda i, j: (0, j))],
        out_specs=pl.BlockSpec((tm, tn), lambda i, j: (i, j)),
        out_shape=jax.ShapeDtypeStruct((m, d), F32),
        compiler_params=_params("parallel", "arbitrary"),
        name="out_proj",
    )(x, merged, w)


def _kv_shift_kernel(*refs):
    n = N_DIL
    caches, news, outs = refs[:n], refs[n:2 * n], refs[2 * n:3 * n]
    ring, in_sem, out_sem = refs[3 * n:3 * n + 3]
    new_bufs, new_sem = refs[3 * n + 3:3 * n + 3 + n], refs[3 * n + 3 + n]

    new_in = [pltpu.make_async_copy(news[g], new_bufs[g], new_sem.at[g]) for g in range(n)]
    for cp in new_in:
        cp.start()

    jobs = []
    for g in range(n):
        buf, n_new = caches[g].shape[1], news[g].shape[1]
        keep = buf - n_new
        for i in range(caches[g].shape[0]):
            for r0 in range(0, keep, KV_COPY_ROWS):
                rows = min(KV_COPY_ROWS, keep - r0)
                jobs.append((caches[g].at[i, pl.ds(n_new + r0, rows)], outs[g].at[i, pl.ds(r0, rows)], rows))

    def load(j):
        src, _, rows = jobs[j]
        slot = j % KV_COPY_SLOTS
        return pltpu.make_async_copy(src, ring.at[slot, pl.ds(0, rows)], in_sem.at[slot])

    def store(j):
        _, dst, rows = jobs[j]
        slot = j % KV_COPY_SLOTS
        return pltpu.make_async_copy(ring.at[slot, pl.ds(0, rows)], dst, out_sem.at[slot])

    ahead = KV_COPY_SLOTS // 2
    for j in range(min(ahead, len(jobs))):
        load(j).start()
    for j in range(len(jobs)):
        load(j).wait()
        store(j).start()
        nxt = j + ahead
        if nxt < len(jobs):
            if nxt >= KV_COPY_SLOTS:
                store(nxt - KV_COPY_SLOTS).wait()
            load(nxt).start()
    for j in range(max(0, len(jobs) - KV_COPY_SLOTS), len(jobs)):
        store(j).wait()

    new_out = []
    for g in range(n):
        keep = caches[g].shape[1] - news[g].shape[1]
        new_in[g].wait()
        new_out.append(pltpu.make_async_copy(new_bufs[g], outs[g].at[:, pl.ds(keep, news[g].shape[1])],
                                             new_sem.at[n + g]))
        new_out[g].start()
    for cp in new_out:
        cp.wait()


def _kv_shift(caches, news):
    flat = lambda a: a.reshape(a.shape[0] * a.shape[1], a.shape[2], a.shape[3] * a.shape[4], a.shape[5])
    cf, nf = [flat(c) for c in caches], [flat(w) for w in news]
    for c, w in zip(cf, nf):
        assert c.shape[1] > w.shape[1]
    tile = cf[0].shape[2:]
    hbm = pl.BlockSpec(memory_space=pl.ANY)
    outs = pl.pallas_call(
        _kv_shift_kernel,
        in_specs=[hbm] * (2 * N_DIL),
        out_specs=[hbm] * N_DIL,
        out_shape=[jax.ShapeDtypeStruct(c.shape, c.dtype) for c in cf],
        scratch_shapes=[pltpu.VMEM((KV_COPY_SLOTS, KV_COPY_ROWS) + tile, F32),
                        pltpu.SemaphoreType.DMA((KV_COPY_SLOTS,)),
                        pltpu.SemaphoreType.DMA((KV_COPY_SLOTS,))]
                       + [pltpu.VMEM(w.shape, F32) for w in nf]
                       + [pltpu.SemaphoreType.DMA((2 * N_DIL,))],
        compiler_params=pltpu.CompilerParams(vmem_limit_bytes=VMEM_LIMIT),
        name="kv_shift",
    )(*cf, *nf)
    return [o.reshape(c.shape) for o, c in zip(outs, caches)]


def _layer(x, cos, sin, lw, nb, seq, sample):
    (norm_w, w_in, q_norm_w, k_norm_w, w_glu, b_glu, w_br_attn, w_br_ssm, w_out, ssm_d, s5prm) = lw
    m, d = x.shape
    aw = d // 2
    qkv = N_DIL * aw
    hg = aw // HEAD_DIM
    xn = _rmsnorm(x, norm_w)
    g_attn = _proj(xn, w_in, 3 * qkv, aw, "silu", BF16)
    u = _proj(xn, w_in, 3 * qkv + aw, aw, "plain", F32)
    g_ssm = _proj(xn, w_in, 3 * qkv + 2 * aw, aw, "silu", BF16)
    gates = _proj(xn, w_in, 3 * qkv + 3 * aw, 2 * d, "sigmoid", BF16)

    new_kv = []
    if sample is None:
        outs, lses = [], []
        for g in range(N_DIL):
            qd, kd, vd, kv = _qkv(xn, w_in, g, nb, seq, q_norm_w, k_norm_w, cos, sin)
            o, lse = _attn_prompt(qd, kd, vd)
            outs.append(o)
            lses.append(lse)
            new_kv.append(kv)
        y, hre, him = _s5_prompt(u, s5prm, ssm_d, nb, seq)
    else:
        caches, layer, h0re, h0im = sample
        q = _proj(xn, w_in, 0, qkv, "qk", F32, q_norm_w, cos, sin)
        k = _proj(xn, w_in, qkv, qkv, "qk", F32, k_norm_w, cos, sin)
        v = _proj(xn, w_in, 2 * qkv, qkv, "plain", F32)
        tok = (nb, seq, N_DIL * hg, HEAD_DIM)
        o, lse = _attn_sample(q.reshape(tok), k.reshape(tok), v.reshape(tok), caches, layer)
        o, lse = o.reshape(m, qkv), lse.reshape(m, qkv)
        outs = [o[:, g * aw:(g + 1) * aw].reshape(1, 1, m, aw) for g in range(N_DIL)]
        lses = [lse[:, g * aw:(g + 1) * aw].reshape(1, 1, m, aw) for g in range(N_DIL)]
        ut = u.reshape(nb, seq, aw).transpose(1, 0, 2)
        yt, hre, him = _s5_sample(ut, h0re, h0im, s5prm, ssm_d)
        y = yt.transpose(1, 0, 2).reshape(m, aw)
        k5, v5 = k.reshape(nb, seq, N_DIL, hg, HEAD_DIM), v.reshape(nb, seq, N_DIL, hg, HEAD_DIM)
        new_kv = [jnp.stack([k5[:, :, g], v5[:, :, g]], axis=2) for g in range(N_DIL)]

    y_ssm = _glu(y, g_ssm, w_glu, b_glu)
    merged = _merge(outs, lses, g_attn, y_ssm, gates, w_br_attn, w_br_ssm)
    x_out = _out_proj(x, merged, w_out)

    n_grp = aw // SSM_GROUP_CH
    state = jnp.stack([hre, him], axis=-1).reshape(nb, n_grp, SSM_STATE, 2)
    return x_out, new_kv, state


def kernel(x_prompt, x_sample, cache_kv_d1, cache_kv_d4, cache_kv_d16, state_ssm, norm_w, w_in, q_norm_w, k_norm_w, ssm_lambda_re, ssm_lambda_im, ssm_log_dt, ssm_b_re, ssm_b_im, ssm_c_re, ssm_c_im, ssm_d, w_glu, b_glu, w_br_attn, w_br_ssm, w_out):
    nb_p, seq_p, d = x_prompt.shape
    nb_s, seq_s, _ = x_sample.shape
    depth = norm_w.shape[0]
    caches = (cache_kv_d1, cache_kv_d4, cache_kv_d16)

    cos_p, sin_p = _rope_tables(jnp.arange(seq_p, dtype=F32))
    cos_s, sin_s = _rope_tables(PAST_LEN + jnp.arange(seq_s, dtype=F32))
    cos_s, sin_s = jnp.tile(cos_s, (nb_s, 1)), jnp.tile(sin_s, (nb_s, 1))

    hp = x_prompt.reshape(nb_p * seq_p, d)
    hs = x_sample.reshape(nb_s * seq_s, d)
    kv_p = [[] for _ in range(N_DIL)]
    kv_s_new = [[] for _ in range(N_DIL)]
    ssm_p, ssm_s = [], []
    for l in range(depth):
        s5prm = _s5_params(ssm_lambda_re[l], ssm_lambda_im[l], ssm_log_dt[l], ssm_b_re[l], ssm_b_im[l],
                           ssm_c_re[l], ssm_c_im[l], _s5_chunk(seq_p) // ROW_TILE)
        lw = (norm_w[l], w_in[l].astype(BF16), q_norm_w[l], k_norm_w[l], w_glu[l].astype(BF16), b_glu[l],
              w_br_attn[l].astype(BF16), w_br_ssm[l].astype(BF16), w_out[l].astype(BF16), ssm_d[l], s5prm)
        hp, new_kv, st = _layer(hp, cos_p, sin_p, lw, nb_p, seq_p, None)
        for g in range(N_DIL):
            kv_p[g].append(new_kv[g])
        ssm_p.append(st)

        n_state = state_ssm.shape[2] * state_ssm.shape[3]
        h0re = state_ssm[l, ..., 0].reshape(nb_s, n_state)
        h0im = state_ssm[l, ..., 1].reshape(nb_s, n_state)
        hs, new_kv_s, st_s = _layer(hs, cos_s, sin_s, lw, nb_s, seq_s, (caches, l, h0re, h0im))
        for g in range(N_DIL):
            kv_s_new[g].append(new_kv_s[g])
        ssm_s.append(st_s)

    kv_s = _kv_shift(caches, [jnp.stack(n) for n in kv_s_new])
    return (hp.reshape(x_prompt.shape), hs.reshape(x_sample.shape),
            jnp.stack(kv_p[0]), jnp.stack(kv_p[1]), jnp.stack(kv_p[2]), jnp.stack(ssm_p),
            kv_s[0], kv_s[1], kv_s[2], jnp.stack(ssm_s))
```

```python
import functools
import math

import jax
import jax.numpy as jnp
from jax import lax
from jax.experimental import pallas as pl
from jax.experimental.pallas import tpu as pltpu

F32 = jnp.float32
BF16 = jnp.bfloat16

HEAD_DIM = 128
WINDOWS = (128, 512, 2048)
DILATIONS = (1, 4, 16)
N_DIL = 3
SPAN = 128
SSM_GROUP_CH = 16
SSM_STATE = 64
GROUPS_PER_CHUNK = 8
ROPE_THETA = 10000.0
NORM_EPS = 1e-6
PAST_LEN = 16384
NEG_BIG = -1e30
VMEM_LIMIT = 56 * 1024 * 1024
ROW_TILE = 8
QK_ROW_CHUNK = 64
KV_COPY_ROWS = 512
KV_COPY_SLOTS = 4


def _params(*sem):
    return pltpu.CompilerParams(dimension_semantics=sem, vmem_limit_bytes=VMEM_LIMIT)


def _sigmoid(x):
    return 1.0 / (1.0 + jnp.exp(-x))


def _row_tile(m, cap):
    t = min(m, cap)
    assert m % t == 0, (m, t)
    return t


def _rmsnorm_kernel(x_ref, w_ref, o_ref):
    x = x_ref[...]
    y = x * lax.rsqrt(jnp.mean(x * x, axis=-1, keepdims=True) + NORM_EPS)
    o_ref[...] = (y * w_ref[...]).astype(o_ref.dtype)


def _rmsnorm(x, w):
    m, d = x.shape
    tm = _row_tile(m, 512)
    return pl.pallas_call(
        _rmsnorm_kernel,
        grid=(m // tm,),
        in_specs=[pl.BlockSpec((tm, d), lambda i: (i, 0)),
                  pl.BlockSpec((1, d), lambda i: (0, 0))],
        out_specs=pl.BlockSpec((tm, d), lambda i: (i, 0)),
        out_shape=jax.ShapeDtypeStruct((m, d), BF16),
        compiler_params=_params("parallel"),
        name="rmsnorm",
    )(x, w.reshape(1, d))


def _cast_kernel(w_ref, o_ref):
    o_ref[...] = w_ref[...].astype(o_ref.dtype)


def _layer_bf16(w, layer):
    _, rows, cols = w.shape
    tr, tc = min(rows, 512), min(cols, 2048)
    assert rows % tr == 0 and cols % tc == 0
    return pl.pallas_call(
        _cast_kernel,
        grid=(rows // tr, cols // tc),
        in_specs=[pl.BlockSpec((None, tr, tc), lambda i, j: (layer, i, j))],
        out_specs=pl.BlockSpec((tr, tc), lambda i, j: (i, j)),
        out_shape=jax.ShapeDtypeStruct((rows, cols), BF16),
        compiler_params=_params("parallel", "parallel"),
        name="cast_bf16",
    )(w)


def _norm_rope_inplace(t_ref, nw_ref, cos_ref, sin_ref, head_major=False):
    rows = cos_ref.shape[0]
    n_heads = t_ref.shape[0] // rows if head_major else t_ref.shape[1] // HEAD_DIM
    rc = min(rows, QK_ROW_CHUNK)
    half = HEAD_DIM // 2
    nw = nw_ref[...]
    nw_swapped = pltpu.roll(nw, half, axis=1)

    def chunk(c, carry):
        r0 = pl.multiple_of(c * rc, rc)
        cs = cos_ref[pl.ds(r0, rc), :] * nw
        ss = sin_ref[pl.ds(r0, rc), :] * nw_swapped
        for h in range(n_heads):
            idx = (pl.ds(h * rows + r0, rc), slice(None)) if head_major else (
                pl.ds(r0, rc), slice(h * HEAD_DIM, (h + 1) * HEAD_DIM))
            a = t_ref[idx]
            r = lax.rsqrt(jnp.mean(a * a, axis=-1, keepdims=True) + NORM_EPS)
            t_ref[idx] = (a * cs + pltpu.roll(a, half, axis=1) * ss) * r
        return carry

    lax.fori_loop(0, rows // rc, chunk, 0)


def _proj_kernel(x_ref, w_ref, *rest, epilogue):
    acc = jnp.dot(x_ref[...], w_ref[...], preferred_element_type=F32)
    if epilogue == "qk":
        nw_ref, cos_ref, sin_ref, o_ref = rest
        o_ref[...] = acc
        _norm_rope_inplace(o_ref, nw_ref, cos_ref, sin_ref)
    else:
        (o_ref,) = rest
        if epilogue == "silu":
            acc = acc * _sigmoid(acc)
        elif epilogue == "sigmoid":
            acc = _sigmoid(acc)
        o_ref[...] = acc.astype(o_ref.dtype)


def _norm_rope_heads(hm_ref, ssq_ref, nw_ref, cos_ref, sin_ref):
    rows = cos_ref.shape[0]
    rc = min(rows, QK_ROW_CHUNK)
    half = HEAD_DIM // 2
    nw = nw_ref[...]
    nw_swapped = pltpu.roll(nw, half, axis=1)
    ones = jnp.ones((HEAD_DIM, HEAD_DIM), BF16)
    for h in range(hm_ref.shape[0] // rows):
        a_all = hm_ref[h * rows:(h + 1) * rows, :]
        ssq_ref[...] = jnp.dot((a_all * a_all).astype(BF16), ones, preferred_element_type=F32)

        def chunk(c, carry):
            r0 = pl.multiple_of(c * rc, rc)
            cs = cos_ref[pl.ds(r0, rc), :] * nw
            ss = sin_ref[pl.ds(r0, rc), :] * nw_swapped
            a = hm_ref[pl.ds(h * rows + r0, rc), :]
            r = lax.rsqrt(ssq_ref[pl.ds(r0, rc), :] * (1.0 / HEAD_DIM) + NORM_EPS)
            hm_ref[pl.ds(h * rows + r0, rc), :] = (a * cs + pltpu.roll(a, half, axis=1) * ss) * r
            return carry

        lax.fori_loop(0, rows // rc, chunk, 0, unroll=True)


def _project_dilated(x, w_ref, nw_ref, cos_ref, sin_ref, hm_ref, ssq_ref, d_ref, dil):
    tm = x.shape[0]
    n_heads = w_ref.shape[1] // HEAD_DIM
    per = tm // dil
    acc = jnp.dot(x, w_ref[...], preferred_element_type=F32)
    for h in range(n_heads):
        hm_ref[h * tm:(h + 1) * tm, :] = acc[:, h * HEAD_DIM:(h + 1) * HEAD_DIM]
    if nw_ref is not None:
        _norm_rope_heads(hm_ref, ssq_ref, nw_ref, cos_ref, sin_ref)
    for h in range(n_heads):
        for r in range(dil):
            rows = pl.ds(h * tm + r, per, stride=dil) if dil > 1 else pl.ds(h * tm, per)
            d_ref[0, r, :, h * HEAD_DIM:(h + 1) * HEAD_DIM] = hm_ref[rows, :].astype(d_ref.dtype)


def _q_kernel(x_ref, w_ref, nw_ref, cos_ref, sin_ref, qd_ref, hm_ref, ssq_ref, *, dil):
    _project_dilated(x_ref[...], w_ref, nw_ref, cos_ref, sin_ref, hm_ref, ssq_ref, qd_ref, dil)


def _kv_kernel(x_ref, wk_ref, wv_ref, nw_ref, cos_ref, sin_ref, kd_ref, vd_ref, kv_ref, hk_ref, hv_ref, ssq_ref,
               *, dil, first_kept):
    half_idx = pl.program_id(2)
    tm = x_ref.shape[0]
    n_heads = wk_ref.shape[1] // HEAD_DIM

    @pl.when(half_idx == 0)
    def _():
        x = x_ref[...]
        _project_dilated(x, wk_ref, nw_ref, cos_ref, sin_ref, hk_ref, ssq_ref, kd_ref, dil)
        _project_dilated(x, wv_ref, None, None, None, hv_ref, None, vd_ref, dil)

    kb = kv_ref.shape[1]
    half = tm // 2

    @pl.when(2 * pl.program_id(1) + half_idx >= first_kept)
    def _():
        base = (half_idx + 1) * half - kb

        def token(t, carry):
            rows = pl.ds(base + t, n_heads, stride=tm)
            kv_ref[0, t, 0] = hk_ref[rows, :]
            kv_ref[0, t, 1] = hv_ref[rows, :]
            return carry

        lax.fori_loop(0, kb, token, 0, unroll=8)


def _qkv(xn, w, g, nb, seq, q_norm_w, k_norm_w, cos, sin):
    m, d = xn.shape
    aw = d // 2
    hg = aw // HEAD_DIM
    dil = DILATIONS[g]
    keep = min(WINDOWS[g], seq)
    tm = _row_tile(seq, 1024)
    per = tm // dil
    half = tm // 2
    kb = min(keep, half)
    bps = seq // tm
    n_half = 2 * bps
    first_kept = (seq - keep) // half if keep >= half else n_half - 1
    assert per % 16 == 0 and keep % kb == 0 and (seq - keep) % kb == 0
    once = pl.Buffered(1)
    nw = lambda a: a.reshape(1, HEAD_DIM)
    dshape = jax.ShapeDtypeStruct((nb, dil, seq // dil, aw), BF16)
    staging = pltpu.VMEM((hg * tm, HEAD_DIM), F32)
    sumsq = pltpu.VMEM((tm, HEAD_DIM), F32)

    qd = pl.pallas_call(
        functools.partial(_q_kernel, dil=dil),
        grid=(nb, bps),
        in_specs=[pl.BlockSpec((tm, d), lambda b, i: (b * bps + i, 0)),
                  pl.BlockSpec((d, aw), lambda b, i: (0, g), pipeline_mode=once),
                  pl.BlockSpec((1, HEAD_DIM), lambda b, i: (0, 0)),
                  pl.BlockSpec((tm, HEAD_DIM), lambda b, i: (i, 0)),
                  pl.BlockSpec((tm, HEAD_DIM), lambda b, i: (i, 0))],
        out_specs=pl.BlockSpec((1, dil, per, aw), lambda b, i: (b, 0, i, 0)),
        out_shape=dshape,
        scratch_shapes=[staging, sumsq],
        compiler_params=_params("parallel", "parallel"),
        name="q_d%d" % dil,
    )(xn, w, nw(q_norm_w), cos, sin)

    wspec = lambda col: pl.BlockSpec((d, aw), lambda b, i, h: (0, col), pipeline_mode=once)
    tspec = pl.BlockSpec((tm, HEAD_DIM), lambda b, i, h: (i, 0))
    dspec = pl.BlockSpec((1, dil, per, aw), lambda b, i, h: (b, 0, i, 0))
    kd, vd, kv = pl.pallas_call(
        functools.partial(_kv_kernel, dil=dil, first_kept=first_kept),
        grid=(nb, bps, 2),
        in_specs=[pl.BlockSpec((tm, d), lambda b, i, h: (b * bps + i, 0)),
                  wspec(N_DIL + g), wspec(2 * N_DIL + g),
                  pl.BlockSpec((1, HEAD_DIM), lambda b, i, h: (0, 0)), tspec, tspec],
        out_specs=[dspec, dspec,
                   pl.BlockSpec((1, kb, 2, hg, HEAD_DIM),
                                lambda b, i, h: (b, jnp.maximum(2 * i + h - first_kept, 0), 0, 0, 0))],
        out_shape=[dshape, dshape, jax.ShapeDtypeStruct((nb, keep, 2, hg, HEAD_DIM), F32)],
        scratch_shapes=[staging, staging, sumsq],
        compiler_params=_params("parallel", "arbitrary", "arbitrary"),
        name="kv_d%d" % dil,
    )(xn, w, w, nw(k_norm_w), cos, sin)
    return qd, kd, vd, kv


def _proj(xn, w, col0, ncols, epilogue, out_dtype, norm_w=None, cos=None, sin=None):
    m, d = xn.shape
    tm = _row_tile(m, 1024)
    tn = min(ncols, 1024)
    assert ncols % tn == 0 and col0 % tn == 0
    jb = col0 // tn
    in_specs = [pl.BlockSpec((tm, d), lambda i, j: (i, 0)),
                pl.BlockSpec((d, tn), lambda i, j: (0, jb + j))]
    args = [xn, w]
    if epilogue == "qk":
        cyc = cos.shape[0] // tm
        in_specs += [pl.BlockSpec((1, HEAD_DIM), lambda i, j: (0, 0)),
                     pl.BlockSpec((tm, HEAD_DIM), lambda i, j: (i % cyc, 0)),
                     pl.BlockSpec((tm, HEAD_DIM), lambda i, j: (i % cyc, 0))]
        args += [norm_w.reshape(1, HEAD_DIM), cos, sin]
    return pl.pallas_call(
        functools.partial(_proj_kernel, epilogue=epilogue),
        grid=(m // tm, ncols // tn),
        in_specs=in_specs,
        out_specs=pl.BlockSpec((tm, tn), lambda i, j: (i, j)),
        out_shape=jax.ShapeDtypeStruct((m, ncols), out_dtype),
        compiler_params=_params("parallel", "arbitrary"),
        name="proj_" + epilogue,
    )(*args)


def _rope_tables(pos):
    half = HEAD_DIM // 2
    inv_freq = jnp.power(ROPE_THETA, -jnp.arange(half, dtype=F32) * (2.0 / HEAD_DIM))
    ang = pos[:, None] * inv_freq[None, :]
    cos, sin = jnp.cos(ang), jnp.sin(ang)
    return jnp.concatenate([cos, cos], axis=-1), jnp.concatenate([-sin, sin], axis=-1)


def _attn_prompt_kernel(q_ref, kp_ref, kc_ref, vp_ref, vc_ref, o_ref, lse_ref):
    not_first = pl.program_id(2) > 0
    a = lax.broadcasted_iota(jnp.int32, (SPAN, 2 * SPAN), 0)
    c = lax.broadcasted_iota(jnp.int32, (SPAN, 2 * SPAN), 1)
    band = (c >= a) & (c <= a + SPAN)
    band_first = band & ((c >= SPAN) | not_first)
    lane = lax.broadcasted_iota(jnp.int32, (SPAN, HEAD_DIM), 1)
    scale = HEAD_DIM ** -0.5
    for j in range(q_ref.shape[2] // SPAN):
        rows = slice(j * SPAN, (j + 1) * SPAN)
        lse_tile = jnp.zeros((SPAN, HEAD_DIM), F32)
        for h in range(q_ref.shape[3] // HEAD_DIM):
            sl = slice(h * HEAD_DIM, (h + 1) * HEAD_DIM)
            q = q_ref[0, 0, rows, sl]
            if j == 0:
                k = jnp.concatenate([kp_ref[0, 0, :, sl], kc_ref[0, 0, rows, sl]], axis=0)
                v = jnp.concatenate([vp_ref[0, 0, :, sl], vc_ref[0, 0, rows, sl]], axis=0)
            else:
                k = kc_ref[0, 0, (j - 1) * SPAN:(j + 1) * SPAN, sl]
                v = vc_ref[0, 0, (j - 1) * SPAN:(j + 1) * SPAN, sl]
            s = lax.dot_general(q, k, (((1,), (1,)), ((), ())), preferred_element_type=F32) * scale
            s = jnp.where(band_first if j == 0 else band, s, NEG_BIG)
            m = jnp.max(s, axis=-1, keepdims=True)
            p = jnp.exp(s - m)
            l = jnp.sum(p, axis=-1, keepdims=True)
            o_ref[0, 0, rows, sl] = jnp.dot(p.astype(BF16), v, preferred_element_type=F32) / l
            lse_tile = jnp.where(lane == h, m + jnp.log(l), lse_tile)
        lse_ref[0, 0, rows, :] = lse_tile


def _attn_prompt(q, k, v):
    nb, dil, length, aw = q.shape
    n_blk = length // SPAN
    qb = min(4, n_blk)
    assert length % SPAN == 0 and n_blk % qb == 0 and aw // HEAD_DIM <= HEAD_DIM
    cur = pl.BlockSpec((1, 1, qb * SPAN, aw), lambda b, r, n: (b, r, n, 0))
    prev = pl.BlockSpec((1, 1, SPAN, aw), lambda b, r, n: (b, r, jnp.maximum(qb * n - 1, 0), 0))
    return pl.pallas_call(
        _attn_prompt_kernel,
        grid=(nb, dil, n_blk // qb),
        in_specs=[cur, prev, cur, prev, cur],
        out_specs=[cur, pl.BlockSpec((1, 1, qb * SPAN, HEAD_DIM), lambda b, r, n: (b, r, n, 0))],
        out_shape=[jax.ShapeDtypeStruct(q.shape, F32),
                   jax.ShapeDtypeStruct((nb, dil, length, HEAD_DIM), F32)],
        compiler_params=_params("parallel", "parallel", "arbitrary"),
        name="attn_prompt_d%d" % dil,
    )(q, k, k, v, v)


def _attn_sample_kernel(q_ref, kn_ref, vn_ref, c1_ref, c4_ref, c16_ref, o_ref, lse_ref):
    n_new, n_heads, _ = q_ref.shape
    hg = n_heads // N_DIL
    scale = HEAD_DIM ** -0.5
    caches = (c1_ref, c4_ref, c16_ref)
    for g in range(N_DIL):
        hs = slice(g * hg, (g + 1) * hg)
        for t in range(n_new):
            q = q_ref[t, hs, :]
            if DILATIONS[g] == 1:
                kc, vc = caches[g][:, 0], caches[g][:, 1]
                rows = lax.broadcasted_iota(jnp.int32, (kc.shape[0], hg, 1), 0)
                s = jnp.sum(kc * q[None], axis=-1, keepdims=True) * scale
                s = jnp.where(rows >= t, s, NEG_BIG)
                new = range(t + 1)
            else:
                kc, vc = caches[g][:, t, 0], caches[g][:, t, 1]
                s = jnp.sum(kc * q[None], axis=-1, keepdims=True) * scale
                new = (t,)
            s_new = [jnp.sum(kn_ref[u, hs, :] * q, axis=-1, keepdims=True) * scale for u in new]
            m = jnp.max(s, axis=0)
            for sn in s_new:
                m = jnp.maximum(m, sn)
            p = jnp.exp(s - m[None])
            l = jnp.sum(p, axis=0)
            acc = jnp.sum(p * vc, axis=0)
            for u, sn in zip(new, s_new):
                pn = jnp.exp(sn - m)
                l = l + pn
                acc = acc + pn * vn_ref[u, hs, :]
            o_ref[t, hs, :] = acc / l
            lse_ref[t, hs, :] = jnp.broadcast_to(m + jnp.log(l), (hg, HEAD_DIM))


def _attn_sample(q, kn, vn, caches, layer):
    nb, n_new, n_heads, _ = q.shape
    hg = n_heads // N_DIL
    views, specs = [], []
    for g in range(N_DIL):
        dil = DILATIONS[g]
        depth, _, buf = caches[g].shape[:3]
        assert buf == WINDOWS[g] and (dil == 1 or dil % n_new == 0)
        if dil == 1:
            views.append(caches[g])
            specs.append(pl.BlockSpec((None, None, buf, 2, hg, HEAD_DIM),
                                      lambda b: (layer, b, 0, 0, 0, 0)))
        else:
            views.append(caches[g].reshape(depth, nb, SPAN, dil, 2, hg, HEAD_DIM))
            specs.append(pl.BlockSpec((None, None, SPAN, n_new, 2, hg, HEAD_DIM),
                                      lambda b: (layer, b, 0, 0, 0, 0, 0)))
    tok = pl.BlockSpec((None, n_new, n_heads, HEAD_DIM), lambda b: (b, 0, 0, 0))
    return pl.pallas_call(
        _attn_sample_kernel,
        grid=(nb,),
        in_specs=[tok, tok, tok] + specs,
        out_specs=[tok, tok],
        out_shape=[jax.ShapeDtypeStruct(q.shape, F32)] * 2,
        compiler_params=_params("parallel"),
        name="attn_sample",
    )(q, kn, vn, *views)


def _s5_params_kernel(lre_ref, lim_ref, ldt_ref, bre_ref, bim_ref, ptab_ref, bbre_ref, bbim_ref):
    dt = jnp.exp(ldt_ref[...])
    lre, lim = lre_ref[...], lim_ref[...]
    xr, xi = lre * dt, lim * dt
    mag = jnp.exp(xr)
    ar, ai = mag * jnp.cos(xi), mag * jnp.sin(xi)
    shape = (ROW_TILE, xr.shape[1])
    abr, abi = jnp.broadcast_to(ar, shape), jnp.broadcast_to(ai, shape)
    ptab_ref[0, 0] = abr
    ptab_ref[1, 0] = abi

    def next_power(i, carry):
        pr, pi = carry
        pr, pi = pr * abr - pi * abi, pr * abi + pi * abr
        ptab_ref[0, i] = pr
        ptab_ref[1, i] = pi
        return pr, pi

    lax.fori_loop(1, ptab_ref.shape[1], next_power, (abr, abi))
    nr, ni = ar - 1.0, ai
    den = lre * lre + lim * lim
    fr = (nr * lre + ni * lim) / den
    fi = (ni * lre - nr * lim) / den
    bre, bim = bre_ref[...], bim_ref[...]
    bbre_ref[...] = fr * bre - fi * bim
    bbim_ref[...] = fr * bim + fi * bre


def _s5_params(lam_re, lam_im, log_dt, b_re, b_im, c_re, c_im, seg_len):
    n_grp, n_st = lam_re.shape
    n_ch = b_re.shape[2]
    n = n_grp * n_st
    gc = GROUPS_PER_CHUNK
    n_chunk = n_grp // gc
    sw = gc * n_st
    to_lanes = lambda b: b.transpose(2, 0, 1).reshape(n_ch, n)
    lane = lambda rows: pl.BlockSpec((rows, sw), lambda j: (0, j))
    ptab, bb_re, bb_im = pl.pallas_call(
        _s5_params_kernel,
        grid=(n_chunk,),
        in_specs=[lane(1)] * 3 + [lane(n_ch)] * 2,
        out_specs=[pl.BlockSpec((2, seg_len, ROW_TILE, sw), lambda j: (0, 0, 0, j)), lane(n_ch), lane(n_ch)],
        out_shape=[jax.ShapeDtypeStruct((2, seg_len, ROW_TILE, n), F32),
                   jax.ShapeDtypeStruct((n_ch, n), F32), jax.ShapeDtypeStruct((n_ch, n), F32)],
        compiler_params=_params("parallel"),
        name="s5_params",
    )(lam_re.reshape(1, n), lam_im.reshape(1, n),
      jnp.broadcast_to(log_dt[:, None], (n_grp, n_st)).reshape(1, n), to_lanes(b_re), to_lanes(b_im))

    eye = jnp.eye(gc, dtype=F32)

    def b_blocks(bb):
        bb = bb.reshape(n_ch, n_chunk, gc, n_st)
        blk = jnp.einsum("cjgp,gh->jgchp", bb, eye)
        return blk.reshape(n_chunk, gc * n_ch, gc * n_st).astype(BF16)

    def c_blocks(cc):
        cc = cc.reshape(n_chunk, gc, n_ch, n_st)
        blk = jnp.einsum("jgcp,gh->jgphc", cc, eye)
        return blk.reshape(n_chunk, gc * n_st, gc * n_ch).astype(BF16)

    b_cat = jnp.concatenate([b_blocks(bb_re), b_blocks(bb_im)], axis=2)
    c_cat = jnp.concatenate([c_blocks(c_re), c_blocks(c_im)], axis=1)
    return ptab, b_cat, c_cat


def _cmul_add(xr, xi, ar, ai, br, bi):
    return xr + ar * br - ai * bi, xi + ar * bi + ai * br


def _s5_scan_kernel(u_ref, b_ref, ptab_ref, c_ref, d_ref, y_ref, hfin_ref, up_ref, s_ref, hb_ref, end_ref):
    tc, sw2 = s_ref.shape
    sw = sw2 // 2
    seg = tc // ROW_TILE
    re, im = slice(0, sw), slice(sw, sw2)

    @pl.when(pl.program_id(2) == 0)
    def _():
        end_ref[...] = jnp.zeros_like(end_ref)

    for i in range(seg):
        up_ref[i * ROW_TILE:(i + 1) * ROW_TILE, :] = u_ref[0, pl.ds(i, ROW_TILE, stride=seg), :]
    s_ref[...] = jnp.dot(up_ref[...].astype(BF16), b_ref[0], preferred_element_type=F32)

    ar, ai = ptab_ref[0, 0], ptab_ref[1, 0]

    def local_step(i, carry):
        hr, hi = carry
        r0 = pl.multiple_of(i * ROW_TILE, ROW_TILE)
        hr, hi = _cmul_add(s_ref[pl.ds(r0, ROW_TILE), re], s_ref[pl.ds(r0, ROW_TILE), im], ar, ai, hr, hi)
        s_ref[pl.ds(r0, ROW_TILE), re] = hr
        s_ref[pl.ds(r0, ROW_TILE), im] = hi
        return hr, hi

    zero = jnp.zeros((ROW_TILE, sw), F32)
    er, ei = lax.fori_loop(0, seg, local_step, (zero, zero), unroll=8)

    row = lax.broadcasted_iota(jnp.int32, (ROW_TILE, sw), 0)
    cr = jnp.where(row == 0, pltpu.roll(end_ref[0], 1, axis=0), pltpu.roll(er, 1, axis=0))
    ci = jnp.where(row == 0, pltpu.roll(end_ref[1], 1, axis=0), pltpu.roll(ei, 1, axis=0))
    mr, mi = ptab_ref[0, seg - 1], ptab_ref[1, seg - 1]
    wr, wi = mr, mi
    for sh in (1, 2, 4):
        gr, gi = jnp.where(row >= sh, wr, 0.0), jnp.where(row >= sh, wi, 0.0)
        cr, ci = _cmul_add(cr, ci, gr, gi, pltpu.roll(cr, sh, axis=0), pltpu.roll(ci, sh, axis=0))
        wr, wi = wr * wr - wi * wi, 2.0 * wr * wi
    fr, fi = _cmul_add(er, ei, mr, mi, cr, ci)
    end_ref[0] = fr
    end_ref[1] = fi
    hfin_ref[0, 0] = fr
    hfin_ref[0, 1] = fi

    def fix_up(k, carry):
        r0 = pl.multiple_of(k * 2 * ROW_TILE, 2 * ROW_TILE)
        hs, ns = [], []
        for half in range(2):
            i = 2 * k + half
            rows = pl.ds(r0 + half * ROW_TILE, ROW_TILE)
            hr, hi = _cmul_add(s_ref[rows, re], s_ref[rows, im], ptab_ref[0, i], ptab_ref[1, i], cr, ci)
            hs.append(hr)
            ns.append(-hi)
        hb_ref[pl.ds(r0, 2 * ROW_TILE), re] = jnp.concatenate(hs, axis=0).astype(BF16)
        hb_ref[pl.ds(r0, 2 * ROW_TILE), im] = jnp.concatenate(ns, axis=0).astype(BF16)
        return carry

    lax.fori_loop(0, seg // 2, fix_up, 0, unroll=4)
    yp = jnp.dot(hb_ref[...], c_ref[0], preferred_element_type=F32) + d_ref[...] * up_ref[...]
    for i in range(seg):
        y_ref[0, pl.ds(i, ROW_TILE, stride=seg), :] = yp[i * ROW_TILE:(i + 1) * ROW_TILE, :]


def _s5_chunk(seq):
    return min(seq, 1024)


def _s5_prompt(u, prm, ssm_d, nb, seq):
    ptab, b_cat, c_cat = prm
    n_chunk, cw, sw2 = b_cat.shape
    sw = sw2 // 2
    width = u.shape[1]
    tc = _s5_chunk(seq)
    seg = tc // ROW_TILE
    assert seq % tc == 0 and seg % 2 == 0 and ptab.shape[1] == seg
    y, hfin = pl.pallas_call(
        _s5_scan_kernel,
        grid=(nb, n_chunk, seq // tc),
        in_specs=[pl.BlockSpec((1, tc, cw), lambda b, j, t: (b, t, j)),
                  pl.BlockSpec((1, cw, sw2), lambda b, j, t: (j, 0, 0)),
                  pl.BlockSpec((2, seg, ROW_TILE, sw), lambda b, j, t: (0, 0, 0, j)),
                  pl.BlockSpec((1, sw2, cw), lambda b, j, t: (j, 0, 0)),
                  pl.BlockSpec((1, cw), lambda b, j, t: (0, j))],
        out_specs=[pl.BlockSpec((1, tc, cw), lambda b, j, t: (b, t, j)),
                   pl.BlockSpec((1, 2, ROW_TILE, sw), lambda b, j, t: (b, 0, 0, j))],
        out_shape=[jax.ShapeDtypeStruct((nb, seq, width), F32),
                   jax.ShapeDtypeStruct((nb, 2, ROW_TILE, n_chunk * sw), F32)],
        scratch_shapes=[pltpu.VMEM((tc, cw), F32), pltpu.VMEM((tc, sw2), F32),
                        pltpu.VMEM((tc, sw2), BF16), pltpu.VMEM((2, ROW_TILE, sw), F32)],
        compiler_params=_params("parallel", "parallel", "arbitrary"),
        name="s5_scan",
    )(u.reshape(nb, seq, width), b_cat, ptab, c_cat, ssm_d.reshape(1, width))
    last = ROW_TILE - 1
    return y.reshape(nb * seq, width), hfin[:, 0, last], hfin[:, 1, last]


def _s5_sample_kernel(u_ref, h0re_ref, h0im_ref, b_ref, ptab_ref, c_ref, d_ref, y_ref, hre_ref, him_ref):
    sw = h0re_ref.shape[1]
    ar, ai = ptab_ref[0, 0, 0:1, :], ptab_ref[1, 0, 0:1, :]
    hr, hi = h0re_ref[...], h0im_ref[...]
    for t in range(u_ref.shape[0]):
        u = u_ref[t]
        bu = jnp.dot(u.astype(BF16), b_ref[0], preferred_element_type=F32)
        hr, hi = _cmul_add(bu[:, :sw], bu[:, sw:], ar, ai, hr, hi)
        hb = jnp.concatenate([hr, -hi], axis=1).astype(BF16)
        y_ref[t] = jnp.dot(hb, c_ref[0], preferred_element_type=F32) + d_ref[...] * u
    hre_ref[...] = hr
    him_ref[...] = hi


def _s5_sample(u, h0re, h0im, prm, ssm_d):
    ptab, b_cat, c_cat = prm
    n_chunk, cw, sw2 = b_cat.shape
    sw = sw2 // 2
    n_new, nb, width = u.shape
    return pl.pallas_call(
        _s5_sample_kernel,
        grid=(n_chunk,),
        in_specs=[pl.BlockSpec((n_new, nb, cw), lambda j: (0, 0, j)),
                  pl.BlockSpec((nb, sw), lambda j: (0, j)),
                  pl.BlockSpec((nb, sw), lambda j: (0, j)),
                  pl.BlockSpec((1, cw, sw2), lambda j: (j, 0, 0)),
                  pl.BlockSpec((2, 1, ROW_TILE, sw), lambda j: (0, 0, 0, j)),
                  pl.BlockSpec((1, sw2, cw), lambda j: (j, 0, 0)),
                  pl.BlockSpec((1, cw), lambda j: (0, j))],
        out_specs=[pl.BlockSpec((n_new, nb, cw), lambda j: (0, 0, j)),
                   pl.BlockSpec((nb, sw), lambda j: (0, j)),
                   pl.BlockSpec((nb, sw), lambda j: (0, j))],
        out_shape=[jax.ShapeDtypeStruct((n_new, nb, width), F32),
                   jax.ShapeDtypeStruct((nb, n_chunk * sw), F32),
                   jax.ShapeDtypeStruct((nb, n_chunk * sw), F32)],
        compiler_params=_params("parallel"),
        name="s5_sample",
    )(u, h0re, h0im, b_cat, ptab, c_cat, ssm_d.reshape(1, width))


def _glu_kernel(y_ref, g_ref, w_ref, b_ref, o_ref):
    y = y_ref[...]
    s = 0.5 * y * (1.0 + jnp.tanh(math.sqrt(2.0 / math.pi) * (y + 0.044715 * (y * y * y))))
    z = jnp.dot(s.astype(BF16), w_ref[...], preferred_element_type=F32) + b_ref[...]
    o_ref[...] = (s * _sigmoid(z) * g_ref[...].astype(F32)).astype(o_ref.dtype)


def _glu(y, gate, w, b):
    m, width = y.shape
    tm = _row_tile(m, 512)
    row = pl.BlockSpec((tm, width), lambda i: (i, 0))
    return pl.pallas_call(
        _glu_kernel,
        grid=(m // tm,),
        in_specs=[row, row, pl.BlockSpec((width, width), lambda i: (0, 0)),
                  pl.BlockSpec((1, width), lambda i: (0, 0))],
        out_specs=row,
        out_shape=jax.ShapeDtypeStruct((m, width), BF16),
        compiler_params=_params("parallel"),
        name="glu",
    )(y, gate, w, b.reshape(1, width))


def _merge_kernel(*refs):
    o_refs, l_refs = refs[:N_DIL], refs[N_DIL:2 * N_DIL]
    ga_ref, ys_ref, ma_ref, ms_ref, wa_ref, ws_ref, out_ref = refs[2 * N_DIL:2 * N_DIL + 7]
    a_ref, otok_ref, ltok_ref = refs[2 * N_DIL + 7:]

    def token_order(ref, cols, stage):
        dil, per = ref.shape[1], ref.shape[2]
        if dil == 1:
            return ref[0, 0, :, cols]
        for r in range(dil):
            stage[pl.ds(r, per, stride=dil), :] = ref[0, r, :, cols]
        return stage[...]

    ls = [token_order(l_refs[g], slice(None), ltok_ref.at[g]) for g in range(N_DIL)]
    mx = jnp.maximum(jnp.maximum(ls[0], ls[1]), ls[2])
    es = [jnp.exp(l - mx) for l in ls]
    den = es[0] + es[1] + es[2]
    wts = [e / den for e in es]
    tm = a_ref.shape[0]
    for h in range(a_ref.shape[1] // HEAD_DIM):
        sl = slice(h * HEAD_DIM, (h + 1) * HEAD_DIM)
        attn = None
        for g in range(N_DIL):
            term = jnp.broadcast_to(wts[g][:, h:h + 1], (tm, HEAD_DIM)) * token_order(o_refs[g], sl, otok_ref.at[g])
            attn = term if attn is None else attn + term
        a_ref[:, sl] = (attn * ga_ref[:, sl].astype(F32)).astype(BF16)
    half = out_ref.shape[1] // 2
    for c in range(2):
        cs = slice(c * half, (c + 1) * half)
        ya = jnp.dot(a_ref[...], wa_ref[:, cs], preferred_element_type=F32)
        yb = jnp.dot(ys_ref[...], ws_ref[:, cs], preferred_element_type=F32)
        out_ref[:, cs] = (ma_ref[:, cs].astype(F32) * ya + ms_ref[:, cs].astype(F32) * yb).astype(out_ref.dtype)


def _merge(outs, lses, g_attn, y_ssm, gates, w_a, w_s):
    m, aw = g_attn.shape
    d = w_a.shape[1]
    nb = outs[0].shape[0]
    seq = m // nb
    tm = _row_tile(seq, 512)
    bps = seq // tm
    row = pl.BlockSpec((tm, aw), lambda i: (i, 0))
    ol_specs = []
    for arr in list(outs) + list(lses):
        dil = arr.shape[1]
        assert tm % (dil * ROW_TILE) == 0
        ol_specs.append(pl.BlockSpec((1, dil, tm // dil, arr.shape[3]),
                                     lambda i: (lax.div(i, bps), 0, lax.rem(i, bps), 0)))
    scratch = [pltpu.VMEM((tm, aw), BF16), pltpu.VMEM((N_DIL, tm, HEAD_DIM), F32),
               pltpu.VMEM((N_DIL, tm, HEAD_DIM), F32)]
    once = pl.Buffered(1)
    return pl.pallas_call(
        _merge_kernel,
        grid=(m // tm,),
        in_specs=ol_specs + [row, row, pl.BlockSpec((tm, d), lambda i: (i, 0)),
                             pl.BlockSpec((tm, d), lambda i: (i, 1)),
                             pl.BlockSpec((aw, d), lambda i: (0, 0), pipeline_mode=once),
                             pl.BlockSpec((aw, d), lambda i: (0, 0), pipeline_mode=once)],
        out_specs=pl.BlockSpec((tm, d), lambda i: (i, 0)),
        out_shape=jax.ShapeDtypeStruct((m, d), BF16),
        scratch_shapes=scratch,
        compiler_params=_params("parallel"),
        name="merge",
    )(*outs, *lses, g_attn, y_ssm, gates, gates, w_a, w_s)


def _out_kernel(x_ref, m_ref, w_ref, o_ref):
    o_ref[...] = x_ref[...] + jnp.dot(m_ref[...], w_ref[...], preferred_element_type=F32)


def _out_proj(x, merged, w):
    m, d = x.shape
    tm = _row_tile(m, 1024)
    tn = min(d, 1024)
    return pl.pallas_call(
        _out_kernel,
        grid=(m // tm, d // tn),
        in_specs=[pl.BlockSpec((tm, tn), lambda i, j: (i, j)),
                  pl.BlockSpec((tm, d), lambda i, j: (i, 0)),
                  pl.BlockSpec((d, tn), lambda i, j: (0, j))],
        out_specs=pl.BlockSpec((tm, tn), lambda i, j: (i, j)),
        out_shape=jax.ShapeDtypeStruct((m, d), F32),
        compiler_params=_params("parallel", "arbitrary"),
        name="out_proj",
    )(x, merged, w)


def _kv_shift_kernel(*refs):
    n = N_DIL
    caches, news, outs = refs[:n], refs[n:2 * n], refs[2 * n:3 * n]
    ring, in_sem, out_sem = refs[3 * n:3 * n + 3]
    new_bufs, new_sem = refs[3 * n + 3:3 * n + 3 + n], refs[3 * n + 3 + n]

    new_in = [pltpu.make_async_copy(news[g], new_bufs[g], new_sem.at[g]) for g in range(n)]
    for cp in new_in:
        cp.start()

    jobs = []
    for g in range(n):
        buf, n_new = caches[g].shape[1], news[g].shape[1]
        keep = buf - n_new
        for i in range(caches[g].shape[0]):
            for r0 in range(0, keep, KV_COPY_ROWS):
                rows = min(KV_COPY_ROWS, keep - r0)
                jobs.append((caches[g].at[i, pl.ds(n_new + r0, rows)], outs[g].at[i, pl.ds(r0, rows)], rows))

    def load(j):
        src, _, rows = jobs[j]
        slot = j % KV_COPY_SLOTS
        return pltpu.make_async_copy(src, ring.at[slot, pl.ds(0, rows)], in_sem.at[slot])

    def store(j):
        _, dst, rows = jobs[j]
        slot = j % KV_COPY_SLOTS
        return pltpu.make_async_copy(ring.at[slot, pl.ds(0, rows)], dst, out_sem.at[slot])

    ahead = KV_COPY_SLOTS // 2
    for j in range(min(ahead, len(jobs))):
        load(j).start()
    for j in range(len(jobs)):
        load(j).wait()
        store(j).start()
        nxt = j + ahead
        if nxt < len(jobs):
            if nxt >= KV_COPY_SLOTS:
                store(nxt - KV_COPY_SLOTS).wait()
            load(nxt).start()
    for j in range(max(0, len(jobs) - KV_COPY_SLOTS), len(jobs)):
        store(j).wait()

    new_out = []
    for g in range(n):
        keep = caches[g].shape[1] - news[g].shape[1]
        new_in[g].wait()
        new_out.append(pltpu.make_async_copy(new_bufs[g], outs[g].at[:, pl.ds(keep, news[g].shape[1])],
                                             new_sem.at[n + g]))
        new_out[g].start()
    for cp in new_out:
        cp.wait()


def _kv_shift(caches, news):
    flat = lambda a: a.reshape(a.shape[0] * a.shape[1], a.shape[2], a.shape[3] * a.shape[4], a.shape[5])
    cf, nf = [flat(c) for c in caches], [flat(w) for w in news]
    for c, w in zip(cf, nf):
        assert c.shape[1] > w.shape[1]
    tile = cf[0].shape[2:]
    hbm = pl.BlockSpec(memory_space=pl.ANY)
    outs = pl.pallas_call(
        _kv_shift_kernel,
        in_specs=[hbm] * (2 * N_DIL),
        out_specs=[hbm] * N_DIL,
        out_shape=[jax.ShapeDtypeStruct(c.shape, c.dtype) for c in cf],
        scratch_shapes=[pltpu.VMEM((KV_COPY_SLOTS, KV_COPY_ROWS) + tile, F32),
                        pltpu.SemaphoreType.DMA((KV_COPY_SLOTS,)),
                        pltpu.SemaphoreType.DMA((KV_COPY_SLOTS,))]
                       + [pltpu.VMEM(w.shape, F32) for w in nf]
                       + [pltpu.SemaphoreType.DMA((2 * N_DIL,))],
        compiler_params=pltpu.CompilerParams(vmem_limit_bytes=VMEM_LIMIT),
        name="kv_shift",
    )(*cf, *nf)
    return [o.reshape(c.shape) for o, c in zip(outs, caches)]


def _layer(x, cos, sin, lw, nb, seq, sample):
    (norm_w, w_in, q_norm_w, k_norm_w, w_glu, b_glu, w_br_attn, w_br_ssm, w_out, ssm_d, s5prm) = lw
    m, d = x.shape
    aw = d // 2
    qkv = N_DIL * aw
    hg = aw // HEAD_DIM
    xn = _rmsnorm(x, norm_w)
    g_attn = _proj(xn, w_in, 3 * qkv, aw, "silu", BF16)
    u = _proj(xn, w_in, 3 * qkv + aw, aw, "plain", F32)
    g_ssm = _proj(xn, w_in, 3 * qkv + 2 * aw, aw, "silu", BF16)
    gates = _proj(xn, w_in, 3 * qkv + 3 * aw, 2 * d, "sigmoid", BF16)

    new_kv = []
    if sample is None:
        outs, lses = [], []
        for g in range(N_DIL):
            qd, kd, vd, kv = _qkv(xn, w_in, g, nb, seq, q_norm_w, k_norm_w, cos, sin)
            o, lse = _attn_prompt(qd, kd, vd)
            outs.append(o)
            lses.append(lse)
            new_kv.append(kv)
        y, hre, him = _s5_prompt(u, s5prm, ssm_d, nb, seq)
    else:
        caches, layer, h0re, h0im = sample
        q = _proj(xn, w_in, 0, qkv, "qk", F32, q_norm_w, cos, sin)
        k = _proj(xn, w_in, qkv, qkv, "qk", F32, k_norm_w, cos, sin)
        v = _proj(xn, w_in, 2 * qkv, qkv, "plain", F32)
        tok = (nb, seq, N_DIL * hg, HEAD_DIM)
        o, lse = _attn_sample(q.reshape(tok), k.reshape(tok), v.reshape(tok), caches, layer)
        o = o.reshape(m, qkv)
        lse = jnp.pad(lse[..., 0].reshape(m, N_DIL, hg), ((0, 0), (0, 0), (0, HEAD_DIM - hg)))
        outs = [o[:, g * aw:(g + 1) * aw].reshape(1, 1, m, aw) for g in range(N_DIL)]
        lses = [lse[:, g].reshape(1, 1, m, HEAD_DIM) for g in range(N_DIL)]
        ut = u.reshape(nb, seq, aw).transpose(1, 0, 2)
        yt, hre, him = _s5_sample(ut, h0re, h0im, s5prm, ssm_d)
        y = yt.transpose(1, 0, 2).reshape(m, aw)
        k5, v5 = k.reshape(nb, seq, N_DIL, hg, HEAD_DIM), v.reshape(nb, seq, N_DIL, hg, HEAD_DIM)
        new_kv = [jnp.stack([k5[:, :, g], v5[:, :, g]], axis=2) for g in range(N_DIL)]

    y_ssm = _glu(y, g_ssm, w_glu, b_glu)
    merged = _merge(outs, lses, g_attn, y_ssm, gates, w_br_attn, w_br_ssm)
    x_out = _out_proj(x, merged, w_out)

    n_grp = aw // SSM_GROUP_CH
    state = jnp.stack([hre, him], axis=-1).reshape(nb, n_grp, SSM_STATE, 2)
    return x_out, new_kv, state


def kernel(x_prompt, x_sample, cache_kv_d1, cache_kv_d4, cache_kv_d16, state_ssm, norm_w, w_in, q_norm_w, k_norm_w, ssm_lambda_re, ssm_lambda_im, ssm_log_dt, ssm_b_re, ssm_b_im, ssm_c_re, ssm_c_im, ssm_d, w_glu, b_glu, w_br_attn, w_br_ssm, w_out):
    nb_p, seq_p, d = x_prompt.shape
    nb_s, seq_s, _ = x_sample.shape
    depth = norm_w.shape[0]
    caches = (cache_kv_d1, cache_kv_d4, cache_kv_d16)

    cos_p, sin_p = _rope_tables(jnp.arange(seq_p, dtype=F32))
    cos_s, sin_s = _rope_tables(PAST_LEN + jnp.arange(seq_s, dtype=F32))
    cos_s, sin_s = jnp.tile(cos_s, (nb_s, 1)), jnp.tile(sin_s, (nb_s, 1))

    hp = x_prompt.reshape(nb_p * seq_p, d)
    hs = x_sample.reshape(nb_s * seq_s, d)
    kv_p = [[] for _ in range(N_DIL)]
    kv_s_new = [[] for _ in range(N_DIL)]
    ssm_p, ssm_s = [], []
    for l in range(depth):
        s5prm = _s5_params(ssm_lambda_re[l], ssm_lambda_im[l], ssm_log_dt[l], ssm_b_re[l], ssm_b_im[l],
                           ssm_c_re[l], ssm_c_im[l], _s5_chunk(seq_p) // ROW_TILE)
        lw = (norm_w[l], _layer_bf16(w_in, l), q_norm_w[l], k_norm_w[l], _layer_bf16(w_glu, l), b_glu[l],
              _layer_bf16(w_br_attn, l), _layer_bf16(w_br_ssm, l), _layer_bf16(w_out, l), ssm_d[l], s5prm)
        hp, new_kv, st = _layer(hp, cos_p, sin_p, lw, nb_p, seq_p, None)
        for g in range(N_DIL):
            kv_p[g].append(new_kv[g])
        ssm_p.append(st)

        n_state = state_ssm.shape[2] * state_ssm.shape[3]
        h0re = state_ssm[l, ..., 0].reshape(nb_s, n_state)
        h0im = state_ssm[l, ..., 1].reshape(nb_s, n_state)
        hs, new_kv_s, st_s = _layer(hs, cos_s, sin_s, lw, nb_s, seq_s, (caches, l, h0re, h0im))
        for g in range(N_DIL):
            kv_s_new[g].append(new_kv_s[g])
        ssm_s.append(st_s)

    kv_s = _kv_shift(caches, [jnp.stack(n) for n in kv_s_new])
    return (hp.reshape(x_prompt.shape), hs.reshape(x_sample.shape),
            jnp.stack(kv_p[0]), jnp.stack(kv_p[1]), jnp.stack(kv_p[2]), jnp.stack(ssm_p),
            kv_s[0], kv_s[1], kv_s[2], jnp.stack(ssm_s))
```

```python
import functools
import math

import jax
import jax.numpy as jnp
from jax import lax
from jax.experimental import pallas as pl
from jax.experimental.pallas import tpu as pltpu

F32 = jnp.float32
BF16 = jnp.bfloat16

HEAD_DIM = 128
WINDOWS = (128, 512, 2048)
DILATIONS = (1, 4, 16)
N_DIL = 3
SPAN = 128
SSM_GROUP_CH = 16
SSM_STATE = 64
GROUPS_PER_CHUNK = 8
ROPE_THETA = 10000.0
NORM_EPS = 1e-6
PAST_LEN = 16384
NEG_BIG = -1e30
VMEM_LIMIT = 56 * 1024 * 1024
ROW_TILE = 8
QK_ROW_CHUNK = 64
KV_COPY_ROWS = 512


def _params(*sem):
    return pltpu.CompilerParams(dimension_semantics=sem, vmem_limit_bytes=VMEM_LIMIT)


def _sigmoid(x):
    return 1.0 / (1.0 + jnp.exp(-x))


def _row_tile(m, cap):
    t = min(m, cap)
    assert m % t == 0, (m, t)
    return t


def _rmsnorm_kernel(x_ref, w_ref, o_ref):
    x = x_ref[...]
    y = x * lax.rsqrt(jnp.mean(x * x, axis=-1, keepdims=True) + NORM_EPS)
    o_ref[...] = (y * w_ref[...]).astype(o_ref.dtype)


def _rmsnorm(x, w):
    m, d = x.shape
    tm = _row_tile(m, 512)
    return pl.pallas_call(
        _rmsnorm_kernel,
        grid=(m // tm,),
        in_specs=[pl.BlockSpec((tm, d), lambda i: (i, 0)),
                  pl.BlockSpec((1, d), lambda i: (0, 0))],
        out_specs=pl.BlockSpec((tm, d), lambda i: (i, 0)),
        out_shape=jax.ShapeDtypeStruct((m, d), BF16),
        compiler_params=_params("parallel"),
        name="rmsnorm",
    )(x, w.reshape(1, d))


def _cast_kernel(w_ref, o_ref):
    o_ref[...] = w_ref[...].astype(o_ref.dtype)


def _layer_bf16(w, layer):
    _, rows, cols = w.shape
    tr, tc = min(rows, 512), min(cols, 2048)
    assert rows % tr == 0 and cols % tc == 0
    return pl.pallas_call(
        _cast_kernel,
        grid=(rows // tr, cols // tc),
        in_specs=[pl.BlockSpec((None, tr, tc), lambda i, j: (layer, i, j))],
        out_specs=pl.BlockSpec((tr, tc), lambda i, j: (i, j)),
        out_shape=jax.ShapeDtypeStruct((rows, cols), BF16),
        compiler_params=_params("parallel", "parallel"),
        name="cast_bf16",
    )(w)


def _norm_rope_inplace(t_ref, nw_ref, cos_ref, sin_ref, head_major=False):
    rows = cos_ref.shape[0]
    n_heads = t_ref.shape[0] // rows if head_major else t_ref.shape[1] // HEAD_DIM
    rc = min(rows, QK_ROW_CHUNK)
    half = HEAD_DIM // 2
    nw = nw_ref[...]
    nw_swapped = pltpu.roll(nw, half, axis=1)

    def chunk(c, carry):
        r0 = pl.multiple_of(c * rc, rc)
        cs = cos_ref[pl.ds(r0, rc), :] * nw
        ss = sin_ref[pl.ds(r0, rc), :] * nw_swapped
        for h in range(n_heads):
            idx = (pl.ds(h * rows + r0, rc), slice(None)) if head_major else (
                pl.ds(r0, rc), slice(h * HEAD_DIM, (h + 1) * HEAD_DIM))
            a = t_ref[idx]
            r = lax.rsqrt(jnp.mean(a * a, axis=-1, keepdims=True) + NORM_EPS)
            t_ref[idx] = (a * cs + pltpu.roll(a, half, axis=1) * ss) * r
        return carry

    lax.fori_loop(0, rows // rc, chunk, 0)


def _proj_kernel(x_ref, w_ref, o_ref, *, epilogue):
    acc = jnp.dot(x_ref[...], w_ref[...], preferred_element_type=F32)
    if epilogue == "silu":
        acc = acc * _sigmoid(acc)
    elif epilogue == "sigmoid":
        acc = _sigmoid(acc)
    o_ref[...] = acc.astype(o_ref.dtype)


def _norm_rope_heads(hm_ref, ssq_ref, nw_ref, cos_ref, sin_ref):
    rows = cos_ref.shape[0]
    rc = min(rows, QK_ROW_CHUNK)
    half = HEAD_DIM // 2
    nw = nw_ref[...]
    nw_swapped = pltpu.roll(nw, half, axis=1)
    ones = jnp.ones((HEAD_DIM, HEAD_DIM), BF16)
    for h in range(hm_ref.shape[0] // rows):
        a_all = hm_ref[h * rows:(h + 1) * rows, :]
        ssq_ref[...] = jnp.dot((a_all * a_all).astype(BF16), ones, preferred_element_type=F32)

        def chunk(c, carry):
            r0 = pl.multiple_of(c * rc, rc)
            cs = cos_ref[pl.ds(r0, rc), :] * nw
            ss = sin_ref[pl.ds(r0, rc), :] * nw_swapped
            a = hm_ref[pl.ds(h * rows + r0, rc), :]
            r = lax.rsqrt(ssq_ref[pl.ds(r0, rc), :] * (1.0 / HEAD_DIM) + NORM_EPS)
            hm_ref[pl.ds(h * rows + r0, rc), :] = (a * cs + pltpu.roll(a, half, axis=1) * ss) * r
            return carry

        lax.fori_loop(0, rows // rc, chunk, 0, unroll=True)


def _project_dilated(x, w_ref, nw_ref, cos_ref, sin_ref, hm_ref, ssq_ref, d_ref, dil):
    tm = x.shape[0]
    n_heads = w_ref.shape[1] // HEAD_DIM
    per = tm // dil
    acc = jnp.dot(x, w_ref[...], preferred_element_type=F32)
    for h in range(n_heads):
        hm_ref[h * tm:(h + 1) * tm, :] = acc[:, h * HEAD_DIM:(h + 1) * HEAD_DIM]
    if nw_ref is not None:
        _norm_rope_heads(hm_ref, ssq_ref, nw_ref, cos_ref, sin_ref)
    for h in range(n_heads):
        for r in range(dil):
            rows = pl.ds(h * tm + r, per, stride=dil) if dil > 1 else pl.ds(h * tm, per)
            d_ref[0, r, :, h * HEAD_DIM:(h + 1) * HEAD_DIM] = hm_ref[rows, :].astype(d_ref.dtype)


def _q_kernel(x_ref, w_ref, nw_ref, cos_ref, sin_ref, qd_ref, hm_ref, ssq_ref, *, dil):
    _project_dilated(x_ref[...], w_ref, nw_ref, cos_ref, sin_ref, hm_ref, ssq_ref, qd_ref, dil)


def _kv_kernel(x_ref, wk_ref, wv_ref, nw_ref, cos_ref, sin_ref, *rest, dil, first_kept):
    kd_ref, vd_ref, kv_ref, hk_ref, hv_ref, ssq_ref = rest[-6:]
    half_idx = pl.program_id(2)
    tm = x_ref.shape[0]
    n_heads = wk_ref.shape[1] // HEAD_DIM

    @pl.when(half_idx == 0)
    def _():
        x = x_ref[...]
        _project_dilated(x, wk_ref, nw_ref, cos_ref, sin_ref, hk_ref, ssq_ref, kd_ref, dil)
        _project_dilated(x, wv_ref, None, None, None, hv_ref, None, vd_ref, dil)

    kb = kv_ref.shape[1]
    half = tm // 2

    @pl.when(2 * pl.program_id(1) + half_idx >= first_kept)
    def _():
        base = (half_idx + 1) * half - kb

        def token(t, carry):
            rows = pl.ds(base + t, n_heads, stride=tm)
            kv_ref[0, t, 0] = hk_ref[rows, :]
            kv_ref[0, t, 1] = hv_ref[rows, :]
            return carry

        lax.fori_loop(0, kb, token, 0, unroll=8)


def _qkv(xn, w, g, nb, seq, q_norm_w, k_norm_w, cos, sin, layer, depth, kv_prev):
    m, d = xn.shape
    aw = d // 2
    hg = aw // HEAD_DIM
    dil = DILATIONS[g]
    keep = min(WINDOWS[g], seq)
    tm = _row_tile(seq, 1024)
    per = tm // dil
    half = tm // 2
    kb = min(keep, half)
    bps = seq // tm
    n_half = 2 * bps
    first_kept = (seq - keep) // half if keep >= half else n_half - 1
    assert per % 16 == 0 and keep % kb == 0 and (seq - keep) % kb == 0
    once = pl.Buffered(1)
    nw = lambda a: a.reshape(1, HEAD_DIM)
    dshape = jax.ShapeDtypeStruct((nb, dil, seq // dil, aw), BF16)
    staging = pltpu.VMEM((hg * tm, HEAD_DIM), F32)
    sumsq = pltpu.VMEM((tm, HEAD_DIM), F32)

    qd = pl.pallas_call(
        functools.partial(_q_kernel, dil=dil),
        grid=(nb, bps),
        in_specs=[pl.BlockSpec((tm, d), lambda b, i: (b * bps + i, 0)),
                  pl.BlockSpec((d, aw), lambda b, i: (0, g), pipeline_mode=once),
                  pl.BlockSpec((1, HEAD_DIM), lambda b, i: (0, 0)),
                  pl.BlockSpec((tm, HEAD_DIM), lambda b, i: (i, 0)),
                  pl.BlockSpec((tm, HEAD_DIM), lambda b, i: (i, 0))],
        out_specs=pl.BlockSpec((1, dil, per, aw), lambda b, i: (b, 0, i, 0)),
        out_shape=dshape,
        scratch_shapes=[staging, sumsq],
        compiler_params=_params("parallel", "parallel"),
        name="q_d%d" % dil,
    )(xn, w, nw(q_norm_w), cos, sin)

    wspec = lambda col: pl.BlockSpec((d, aw), lambda b, i, h: (0, col), pipeline_mode=once)
    tspec = pl.BlockSpec((tm, HEAD_DIM), lambda b, i, h: (i, 0))
    dspec = pl.BlockSpec((1, dil, per, aw), lambda b, i, h: (b, 0, i, 0))
    kv_shape = jax.ShapeDtypeStruct((depth, nb, keep, 2, hg, HEAD_DIM), F32)
    if kv_prev is None:
        kv_prev = jnp.zeros(kv_shape.shape, kv_shape.dtype)
    kd, vd, kv = pl.pallas_call(
        functools.partial(_kv_kernel, dil=dil, first_kept=first_kept),
        grid=(nb, bps, 2),
        in_specs=[pl.BlockSpec((tm, d), lambda b, i, h: (b * bps + i, 0)),
                  wspec(N_DIL + g), wspec(2 * N_DIL + g),
                  pl.BlockSpec((1, HEAD_DIM), lambda b, i, h: (0, 0)), tspec, tspec,
                  pl.BlockSpec(memory_space=pl.ANY)],
        out_specs=[dspec, dspec,
                   pl.BlockSpec((None, 1, kb, 2, hg, HEAD_DIM),
                                lambda b, i, h: (layer, b, jnp.maximum(2 * i + h - first_kept, 0), 0, 0, 0))],
        out_shape=[dshape, dshape, kv_shape],
        scratch_shapes=[staging, staging, sumsq],
        input_output_aliases={6: 2},
        compiler_params=_params("parallel", "arbitrary", "arbitrary"),
        name="kv_d%d" % dil,
    )(xn, w, w, nw(k_norm_w), cos, sin, kv_prev)
    return qd, kd, vd, kv


def _proj(xn, w, col0, ncols, epilogue, out_dtype):
    m, d = xn.shape
    tm = _row_tile(m, 1024)
    tn = min(ncols, 1024)
    assert ncols % tn == 0 and col0 % tn == 0
    jb = col0 // tn
    return pl.pallas_call(
        functools.partial(_proj_kernel, epilogue=epilogue),
        grid=(m // tm, ncols // tn),
        in_specs=[pl.BlockSpec((tm, d), lambda i, j: (i, 0)),
                  pl.BlockSpec((d, tn), lambda i, j: (0, jb + j))],
        out_specs=pl.BlockSpec((tm, tn), lambda i, j: (i, j)),
        out_shape=jax.ShapeDtypeStruct((m, ncols), out_dtype),
        compiler_params=_params("parallel", "arbitrary"),
        name="proj_" + epilogue,
    )(xn, w)


def _rope_tables(pos):
    half = HEAD_DIM // 2
    inv_freq = jnp.power(ROPE_THETA, -jnp.arange(half, dtype=F32) * (2.0 / HEAD_DIM))
    ang = pos[:, None] * inv_freq[None, :]
    cos, sin = jnp.cos(ang), jnp.sin(ang)
    return jnp.concatenate([cos, cos], axis=-1), jnp.concatenate([-sin, sin], axis=-1)


def _attn_prompt_kernel(q_ref, kp_ref, kc_ref, vp_ref, vc_ref, o_ref, lse_ref):
    not_first = pl.program_id(2) > 0
    a = lax.broadcasted_iota(jnp.int32, (SPAN, 2 * SPAN), 0)
    c = lax.broadcasted_iota(jnp.int32, (SPAN, 2 * SPAN), 1)
    band = (c >= a) & (c <= a + SPAN)
    band_first = band & ((c >= SPAN) | not_first)
    lane = lax.broadcasted_iota(jnp.int32, (SPAN, HEAD_DIM), 1)
    scale = HEAD_DIM ** -0.5
    for j in range(q_ref.shape[2] // SPAN):
        rows = slice(j * SPAN, (j + 1) * SPAN)
        lse_tile = jnp.zeros((SPAN, HEAD_DIM), F32)
        for h in range(q_ref.shape[3] // HEAD_DIM):
            sl = slice(h * HEAD_DIM, (h + 1) * HEAD_DIM)
            q = q_ref[0, 0, rows, sl]
            if j == 0:
                k = jnp.concatenate([kp_ref[0, 0, :, sl], kc_ref[0, 0, rows, sl]], axis=0)
                v = jnp.concatenate([vp_ref[0, 0, :, sl], vc_ref[0, 0, rows, sl]], axis=0)
            else:
                k = kc_ref[0, 0, (j - 1) * SPAN:(j + 1) * SPAN, sl]
                v = vc_ref[0, 0, (j - 1) * SPAN:(j + 1) * SPAN, sl]
            s = lax.dot_general(q, k, (((1,), (1,)), ((), ())), preferred_element_type=F32) * scale
            s = jnp.where(band_first if j == 0 else band, s, NEG_BIG)
            m = jnp.max(s, axis=-1, keepdims=True)
            p = jnp.exp(s - m)
            l = jnp.sum(p, axis=-1, keepdims=True)
            o_ref[0, 0, rows, sl] = jnp.dot(p.astype(BF16), v, preferred_element_type=F32) / l
            lse_tile = jnp.where(lane == h, m + jnp.log(l), lse_tile)
        lse_ref[0, 0, rows, :] = lse_tile


def _attn_prompt(q, k, v):
    nb, dil, length, aw = q.shape
    n_blk = length // SPAN
    qb = min(4, n_blk)
    assert length % SPAN == 0 and n_blk % qb == 0 and aw // HEAD_DIM <= HEAD_DIM
    cur = pl.BlockSpec((1, 1, qb * SPAN, aw), lambda b, r, n: (b, r, n, 0))
    prev = pl.BlockSpec((1, 1, SPAN, aw), lambda b, r, n: (b, r, jnp.maximum(qb * n - 1, 0), 0))
    return pl.pallas_call(
        _attn_prompt_kernel,
        grid=(nb, dil, n_blk // qb),
        in_specs=[cur, prev, cur, prev, cur],
        out_specs=[cur, pl.BlockSpec((1, 1, qb * SPAN, HEAD_DIM), lambda b, r, n: (b, r, n, 0))],
        out_shape=[jax.ShapeDtypeStruct(q.shape, F32),
                   jax.ShapeDtypeStruct((nb, dil, length, HEAD_DIM), F32)],
        compiler_params=_params("parallel", "parallel", "arbitrary"),
        name="attn_prompt_d%d" % dil,
    )(q, k, k, v, v)


def _attn_sample_kernel(q_ref, kn_ref, vn_ref, c1_ref, c4_ref, c16_ref, o_ref, lse_ref):
    n_new, n_heads, _ = q_ref.shape
    hg = n_heads // N_DIL
    scale = HEAD_DIM ** -0.5
    caches = (c1_ref, c4_ref, c16_ref)
    for g in range(N_DIL):
        hs = slice(g * hg, (g + 1) * hg)
        for t in range(n_new):
            q = q_ref[t, hs, :]
            if DILATIONS[g] == 1:
                kc, vc = caches[g][:, 0], caches[g][:, 1]
                rows = lax.broadcasted_iota(jnp.int32, (kc.shape[0], hg, 1), 0)
                s = jnp.sum(kc * q[None], axis=-1, keepdims=True) * scale
                s = jnp.where(rows >= t, s, NEG_BIG)
                new = range(t + 1)
            else:
                kc, vc = caches[g][:, t, 0], caches[g][:, t, 1]
                s = jnp.sum(kc * q[None], axis=-1, keepdims=True) * scale
                new = (t,)
            s_new = [jnp.sum(kn_ref[u, hs, :] * q, axis=-1, keepdims=True) * scale for u in new]
            m = jnp.max(s, axis=0)
            for sn in s_new:
                m = jnp.maximum(m, sn)
            p = jnp.exp(s - m[None])
            l = jnp.sum(p, axis=0)
            acc = jnp.sum(p * vc, axis=0)
            for u, sn in zip(new, s_new):
                pn = jnp.exp(sn - m)
                l = l + pn
                acc = acc + pn * vn_ref[u, hs, :]
            o_ref[t, hs, :] = acc / l
            lse_ref[t, hs, :] = jnp.broadcast_to(m + jnp.log(l), (hg, HEAD_DIM))


def _attn_sample(q, kn, vn, caches, layer):
    nb, n_new, n_heads, _ = q.shape
    hg = n_heads // N_DIL
    views, specs = [], []
    for g in range(N_DIL):
        dil = DILATIONS[g]
        depth, _, buf = caches[g].shape[:3]
        assert buf == WINDOWS[g] and (dil == 1 or dil % n_new == 0)
        if dil == 1:
            views.append(caches[g])
            specs.append(pl.BlockSpec((None, None, buf, 2, hg, HEAD_DIM),
                                      lambda b: (layer, b, 0, 0, 0, 0)))
        else:
            views.append(caches[g].reshape(depth, nb, SPAN, dil, 2, hg, HEAD_DIM))
            specs.append(pl.BlockSpec((None, None, SPAN, n_new, 2, hg, HEAD_DIM),
                                      lambda b: (layer, b, 0, 0, 0, 0, 0)))
    tok = pl.BlockSpec((None, n_new, n_heads, HEAD_DIM), lambda b: (b, 0, 0, 0))
    return pl.pallas_call(
        _attn_sample_kernel,
        grid=(nb,),
        in_specs=[tok, tok, tok] + specs,
        out_specs=[tok, tok],
        out_shape=[jax.ShapeDtypeStruct(q.shape, F32)] * 2,
        compiler_params=_params("parallel"),
        name="attn_sample",
    )(q, kn, vn, *views)


def _s5_params_kernel(lre_ref, lim_ref, ldt_ref, bre_ref, bim_ref, ptab_ref, bbre_ref, bbim_ref):
    dt = jnp.exp(ldt_ref[...])
    lre, lim = lre_ref[...], lim_ref[...]
    xr, xi = lre * dt, lim * dt
    mag = jnp.exp(xr)
    ar, ai = mag * jnp.cos(xi), mag * jnp.sin(xi)
    shape = (ROW_TILE, xr.shape[1])
    abr, abi = jnp.broadcast_to(ar, shape), jnp.broadcast_to(ai, shape)
    ptab_ref[0, 0] = abr
    ptab_ref[1, 0] = abi

    def next_power(i, carry):
        pr, pi = carry
        pr, pi = pr * abr - pi * abi, pr * abi + pi * abr
        ptab_ref[0, i] = pr
        ptab_ref[1, i] = pi
        return pr, pi

    lax.fori_loop(1, ptab_ref.shape[1], next_power, (abr, abi))
    nr, ni = ar - 1.0, ai
    den = lre * lre + lim * lim
    fr = (nr * lre + ni * lim) / den
    fi = (ni * lre - nr * lim) / den
    bre, bim = bre_ref[...], bim_ref[...]
    bbre_ref[...] = fr * bre - fi * bim
    bbim_ref[...] = fr * bim + fi * bre


def _s5_params(lam_re, lam_im, log_dt, b_re, b_im, c_re, c_im, seg_len):
    n_grp, n_st = lam_re.shape
    n_ch = b_re.shape[2]
    n = n_grp * n_st
    gc = GROUPS_PER_CHUNK
    n_chunk = n_grp // gc
    sw = gc * n_st
    to_lanes = lambda b: b.transpose(2, 0, 1).reshape(n_ch, n)
    lane = lambda rows: pl.BlockSpec((rows, sw), lambda j: (0, j))
    ptab, bb_re, bb_im = pl.pallas_call(
        _s5_params_kernel,
        grid=(n_chunk,),
        in_specs=[lane(1)] * 3 + [lane(n_ch)] * 2,
        out_specs=[pl.BlockSpec((2, seg_len, ROW_TILE, sw), lambda j: (0, 0, 0, j)), lane(n_ch), lane(n_ch)],
        out_shape=[jax.ShapeDtypeStruct((2, seg_len, ROW_TILE, n), F32),
                   jax.ShapeDtypeStruct((n_ch, n), F32), jax.ShapeDtypeStruct((n_ch, n), F32)],
        compiler_params=_params("parallel"),
        name="s5_params",
    )(lam_re.reshape(1, n), lam_im.reshape(1, n),
      jnp.broadcast_to(log_dt[:, None], (n_grp, n_st)).reshape(1, n), to_lanes(b_re), to_lanes(b_im))

    eye = jnp.eye(gc, dtype=F32)

    def b_blocks(bb):
        bb = bb.reshape(n_ch, n_chunk, gc, n_st)
        blk = jnp.einsum("cjgp,gh->jgchp", bb, eye)
        return blk.reshape(n_chunk, gc * n_ch, gc * n_st).astype(BF16)

    def c_blocks(cc):
        cc = cc.reshape(n_chunk, gc, n_ch, n_st)
        blk = jnp.einsum("jgcp,gh->jgphc", cc, eye)
        return blk.reshape(n_chunk, gc * n_st, gc * n_ch).astype(BF16)

    b_cat = jnp.concatenate([b_blocks(bb_re), b_blocks(bb_im)], axis=2)
    c_cat = jnp.concatenate([c_blocks(c_re), c_blocks(c_im)], axis=1)
    return ptab, b_cat, c_cat


def _cmul_add(xr, xi, ar, ai, br, bi):
    return xr + ar * br - ai * bi, xi + ar * bi + ai * br


def _s5_scan_kernel(u_ref, b_ref, ptab_ref, c_ref, d_ref, y_ref, hfin_ref, up_ref, s_ref, hb_ref, end_ref):
    tc, sw2 = s_ref.shape
    sw = sw2 // 2
    seg = tc // ROW_TILE
    re, im = slice(0, sw), slice(sw, sw2)

    @pl.when(pl.program_id(2) == 0)
    def _():
        end_ref[...] = jnp.zeros_like(end_ref)

    for i in range(seg):
        up_ref[i * ROW_TILE:(i + 1) * ROW_TILE, :] = u_ref[0, pl.ds(i, ROW_TILE, stride=seg), :]
    s_ref[...] = jnp.dot(up_ref[...].astype(BF16), b_ref[0], preferred_element_type=F32)

    ar, ai = ptab_ref[0, 0], ptab_ref[1, 0]

    def local_step(i, carry):
        hr, hi = carry
        r0 = pl.multiple_of(i * ROW_TILE, ROW_TILE)
        hr, hi = _cmul_add(s_ref[pl.ds(r0, ROW_TILE), re], s_ref[pl.ds(r0, ROW_TILE), im], ar, ai, hr, hi)
        s_ref[pl.ds(r0, ROW_TILE), re] = hr
        s_ref[pl.ds(r0, ROW_TILE), im] = hi
        return hr, hi

    zero = jnp.zeros((ROW_TILE, sw), F32)
    er, ei = lax.fori_loop(0, seg, local_step, (zero, zero), unroll=8)

    row = lax.broadcasted_iota(jnp.int32, (ROW_TILE, sw), 0)
    cr = jnp.where(row == 0, pltpu.roll(end_ref[0], 1, axis=0), pltpu.roll(er, 1, axis=0))
    ci = jnp.where(row == 0, pltpu.roll(end_ref[1], 1, axis=0), pltpu.roll(ei, 1, axis=0))
    mr, mi = ptab_ref[0, seg - 1], ptab_ref[1, seg - 1]
    wr, wi = mr, mi
    for sh in (1, 2, 4):
        gr, gi = jnp.where(row >= sh, wr, 0.0), jnp.where(row >= sh, wi, 0.0)
        cr, ci = _cmul_add(cr, ci, gr, gi, pltpu.roll(cr, sh, axis=0), pltpu.roll(ci, sh, axis=0))
        wr, wi = wr * wr - wi * wi, 2.0 * wr * wi
    fr, fi = _cmul_add(er, ei, mr, mi, cr, ci)
    end_ref[0] = fr
    end_ref[1] = fi
    hfin_ref[0, 0] = fr
    hfin_ref[0, 1] = fi

    def fix_up(k, carry):
        r0 = pl.multiple_of(k * 2 * ROW_TILE, 2 * ROW_TILE)
        hs, ns = [], []
        for half in range(2):
            i = 2 * k + half
            rows = pl.ds(r0 + half * ROW_TILE, ROW_TILE)
            hr, hi = _cmul_add(s_ref[rows, re], s_ref[rows, im], ptab_ref[0, i], ptab_ref[1, i], cr, ci)
            hs.append(hr)
            ns.append(-hi)
        hb_ref[pl.ds(r0, 2 * ROW_TILE), re] = jnp.concatenate(hs, axis=0).astype(BF16)
        hb_ref[pl.ds(r0, 2 * ROW_TILE), im] = jnp.concatenate(ns, axis=0).astype(BF16)
        return carry

    lax.fori_loop(0, seg // 2, fix_up, 0, unroll=4)
    yp = jnp.dot(hb_ref[...], c_ref[0], preferred_element_type=F32) + d_ref[...] * up_ref[...]
    for i in range(seg):
        y_ref[0, pl.ds(i, ROW_TILE, stride=seg), :] = yp[i * ROW_TILE:(i + 1) * ROW_TILE, :]


def _s5_chunk(seq):
    return min(seq, 1024)


def _s5_prompt(u, prm, ssm_d, nb, seq):
    ptab, b_cat, c_cat = prm
    n_chunk, cw, sw2 = b_cat.shape
    sw = sw2 // 2
    width = u.shape[1]
    tc = _s5_chunk(seq)
    seg = tc // ROW_TILE
    assert seq % tc == 0 and seg % 2 == 0 and ptab.shape[1] == seg
    y, hfin = pl.pallas_call(
        _s5_scan_kernel,
        grid=(nb, n_chunk, seq // tc),
        in_specs=[pl.BlockSpec((1, tc, cw), lambda b, j, t: (b, t, j)),
                  pl.BlockSpec((1, cw, sw2), lambda b, j, t: (j, 0, 0)),
                  pl.BlockSpec((2, seg, ROW_TILE, sw), lambda b, j, t: (0, 0, 0, j)),
                  pl.BlockSpec((1, sw2, cw), lambda b, j, t: (j, 0, 0)),
                  pl.BlockSpec((1, cw), lambda b, j, t: (0, j))],
        out_specs=[pl.BlockSpec((1, tc, cw), lambda b, j, t: (b, t, j)),
                   pl.BlockSpec((1, 2, ROW_TILE, sw), lambda b, j, t: (b, 0, 0, j))],
        out_shape=[jax.ShapeDtypeStruct((nb, seq, width), F32),
                   jax.ShapeDtypeStruct((nb, 2, ROW_TILE, n_chunk * sw), F32)],
        scratch_shapes=[pltpu.VMEM((tc, cw), F32), pltpu.VMEM((tc, sw2), F32),
                        pltpu.VMEM((tc, sw2), BF16), pltpu.VMEM((2, ROW_TILE, sw), F32)],
        compiler_params=_params("parallel", "parallel", "arbitrary"),
        name="s5_scan",
    )(u.reshape(nb, seq, width), b_cat, ptab, c_cat, ssm_d.reshape(1, width))
    last = ROW_TILE - 1
    return y.reshape(nb * seq, width), hfin[:, 0, last], hfin[:, 1, last]


def _s5_sample_kernel(u_ref, h0re_ref, h0im_ref, b_ref, ptab_ref, c_ref, d_ref, y_ref, hre_ref, him_ref):
    sw = h0re_ref.shape[1]
    ar, ai = ptab_ref[0, 0, 0:1, :], ptab_ref[1, 0, 0:1, :]
    hr, hi = h0re_ref[...], h0im_ref[...]
    for t in range(u_ref.shape[0]):
        u = u_ref[t]
        bu = jnp.dot(u.astype(BF16), b_ref[0], preferred_element_type=F32)
        hr, hi = _cmul_add(bu[:, :sw], bu[:, sw:], ar, ai, hr, hi)
        hb = jnp.concatenate([hr, -hi], axis=1).astype(BF16)
        y_ref[t] = jnp.dot(hb, c_ref[0], preferred_element_type=F32) + d_ref[...] * u
    hre_ref[...] = hr
    him_ref[...] = hi


def _s5_sample(u, h0re, h0im, prm, ssm_d):
    ptab, b_cat, c_cat = prm
    n_chunk, cw, sw2 = b_cat.shape
    sw = sw2 // 2
    n_new, nb, width = u.shape
    return pl.pallas_call(
        _s5_sample_kernel,
        grid=(n_chunk,),
        in_specs=[pl.BlockSpec((n_new, nb, cw), lambda j: (0, 0, j)),
                  pl.BlockSpec((nb, sw), lambda j: (0, j)),
                  pl.BlockSpec((nb, sw), lambda j: (0, j)),
                  pl.BlockSpec((1, cw, sw2), lambda j: (j, 0, 0)),
                  pl.BlockSpec((2, 1, ROW_TILE, sw), lambda j: (0, 0, 0, j)),
                  pl.BlockSpec((1, sw2, cw), lambda j: (j, 0, 0)),
                  pl.BlockSpec((1, cw), lambda j: (0, j))],
        out_specs=[pl.BlockSpec((n_new, nb, cw), lambda j: (0, 0, j)),
                   pl.BlockSpec((nb, sw), lambda j: (0, j)),
                   pl.BlockSpec((nb, sw), lambda j: (0, j))],
        out_shape=[jax.ShapeDtypeStruct((n_new, nb, width), F32),
                   jax.ShapeDtypeStruct((nb, n_chunk * sw), F32),
                   jax.ShapeDtypeStruct((nb, n_chunk * sw), F32)],
        compiler_params=_params("parallel"),
        name="s5_sample",
    )(u, h0re, h0im, b_cat, ptab, c_cat, ssm_d.reshape(1, width))


def _glu_kernel(y_ref, g_ref, w_ref, b_ref, o_ref):
    y = y_ref[...]
    s = 0.5 * y * (1.0 + jnp.tanh(math.sqrt(2.0 / math.pi) * (y + 0.044715 * (y * y * y))))
    z = jnp.dot(s.astype(BF16), w_ref[...], preferred_element_type=F32) + b_ref[...]
    o_ref[...] = (s * _sigmoid(z) * g_ref[...].astype(F32)).astype(o_ref.dtype)


def _glu(y, gate, w, b):
    m, width = y.shape
    tm = _row_tile(m, 512)
    row = pl.BlockSpec((tm, width), lambda i: (i, 0))
    return pl.pallas_call(
        _glu_kernel,
        grid=(m // tm,),
        in_specs=[row, row, pl.BlockSpec((width, width), lambda i: (0, 0)),
                  pl.BlockSpec((1, width), lambda i: (0, 0))],
        out_specs=row,
        out_shape=jax.ShapeDtypeStruct((m, width), BF16),
        compiler_params=_params("parallel"),
        name="glu",
    )(y, gate, w, b.reshape(1, width))


def _merge_kernel(*refs):
    o_refs, l_refs = refs[:N_DIL], refs[N_DIL:2 * N_DIL]
    ga_ref, ys_ref, ma_ref, ms_ref, wa_ref, ws_ref, out_ref = refs[2 * N_DIL:2 * N_DIL + 7]
    a_ref, otok_ref, ltok_ref = refs[2 * N_DIL + 7:]

    def token_order(ref, cols, stage):
        dil, per = ref.shape[1], ref.shape[2]
        if dil == 1:
            return ref[0, 0, :, cols]
        for r in range(dil):
            stage[pl.ds(r, per, stride=dil), :] = ref[0, r, :, cols]
        return stage[...]

    ls = [token_order(l_refs[g], slice(None), ltok_ref.at[g]) for g in range(N_DIL)]
    mx = jnp.maximum(jnp.maximum(ls[0], ls[1]), ls[2])
    es = [jnp.exp(l - mx) for l in ls]
    den = es[0] + es[1] + es[2]
    wts = [e / den for e in es]
    tm = a_ref.shape[0]
    for h in range(a_ref.shape[1] // HEAD_DIM):
        sl = slice(h * HEAD_DIM, (h + 1) * HEAD_DIM)
        attn = None
        for g in range(N_DIL):
            term = jnp.broadcast_to(wts[g][:, h:h + 1], (tm, HEAD_DIM)) * token_order(o_refs[g], sl, otok_ref.at[g])
            attn = term if attn is None else attn + term
        a_ref[:, sl] = (attn * ga_ref[:, sl].astype(F32)).astype(BF16)
    half = out_ref.shape[1] // 2
    for c in range(2):
        cs = slice(c * half, (c + 1) * half)
        ya = jnp.dot(a_ref[...], wa_ref[:, cs], preferred_element_type=F32)
        yb = jnp.dot(ys_ref[...], ws_ref[:, cs], preferred_element_type=F32)
        out_ref[:, cs] = (ma_ref[:, cs].astype(F32) * ya + ms_ref[:, cs].astype(F32) * yb).astype(out_ref.dtype)


def _merge(outs, lses, g_attn, y_ssm, gates, w_a, w_s):
    m, aw = g_attn.shape
    d = w_a.shape[1]
    nb = outs[0].shape[0]
    seq = m // nb
    tm = _row_tile(seq, 512)
    bps = seq // tm
    row = pl.BlockSpec((tm, aw), lambda i: (i, 0))
    ol_specs = []
    for arr in list(outs) + list(lses):
        dil = arr.shape[1]
        assert tm % (dil * ROW_TILE) == 0
        ol_specs.append(pl.BlockSpec((1, dil, tm // dil, arr.shape[3]),
                                     lambda i: (lax.div(i, bps), 0, lax.rem(i, bps), 0)))
    scratch = [pltpu.VMEM((tm, aw), BF16), pltpu.VMEM((N_DIL, tm, HEAD_DIM), F32),
               pltpu.VMEM((N_DIL, tm, HEAD_DIM), F32)]
    once = pl.Buffered(1)
    return pl.pallas_call(
        _merge_kernel,
        grid=(m // tm,),
        in_specs=ol_specs + [row, row, pl.BlockSpec((tm, d), lambda i: (i, 0)),
                             pl.BlockSpec((tm, d), lambda i: (i, 1)),
                             pl.BlockSpec((aw, d), lambda i: (0, 0), pipeline_mode=once),
                             pl.BlockSpec((aw, d), lambda i: (0, 0), pipeline_mode=once)],
        out_specs=pl.BlockSpec((tm, d), lambda i: (i, 0)),
        out_shape=jax.ShapeDtypeStruct((m, d), BF16),
        scratch_shapes=scratch,
        compiler_params=_params("parallel"),
        name="merge",
    )(*outs, *lses, g_attn, y_ssm, gates, gates, w_a, w_s)


def _out_kernel(x_ref, m_ref, w_ref, o_ref):
    o_ref[...] = x_ref[...] + jnp.dot(m_ref[...], w_ref[...], preferred_element_type=F32)


def _out_proj(x, merged, w):
    m, d = x.shape
    tm = _row_tile(m, 1024)
    tn = min(d, 1024)
    return pl.pallas_call(
        _out_kernel,
        grid=(m // tm, d // tn),
        in_specs=[pl.BlockSpec((tm, tn), lambda i, j: (i, j)),
                  pl.BlockSpec((tm, d), lambda i, j: (i, 0)),
                  pl.BlockSpec((d, tn), lambda i, j: (0, j))],
        out_specs=pl.BlockSpec((tm, tn), lambda i, j: (i, j)),
        out_shape=jax.ShapeDtypeStruct((m, d), F32),
        compiler_params=_params("parallel", "arbitrary"),
        name="out_proj",
    )(x, merged, w)


def _kv_shift_kernel(*refs):
    n = N_DIL
    caches, news, outs, sem = refs[:n], refs[n:2 * n], refs[2 * n:3 * n], refs[3 * n]
    copies = []
    for g in range(n):
        buf, n_new = caches[g].shape[1], news[g].shape[1]
        keep = buf - n_new
        for i in range(caches[g].shape[0]):
            for r0 in range(0, keep, KV_COPY_ROWS):
                rows = min(KV_COPY_ROWS, keep - r0)
                copies.append(pltpu.make_async_copy(caches[g].at[i, pl.ds(n_new + r0, rows)],
                                                    outs[g].at[i, pl.ds(r0, rows)], sem.at[len(copies)]))
            copies.append(pltpu.make_async_copy(news[g].at[i], outs[g].at[i, pl.ds(keep, n_new)],
                                                sem.at[len(copies)]))
    for cp in copies:
        cp.start()
    for cp in copies:
        cp.wait()


def _kv_shift_copies(cf, nf):
    return sum(c.shape[0] * (-(-(c.shape[1] - w.shape[1]) // KV_COPY_ROWS) + 1) for c, w in zip(cf, nf))


def _kv_shift(caches, news):
    flat = lambda a: a.reshape(a.shape[0] * a.shape[1], a.shape[2], a.shape[3] * a.shape[4], a.shape[5])
    cf, nf = [flat(c) for c in caches], [flat(w) for w in news]
    for c, w in zip(cf, nf):
        assert c.shape[1] > w.shape[1]
    hbm = pl.BlockSpec(memory_space=pl.ANY)
    outs = pl.pallas_call(
        _kv_shift_kernel,
        in_specs=[hbm] * (2 * N_DIL),
        out_specs=[hbm] * N_DIL,
        out_shape=[jax.ShapeDtypeStruct(c.shape, c.dtype) for c in cf],
        scratch_shapes=[pltpu.SemaphoreType.DMA((_kv_shift_copies(cf, nf),))],
        name="kv_shift",
    )(*cf, *nf)
    return [o.reshape(c.shape) for o, c in zip(outs, caches)]


def _sample_front_kernel(x_ref, gw_ref, w_ref, nwq_ref, nwk_ref, cos_ref, sin_ref, o_ref, nw_ref):
    j = pl.program_id(0)
    x = x_ref[...]
    xn = x * lax.rsqrt(jnp.mean(x * x, axis=-1, keepdims=True) + NORM_EPS) * gw_ref[...]
    acc = jnp.dot(xn.astype(BF16), w_ref[...], preferred_element_type=F32)
    qk, v_end = 2 * N_DIL, 3 * N_DIL

    @pl.when(j < qk)
    def _():
        nw_ref[...] = jnp.where(j < N_DIL, nwq_ref[...], nwk_ref[...])
        o_ref[...] = acc
        _norm_rope_inplace(o_ref, nw_ref, cos_ref, sin_ref)

    @pl.when(((j >= qk) & (j < v_end)) | (j == v_end + 1))
    def _():
        o_ref[...] = acc

    @pl.when((j == v_end) | (j == v_end + 2))
    def _():
        o_ref[...] = acc * _sigmoid(acc)

    @pl.when(j > v_end + 2)
    def _():
        o_ref[...] = _sigmoid(acc)


def _sample_front(x, norm_w, w, q_norm_w, k_norm_w, cos, sin):
    m, d = x.shape
    aw = d // 2
    n_cols = w.shape[1]
    row = pl.BlockSpec((m, HEAD_DIM), lambda j: (0, 0))
    one = pl.BlockSpec((1, HEAD_DIM), lambda j: (0, 0))
    return pl.pallas_call(
        _sample_front_kernel,
        grid=(n_cols // aw,),
        in_specs=[pl.BlockSpec((m, d), lambda j: (0, 0)), pl.BlockSpec((1, d), lambda j: (0, 0)),
                  pl.BlockSpec((d, aw), lambda j: (0, j)), one, one, row, row],
        out_specs=pl.BlockSpec((m, aw), lambda j: (0, j)),
        out_shape=jax.ShapeDtypeStruct((m, n_cols), F32),
        scratch_shapes=[pltpu.VMEM((1, HEAD_DIM), F32)],
        compiler_params=_params("arbitrary"),
        name="sample_front",
    )(x, norm_w.reshape(1, d), w, q_norm_w.reshape(1, HEAD_DIM), k_norm_w.reshape(1, HEAD_DIM), cos, sin)


def _layer(x, cos, sin, lw, nb, seq, layer, depth, kv_prev, sample):
    (norm_w, w_in, q_norm_w, k_norm_w, w_glu, b_glu, w_br_attn, w_br_ssm, w_out, ssm_d, s5prm) = lw
    m, d = x.shape
    aw = d // 2
    qkv = N_DIL * aw
    hg = aw // HEAD_DIM

    new_kv = []
    if sample is None:
        xn = _rmsnorm(x, norm_w)
        g_attn = _proj(xn, w_in, 3 * qkv, aw, "silu", BF16)
        u = _proj(xn, w_in, 3 * qkv + aw, aw, "plain", F32)
        g_ssm = _proj(xn, w_in, 3 * qkv + 2 * aw, aw, "silu", BF16)
        gates = _proj(xn, w_in, 3 * qkv + 3 * aw, 2 * d, "sigmoid", BF16)
        outs, lses = [], []
        for g in range(N_DIL):
            qd, kd, vd, kv = _qkv(xn, w_in, g, nb, seq, q_norm_w, k_norm_w, cos, sin, layer, depth,
                                  None if kv_prev is None else kv_prev[g])
            o, lse = _attn_prompt(qd, kd, vd)
            outs.append(o)
            lses.append(lse)
            new_kv.append(kv)
        y, hre, him = _s5_prompt(u, s5prm, ssm_d, nb, seq)
    else:
        caches, h0re, h0im = sample
        front = _sample_front(x, norm_w, w_in, q_norm_w, k_norm_w, cos, sin)
        q, k, v = front[:, :qkv], front[:, qkv:2 * qkv], front[:, 2 * qkv:3 * qkv]
        g_attn, u = front[:, 3 * qkv:3 * qkv + aw], front[:, 3 * qkv + aw:3 * qkv + 2 * aw]
        g_ssm, gates = front[:, 3 * qkv + 2 * aw:3 * qkv + 3 * aw], front[:, 3 * qkv + 3 * aw:]
        tok = (nb, seq, N_DIL * hg, HEAD_DIM)
        o, lse = _attn_sample(q.reshape(tok), k.reshape(tok), v.reshape(tok), caches, layer)
        o = o.reshape(m, qkv)
        lse = jnp.pad(lse[..., 0].reshape(m, N_DIL, hg), ((0, 0), (0, 0), (0, HEAD_DIM - hg)))
        outs = [o[:, g * aw:(g + 1) * aw].reshape(1, 1, m, aw) for g in range(N_DIL)]
        lses = [lse[:, g].reshape(1, 1, m, HEAD_DIM) for g in range(N_DIL)]
        ut = u.reshape(nb, seq, aw).transpose(1, 0, 2)
        yt, hre, him = _s5_sample(ut, h0re, h0im, s5prm, ssm_d)
        y = yt.transpose(1, 0, 2).reshape(m, aw)
        k5, v5 = k.reshape(nb, seq, N_DIL, hg, HEAD_DIM), v.reshape(nb, seq, N_DIL, hg, HEAD_DIM)
        new_kv = [jnp.stack([k5[:, :, g], v5[:, :, g]], axis=2) for g in range(N_DIL)]

    y_ssm = _glu(y, g_ssm, w_glu, b_glu)
    merged = _merge(outs, lses, g_attn, y_ssm, gates, w_br_attn, w_br_ssm)
    x_out = _out_proj(x, merged, w_out)

    n_grp = aw // SSM_GROUP_CH
    state = jnp.stack([hre, him], axis=-1).reshape(nb, n_grp, SSM_STATE, 2)
    return x_out, new_kv, state


def kernel(x_prompt, x_sample, cache_kv_d1, cache_kv_d4, cache_kv_d16, state_ssm, norm_w, w_in, q_norm_w, k_norm_w, ssm_lambda_re, ssm_lambda_im, ssm_log_dt, ssm_b_re, ssm_b_im, ssm_c_re, ssm_c_im, ssm_d, w_glu, b_glu, w_br_attn, w_br_ssm, w_out):
    nb_p, seq_p, d = x_prompt.shape
    nb_s, seq_s, _ = x_sample.shape
    depth = norm_w.shape[0]
    caches = (cache_kv_d1, cache_kv_d4, cache_kv_d16)

    cos_p, sin_p = _rope_tables(jnp.arange(seq_p, dtype=F32))
    cos_s, sin_s = _rope_tables(PAST_LEN + jnp.arange(seq_s, dtype=F32))
    cos_s, sin_s = jnp.tile(cos_s, (nb_s, 1)), jnp.tile(sin_s, (nb_s, 1))

    hp = x_prompt.reshape(nb_p * seq_p, d)
    hs = x_sample.reshape(nb_s * seq_s, d)
    kv_p = None
    kv_s_new = [[] for _ in range(N_DIL)]
    ssm_p, ssm_s = [], []
    for l in range(depth):
        s5prm = _s5_params(ssm_lambda_re[l], ssm_lambda_im[l], ssm_log_dt[l], ssm_b_re[l], ssm_b_im[l],
                           ssm_c_re[l], ssm_c_im[l], _s5_chunk(seq_p) // ROW_TILE)
        lw = (norm_w[l], _layer_bf16(w_in, l), q_norm_w[l], k_norm_w[l], _layer_bf16(w_glu, l), b_glu[l],
              _layer_bf16(w_br_attn, l), _layer_bf16(w_br_ssm, l), _layer_bf16(w_out, l), ssm_d[l], s5prm)
        hp, kv_p, st = _layer(hp, cos_p, sin_p, lw, nb_p, seq_p, l, depth, kv_p, None)
        ssm_p.append(st)

        n_state = state_ssm.shape[2] * state_ssm.shape[3]
        h0re = state_ssm[l, ..., 0].reshape(nb_s, n_state)
        h0im = state_ssm[l, ..., 1].reshape(nb_s, n_state)
        hs, new_kv_s, st_s = _layer(hs, cos_s, sin_s, lw, nb_s, seq_s, l, depth, None, (caches, h0re, h0im))
        for g in range(N_DIL):
            kv_s_new[g].append(new_kv_s[g])
        ssm_s.append(st_s)

    kv_s = _kv_shift(caches, [jnp.stack(n) for n in kv_s_new])
    return (hp.reshape(x_prompt.shape), hs.reshape(x_sample.shape), kv_p[0], kv_p[1], kv_p[2], jnp.stack(ssm_p),
            kv_s[0], kv_s[1], kv_s[2], jnp.stack(ssm_s))
```

```python
import functools
import math

import jax
import jax.numpy as jnp
from jax import lax
from jax.experimental import pallas as pl
from jax.experimental.pallas import tpu as pltpu

F32 = jnp.float32
BF16 = jnp.bfloat16

HEAD_DIM = 128
WINDOWS = (128, 512, 2048)
DILATIONS = (1, 4, 16)
N_DIL = 3
SPAN = 128
SSM_GROUP_CH = 16
SSM_STATE = 64
GROUPS_PER_CHUNK = 8
ROPE_THETA = 10000.0
NORM_EPS = 1e-6
PAST_LEN = 16384
NEG_BIG = -1e30
VMEM_LIMIT = 56 * 1024 * 1024
ROW_TILE = 8
QK_ROW_CHUNK = 64
KV_COPY_BYTES = 4 * 1024 * 1024


def _params(*sem):
    return pltpu.CompilerParams(dimension_semantics=sem, vmem_limit_bytes=VMEM_LIMIT)


def _sigmoid(x):
    return 1.0 / (1.0 + jnp.exp(-x))


def _row_tile(m, cap):
    t = min(m, cap)
    assert m % t == 0, (m, t)
    return t


def _rmsnorm_kernel(x_ref, w_ref, o_ref):
    x = x_ref[...]
    y = x * lax.rsqrt(jnp.mean(x * x, axis=-1, keepdims=True) + NORM_EPS)
    o_ref[...] = (y * w_ref[...]).astype(o_ref.dtype)


def _rmsnorm(x, w):
    m, d = x.shape
    tm = _row_tile(m, 512)
    return pl.pallas_call(
        _rmsnorm_kernel,
        grid=(m // tm,),
        in_specs=[pl.BlockSpec((tm, d), lambda i: (i, 0)),
                  pl.BlockSpec((1, d), lambda i: (0, 0))],
        out_specs=pl.BlockSpec((tm, d), lambda i: (i, 0)),
        out_shape=jax.ShapeDtypeStruct((m, d), BF16),
        compiler_params=_params("parallel"),
        name="rmsnorm",
    )(x, w.reshape(1, d))


def _cast_kernel(w_ref, o_ref):
    o_ref[...] = w_ref[...].astype(o_ref.dtype)


def _layer_bf16(w, layer):
    _, rows, cols = w.shape
    tr, tc = min(rows, 512), min(cols, 2048)
    assert rows % tr == 0 and cols % tc == 0
    return pl.pallas_call(
        _cast_kernel,
        grid=(rows // tr, cols // tc),
        in_specs=[pl.BlockSpec((None, tr, tc), lambda i, j: (layer, i, j))],
        out_specs=pl.BlockSpec((tr, tc), lambda i, j: (i, j)),
        out_shape=jax.ShapeDtypeStruct((rows, cols), BF16),
        compiler_params=_params("parallel", "parallel"),
        name="cast_bf16",
    )(w)


def _norm_rope_inplace(t_ref, nw_ref, cos_ref, sin_ref, head_major=False):
    rows = cos_ref.shape[0]
    n_heads = t_ref.shape[0] // rows if head_major else t_ref.shape[1] // HEAD_DIM
    rc = min(rows, QK_ROW_CHUNK)
    half = HEAD_DIM // 2
    nw = nw_ref[...]
    nw_swapped = pltpu.roll(nw, half, axis=1)

    def chunk(c, carry):
        r0 = pl.multiple_of(c * rc, rc)
        cs = cos_ref[pl.ds(r0, rc), :] * nw
        ss = sin_ref[pl.ds(r0, rc), :] * nw_swapped
        for h in range(n_heads):
            idx = (pl.ds(h * rows + r0, rc), slice(None)) if head_major else (
                pl.ds(r0, rc), slice(h * HEAD_DIM, (h + 1) * HEAD_DIM))
            a = t_ref[idx]
            r = lax.rsqrt(jnp.mean(a * a, axis=-1, keepdims=True) + NORM_EPS)
            t_ref[idx] = (a * cs + pltpu.roll(a, half, axis=1) * ss) * r
        return carry

    lax.fori_loop(0, rows // rc, chunk, 0)


def _proj_kernel(x_ref, w_ref, o_ref, *, epilogue):
    acc = jnp.dot(x_ref[...], w_ref[...], preferred_element_type=F32)
    if epilogue == "silu":
        acc = acc * _sigmoid(acc)
    elif epilogue == "sigmoid":
        acc = _sigmoid(acc)
    o_ref[...] = acc.astype(o_ref.dtype)


def _norm_rope_heads(hm_ref, ssq_ref, nw_ref, cos_ref, sin_ref):
    rows = cos_ref.shape[0]
    rc = min(rows, QK_ROW_CHUNK)
    half = HEAD_DIM // 2
    nw = nw_ref[...]
    nw_swapped = pltpu.roll(nw, half, axis=1)
    ones = jnp.ones((HEAD_DIM, HEAD_DIM), BF16)
    for h in range(hm_ref.shape[0] // rows):
        a_all = hm_ref[h * rows:(h + 1) * rows, :]
        ssq_ref[...] = jnp.dot((a_all * a_all).astype(BF16), ones, preferred_element_type=F32)

        def chunk(c, carry):
            r0 = pl.multiple_of(c * rc, rc)
            cs = cos_ref[pl.ds(r0, rc), :] * nw
            ss = sin_ref[pl.ds(r0, rc), :] * nw_swapped
            a = hm_ref[pl.ds(h * rows + r0, rc), :]
            r = lax.rsqrt(ssq_ref[pl.ds(r0, rc), :] * (1.0 / HEAD_DIM) + NORM_EPS)
            hm_ref[pl.ds(h * rows + r0, rc), :] = (a * cs + pltpu.roll(a, half, axis=1) * ss) * r
            return carry

        lax.fori_loop(0, rows // rc, chunk, 0, unroll=True)


def _project_dilated(x, w_ref, nw_ref, cos_ref, sin_ref, hm_ref, ssq_ref, d_ref, dil):
    tm = x.shape[0]
    n_heads = w_ref.shape[1] // HEAD_DIM
    per = tm // dil
    acc = jnp.dot(x, w_ref[...], preferred_element_type=F32)
    for h in range(n_heads):
        hm_ref[h * tm:(h + 1) * tm, :] = acc[:, h * HEAD_DIM:(h + 1) * HEAD_DIM]
    if nw_ref is not None:
        _norm_rope_heads(hm_ref, ssq_ref, nw_ref, cos_ref, sin_ref)
    for h in range(n_heads):
        for r in range(dil):
            rows = pl.ds(h * tm + r, per, stride=dil) if dil > 1 else pl.ds(h * tm, per)
            d_ref[0, r, :, h * HEAD_DIM:(h + 1) * HEAD_DIM] = hm_ref[rows, :].astype(d_ref.dtype)


def _q_kernel(x_ref, w_ref, nw_ref, cos_ref, sin_ref, qd_ref, hm_ref, ssq_ref, *, dil):
    _project_dilated(x_ref[...], w_ref, nw_ref, cos_ref, sin_ref, hm_ref, ssq_ref, qd_ref, dil)


def _kv_kernel(x_ref, wk_ref, wv_ref, nw_ref, cos_ref, sin_ref, *rest, dil, first_kept):
    kd_ref, vd_ref, kv_ref, hk_ref, hv_ref, ssq_ref = rest[-6:]
    half_idx = pl.program_id(2)
    tm = x_ref.shape[0]
    n_heads = wk_ref.shape[1] // HEAD_DIM

    @pl.when(half_idx == 0)
    def _():
        x = x_ref[...]
        _project_dilated(x, wk_ref, nw_ref, cos_ref, sin_ref, hk_ref, ssq_ref, kd_ref, dil)
        _project_dilated(x, wv_ref, None, None, None, hv_ref, None, vd_ref, dil)

    kb = kv_ref.shape[1]
    half = tm // 2

    @pl.when(2 * pl.program_id(1) + half_idx >= first_kept)
    def _():
        base = (half_idx + 1) * half - kb

        def token(t, carry):
            rows = pl.ds(base + t, n_heads, stride=tm)
            kv_ref[0, t, 0] = hk_ref[rows, :]
            kv_ref[0, t, 1] = hv_ref[rows, :]
            return carry

        lax.fori_loop(0, kb, token, 0, unroll=8)


def _qkv(xn, w, g, nb, seq, q_norm_w, k_norm_w, cos, sin, layer, depth, kv_prev):
    m, d = xn.shape
    aw = d // 2
    hg = aw // HEAD_DIM
    dil = DILATIONS[g]
    keep = min(WINDOWS[g], seq)
    tm = _row_tile(seq, 1024)
    per = tm // dil
    half = tm // 2
    kb = min(keep, half)
    bps = seq // tm
    n_half = 2 * bps
    first_kept = (seq - keep) // half if keep >= half else n_half - 1
    assert per % 16 == 0 and keep % kb == 0 and (seq - keep) % kb == 0
    once = pl.Buffered(1)
    nw = lambda a: a.reshape(1, HEAD_DIM)
    dshape = jax.ShapeDtypeStruct((nb, dil, seq // dil, aw), BF16)
    staging = pltpu.VMEM((hg * tm, HEAD_DIM), F32)
    sumsq = pltpu.VMEM((tm, HEAD_DIM), F32)

    qd = pl.pallas_call(
        functools.partial(_q_kernel, dil=dil),
        grid=(nb, bps),
        in_specs=[pl.BlockSpec((tm, d), lambda b, i: (b * bps + i, 0)),
                  pl.BlockSpec((d, aw), lambda b, i: (0, g), pipeline_mode=once),
                  pl.BlockSpec((1, HEAD_DIM), lambda b, i: (0, 0)),
                  pl.BlockSpec((tm, HEAD_DIM), lambda b, i: (i, 0)),
                  pl.BlockSpec((tm, HEAD_DIM), lambda b, i: (i, 0))],
        out_specs=pl.BlockSpec((1, dil, per, aw), lambda b, i: (b, 0, i, 0)),
        out_shape=dshape,
        scratch_shapes=[staging, sumsq],
        compiler_params=_params("parallel", "parallel"),
        name="q_d%d" % dil,
    )(xn, w, nw(q_norm_w), cos, sin)

    wspec = lambda col: pl.BlockSpec((d, aw), lambda b, i, h: (0, col), pipeline_mode=once)
    tspec = pl.BlockSpec((tm, HEAD_DIM), lambda b, i, h: (i, 0))
    dspec = pl.BlockSpec((1, dil, per, aw), lambda b, i, h: (b, 0, i, 0))
    kv_shape = jax.ShapeDtypeStruct((depth, nb, keep, 2, hg, HEAD_DIM), F32)
    if kv_prev is None:
        kv_prev = jnp.zeros(kv_shape.shape, kv_shape.dtype)
    kd, vd, kv = pl.pallas_call(
        functools.partial(_kv_kernel, dil=dil, first_kept=first_kept),
        grid=(nb, bps, 2),
        in_specs=[pl.BlockSpec((tm, d), lambda b, i, h: (b * bps + i, 0)),
                  wspec(N_DIL + g), wspec(2 * N_DIL + g),
                  pl.BlockSpec((1, HEAD_DIM), lambda b, i, h: (0, 0)), tspec, tspec,
                  pl.BlockSpec(memory_space=pl.ANY)],
        out_specs=[dspec, dspec,
                   pl.BlockSpec((None, 1, kb, 2, hg, HEAD_DIM),
                                lambda b, i, h: (layer, b, jnp.maximum(2 * i + h - first_kept, 0), 0, 0, 0))],
        out_shape=[dshape, dshape, kv_shape],
        scratch_shapes=[staging, staging, sumsq],
        input_output_aliases={6: 2},
        compiler_params=_params("parallel", "arbitrary", "arbitrary"),
        name="kv_d%d" % dil,
    )(xn, w, w, nw(k_norm_w), cos, sin, kv_prev)
    return qd, kd, vd, kv


def _proj(xn, w, col0, ncols, epilogue, out_dtype):
    m, d = xn.shape
    tm = _row_tile(m, 1024)
    tn = min(ncols, 1024)
    assert ncols % tn == 0 and col0 % tn == 0
    jb = col0 // tn
    return pl.pallas_call(
        functools.partial(_proj_kernel, epilogue=epilogue),
        grid=(m // tm, ncols // tn),
        in_specs=[pl.BlockSpec((tm, d), lambda i, j: (i, 0)),
                  pl.BlockSpec((d, tn), lambda i, j: (0, jb + j))],
        out_specs=pl.BlockSpec((tm, tn), lambda i, j: (i, j)),
        out_shape=jax.ShapeDtypeStruct((m, ncols), out_dtype),
        compiler_params=_params("parallel", "arbitrary"),
        name="proj_" + epilogue,
    )(xn, w)


def _rope_tables(pos):
    half = HEAD_DIM // 2
    inv_freq = jnp.power(ROPE_THETA, -jnp.arange(half, dtype=F32) * (2.0 / HEAD_DIM))
    ang = pos[:, None] * inv_freq[None, :]
    cos, sin = jnp.cos(ang), jnp.sin(ang)
    return jnp.concatenate([cos, cos], axis=-1), jnp.concatenate([-sin, sin], axis=-1)


def _attn_prompt_kernel(q_ref, kp_ref, kc_ref, vp_ref, vc_ref, o_ref, lse_ref):
    not_first = pl.program_id(2) > 0
    a = lax.broadcasted_iota(jnp.int32, (SPAN, 2 * SPAN), 0)
    c = lax.broadcasted_iota(jnp.int32, (SPAN, 2 * SPAN), 1)
    band = (c >= a) & (c <= a + SPAN)
    band_first = band & ((c >= SPAN) | not_first)
    lane = lax.broadcasted_iota(jnp.int32, (SPAN, HEAD_DIM), 1)
    scale = HEAD_DIM ** -0.5
    for j in range(q_ref.shape[2] // SPAN):
        rows = slice(j * SPAN, (j + 1) * SPAN)
        lse_tile = jnp.zeros((SPAN, HEAD_DIM), F32)
        for h in range(q_ref.shape[3] // HEAD_DIM):
            sl = slice(h * HEAD_DIM, (h + 1) * HEAD_DIM)
            q = q_ref[0, 0, rows, sl]
            if j == 0:
                k = jnp.concatenate([kp_ref[0, 0, :, sl], kc_ref[0, 0, rows, sl]], axis=0)
                v = jnp.concatenate([vp_ref[0, 0, :, sl], vc_ref[0, 0, rows, sl]], axis=0)
            else:
                k = kc_ref[0, 0, (j - 1) * SPAN:(j + 1) * SPAN, sl]
                v = vc_ref[0, 0, (j - 1) * SPAN:(j + 1) * SPAN, sl]
            s = lax.dot_general(q, k, (((1,), (1,)), ((), ())), preferred_element_type=F32) * scale
            s = jnp.where(band_first if j == 0 else band, s, NEG_BIG)
            m = jnp.max(s, axis=-1, keepdims=True)
            p = jnp.exp(s - m)
            l = jnp.sum(p, axis=-1, keepdims=True)
            o_ref[0, 0, rows, sl] = jnp.dot(p.astype(BF16), v, preferred_element_type=F32) / l
            lse_tile = jnp.where(lane == h, m + jnp.log(l), lse_tile)
        lse_ref[0, 0, rows, :] = lse_tile


def _attn_prompt(q, k, v):
    nb, dil, length, aw = q.shape
    n_blk = length // SPAN
    qb = min(4, n_blk)
    assert length % SPAN == 0 and n_blk % qb == 0 and aw // HEAD_DIM <= HEAD_DIM
    cur = pl.BlockSpec((1, 1, qb * SPAN, aw), lambda b, r, n: (b, r, n, 0))
    prev = pl.BlockSpec((1, 1, SPAN, aw), lambda b, r, n: (b, r, jnp.maximum(qb * n - 1, 0), 0))
    return pl.pallas_call(
        _attn_prompt_kernel,
        grid=(nb, dil, n_blk // qb),
        in_specs=[cur, prev, cur, prev, cur],
        out_specs=[cur, pl.BlockSpec((1, 1, qb * SPAN, HEAD_DIM), lambda b, r, n: (b, r, n, 0))],
        out_shape=[jax.ShapeDtypeStruct(q.shape, F32),
                   jax.ShapeDtypeStruct((nb, dil, length, HEAD_DIM), F32)],
        compiler_params=_params("parallel", "parallel", "arbitrary"),
        name="attn_prompt_d%d" % dil,
    )(q, k, k, v, v)


def _attn_sample_kernel(q_ref, kn_ref, vn_ref, c1_ref, c4_ref, c16_ref, o_ref, lse_ref):
    n_new, n_heads, _ = q_ref.shape
    hg = n_heads // N_DIL
    scale = HEAD_DIM ** -0.5
    caches = (c1_ref, c4_ref, c16_ref)
    for g in range(N_DIL):
        hs = slice(g * hg, (g + 1) * hg)
        for t in range(n_new):
            q = q_ref[t, hs, :]
            if DILATIONS[g] == 1:
                kc, vc = caches[g][:, 0], caches[g][:, 1]
                rows = lax.broadcasted_iota(jnp.int32, (kc.shape[0], hg, 1), 0)
                s = jnp.sum(kc * q[None], axis=-1, keepdims=True) * scale
                s = jnp.where(rows >= t, s, NEG_BIG)
                new = range(t + 1)
            else:
                kc, vc = caches[g][:, t, 0], caches[g][:, t, 1]
                s = jnp.sum(kc * q[None], axis=-1, keepdims=True) * scale
                new = (t,)
            s_new = [jnp.sum(kn_ref[u, hs, :] * q, axis=-1, keepdims=True) * scale for u in new]
            m = jnp.max(s, axis=0)
            for sn in s_new:
                m = jnp.maximum(m, sn)
            p = jnp.exp(s - m[None])
            l = jnp.sum(p, axis=0)
            acc = jnp.sum(p * vc, axis=0)
            for u, sn in zip(new, s_new):
                pn = jnp.exp(sn - m)
                l = l + pn
                acc = acc + pn * vn_ref[u, hs, :]
            o_ref[t, hs, :] = acc / l
            lse_ref[t, hs, :] = jnp.broadcast_to(m + jnp.log(l), (hg, HEAD_DIM))


def _attn_sample(q, kn, vn, caches, layer):
    nb, n_new, n_heads, _ = q.shape
    hg = n_heads // N_DIL
    views, specs = [], []
    for g in range(N_DIL):
        dil = DILATIONS[g]
        depth, _, buf = caches[g].shape[:3]
        assert buf == WINDOWS[g] and (dil == 1 or dil % n_new == 0)
        if dil == 1:
            views.append(caches[g])
            specs.append(pl.BlockSpec((None, None, buf, 2, hg, HEAD_DIM),
                                      lambda b: (layer, b, 0, 0, 0, 0)))
        else:
            views.append(caches[g].reshape(depth, nb, SPAN, dil, 2, hg, HEAD_DIM))
            specs.append(pl.BlockSpec((None, None, SPAN, n_new, 2, hg, HEAD_DIM),
                                      lambda b: (layer, b, 0, 0, 0, 0, 0)))
    tok = pl.BlockSpec((None, n_new, n_heads, HEAD_DIM), lambda b: (b, 0, 0, 0))
    return pl.pallas_call(
        _attn_sample_kernel,
        grid=(nb,),
        in_specs=[tok, tok, tok] + specs,
        out_specs=[tok, tok],
        out_shape=[jax.ShapeDtypeStruct(q.shape, F32)] * 2,
        compiler_params=_params("parallel"),
        name="attn_sample",
    )(q, kn, vn, *views)


def _s5_params_kernel(lre_ref, lim_ref, ldt_ref, bre_ref, bim_ref, ptab_ref, bbre_ref, bbim_ref):
    dt = jnp.exp(ldt_ref[...])
    lre, lim = lre_ref[...], lim_ref[...]
    xr, xi = lre * dt, lim * dt
    mag = jnp.exp(xr)
    ar, ai = mag * jnp.cos(xi), mag * jnp.sin(xi)
    shape = (ROW_TILE, xr.shape[1])
    abr, abi = jnp.broadcast_to(ar, shape), jnp.broadcast_to(ai, shape)
    ptab_ref[0, 0] = abr
    ptab_ref[1, 0] = abi

    def next_power(i, carry):
        pr, pi = carry
        pr, pi = pr * abr - pi * abi, pr * abi + pi * abr
        ptab_ref[0, i] = pr
        ptab_ref[1, i] = pi
        return pr, pi

    lax.fori_loop(1, ptab_ref.shape[1], next_power, (abr, abi))
    nr, ni = ar - 1.0, ai
    den = lre * lre + lim * lim
    fr = (nr * lre + ni * lim) / den
    fi = (ni * lre - nr * lim) / den
    bre, bim = bre_ref[...], bim_ref[...]
    bbre_ref[...] = fr * bre - fi * bim
    bbim_ref[...] = fr * bim + fi * bre


def _s5_params(lam_re, lam_im, log_dt, b_re, b_im, c_re, c_im, seg_len):
    n_grp, n_st = lam_re.shape
    n_ch = b_re.shape[2]
    n = n_grp * n_st
    gc = GROUPS_PER_CHUNK
    n_chunk = n_grp // gc
    sw = gc * n_st
    to_lanes = lambda b: b.transpose(2, 0, 1).reshape(n_ch, n)
    lane = lambda rows: pl.BlockSpec((rows, sw), lambda j: (0, j))
    ptab, bb_re, bb_im = pl.pallas_call(
        _s5_params_kernel,
        grid=(n_chunk,),
        in_specs=[lane(1)] * 3 + [lane(n_ch)] * 2,
        out_specs=[pl.BlockSpec((2, seg_len, ROW_TILE, sw), lambda j: (0, 0, 0, j)), lane(n_ch), lane(n_ch)],
        out_shape=[jax.ShapeDtypeStruct((2, seg_len, ROW_TILE, n), F32),
                   jax.ShapeDtypeStruct((n_ch, n), F32), jax.ShapeDtypeStruct((n_ch, n), F32)],
        compiler_params=_params("parallel"),
        name="s5_params",
    )(lam_re.reshape(1, n), lam_im.reshape(1, n),
      jnp.broadcast_to(log_dt[:, None], (n_grp, n_st)).reshape(1, n), to_lanes(b_re), to_lanes(b_im))

    eye = jnp.eye(gc, dtype=F32)

    def b_blocks(bb):
        bb = bb.reshape(n_ch, n_chunk, gc, n_st)
        blk = jnp.einsum("cjgp,gh->jgchp", bb, eye)
        return blk.reshape(n_chunk, gc * n_ch, gc * n_st).astype(BF16)

    def c_blocks(cc):
        cc = cc.reshape(n_chunk, gc, n_ch, n_st)
        blk = jnp.einsum("jgcp,gh->jgphc", cc, eye)
        return blk.reshape(n_chunk, gc * n_st, gc * n_ch).astype(BF16)

    b_cat = jnp.concatenate([b_blocks(bb_re), b_blocks(bb_im)], axis=2)
    c_cat = jnp.concatenate([c_blocks(c_re), c_blocks(c_im)], axis=1)
    return ptab, b_cat, c_cat


def _cmul_add(xr, xi, ar, ai, br, bi):
    return xr + ar * br - ai * bi, xi + ar * bi + ai * br


def _shift_plan(cache_shape, new_shape, n_steps):
    lb_n, buf = cache_shape[:2]
    keep = buf - new_shape[1]
    row_bytes = 4 * math.prod(cache_shape[2:])
    for per_row in range(1, keep + 1):
        n_jobs = lb_n * per_row
        if keep % per_row == 0 and keep // per_row * row_bytes <= KV_COPY_BYTES and n_steps % n_jobs == 0:
            return keep // per_row, per_row, n_steps // n_jobs
    raise ValueError("window shift does not fit the grid")


def _window_shift_step(step, n_steps, caches, news, outs, rings, nbufs, in_sem, out_sem, new_sem):
    last = step == n_steps - 1
    for g in range(N_DIL):
        cache, out, ring = caches[g], outs[g], rings[g]
        n_new = news[g].shape[1]
        keep = cache.shape[1] - n_new
        rows, per_row, period = _shift_plan(cache.shape, news[g].shape, n_steps)
        n_jobs = cache.shape[0] * per_row
        assert n_jobs >= 2

        def load(k, cache=cache, ring=ring, g=g, rows=rows, per_row=per_row, n_new=n_new):
            src = cache.at[lax.div(k, per_row), pl.ds(n_new + lax.rem(k, per_row) * rows, rows)]
            return pltpu.make_async_copy(src, ring.at[lax.rem(k, 2)], in_sem.at[g, lax.rem(k, 2)])

        def store(k, out=out, ring=ring, g=g, rows=rows, per_row=per_row):
            dst = out.at[lax.div(k, per_row), pl.ds(lax.rem(k, per_row) * rows, rows)]
            return pltpu.make_async_copy(ring.at[lax.rem(k, 2)], dst, out_sem.at[g, lax.rem(k, 2)])

        def issue(k, load=load, store=store):
            @pl.when(k >= 2)
            def _():
                store(k - 2).wait()
            load(k).start()

        def forward(k, load=load, store=store):
            load(k).wait()
            store(k).start()

        def drain(k, store=store):
            store(k - 1).wait()
            store(k).wait()

        if period == 1:
            issue(step)

            @pl.when(step >= 1)
            def _(forward=forward):
                forward(step - 1)

            @pl.when(last)
            def _(forward=forward, drain=drain):
                forward(step)
                drain(step)
        else:
            k = lax.div(step, period)
            phase = lax.rem(step, period)

            @pl.when(phase == period - 2)
            def _(issue=issue, k=k):
                issue(k)

            @pl.when(phase == period - 1)
            def _(forward=forward, k=k):
                forward(k)

            @pl.when(last)
            def _(drain=drain, k=k):
                drain(k)

        new_in = pltpu.make_async_copy(news[g], nbufs[g], new_sem.at[g])
        new_out = pltpu.make_async_copy(nbufs[g], out.at[:, pl.ds(keep, n_new)], new_sem.at[N_DIL + g])

        @pl.when(step == 0)
        def _(new_in=new_in, new_out=new_out):
            new_in.start()
            new_in.wait()
            new_out.start()

        @pl.when(last)
        def _(new_out=new_out):
            new_out.wait()


def _s5_scan_kernel(u_ref, b_ref, ptab_ref, c_ref, d_ref, *rest, with_shift):
    if with_shift:
        n = N_DIL
        caches, news = rest[:n], rest[n:2 * n]
        y_ref, hfin_ref = rest[2 * n:2 * n + 2]
        outs = rest[2 * n + 2:3 * n + 2]
        up_ref, s_ref, hb_ref, end_ref = rest[3 * n + 2:3 * n + 6]
        rings, nbufs = rest[3 * n + 6:4 * n + 6], rest[4 * n + 6:5 * n + 6]
        in_sem, out_sem, new_sem = rest[5 * n + 6:]
        _, n_j, n_t = with_shift
        step = (pl.program_id(0) * n_j + pl.program_id(1)) * n_t + pl.program_id(2)
        _window_shift_step(step, math.prod(with_shift), caches, news, outs, rings, nbufs,
                           in_sem, out_sem, new_sem)
    else:
        y_ref, hfin_ref, up_ref, s_ref, hb_ref, end_ref = rest
    tc, sw2 = s_ref.shape
    sw = sw2 // 2
    seg = tc // ROW_TILE
    re, im = slice(0, sw), slice(sw, sw2)

    @pl.when(pl.program_id(2) == 0)
    def _():
        end_ref[...] = jnp.zeros_like(end_ref)

    for i in range(seg):
        up_ref[i * ROW_TILE:(i + 1) * ROW_TILE, :] = u_ref[0, pl.ds(i, ROW_TILE, stride=seg), :]
    s_ref[...] = jnp.dot(up_ref[...].astype(BF16), b_ref[0], preferred_element_type=F32)

    ar, ai = ptab_ref[0, 0], ptab_ref[1, 0]

    def local_step(i, carry):
        hr, hi = carry
        r0 = pl.multiple_of(i * ROW_TILE, ROW_TILE)
        hr, hi = _cmul_add(s_ref[pl.ds(r0, ROW_TILE), re], s_ref[pl.ds(r0, ROW_TILE), im], ar, ai, hr, hi)
        s_ref[pl.ds(r0, ROW_TILE), re] = hr
        s_ref[pl.ds(r0, ROW_TILE), im] = hi
        return hr, hi

    zero = jnp.zeros((ROW_TILE, sw), F32)
    er, ei = lax.fori_loop(0, seg, local_step, (zero, zero), unroll=8)

    row = lax.broadcasted_iota(jnp.int32, (ROW_TILE, sw), 0)
    cr = jnp.where(row == 0, pltpu.roll(end_ref[0], 1, axis=0), pltpu.roll(er, 1, axis=0))
    ci = jnp.where(row == 0, pltpu.roll(end_ref[1], 1, axis=0), pltpu.roll(ei, 1, axis=0))
    mr, mi = ptab_ref[0, seg - 1], ptab_ref[1, seg - 1]
    wr, wi = mr, mi
    for sh in (1, 2, 4):
        gr, gi = jnp.where(row >= sh, wr, 0.0), jnp.where(row >= sh, wi, 0.0)
        cr, ci = _cmul_add(cr, ci, gr, gi, pltpu.roll(cr, sh, axis=0), pltpu.roll(ci, sh, axis=0))
        wr, wi = wr * wr - wi * wi, 2.0 * wr * wi
    fr, fi = _cmul_add(er, ei, mr, mi, cr, ci)
    end_ref[0] = fr
    end_ref[1] = fi
    hfin_ref[0, 0] = fr
    hfin_ref[0, 1] = fi

    def fix_up(k, carry):
        r0 = pl.multiple_of(k * 2 * ROW_TILE, 2 * ROW_TILE)
        hs, ns = [], []
        for half in range(2):
            i = 2 * k + half
            rows = pl.ds(r0 + half * ROW_TILE, ROW_TILE)
            hr, hi = _cmul_add(s_ref[rows, re], s_ref[rows, im], ptab_ref[0, i], ptab_ref[1, i], cr, ci)
            hs.append(hr)
            ns.append(-hi)
        hb_ref[pl.ds(r0, 2 * ROW_TILE), re] = jnp.concatenate(hs, axis=0).astype(BF16)
        hb_ref[pl.ds(r0, 2 * ROW_TILE), im] = jnp.concatenate(ns, axis=0).astype(BF16)
        return carry

    lax.fori_loop(0, seg // 2, fix_up, 0, unroll=4)
    yp = jnp.dot(hb_ref[...], c_ref[0], preferred_element_type=F32) + d_ref[...] * up_ref[...]
    for i in range(seg):
        y_ref[0, pl.ds(i, ROW_TILE, stride=seg), :] = yp[i * ROW_TILE:(i + 1) * ROW_TILE, :]


def _s5_chunk(seq):
    return min(seq, 1024)


def _flat_window(a):
    return a.reshape(a.shape[0] * a.shape[1], a.shape[2], a.shape[3] * a.shape[4], a.shape[5])


def _s5_prompt(u, prm, ssm_d, nb, seq, shift=None):
    ptab, b_cat, c_cat = prm
    n_chunk, cw, sw2 = b_cat.shape
    sw = sw2 // 2
    width = u.shape[1]
    tc = _s5_chunk(seq)
    seg = tc // ROW_TILE
    assert seq % tc == 0 and seg % 2 == 0 and ptab.shape[1] == seg
    grid = (nb, n_chunk, seq // tc)
    in_specs = [pl.BlockSpec((1, tc, cw), lambda b, j, t: (b, t, j)),
                pl.BlockSpec((1, cw, sw2), lambda b, j, t: (j, 0, 0)),
                pl.BlockSpec((2, seg, ROW_TILE, sw), lambda b, j, t: (0, 0, 0, j)),
                pl.BlockSpec((1, sw2, cw), lambda b, j, t: (j, 0, 0)),
                pl.BlockSpec((1, cw), lambda b, j, t: (0, j))]
    out_specs = [pl.BlockSpec((1, tc, cw), lambda b, j, t: (b, t, j)),
                 pl.BlockSpec((1, 2, ROW_TILE, sw), lambda b, j, t: (b, 0, 0, j))]
    out_shape = [jax.ShapeDtypeStruct((nb, seq, width), F32),
                 jax.ShapeDtypeStruct((nb, 2, ROW_TILE, n_chunk * sw), F32)]
    scratch = [pltpu.VMEM((tc, cw), F32), pltpu.VMEM((tc, sw2), F32),
               pltpu.VMEM((tc, sw2), BF16), pltpu.VMEM((2, ROW_TILE, sw), F32)]
    args = [u.reshape(nb, seq, width), b_cat, ptab, c_cat, ssm_d.reshape(1, width)]
    if shift is not None:
        caches, news = shift
        cf, nf = [_flat_window(c) for c in caches], [_flat_window(w) for w in news]
        hbm = pl.BlockSpec(memory_space=pl.ANY)
        in_specs += [hbm] * (2 * N_DIL)
        out_specs += [hbm] * N_DIL
        out_shape += [jax.ShapeDtypeStruct(c.shape, c.dtype) for c in cf]
        args += cf + nf
        for c, w in zip(cf, nf):
            rows = _shift_plan(c.shape, w.shape, math.prod(grid))[0]
            scratch.append(pltpu.VMEM((2, rows) + c.shape[2:], c.dtype))
        scratch += [pltpu.VMEM(w.shape, w.dtype) for w in nf]
        scratch += [pltpu.SemaphoreType.DMA((N_DIL, 2)), pltpu.SemaphoreType.DMA((N_DIL, 2)),
                    pltpu.SemaphoreType.DMA((2 * N_DIL,))]
    res = pl.pallas_call(
        functools.partial(_s5_scan_kernel, with_shift=grid if shift is not None else None),
        grid=grid,
        in_specs=in_specs,
        out_specs=out_specs,
        out_shape=out_shape,
        scratch_shapes=scratch,
        compiler_params=_params("arbitrary", "arbitrary", "arbitrary") if shift is not None
        else _params("parallel", "parallel", "arbitrary"),
        name="s5_scan",
    )(*args)
    y, hfin = res[0], res[1]
    last = ROW_TILE - 1
    shifted = None if shift is None else [o.reshape(c.shape) for o, c in zip(res[2:], shift[0])]
    return y.reshape(nb * seq, width), hfin[:, 0, last], hfin[:, 1, last], shifted


def _s5_sample_kernel(u_ref, h0re_ref, h0im_ref, b_ref, ptab_ref, c_ref, d_ref, y_ref, hre_ref, him_ref):
    sw = h0re_ref.shape[1]
    ar, ai = ptab_ref[0, 0, 0:1, :], ptab_ref[1, 0, 0:1, :]
    hr, hi = h0re_ref[...], h0im_ref[...]
    for t in range(u_ref.shape[0]):
        u = u_ref[t]
        bu = jnp.dot(u.astype(BF16), b_ref[0], preferred_element_type=F32)
        hr, hi = _cmul_add(bu[:, :sw], bu[:, sw:], ar, ai, hr, hi)
        hb = jnp.concatenate([hr, -hi], axis=1).astype(BF16)
        y_ref[t] = jnp.dot(hb, c_ref[0], preferred_element_type=F32) + d_ref[...] * u
    hre_ref[...] = hr
    him_ref[...] = hi


def _s5_sample(u, h0re, h0im, prm, ssm_d):
    ptab, b_cat, c_cat = prm
    n_chunk, cw, sw2 = b_cat.shape
    sw = sw2 // 2
    n_new, nb, width = u.shape
    return pl.pallas_call(
        _s5_sample_kernel,
        grid=(n_chunk,),
        in_specs=[pl.BlockSpec((n_new, nb, cw), lambda j: (0, 0, j)),
                  pl.BlockSpec((nb, sw), lambda j: (0, j)),
                  pl.BlockSpec((nb, sw), lambda j: (0, j)),
                  pl.BlockSpec((1, cw, sw2), lambda j: (j, 0, 0)),
                  pl.BlockSpec((2, 1, ROW_TILE, sw), lambda j: (0, 0, 0, j)),
                  pl.BlockSpec((1, sw2, cw), lambda j: (j, 0, 0)),
                  pl.BlockSpec((1, cw), lambda j: (0, j))],
        out_specs=[pl.BlockSpec((n_new, nb, cw), lambda j: (0, 0, j)),
                   pl.BlockSpec((nb, sw), lambda j: (0, j)),
                   pl.BlockSpec((nb, sw), lambda j: (0, j))],
        out_shape=[jax.ShapeDtypeStruct((n_new, nb, width), F32),
                   jax.ShapeDtypeStruct((nb, n_chunk * sw), F32),
                   jax.ShapeDtypeStruct((nb, n_chunk * sw), F32)],
        compiler_params=_params("parallel"),
        name="s5_sample",
    )(u, h0re, h0im, b_cat, ptab, c_cat, ssm_d.reshape(1, width))


def _glu_kernel(y_ref, g_ref, w_ref, b_ref, o_ref):
    y = y_ref[...]
    s = 0.5 * y * (1.0 + jnp.tanh(math.sqrt(2.0 / math.pi) * (y + 0.044715 * (y * y * y))))
    z = jnp.dot(s.astype(BF16), w_ref[...], preferred_element_type=F32) + b_ref[...]
    o_ref[...] = (s * _sigmoid(z) * g_ref[...].astype(F32)).astype(o_ref.dtype)


def _glu(y, gate, w, b):
    m, width = y.shape
    tm = _row_tile(m, 512)
    row = pl.BlockSpec((tm, width), lambda i: (i, 0))
    return pl.pallas_call(
        _glu_kernel,
        grid=(m // tm,),
        in_specs=[row, row, pl.BlockSpec((width, width), lambda i: (0, 0)),
                  pl.BlockSpec((1, width), lambda i: (0, 0))],
        out_specs=row,
        out_shape=jax.ShapeDtypeStruct((m, width), BF16),
        compiler_params=_params("parallel"),
        name="glu",
    )(y, gate, w, b.reshape(1, width))


def _merge_kernel(*refs):
    o_refs, l_refs = refs[:N_DIL], refs[N_DIL:2 * N_DIL]
    ga_ref, ys_ref, ma_ref, ms_ref, wa_ref, ws_ref, out_ref = refs[2 * N_DIL:2 * N_DIL + 7]
    a_ref, otok_ref, ltok_ref = refs[2 * N_DIL + 7:]

    def token_order(ref, cols, stage):
        dil, per = ref.shape[1], ref.shape[2]
        if dil == 1:
            return ref[0, 0, :, cols]
        for r in range(dil):
            stage[pl.ds(r, per, stride=dil), :] = ref[0, r, :, cols]
        return stage[...]

    ls = [token_order(l_refs[g], slice(None), ltok_ref.at[g]) for g in range(N_DIL)]
    mx = jnp.maximum(jnp.maximum(ls[0], ls[1]), ls[2])
    es = [jnp.exp(l - mx) for l in ls]
    den = es[0] + es[1] + es[2]
    wts = [e / den for e in es]
    tm = a_ref.shape[0]
    for h in range(a_ref.shape[1] // HEAD_DIM):
        sl = slice(h * HEAD_DIM, (h + 1) * HEAD_DIM)
        attn = None
        for g in range(N_DIL):
            term = jnp.broadcast_to(wts[g][:, h:h + 1], (tm, HEAD_DIM)) * token_order(o_refs[g], sl, otok_ref.at[g])
            attn = term if attn is None else attn + term
        a_ref[:, sl] = (attn * ga_ref[:, sl].astype(F32)).astype(BF16)
    half = out_ref.shape[1] // 2
    for c in range(2):
        cs = slice(c * half, (c + 1) * half)
        ya = jnp.dot(a_ref[...], wa_ref[:, cs], preferred_element_type=F32)
        yb = jnp.dot(ys_ref[...], ws_ref[:, cs], preferred_element_type=F32)
        out_ref[:, cs] = (ma_ref[:, cs].astype(F32) * ya + ms_ref[:, cs].astype(F32) * yb).astype(out_ref.dtype)


def _merge(outs, lses, g_attn, y_ssm, gates, w_a, w_s):
    m, aw = g_attn.shape
    d = w_a.shape[1]
    nb = outs[0].shape[0]
    seq = m // nb
    tm = _row_tile(seq, 512)
    bps = seq // tm
    row = pl.BlockSpec((tm, aw), lambda i: (i, 0))
    ol_specs = []
    for arr in list(outs) + list(lses):
        dil = arr.shape[1]
        assert tm % (dil * ROW_TILE) == 0
        ol_specs.append(pl.BlockSpec((1, dil, tm // dil, arr.shape[3]),
                                     lambda i: (lax.div(i, bps), 0, lax.rem(i, bps), 0)))
    scratch = [pltpu.VMEM((tm, aw), BF16), pltpu.VMEM((N_DIL, tm, HEAD_DIM), F32),
               pltpu.VMEM((N_DIL, tm, HEAD_DIM), F32)]
    once = pl.Buffered(1)
    return pl.pallas_call(
        _merge_kernel,
        grid=(m // tm,),
        in_specs=ol_specs + [row, row, pl.BlockSpec((tm, d), lambda i: (i, 0)),
                             pl.BlockSpec((tm, d), lambda i: (i, 1)),
                             pl.BlockSpec((aw, d), lambda i: (0, 0), pipeline_mode=once),
                             pl.BlockSpec((aw, d), lambda i: (0, 0), pipeline_mode=once)],
        out_specs=pl.BlockSpec((tm, d), lambda i: (i, 0)),
        out_shape=jax.ShapeDtypeStruct((m, d), BF16),
        scratch_shapes=scratch,
        compiler_params=_params("parallel"),
        name="merge",
    )(*outs, *lses, g_attn, y_ssm, gates, gates, w_a, w_s)


def _out_kernel(x_ref, m_ref, w_ref, o_ref):
    o_ref[...] = x_ref[...] + jnp.dot(m_ref[...], w_ref[...], preferred_element_type=F32)


def _out_proj(x, merged, w):
    m, d = x.shape
    tm = _row_tile(m, 1024)
    tn = min(d, 1024)
    return pl.pallas_call(
        _out_kernel,
        grid=(m // tm, d // tn),
        in_specs=[pl.BlockSpec((tm, tn), lambda i, j: (i, j)),
                  pl.BlockSpec((tm, d), lambda i, j: (i, 0)),
                  pl.BlockSpec((d, tn), lambda i, j: (0, j))],
        out_specs=pl.BlockSpec((tm, tn), lambda i, j: (i, j)),
        out_shape=jax.ShapeDtypeStruct((m, d), F32),
        compiler_params=_params("parallel", "arbitrary"),
        name="out_proj",
    )(x, merged, w)


def _sample_front_kernel(x_ref, gw_ref, w_ref, nwq_ref, nwk_ref, cos_ref, sin_ref, o_ref, nw_ref):
    j = pl.program_id(0)
    x = x_ref[...]
    xn = x * lax.rsqrt(jnp.mean(x * x, axis=-1, keepdims=True) + NORM_EPS) * gw_ref[...]
    acc = jnp.dot(xn.astype(BF16), w_ref[...], preferred_element_type=F32)
    qk, v_end = 2 * N_DIL, 3 * N_DIL

    @pl.when(j < qk)
    def _():
        nw_ref[...] = jnp.where(j < N_DIL, nwq_ref[...], nwk_ref[...])
        o_ref[...] = acc
        _norm_rope_inplace(o_ref, nw_ref, cos_ref, sin_ref)

    @pl.when(((j >= qk) & (j < v_end)) | (j == v_end + 1))
    def _():
        o_ref[...] = acc

    @pl.when((j == v_end) | (j == v_end + 2))
    def _():
        o_ref[...] = acc * _sigmoid(acc)

    @pl.when(j > v_end + 2)
    def _():
        o_ref[...] = _sigmoid(acc)


def _sample_front(x, norm_w, w, q_norm_w, k_norm_w, cos, sin):
    m, d = x.shape
    aw = d // 2
    n_cols = w.shape[1]
    row = pl.BlockSpec((m, HEAD_DIM), lambda j: (0, 0))
    one = pl.BlockSpec((1, HEAD_DIM), lambda j: (0, 0))
    return pl.pallas_call(
        _sample_front_kernel,
        grid=(n_cols // aw,),
        in_specs=[pl.BlockSpec((m, d), lambda j: (0, 0)), pl.BlockSpec((1, d), lambda j: (0, 0)),
                  pl.BlockSpec((d, aw), lambda j: (0, j)), one, one, row, row],
        out_specs=pl.BlockSpec((m, aw), lambda j: (0, j)),
        out_shape=jax.ShapeDtypeStruct((m, n_cols), F32),
        scratch_shapes=[pltpu.VMEM((1, HEAD_DIM), F32)],
        compiler_params=_params("arbitrary"),
        name="sample_front",
    )(x, norm_w.reshape(1, d), w, q_norm_w.reshape(1, HEAD_DIM), k_norm_w.reshape(1, HEAD_DIM), cos, sin)


def _layer(x, cos, sin, lw, nb, seq, layer, depth, kv_prev, sample, shift=None):
    (norm_w, w_in, q_norm_w, k_norm_w, w_glu, b_glu, w_br_attn, w_br_ssm, w_out, ssm_d, s5prm) = lw
    m, d = x.shape
    aw = d // 2
    qkv = N_DIL * aw
    hg = aw // HEAD_DIM

    new_kv = []
    if sample is None:
        xn = _rmsnorm(x, norm_w)
        g_attn = _proj(xn, w_in, 3 * qkv, aw, "silu", BF16)
        u = _proj(xn, w_in, 3 * qkv + aw, aw, "plain", F32)
        g_ssm = _proj(xn, w_in, 3 * qkv + 2 * aw, aw, "silu", BF16)
        gates = _proj(xn, w_in, 3 * qkv + 3 * aw, 2 * d, "sigmoid", BF16)
        outs, lses = [], []
        for g in range(N_DIL):
            qd, kd, vd, kv = _qkv(xn, w_in, g, nb, seq, q_norm_w, k_norm_w, cos, sin, layer, depth,
                                  None if kv_prev is None else kv_prev[g])
            o, lse = _attn_prompt(qd, kd, vd)
            outs.append(o)
            lses.append(lse)
            new_kv.append(kv)
        y, hre, him, shifted = _s5_prompt(u, s5prm, ssm_d, nb, seq, shift)
    else:
        caches, h0re, h0im = sample
        front = _sample_front(x, norm_w, w_in, q_norm_w, k_norm_w, cos, sin)
        q, k, v = front[:, :qkv], front[:, qkv:2 * qkv], front[:, 2 * qkv:3 * qkv]
        g_attn, u = front[:, 3 * qkv:3 * qkv + aw], front[:, 3 * qkv + aw:3 * qkv + 2 * aw]
        g_ssm, gates = front[:, 3 * qkv + 2 * aw:3 * qkv + 3 * aw], front[:, 3 * qkv + 3 * aw:]
        tok = (nb, seq, N_DIL * hg, HEAD_DIM)
        o, lse = _attn_sample(q.reshape(tok), k.reshape(tok), v.reshape(tok), caches, layer)
        o = o.reshape(m, qkv)
        lse = jnp.pad(lse[..., 0].reshape(m, N_DIL, hg), ((0, 0), (0, 0), (0, HEAD_DIM - hg)))
        outs = [o[:, g * aw:(g + 1) * aw].reshape(1, 1, m, aw) for g in range(N_DIL)]
        lses = [lse[:, g].reshape(1, 1, m, HEAD_DIM) for g in range(N_DIL)]
        ut = u.reshape(nb, seq, aw).transpose(1, 0, 2)
        yt, hre, him = _s5_sample(ut, h0re, h0im, s5prm, ssm_d)
        y = yt.transpose(1, 0, 2).reshape(m, aw)
        k5, v5 = k.reshape(nb, seq, N_DIL, hg, HEAD_DIM), v.reshape(nb, seq, N_DIL, hg, HEAD_DIM)
        new_kv = [jnp.stack([k5[:, :, g], v5[:, :, g]], axis=2) for g in range(N_DIL)]
        shifted = None

    y_ssm = _glu(y, g_ssm, w_glu, b_glu)
    merged = _merge(outs, lses, g_attn, y_ssm, gates, w_br_attn, w_br_ssm)
    x_out = _out_proj(x, merged, w_out)

    n_grp = aw // SSM_GROUP_CH
    state = jnp.stack([hre, him], axis=-1).reshape(nb, n_grp, SSM_STATE, 2)
    return x_out, new_kv, state, shifted


def kernel(x_prompt, x_sample, cache_kv_d1, cache_kv_d4, cache_kv_d16, state_ssm, norm_w, w_in, q_norm_w, k_norm_w, ssm_lambda_re, ssm_lambda_im, ssm_log_dt, ssm_b_re, ssm_b_im, ssm_c_re, ssm_c_im, ssm_d, w_glu, b_glu, w_br_attn, w_br_ssm, w_out):
    nb_p, seq_p, d = x_prompt.shape
    nb_s, seq_s, _ = x_sample.shape
    depth = norm_w.shape[0]
    caches = (cache_kv_d1, cache_kv_d4, cache_kv_d16)

    cos_p, sin_p = _rope_tables(jnp.arange(seq_p, dtype=F32))
    cos_s, sin_s = _rope_tables(PAST_LEN + jnp.arange(seq_s, dtype=F32))
    cos_s, sin_s = jnp.tile(cos_s, (nb_s, 1)), jnp.tile(sin_s, (nb_s, 1))

    hp = x_prompt.reshape(nb_p * seq_p, d)
    hs = x_sample.reshape(nb_s * seq_s, d)
    lws = []
    for l in range(depth):
        s5prm = _s5_params(ssm_lambda_re[l], ssm_lambda_im[l], ssm_log_dt[l], ssm_b_re[l], ssm_b_im[l],
                           ssm_c_re[l], ssm_c_im[l], _s5_chunk(seq_p) // ROW_TILE)
        lws.append((norm_w[l], _layer_bf16(w_in, l), q_norm_w[l], k_norm_w[l], _layer_bf16(w_glu, l), b_glu[l],
                    _layer_bf16(w_br_attn, l), _layer_bf16(w_br_ssm, l), _layer_bf16(w_out, l), ssm_d[l], s5prm))

    kv_s_new = [[] for _ in range(N_DIL)]
    ssm_s = []
    n_state = state_ssm.shape[2] * state_ssm.shape[3]
    for l in range(depth):
        h0re = state_ssm[l, ..., 0].reshape(nb_s, n_state)
        h0im = state_ssm[l, ..., 1].reshape(nb_s, n_state)
        hs, new_kv_s, st_s, _ = _layer(hs, cos_s, sin_s, lws[l], nb_s, seq_s, l, depth, None,
                                       (caches, h0re, h0im))
        for g in range(N_DIL):
            kv_s_new[g].append(new_kv_s[g])
        ssm_s.append(st_s)

    kv_p, kv_s, ssm_p = None, None, []
    for l in range(depth):
        shift = (caches, [jnp.stack(n) for n in kv_s_new]) if l == 0 else None
        hp, kv_p, st, shifted = _layer(hp, cos_p, sin_p, lws[l], nb_p, seq_p, l, depth, kv_p, None, shift)
        kv_s = shifted if shifted is not None else kv_s
        ssm_p.append(st)

    return (hp.reshape(x_prompt.shape), hs.reshape(x_sample.shape), kv_p[0], kv_p[1], kv_p[2], jnp.stack(ssm_p),
            kv_s[0], kv_s[1], kv_s[2], jnp.stack(ssm_s))
```

```python
import functools
import math

import jax
import jax.numpy as jnp
from jax import lax
from jax.experimental import pallas as pl
from jax.experimental.pallas import tpu as pltpu

F32 = jnp.float32
BF16 = jnp.bfloat16

HEAD_DIM = 128
WINDOWS = (128, 512, 2048)
DILATIONS = (1, 4, 16)
N_DIL = 3
SPAN = 128
SSM_GROUP_CH = 16
SSM_STATE = 64
GROUPS_PER_CHUNK = 8
ROPE_THETA = 10000.0
NORM_EPS = 1e-6
PAST_LEN = 16384
NEG_BIG = -1e30
VMEM_LIMIT = 56 * 1024 * 1024
ROW_TILE = 8
QK_ROW_CHUNK = 64
KV_COPY_BYTES = 4 * 1024 * 1024


def _params(*sem):
    return pltpu.CompilerParams(dimension_semantics=sem, vmem_limit_bytes=VMEM_LIMIT)


def _sigmoid(x):
    return 1.0 / (1.0 + jnp.exp(-x))


def _row_tile(m, cap):
    t = min(m, cap)
    assert m % t == 0, (m, t)
    return t


def _rmsnorm_kernel(x_ref, w_ref, o_ref):
    x = x_ref[...]
    y = x * lax.rsqrt(jnp.mean(x * x, axis=-1, keepdims=True) + NORM_EPS)
    o_ref[...] = (y * w_ref[...]).astype(o_ref.dtype)


def _rmsnorm(x, w):
    m, d = x.shape
    tm = _row_tile(m, 512)
    return pl.pallas_call(
        _rmsnorm_kernel,
        grid=(m // tm,),
        in_specs=[pl.BlockSpec((tm, d), lambda i: (i, 0)),
                  pl.BlockSpec((1, d), lambda i: (0, 0))],
        out_specs=pl.BlockSpec((tm, d), lambda i: (i, 0)),
        out_shape=jax.ShapeDtypeStruct((m, d), BF16),
        compiler_params=_params("parallel"),
        name="rmsnorm",
    )(x, w.reshape(1, d))


def _cast_kernel(w_ref, o_ref):
    o_ref[...] = w_ref[...].astype(o_ref.dtype)


def _layer_bf16(w, layer):
    _, rows, cols = w.shape
    tr, tc = min(rows, 512), min(cols, 2048)
    assert rows % tr == 0 and cols % tc == 0
    return pl.pallas_call(
        _cast_kernel,
        grid=(rows // tr, cols // tc),
        in_specs=[pl.BlockSpec((None, tr, tc), lambda i, j: (layer, i, j))],
        out_specs=pl.BlockSpec((tr, tc), lambda i, j: (i, j)),
        out_shape=jax.ShapeDtypeStruct((rows, cols), BF16),
        compiler_params=_params("parallel", "parallel"),
        name="cast_bf16",
    )(w)


def _norm_rope_inplace(t_ref, nw_ref, cos_ref, sin_ref, head_major=False):
    rows = cos_ref.shape[0]
    n_heads = t_ref.shape[0] // rows if head_major else t_ref.shape[1] // HEAD_DIM
    rc = min(rows, QK_ROW_CHUNK)
    half = HEAD_DIM // 2
    nw = nw_ref[...]
    nw_swapped = pltpu.roll(nw, half, axis=1)

    def chunk(c, carry):
        r0 = pl.multiple_of(c * rc, rc)
        cs = cos_ref[pl.ds(r0, rc), :] * nw
        ss = sin_ref[pl.ds(r0, rc), :] * nw_swapped
        for h in range(n_heads):
            idx = (pl.ds(h * rows + r0, rc), slice(None)) if head_major else (
                pl.ds(r0, rc), slice(h * HEAD_DIM, (h + 1) * HEAD_DIM))
            a = t_ref[idx]
            r = lax.rsqrt(jnp.mean(a * a, axis=-1, keepdims=True) + NORM_EPS)
            t_ref[idx] = (a * cs + pltpu.roll(a, half, axis=1) * ss) * r
        return carry

    lax.fori_loop(0, rows // rc, chunk, 0)


TAIL_SLABS = 6


def _proj_tail_kernel(x_ref, w_ref, gate_ref, u_ref):
    j = pl.program_id(0)
    acc = jnp.dot(x_ref[...], w_ref[...], preferred_element_type=F32)

    @pl.when((j == 0) | (j == 2))
    def _():
        gate_ref[...] = (acc * _sigmoid(acc)).astype(gate_ref.dtype)

    @pl.when(j == 1)
    def _():
        u_ref[...] = acc

    @pl.when(j > 2)
    def _():
        gate_ref[...] = _sigmoid(acc).astype(gate_ref.dtype)


def _proj_tail(xn, w):
    m, d = xn.shape
    aw = d // 2
    tm = _row_tile(m, 1024)
    n_i = m // tm
    col0 = 3 * N_DIL
    assert w.shape[1] == (col0 + TAIL_SLABS + 1) * aw

    def gate_map(j, i):
        slab = jnp.where(j == 0, 0, j - 1)
        return (slab, jnp.where(j == 1, n_i - 1, i), 0)

    def u_map(j, i):
        return (jnp.where(j < 1, 0, jnp.where(j == 1, i, n_i - 1)), 0)

    return pl.pallas_call(
        _proj_tail_kernel,
        grid=(TAIL_SLABS + 1, n_i),
        in_specs=[pl.BlockSpec((tm, d), lambda j, i: (i, 0)),
                  pl.BlockSpec((d, aw), lambda j, i: (0, col0 + j))],
        out_specs=[pl.BlockSpec((None, tm, aw), gate_map), pl.BlockSpec((tm, aw), u_map)],
        out_shape=[jax.ShapeDtypeStruct((TAIL_SLABS, m, aw), BF16), jax.ShapeDtypeStruct((m, aw), F32)],
        compiler_params=_params("arbitrary", "arbitrary"),
        name="proj_tail",
    )(xn, w)


def _norm_rope_heads(hm_ref, ssq_ref, nw_ref, cos_ref, sin_ref):
    rows = cos_ref.shape[0]
    rc = min(rows, QK_ROW_CHUNK)
    half = HEAD_DIM // 2
    nw = nw_ref[...]
    nw_swapped = pltpu.roll(nw, half, axis=1)
    ones = jnp.ones((HEAD_DIM, HEAD_DIM), BF16)
    for h in range(hm_ref.shape[0] // rows):
        a_all = hm_ref[h * rows:(h + 1) * rows, :]
        ssq_ref[...] = jnp.dot((a_all * a_all).astype(BF16), ones, preferred_element_type=F32)

        def chunk(c, carry):
            r0 = pl.multiple_of(c * rc, rc)
            cs = cos_ref[pl.ds(r0, rc), :] * nw
            ss = sin_ref[pl.ds(r0, rc), :] * nw_swapped
            a = hm_ref[pl.ds(h * rows + r0, rc), :]
            r = lax.rsqrt(ssq_ref[pl.ds(r0, rc), :] * (1.0 / HEAD_DIM) + NORM_EPS)
            hm_ref[pl.ds(h * rows + r0, rc), :] = (a * cs + pltpu.roll(a, half, axis=1) * ss) * r
            return carry

        lax.fori_loop(0, rows // rc, chunk, 0, unroll=True)


def _project_dilated(x, w_ref, nw_ref, cos_ref, sin_ref, hm_ref, ssq_ref, d_ref, dil):
    tm = x.shape[0]
    n_heads = w_ref.shape[1] // HEAD_DIM
    per = tm // dil
    acc = jnp.dot(x, w_ref[...], preferred_element_type=F32)
    for h in range(n_heads):
        hm_ref[h * tm:(h + 1) * tm, :] = acc[:, h * HEAD_DIM:(h + 1) * HEAD_DIM]
    if nw_ref is not None:
        _norm_rope_heads(hm_ref, ssq_ref, nw_ref, cos_ref, sin_ref)
    for h in range(n_heads):
        for r in range(dil):
            rows = pl.ds(h * tm + r, per, stride=dil) if dil > 1 else pl.ds(h * tm, per)
            d_ref[0, r, :, h * HEAD_DIM:(h + 1) * HEAD_DIM] = hm_ref[rows, :].astype(d_ref.dtype)


def _q_kernel(x_ref, w_ref, nw_ref, cos_ref, sin_ref, *rest):
    qd_refs, (hm_ref, ssq_ref) = rest[:N_DIL], rest[N_DIL:]
    for g in range(N_DIL):
        @pl.when(pl.program_id(0) == g)
        def _(g=g):
            _project_dilated(x_ref[...], w_ref, nw_ref, cos_ref, sin_ref, hm_ref, ssq_ref, qd_refs[g],
                             DILATIONS[g])


def _q_all(xn, w, nb, seq, q_norm_w, cos, sin):
    m, d = xn.shape
    aw = d // 2
    tm = _row_tile(seq, 1024)
    bps = seq // tm
    out_specs, out_shape = [], []
    for g in range(N_DIL):
        dil = DILATIONS[g]
        assert (tm // dil) % 16 == 0

        def q_map(gi, b, i, g=g):
            return (jnp.where(gi < g, 0, jnp.where(gi == g, b, nb - 1)), 0,
                    jnp.where(gi < g, 0, jnp.where(gi == g, i, bps - 1)), 0)

        out_specs.append(pl.BlockSpec((1, dil, tm // dil, aw), q_map))
        out_shape.append(jax.ShapeDtypeStruct((nb, dil, seq // dil, aw), BF16))
    return pl.pallas_call(
        _q_kernel,
        grid=(N_DIL, nb, bps),
        in_specs=[pl.BlockSpec((tm, d), lambda g, b, i: (b * bps + i, 0)),
                  pl.BlockSpec((d, aw), lambda g, b, i: (0, g)),
                  pl.BlockSpec((1, HEAD_DIM), lambda g, b, i: (0, 0)),
                  pl.BlockSpec((tm, HEAD_DIM), lambda g, b, i: (i, 0)),
                  pl.BlockSpec((tm, HEAD_DIM), lambda g, b, i: (i, 0))],
        out_specs=out_specs,
        out_shape=out_shape,
        scratch_shapes=[pltpu.VMEM((aw // HEAD_DIM * tm, HEAD_DIM), F32), pltpu.VMEM((tm, HEAD_DIM), F32)],
        compiler_params=_params("arbitrary", "arbitrary", "arbitrary"),
        name="q_all",
    )(xn, w, q_norm_w.reshape(1, HEAD_DIM), cos, sin)


def _kv_kernel(x_ref, wk_ref, wv_ref, nw_ref, cos_ref, sin_ref, *rest, dil, first_kept):
    kd_ref, vd_ref, kv_ref, hk_ref, hv_ref, ssq_ref = rest[-6:]
    half_idx = pl.program_id(2)
    tm = x_ref.shape[0]
    n_heads = wk_ref.shape[1] // HEAD_DIM

    @pl.when(half_idx == 0)
    def _():
        x = x_ref[...]
        _project_dilated(x, wk_ref, nw_ref, cos_ref, sin_ref, hk_ref, ssq_ref, kd_ref, dil)
        _project_dilated(x, wv_ref, None, None, None, hv_ref, None, vd_ref, dil)

    kb = kv_ref.shape[1]
    half = tm // 2

    @pl.when(2 * pl.program_id(1) + half_idx >= first_kept)
    def _():
        base = (half_idx + 1) * half - kb

        def token(t, carry):
            rows = pl.ds(base + t, n_heads, stride=tm)
            kv_ref[0, t, 0] = hk_ref[rows, :]
            kv_ref[0, t, 1] = hv_ref[rows, :]
            return carry

        lax.fori_loop(0, kb, token, 0, unroll=8)


def _kv(xn, w, g, nb, seq, k_norm_w, cos, sin, layer, depth, kv_prev):
    m, d = xn.shape
    aw = d // 2
    hg = aw // HEAD_DIM
    dil = DILATIONS[g]
    keep = min(WINDOWS[g], seq)
    tm = _row_tile(seq, 1024)
    per = tm // dil
    half = tm // 2
    kb = min(keep, half)
    bps = seq // tm
    n_half = 2 * bps
    first_kept = (seq - keep) // half if keep >= half else n_half - 1
    assert per % 16 == 0 and keep % kb == 0 and (seq - keep) % kb == 0
    once = pl.Buffered(1)
    nw = lambda a: a.reshape(1, HEAD_DIM)
    dshape = jax.ShapeDtypeStruct((nb, dil, seq // dil, aw), BF16)
    staging = pltpu.VMEM((hg * tm, HEAD_DIM), F32)
    sumsq = pltpu.VMEM((tm, HEAD_DIM), F32)

    wspec = lambda col: pl.BlockSpec((d, aw), lambda b, i, h: (0, col), pipeline_mode=once)
    tspec = pl.BlockSpec((tm, HEAD_DIM), lambda b, i, h: (i, 0))
    dspec = pl.BlockSpec((1, dil, per, aw), lambda b, i, h: (b, 0, i, 0))
    kv_shape = jax.ShapeDtypeStruct((depth, nb, keep, 2, hg, HEAD_DIM), F32)
    if kv_prev is None:
        kv_prev = jnp.zeros(kv_shape.shape, kv_shape.dtype)
    kd, vd, kv = pl.pallas_call(
        functools.partial(_kv_kernel, dil=dil, first_kept=first_kept),
        grid=(nb, bps, 2),
        in_specs=[pl.BlockSpec((tm, d), lambda b, i, h: (b * bps + i, 0)),
                  wspec(N_DIL + g), wspec(2 * N_DIL + g),
                  pl.BlockSpec((1, HEAD_DIM), lambda b, i, h: (0, 0)), tspec, tspec,
                  pl.BlockSpec(memory_space=pl.ANY)],
        out_specs=[dspec, dspec,
                   pl.BlockSpec((None, 1, kb, 2, hg, HEAD_DIM),
                                lambda b, i, h: (layer, b, jnp.maximum(2 * i + h - first_kept, 0), 0, 0, 0))],
        out_shape=[dshape, dshape, kv_shape],
        scratch_shapes=[staging, staging, sumsq],
        input_output_aliases={6: 2},
        compiler_params=_params("parallel", "arbitrary", "arbitrary"),
        name="kv_d%d" % dil,
    )(xn, w, w, nw(k_norm_w), cos, sin, kv_prev)
    return kd, vd, kv


def _rope_tables(pos):
    half = HEAD_DIM // 2
    inv_freq = jnp.power(ROPE_THETA, -jnp.arange(half, dtype=F32) * (2.0 / HEAD_DIM))
    ang = pos[:, None] * inv_freq[None, :]
    cos, sin = jnp.cos(ang), jnp.sin(ang)
    return jnp.concatenate([cos, cos], axis=-1), jnp.concatenate([-sin, sin], axis=-1)


def _attn_prompt_kernel(q_ref, kp_ref, kc_ref, vp_ref, vc_ref, o_ref, lse_ref):
    not_first = pl.program_id(2) > 0
    a = lax.broadcasted_iota(jnp.int32, (SPAN, 2 * SPAN), 0)
    c = lax.broadcasted_iota(jnp.int32, (SPAN, 2 * SPAN), 1)
    band = (c >= a) & (c <= a + SPAN)
    band_first = band & ((c >= SPAN) | not_first)
    lane = lax.broadcasted_iota(jnp.int32, (SPAN, HEAD_DIM), 1)
    scale = HEAD_DIM ** -0.5
    for j in range(q_ref.shape[2] // SPAN):
        rows = slice(j * SPAN, (j + 1) * SPAN)
        lse_tile = jnp.zeros((SPAN, HEAD_DIM), F32)
        for h in range(q_ref.shape[3] // HEAD_DIM):
            sl = slice(h * HEAD_DIM, (h + 1) * HEAD_DIM)
            q = q_ref[0, 0, rows, sl]
            if j == 0:
                k = jnp.concatenate([kp_ref[0, 0, :, sl], kc_ref[0, 0, rows, sl]], axis=0)
                v = jnp.concatenate([vp_ref[0, 0, :, sl], vc_ref[0, 0, rows, sl]], axis=0)
            else:
                k = kc_ref[0, 0, (j - 1) * SPAN:(j + 1) * SPAN, sl]
                v = vc_ref[0, 0, (j - 1) * SPAN:(j + 1) * SPAN, sl]
            s = lax.dot_general(q, k, (((1,), (1,)), ((), ())), preferred_element_type=F32) * scale
            s = jnp.where(band_first if j == 0 else band, s, NEG_BIG)
            m = jnp.max(s, axis=-1, keepdims=True)
            p = jnp.exp(s - m)
            l = jnp.sum(p, axis=-1, keepdims=True)
            o_ref[0, 0, rows, sl] = jnp.dot(p.astype(BF16), v, preferred_element_type=F32) / l
            lse_tile = jnp.where(lane == h, m + jnp.log(l), lse_tile)
        lse_ref[0, 0, rows, :] = lse_tile


def _attn_prompt(q, k, v):
    nb, dil, length, aw = q.shape
    n_blk = length // SPAN
    qb = min(4, n_blk)
    assert length % SPAN == 0 and n_blk % qb == 0 and aw // HEAD_DIM <= HEAD_DIM
    cur = pl.BlockSpec((1, 1, qb * SPAN, aw), lambda b, r, n: (b, r, n, 0))
    prev = pl.BlockSpec((1, 1, SPAN, aw), lambda b, r, n: (b, r, jnp.maximum(qb * n - 1, 0), 0))
    return pl.pallas_call(
        _attn_prompt_kernel,
        grid=(nb, dil, n_blk // qb),
        in_specs=[cur, prev, cur, prev, cur],
        out_specs=[cur, pl.BlockSpec((1, 1, qb * SPAN, HEAD_DIM), lambda b, r, n: (b, r, n, 0))],
        out_shape=[jax.ShapeDtypeStruct(q.shape, F32),
                   jax.ShapeDtypeStruct((nb, dil, length, HEAD_DIM), F32)],
        compiler_params=_params("parallel", "parallel", "arbitrary"),
        name="attn_prompt_d%d" % dil,
    )(q, k, k, v, v)


def _attn_sample_kernel(q_ref, kn_ref, vn_ref, c1_ref, c4_ref, c16_ref, o_ref, lse_ref):
    n_new, n_heads, _ = q_ref.shape
    hg = n_heads // N_DIL
    scale = HEAD_DIM ** -0.5
    caches = (c1_ref, c4_ref, c16_ref)
    for g in range(N_DIL):
        hs = slice(g * hg, (g + 1) * hg)
        for t in range(n_new):
            q = q_ref[t, hs, :]
            if DILATIONS[g] == 1:
                kc, vc = caches[g][:, 0], caches[g][:, 1]
                rows = lax.broadcasted_iota(jnp.int32, (kc.shape[0], hg, 1), 0)
                s = jnp.sum(kc * q[None], axis=-1, keepdims=True) * scale
                s = jnp.where(rows >= t, s, NEG_BIG)
                new = range(t + 1)
            else:
                kc, vc = caches[g][:, t, 0], caches[g][:, t, 1]
                s = jnp.sum(kc * q[None], axis=-1, keepdims=True) * scale
                new = (t,)
            s_new = [jnp.sum(kn_ref[u, hs, :] * q, axis=-1, keepdims=True) * scale for u in new]
            m = jnp.max(s, axis=0)
            for sn in s_new:
                m = jnp.maximum(m, sn)
            p = jnp.exp(s - m[None])
            l = jnp.sum(p, axis=0)
            acc = jnp.sum(p * vc, axis=0)
            for u, sn in zip(new, s_new):
                pn = jnp.exp(sn - m)
                l = l + pn
                acc = acc + pn * vn_ref[u, hs, :]
            o_ref[t, hs, :] = acc / l
            lse_ref[t, hs, :] = jnp.broadcast_to(m + jnp.log(l), (hg, HEAD_DIM))


def _attn_sample(q, kn, vn, caches, layer):
    nb, n_new, n_heads, _ = q.shape
    hg = n_heads // N_DIL
    views, specs = [], []
    for g in range(N_DIL):
        dil = DILATIONS[g]
        depth, _, buf = caches[g].shape[:3]
        assert buf == WINDOWS[g] and (dil == 1 or dil % n_new == 0)
        if dil == 1:
            views.append(caches[g])
            specs.append(pl.BlockSpec((None, None, buf, 2, hg, HEAD_DIM),
                                      lambda b: (layer, b, 0, 0, 0, 0)))
        else:
            views.append(caches[g].reshape(depth, nb, SPAN, dil, 2, hg, HEAD_DIM))
            specs.append(pl.BlockSpec((None, None, SPAN, n_new, 2, hg, HEAD_DIM),
                                      lambda b: (layer, b, 0, 0, 0, 0, 0)))
    tok = pl.BlockSpec((None, n_new, n_heads, HEAD_DIM), lambda b: (b, 0, 0, 0))
    return pl.pallas_call(
        _attn_sample_kernel,
        grid=(nb,),
        in_specs=[tok, tok, tok] + specs,
        out_specs=[tok, tok],
        out_shape=[jax.ShapeDtypeStruct(q.shape, F32)] * 2,
        compiler_params=_params("parallel"),
        name="attn_sample",
    )(q, kn, vn, *views)


def _s5_params_kernel(lre_ref, lim_ref, ldt_ref, bre_ref, bim_ref, ptab_ref, bbre_ref, bbim_ref):
    dt = jnp.exp(ldt_ref[...])
    lre, lim = lre_ref[...], lim_ref[...]
    xr, xi = lre * dt, lim * dt
    mag = jnp.exp(xr)
    ar, ai = mag * jnp.cos(xi), mag * jnp.sin(xi)
    shape = (ROW_TILE, xr.shape[1])
    abr, abi = jnp.broadcast_to(ar, shape), jnp.broadcast_to(ai, shape)
    ptab_ref[0, 0] = abr
    ptab_ref[1, 0] = abi

    def next_power(i, carry):
        pr, pi = carry
        pr, pi = pr * abr - pi * abi, pr * abi + pi * abr
        ptab_ref[0, i] = pr
        ptab_ref[1, i] = pi
        return pr, pi

    lax.fori_loop(1, ptab_ref.shape[1], next_power, (abr, abi))
    nr, ni = ar - 1.0, ai
    den = lre * lre + lim * lim
    fr = (nr * lre + ni * lim) / den
    fi = (ni * lre - nr * lim) / den
    bre, bim = bre_ref[...], bim_ref[...]
    bbre_ref[...] = fr * bre - fi * bim
    bbim_ref[...] = fr * bim + fi * bre


def _s5_params(lam_re, lam_im, log_dt, b_re, b_im, c_re, c_im, seg_len):
    n_grp, n_st = lam_re.shape
    n_ch = b_re.shape[2]
    n = n_grp * n_st
    gc = GROUPS_PER_CHUNK
    n_chunk = n_grp // gc
    sw = gc * n_st
    to_lanes = lambda b: b.transpose(2, 0, 1).reshape(n_ch, n)
    lane = lambda rows: pl.BlockSpec((rows, sw), lambda j: (0, j))
    ptab, bb_re, bb_im = pl.pallas_call(
        _s5_params_kernel,
        grid=(n_chunk,),
        in_specs=[lane(1)] * 3 + [lane(n_ch)] * 2,
        out_specs=[pl.BlockSpec((2, seg_len, ROW_TILE, sw), lambda j: (0, 0, 0, j)), lane(n_ch), lane(n_ch)],
        out_shape=[jax.ShapeDtypeStruct((2, seg_len, ROW_TILE, n), F32),
                   jax.ShapeDtypeStruct((n_ch, n), F32), jax.ShapeDtypeStruct((n_ch, n), F32)],
        compiler_params=_params("parallel"),
        name="s5_params",
    )(lam_re.reshape(1, n), lam_im.reshape(1, n),
      jnp.broadcast_to(log_dt[:, None], (n_grp, n_st)).reshape(1, n), to_lanes(b_re), to_lanes(b_im))

    eye = jnp.eye(gc, dtype=F32)

    def b_blocks(bb):
        bb = bb.reshape(n_ch, n_chunk, gc, n_st)
        blk = jnp.einsum("cjgp,gh->jgchp", bb, eye)
        return blk.reshape(n_chunk, gc * n_ch, gc * n_st).astype(BF16)

    def c_blocks(cc):
        cc = cc.reshape(n_chunk, gc, n_ch, n_st)
        blk = jnp.einsum("jgcp,gh->jgphc", cc, eye)
        return blk.reshape(n_chunk, gc * n_st, gc * n_ch).astype(BF16)

    b_cat = jnp.concatenate([b_blocks(bb_re), b_blocks(bb_im)], axis=2)
    c_cat = jnp.concatenate([c_blocks(c_re), c_blocks(c_im)], axis=1)
    return ptab, b_cat, c_cat


def _cmul_add(xr, xi, ar, ai, br, bi):
    return xr + ar * br - ai * bi, xi + ar * bi + ai * br


def _shift_plan(cache_shape, new_shape, n_steps):
    lb_n, buf = cache_shape[:2]
    keep = buf - new_shape[1]
    row_bytes = 4 * math.prod(cache_shape[2:])
    for per_row in range(1, keep + 1):
        n_jobs = lb_n * per_row
        if keep % per_row == 0 and keep // per_row * row_bytes <= KV_COPY_BYTES and n_steps % n_jobs == 0:
            return keep // per_row, per_row, n_steps // n_jobs
    raise ValueError("window shift does not fit the grid")


def _shift_lag(period):
    return 1 if period == 2 else 2


def _shift_slots(period):
    return 4 if period == 1 else 2


def _window_shift_step(step, n_steps, caches, news, outs, rings, nbufs, in_sem, out_sem, new_sem):
    last = step == n_steps - 1
    for g in range(N_DIL):
        cache, out, ring = caches[g], outs[g], rings[g]
        n_new = news[g].shape[1]
        keep = cache.shape[1] - n_new
        rows, per_row, period = _shift_plan(cache.shape, news[g].shape, n_steps)
        n_jobs = cache.shape[0] * per_row
        slots = ring.shape[0]
        lag = _shift_lag(period)
        assert n_jobs >= slots and (period == 1 or lag < period)

        def load(k, cache=cache, ring=ring, g=g, rows=rows, per_row=per_row, n_new=n_new, slots=slots):
            src = cache.at[lax.div(k, per_row), pl.ds(n_new + lax.rem(k, per_row) * rows, rows)]
            return pltpu.make_async_copy(src, ring.at[lax.rem(k, slots)], in_sem.at[g, lax.rem(k, slots)])

        def store(k, out=out, ring=ring, g=g, rows=rows, per_row=per_row, slots=slots):
            dst = out.at[lax.div(k, per_row), pl.ds(lax.rem(k, per_row) * rows, rows)]
            return pltpu.make_async_copy(ring.at[lax.rem(k, slots)], dst, out_sem.at[g, lax.rem(k, slots)])

        def issue(k, load=load, store=store, slots=slots):
            @pl.when(k >= slots)
            def _():
                store(k - slots).wait()
            load(k).start()

        def forward(k, load=load, store=store):
            load(k).wait()
            store(k).start()

        def drain(k, store=store, slots=slots):
            for back in range(slots - 1, -1, -1):
                store(k - back).wait()

        if period == 1:
            issue(step)

            @pl.when(step >= lag)
            def _(forward=forward, lag=lag):
                forward(step - lag)

            @pl.when(last)
            def _(forward=forward, drain=drain, lag=lag):
                for back in range(lag - 1, -1, -1):
                    forward(step - back)
                drain(step)
        else:
            k = lax.div(step, period)
            phase = lax.rem(step, period)

            @pl.when(phase == 0)
            def _(issue=issue, k=k):
                issue(k)

            @pl.when(phase == lag)
            def _(forward=forward, k=k):
                forward(k)

            @pl.when(last)
            def _(drain=drain, k=k):
                drain(k)

        new_in = pltpu.make_async_copy(news[g], nbufs[g], new_sem.at[g])
        new_out = pltpu.make_async_copy(nbufs[g], out.at[:, pl.ds(keep, n_new)], new_sem.at[N_DIL + g])

        @pl.when(step == 0)
        def _(new_in=new_in, new_out=new_out):
            new_in.start()
            new_in.wait()
            new_out.start()

        @pl.when(last)
        def _(new_out=new_out):
            new_out.wait()


def _s5_scan_kernel(u_ref, b_ref, ptab_ref, c_ref, d_ref, *rest, with_shift):
    if with_shift:
        n = N_DIL
        caches, news = rest[:n], rest[n:2 * n]
        y_ref, hfin_ref = rest[2 * n:2 * n + 2]
        outs = rest[2 * n + 2:3 * n + 2]
        up_ref, s_ref, hb_ref, end_ref = rest[3 * n + 2:3 * n + 6]
        rings, nbufs = rest[3 * n + 6:4 * n + 6], rest[4 * n + 6:5 * n + 6]
        in_sem, out_sem, new_sem = rest[5 * n + 6:]
        _, n_j, n_t = with_shift
        step = (pl.program_id(0) * n_j + pl.program_id(1)) * n_t + pl.program_id(2)
        _window_shift_step(step, math.prod(with_shift), caches, news, outs, rings, nbufs,
                           in_sem, out_sem, new_sem)
    else:
        y_ref, hfin_ref, up_ref, s_ref, hb_ref, end_ref = rest
    tc, sw2 = s_ref.shape
    sw = sw2 // 2
    seg = tc // ROW_TILE
    re, im = slice(0, sw), slice(sw, sw2)

    @pl.when(pl.program_id(2) == 0)
    def _():
        end_ref[...] = jnp.zeros_like(end_ref)

    for i in range(seg):
        up_ref[i * ROW_TILE:(i + 1) * ROW_TILE, :] = u_ref[0, pl.ds(i, ROW_TILE, stride=seg), :]
    s_ref[...] = jnp.dot(up_ref[...].astype(BF16), b_ref[0], preferred_element_type=F32)

    ar, ai = ptab_ref[0, 0], ptab_ref[1, 0]

    def local_step(i, carry):
        hr, hi = carry
        r0 = pl.multiple_of(i * ROW_TILE, ROW_TILE)
        hr, hi = _cmul_add(s_ref[pl.ds(r0, ROW_TILE), re], s_ref[pl.ds(r0, ROW_TILE), im], ar, ai, hr, hi)
        s_ref[pl.ds(r0, ROW_TILE), re] = hr
        s_ref[pl.ds(r0, ROW_TILE), im] = hi
        return hr, hi

    zero = jnp.zeros((ROW_TILE, sw), F32)
    er, ei = lax.fori_loop(0, seg, local_step, (zero, zero), unroll=8)

    row = lax.broadcasted_iota(jnp.int32, (ROW_TILE, sw), 0)
    cr = jnp.where(row == 0, pltpu.roll(end_ref[0], 1, axis=0), pltpu.roll(er, 1, axis=0))
    ci = jnp.where(row == 0, pltpu.roll(end_ref[1], 1, axis=0), pltpu.roll(ei, 1, axis=0))
    mr, mi = ptab_ref[0, seg - 1], ptab_ref[1, seg - 1]
    wr, wi = mr, mi
    for sh in (1, 2, 4):
        gr, gi = jnp.where(row >= sh, wr, 0.0), jnp.where(row >= sh, wi, 0.0)
        cr, ci = _cmul_add(cr, ci, gr, gi, pltpu.roll(cr, sh, axis=0), pltpu.roll(ci, sh, axis=0))
        wr, wi = wr * wr - wi * wi, 2.0 * wr * wi
    fr, fi = _cmul_add(er, ei, mr, mi, cr, ci)
    end_ref[0] = fr
    end_ref[1] = fi
    hfin_ref[0, 0] = fr
    hfin_ref[0, 1] = fi

    def fix_up(k, carry):
        r0 = pl.multiple_of(k * 2 * ROW_TILE, 2 * ROW_TILE)
        hs, ns = [], []
        for half in range(2):
            i = 2 * k + half
            rows = pl.ds(r0 + half * ROW_TILE, ROW_TILE)
            hr, hi = _cmul_add(s_ref[rows, re], s_ref[rows, im], ptab_ref[0, i], ptab_ref[1, i], cr, ci)
            hs.append(hr)
            ns.append(-hi)
        hb_ref[pl.ds(r0, 2 * ROW_TILE), re] = jnp.concatenate(hs, axis=0).astype(BF16)
        hb_ref[pl.ds(r0, 2 * ROW_TILE), im] = jnp.concatenate(ns, axis=0).astype(BF16)
        return carry

    lax.fori_loop(0, seg // 2, fix_up, 0, unroll=4)
    yp = jnp.dot(hb_ref[...], c_ref[0], preferred_element_type=F32) + d_ref[...] * up_ref[...]
    for i in range(seg):
        y_ref[0, pl.ds(i, ROW_TILE, stride=seg), :] = yp[i * ROW_TILE:(i + 1) * ROW_TILE, :]


def _s5_chunk(seq):
    return min(seq, 1024)


def _flat_window(a):
    return a.reshape(a.shape[0] * a.shape[1], a.shape[2], a.shape[3] * a.shape[4], a.shape[5])


def _s5_prompt(u, prm, ssm_d, nb, seq, shift=None):
    ptab, b_cat, c_cat = prm
    n_chunk, cw, sw2 = b_cat.shape
    sw = sw2 // 2
    width = u.shape[1]
    tc = _s5_chunk(seq)
    seg = tc // ROW_TILE
    assert seq % tc == 0 and seg % 2 == 0 and ptab.shape[1] == seg
    grid = (nb, n_chunk, seq // tc)
    in_specs = [pl.BlockSpec((1, tc, cw), lambda b, j, t: (b, t, j)),
                pl.BlockSpec((1, cw, sw2), lambda b, j, t: (j, 0, 0)),
                pl.BlockSpec((2, seg, ROW_TILE, sw), lambda b, j, t: (0, 0, 0, j)),
                pl.BlockSpec((1, sw2, cw), lambda b, j, t: (j, 0, 0)),
                pl.BlockSpec((1, cw), lambda b, j, t: (0, j))]
    out_specs = [pl.BlockSpec((1, tc, cw), lambda b, j, t: (b, t, j)),
                 pl.BlockSpec((1, 2, ROW_TILE, sw), lambda b, j, t: (b, 0, 0, j))]
    out_shape = [jax.ShapeDtypeStruct((nb, seq, width), F32),
                 jax.ShapeDtypeStruct((nb, 2, ROW_TILE, n_chunk * sw), F32)]
    scratch = [pltpu.VMEM((tc, cw), F32), pltpu.VMEM((tc, sw2), F32),
               pltpu.VMEM((tc, sw2), BF16), pltpu.VMEM((2, ROW_TILE, sw), F32)]
    args = [u.reshape(nb, seq, width), b_cat, ptab, c_cat, ssm_d.reshape(1, width)]
    if shift is not None:
        caches, news = shift
        cf, nf = [_flat_window(c) for c in caches], [_flat_window(w) for w in news]
        hbm = pl.BlockSpec(memory_space=pl.ANY)
        in_specs += [hbm] * (2 * N_DIL)
        out_specs += [hbm] * N_DIL
        out_shape += [jax.ShapeDtypeStruct(c.shape, c.dtype) for c in cf]
        args += cf + nf
        max_slots = 0
        for c, w in zip(cf, nf):
            rows, _, period = _shift_plan(c.shape, w.shape, math.prod(grid))
            scratch.append(pltpu.VMEM((_shift_slots(period), rows) + c.shape[2:], c.dtype))
            max_slots = max(max_slots, _shift_slots(period))
        scratch += [pltpu.VMEM(w.shape, w.dtype) for w in nf]
        scratch += [pltpu.SemaphoreType.DMA((N_DIL, max_slots)), pltpu.SemaphoreType.DMA((N_DIL, max_slots)),
                    pltpu.SemaphoreType.DMA((2 * N_DIL,))]
    res = pl.pallas_call(
        functools.partial(_s5_scan_kernel, with_shift=grid if shift is not None else None),
        grid=grid,
        in_specs=in_specs,
        out_specs=out_specs,
        out_shape=out_shape,
        scratch_shapes=scratch,
        compiler_params=_params("arbitrary", "arbitrary", "arbitrary") if shift is not None
        else _params("parallel", "parallel", "arbitrary"),
        name="s5_scan",
    )(*args)
    y, hfin = res[0], res[1]
    last = ROW_TILE - 1
    shifted = None if shift is None else [o.reshape(c.shape) for o, c in zip(res[2:], shift[0])]
    return y.reshape(nb * seq, width), hfin[:, 0, last], hfin[:, 1, last], shifted


def _s5_sample_kernel(u_ref, h0re_ref, h0im_ref, b_ref, ptab_ref, c_ref, d_ref, y_ref, hre_ref, him_ref):
    sw = h0re_ref.shape[1]
    ar, ai = ptab_ref[0, 0, 0:1, :], ptab_ref[1, 0, 0:1, :]
    hr, hi = h0re_ref[...], h0im_ref[...]
    for t in range(u_ref.shape[0]):
        u = u_ref[t]
        bu = jnp.dot(u.astype(BF16), b_ref[0], preferred_element_type=F32)
        hr, hi = _cmul_add(bu[:, :sw], bu[:, sw:], ar, ai, hr, hi)
        hb = jnp.concatenate([hr, -hi], axis=1).astype(BF16)
        y_ref[t] = jnp.dot(hb, c_ref[0], preferred_element_type=F32) + d_ref[...] * u
    hre_ref[...] = hr
    him_ref[...] = hi


def _s5_sample(u, h0re, h0im, prm, ssm_d):
    ptab, b_cat, c_cat = prm
    n_chunk, cw, sw2 = b_cat.shape
    sw = sw2 // 2
    n_new, nb, width = u.shape
    return pl.pallas_call(
        _s5_sample_kernel,
        grid=(n_chunk,),
        in_specs=[pl.BlockSpec((n_new, nb, cw), lambda j: (0, 0, j)),
                  pl.BlockSpec((nb, sw), lambda j: (0, j)),
                  pl.BlockSpec((nb, sw), lambda j: (0, j)),
                  pl.BlockSpec((1, cw, sw2), lambda j: (j, 0, 0)),
                  pl.BlockSpec((2, 1, ROW_TILE, sw), lambda j: (0, 0, 0, j)),
                  pl.BlockSpec((1, sw2, cw), lambda j: (j, 0, 0)),
                  pl.BlockSpec((1, cw), lambda j: (0, j))],
        out_specs=[pl.BlockSpec((n_new, nb, cw), lambda j: (0, 0, j)),
                   pl.BlockSpec((nb, sw), lambda j: (0, j)),
                   pl.BlockSpec((nb, sw), lambda j: (0, j))],
        out_shape=[jax.ShapeDtypeStruct((n_new, nb, width), F32),
                   jax.ShapeDtypeStruct((nb, n_chunk * sw), F32),
                   jax.ShapeDtypeStruct((nb, n_chunk * sw), F32)],
        compiler_params=_params("parallel"),
        name="s5_sample",
    )(u, h0re, h0im, b_cat, ptab, c_cat, ssm_d.reshape(1, width))


def _glu_kernel(y_ref, g_ref, w_ref, b_ref, o_ref):
    y = y_ref[...]
    s = 0.5 * y * (1.0 + jnp.tanh(math.sqrt(2.0 / math.pi) * (y + 0.044715 * (y * y * y))))
    z = jnp.dot(s.astype(BF16), w_ref[...], preferred_element_type=F32) + b_ref[...]
    o_ref[...] = (s * _sigmoid(z) * g_ref[...].astype(F32)).astype(o_ref.dtype)


def _glu(y, tail, w, b):
    m, width = y.shape
    tm = _row_tile(m, 512)
    row = pl.BlockSpec((tm, width), lambda i: (i, 0))
    return pl.pallas_call(
        _glu_kernel,
        grid=(m // tm,),
        in_specs=[row, pl.BlockSpec((None, tm, width), lambda i: (1, i, 0)),
                  pl.BlockSpec((width, width), lambda i: (0, 0)),
                  pl.BlockSpec((1, width), lambda i: (0, 0))],
        out_specs=row,
        out_shape=jax.ShapeDtypeStruct((m, width), BF16),
        compiler_params=_params("parallel"),
        name="glu",
    )(y, tail, w, b.reshape(1, width))


def _merge_kernel(*refs):
    o_refs, l_refs = refs[:N_DIL], refs[N_DIL:2 * N_DIL]
    ga_ref, ys_ref, ma_ref, ms_ref, wa_ref, ws_ref, out_ref = refs[2 * N_DIL:2 * N_DIL + 7]
    a_ref, otok_ref, ltok_ref = refs[2 * N_DIL + 7:]

    def token_order(ref, cols, stage):
        dil, per = ref.shape[1], ref.shape[2]
        if dil == 1:
            return ref[0, 0, :, cols]
        for r in range(dil):
            stage[pl.ds(r, per, stride=dil), :] = ref[0, r, :, cols]
        return stage[...]

    ls = [token_order(l_refs[g], slice(None), ltok_ref.at[g]) for g in range(N_DIL)]
    mx = jnp.maximum(jnp.maximum(ls[0], ls[1]), ls[2])
    es = [jnp.exp(l - mx) for l in ls]
    den = es[0] + es[1] + es[2]
    wts = [e / den for e in es]
    tm = a_ref.shape[0]
    for h in range(a_ref.shape[1] // HEAD_DIM):
        sl = slice(h * HEAD_DIM, (h + 1) * HEAD_DIM)
        attn = None
        for g in range(N_DIL):
            term = jnp.broadcast_to(wts[g][:, h:h + 1], (tm, HEAD_DIM)) * token_order(o_refs[g], sl, otok_ref.at[g])
            attn = term if attn is None else attn + term
        a_ref[:, sl] = (attn * ga_ref[:, sl].astype(F32)).astype(BF16)
    half = out_ref.shape[1] // 2
    for c in range(2):
        cs = slice(c * half, (c + 1) * half)
        ya = jnp.dot(a_ref[...], wa_ref[:, cs], preferred_element_type=F32)
        yb = jnp.dot(ys_ref[...], ws_ref[:, cs], preferred_element_type=F32)
        out_ref[:, cs] = (ma_ref[c].astype(F32) * ya + ms_ref[c].astype(F32) * yb).astype(out_ref.dtype)


def _merge(outs, lses, tail, y_ssm, w_a, w_s):
    _, m, aw = tail.shape
    d = w_a.shape[1]
    assert d == 2 * aw
    nb = outs[0].shape[0]
    seq = m // nb
    tm = _row_tile(seq, 512)
    bps = seq // tm
    row = pl.BlockSpec((tm, aw), lambda i: (i, 0))
    ol_specs = []
    for arr in list(outs) + list(lses):
        dil = arr.shape[1]
        assert tm % (dil * ROW_TILE) == 0
        ol_specs.append(pl.BlockSpec((1, dil, tm // dil, arr.shape[3]),
                                     lambda i: (lax.div(i, bps), 0, lax.rem(i, bps), 0)))
    scratch = [pltpu.VMEM((tm, aw), BF16), pltpu.VMEM((N_DIL, tm, HEAD_DIM), F32),
               pltpu.VMEM((N_DIL, tm, HEAD_DIM), F32)]
    once = pl.Buffered(1)
    return pl.pallas_call(
        _merge_kernel,
        grid=(m // tm,),
        in_specs=ol_specs + [pl.BlockSpec((None, tm, aw), lambda i: (0, i, 0)), row,
                             pl.BlockSpec((2, tm, aw), lambda i: (1, i, 0)),
                             pl.BlockSpec((2, tm, aw), lambda i: (2, i, 0)),
                             pl.BlockSpec((aw, d), lambda i: (0, 0), pipeline_mode=once),
                             pl.BlockSpec((aw, d), lambda i: (0, 0), pipeline_mode=once)],
        out_specs=pl.BlockSpec((tm, d), lambda i: (i, 0)),
        out_shape=jax.ShapeDtypeStruct((m, d), BF16),
        scratch_shapes=scratch,
        compiler_params=_params("parallel"),
        name="merge",
    )(*outs, *lses, tail, y_ssm, tail, tail, w_a, w_s)


def _out_kernel(x_ref, m_ref, w_ref, o_ref):
    o_ref[...] = x_ref[...] + jnp.dot(m_ref[...], w_ref[...], preferred_element_type=F32)


def _out_proj(x, merged, w):
    m, d = x.shape
    tm = _row_tile(m, 1024)
    tn = min(d, 1024)
    return pl.pallas_call(
        _out_kernel,
        grid=(m // tm, d // tn),
        in_specs=[pl.BlockSpec((tm, tn), lambda i, j: (i, j)),
                  pl.BlockSpec((tm, d), lambda i, j: (i, 0)),
                  pl.BlockSpec((d, tn), lambda i, j: (0, j))],
        out_specs=pl.BlockSpec((tm, tn), lambda i, j: (i, j)),
        out_shape=jax.ShapeDtypeStruct((m, d), F32),
        compiler_params=_params("parallel", "arbitrary"),
        name="out_proj",
    )(x, merged, w)


def _sample_front_kernel(x_ref, gw_ref, w_ref, nwq_ref, nwk_ref, cos_ref, sin_ref, o_ref, nw_ref):
    j = pl.program_id(0)
    x = x_ref[...]
    xn = x * lax.rsqrt(jnp.mean(x * x, axis=-1, keepdims=True) + NORM_EPS) * gw_ref[...]
    acc = jnp.dot(xn.astype(BF16), w_ref[...], preferred_element_type=F32)
    qk, v_end = 2 * N_DIL, 3 * N_DIL

    @pl.when(j < qk)
    def _():
        nw_ref[...] = jnp.where(j < N_DIL, nwq_ref[...], nwk_ref[...])
        o_ref[...] = acc
        _norm_rope_inplace(o_ref, nw_ref, cos_ref, sin_ref)

    @pl.when(((j >= qk) & (j < v_end)) | (j == v_end + 1))
    def _():
        o_ref[...] = acc

    @pl.when((j == v_end) | (j == v_end + 2))
    def _():
        o_ref[...] = acc * _sigmoid(acc)

    @pl.when(j > v_end + 2)
    def _():
        o_ref[...] = _sigmoid(acc)


def _sample_front(x, norm_w, w, q_norm_w, k_norm_w, cos, sin):
    m, d = x.shape
    aw = d // 2
    n_cols = w.shape[1]
    row = pl.BlockSpec((m, HEAD_DIM), lambda j: (0, 0))
    one = pl.BlockSpec((1, HEAD_DIM), lambda j: (0, 0))
    return pl.pallas_call(
        _sample_front_kernel,
        grid=(n_cols // aw,),
        in_specs=[pl.BlockSpec((m, d), lambda j: (0, 0)), pl.BlockSpec((1, d), lambda j: (0, 0)),
                  pl.BlockSpec((d, aw), lambda j: (0, j)), one, one, row, row],
        out_specs=pl.BlockSpec((m, aw), lambda j: (0, j)),
        out_shape=jax.ShapeDtypeStruct((m, n_cols), F32),
        scratch_shapes=[pltpu.VMEM((1, HEAD_DIM), F32)],
        compiler_params=_params("arbitrary"),
        name="sample_front",
    )(x, norm_w.reshape(1, d), w, q_norm_w.reshape(1, HEAD_DIM), k_norm_w.reshape(1, HEAD_DIM), cos, sin)


def _layer(x, cos, sin, lw, nb, seq, layer, depth, kv_prev, sample, shift=None):
    (norm_w, w_in, q_norm_w, k_norm_w, w_glu, b_glu, w_br_attn, w_br_ssm, w_out, ssm_d, s5prm) = lw
    m, d = x.shape
    aw = d // 2
    qkv = N_DIL * aw
    hg = aw // HEAD_DIM

    new_kv = []
    if sample is None:
        xn = _rmsnorm(x, norm_w)
        tail, u = _proj_tail(xn, w_in)
        qds = _q_all(xn, w_in, nb, seq, q_norm_w, cos, sin)
        outs, lses = [], []
        for g in range(N_DIL):
            kd, vd, kv = _kv(xn, w_in, g, nb, seq, k_norm_w, cos, sin, layer, depth,
                             None if kv_prev is None else kv_prev[g])
            o, lse = _attn_prompt(qds[g], kd, vd)
            outs.append(o)
            lses.append(lse)
            new_kv.append(kv)
        y, hre, him, shifted = _s5_prompt(u, s5prm, ssm_d, nb, seq, shift)
    else:
        caches, h0re, h0im = sample
        front = _sample_front(x, norm_w, w_in, q_norm_w, k_norm_w, cos, sin)
        q, k, v = front[:, :qkv], front[:, qkv:2 * qkv], front[:, 2 * qkv:3 * qkv]
        blocks = front[:, 3 * qkv:].reshape(m, TAIL_SLABS + 1, aw)
        u = blocks[:, 1]
        tail = jnp.concatenate([blocks[:, 0:1], blocks[:, 2:]], axis=1).transpose(1, 0, 2)
        tok = (nb, seq, N_DIL * hg, HEAD_DIM)
        o, lse = _attn_sample(q.reshape(tok), k.reshape(tok), v.reshape(tok), caches, layer)
        o = o.reshape(m, qkv)
        lse = jnp.pad(lse[..., 0].reshape(m, N_DIL, hg), ((0, 0), (0, 0), (0, HEAD_DIM - hg)))
        outs = [o[:, g * aw:(g + 1) * aw].reshape(1, 1, m, aw) for g in range(N_DIL)]
        lses = [lse[:, g].reshape(1, 1, m, HEAD_DIM) for g in range(N_DIL)]
        ut = u.reshape(nb, seq, aw).transpose(1, 0, 2)
        yt, hre, him = _s5_sample(ut, h0re, h0im, s5prm, ssm_d)
        y = yt.transpose(1, 0, 2).reshape(m, aw)
        k5, v5 = k.reshape(nb, seq, N_DIL, hg, HEAD_DIM), v.reshape(nb, seq, N_DIL, hg, HEAD_DIM)
        new_kv = [jnp.stack([k5[:, :, g], v5[:, :, g]], axis=2) for g in range(N_DIL)]
        shifted = None

    y_ssm = _glu(y, tail, w_glu, b_glu)
    merged = _merge(outs, lses, tail, y_ssm, w_br_attn, w_br_ssm)
    x_out = _out_proj(x, merged, w_out)

    n_grp = aw // SSM_GROUP_CH
    state = jnp.stack([hre, him], axis=-1).reshape(nb, n_grp, SSM_STATE, 2)
    return x_out, new_kv, state, shifted


def kernel(x_prompt, x_sample, cache_kv_d1, cache_kv_d4, cache_kv_d16, state_ssm, norm_w, w_in, q_norm_w, k_norm_w, ssm_lambda_re, ssm_lambda_im, ssm_log_dt, ssm_b_re, ssm_b_im, ssm_c_re, ssm_c_im, ssm_d, w_glu, b_glu, w_br_attn, w_br_ssm, w_out):
    nb_p, seq_p, d = x_prompt.shape
    nb_s, seq_s, _ = x_sample.shape
    depth = norm_w.shape[0]
    caches = (cache_kv_d1, cache_kv_d4, cache_kv_d16)

    cos_p, sin_p = _rope_tables(jnp.arange(seq_p, dtype=F32))
    cos_s, sin_s = _rope_tables(PAST_LEN + jnp.arange(seq_s, dtype=F32))
    cos_s, sin_s = jnp.tile(cos_s, (nb_s, 1)), jnp.tile(sin_s, (nb_s, 1))

    hp = x_prompt.reshape(nb_p * seq_p, d)
    hs = x_sample.reshape(nb_s * seq_s, d)
    lws = []
    for l in range(depth):
        s5prm = _s5_params(ssm_lambda_re[l], ssm_lambda_im[l], ssm_log_dt[l], ssm_b_re[l], ssm_b_im[l],
                           ssm_c_re[l], ssm_c_im[l], _s5_chunk(seq_p) // ROW_TILE)
        lws.append((norm_w[l], _layer_bf16(w_in, l), q_norm_w[l], k_norm_w[l], _layer_bf16(w_glu, l), b_glu[l],
                    _layer_bf16(w_br_attn, l), _layer_bf16(w_br_ssm, l), _layer_bf16(w_out, l), ssm_d[l], s5prm))

    kv_s_new = [[] for _ in range(N_DIL)]
    ssm_s = []
    n_state = state_ssm.shape[2] * state_ssm.shape[3]
    for l in range(depth):
        h0re = state_ssm[l, ..., 0].reshape(nb_s, n_state)
        h0im = state_ssm[l, ..., 1].reshape(nb_s, n_state)
        hs, new_kv_s, st_s, _ = _layer(hs, cos_s, sin_s, lws[l], nb_s, seq_s, l, depth, None,
                                       (caches, h0re, h0im))
        for g in range(N_DIL):
            kv_s_new[g].append(new_kv_s[g])
        ssm_s.append(st_s)

    kv_p, kv_s, ssm_p = None, None, []
    for l in range(depth):
        shift = (caches, [jnp.stack(n) for n in kv_s_new]) if l == 0 else None
        hp, kv_p, st, shifted = _layer(hp, cos_p, sin_p, lws[l], nb_p, seq_p, l, depth, kv_p, None, shift)
        kv_s = shifted if shifted is not None else kv_s
        ssm_p.append(st)

    return (hp.reshape(x_prompt.shape), hs.reshape(x_sample.shape), kv_p[0], kv_p[1], kv_p[2], jnp.stack(ssm_p),
            kv_s[0], kv_s[1], kv_s[2], jnp.stack(ssm_s))
```

```python
import functools
import math

import jax
import jax.numpy as jnp
from jax import lax
from jax.experimental import pallas as pl
from jax.experimental.pallas import tpu as pltpu

F32 = jnp.float32
BF16 = jnp.bfloat16

HEAD_DIM = 128
WINDOWS = (128, 512, 2048)
DILATIONS = (1, 4, 16)
N_DIL = 3
SPAN = 128
SSM_GROUP_CH = 16
SSM_STATE = 64
GROUPS_PER_CHUNK = 8
ROPE_THETA = 10000.0
NORM_EPS = 1e-6
PAST_LEN = 16384
NEG_BIG = -1e30
VMEM_LIMIT = 56 * 1024 * 1024
ROW_TILE = 8
MXU_COLS = 256
QK_ROW_CHUNK = 64
KV_COPY_BYTES = 4 * 1024 * 1024


def _params(*sem):
    return pltpu.CompilerParams(dimension_semantics=sem, vmem_limit_bytes=VMEM_LIMIT)


def _sigmoid(x):
    return 1.0 / (1.0 + jnp.exp(-x))


def _row_tile(m, cap):
    t = min(m, cap)
    assert m % t == 0, (m, t)
    return t


def _rmsnorm_kernel(x_ref, w_ref, o_ref):
    x = x_ref[...]
    y = x * lax.rsqrt(jnp.mean(x * x, axis=-1, keepdims=True) + NORM_EPS)
    o_ref[...] = (y * w_ref[...]).astype(o_ref.dtype)


def _rmsnorm(x, w):
    m, d = x.shape
    tm = _row_tile(m, 512)
    return pl.pallas_call(
        _rmsnorm_kernel,
        grid=(m // tm,),
        in_specs=[pl.BlockSpec((tm, d), lambda i: (i, 0)),
                  pl.BlockSpec((1, d), lambda i: (0, 0))],
        out_specs=pl.BlockSpec((tm, d), lambda i: (i, 0)),
        out_shape=jax.ShapeDtypeStruct((m, d), BF16),
        compiler_params=_params("parallel"),
        name="rmsnorm",
    )(x, w.reshape(1, d))


def _cast_kernel(w_ref, o_ref):
    o_ref[...] = w_ref[...].astype(o_ref.dtype)


def _layer_bf16(w, layer):
    _, rows, cols = w.shape
    tr, tc = min(rows, 512), min(cols, 2048)
    assert rows % tr == 0 and cols % tc == 0
    return pl.pallas_call(
        _cast_kernel,
        grid=(rows // tr, cols // tc),
        in_specs=[pl.BlockSpec((None, tr, tc), lambda i, j: (layer, i, j))],
        out_specs=pl.BlockSpec((tr, tc), lambda i, j: (i, j)),
        out_shape=jax.ShapeDtypeStruct((rows, cols), BF16),
        compiler_params=_params("parallel", "parallel"),
        name="cast_bf16",
    )(w)


def _norm_rope_inplace(t_ref, nw_ref, cos_ref, sin_ref, head_major=False):
    rows = cos_ref.shape[0]
    n_heads = t_ref.shape[0] // rows if head_major else t_ref.shape[1] // HEAD_DIM
    rc = min(rows, QK_ROW_CHUNK)
    half = HEAD_DIM // 2
    nw = nw_ref[...]
    nw_swapped = pltpu.roll(nw, half, axis=1)

    def chunk(c, carry):
        r0 = pl.multiple_of(c * rc, rc)
        cs = cos_ref[pl.ds(r0, rc), :] * nw
        ss = sin_ref[pl.ds(r0, rc), :] * nw_swapped
        for h in range(n_heads):
            idx = (pl.ds(h * rows + r0, rc), slice(None)) if head_major else (
                pl.ds(r0, rc), slice(h * HEAD_DIM, (h + 1) * HEAD_DIM))
            a = t_ref[idx]
            r = lax.rsqrt(jnp.mean(a * a, axis=-1, keepdims=True) + NORM_EPS)
            t_ref[idx] = (a * cs + pltpu.roll(a, half, axis=1) * ss) * r
        return carry

    lax.fori_loop(0, rows // rc, chunk, 0)


def _proj_kernel(x_ref, w_ref, o_ref, *, epilogue):
    acc = jnp.dot(x_ref[...], w_ref[...], preferred_element_type=F32)
    if epilogue == "silu":
        acc = acc * _sigmoid(acc)
    elif epilogue == "sigmoid":
        acc = _sigmoid(acc)
    o_ref[...] = acc.astype(o_ref.dtype)


def _proj(xn, w, col0, ncols, epilogue, out_dtype):
    m, d = xn.shape
    tm = _row_tile(m, 1024)
    tn = min(ncols, 1024)
    assert ncols % tn == 0 and col0 % tn == 0
    jb = col0 // tn
    return pl.pallas_call(
        functools.partial(_proj_kernel, epilogue=epilogue),
        grid=(m // tm, ncols // tn),
        in_specs=[pl.BlockSpec((tm, d), lambda i, j: (i, 0)),
                  pl.BlockSpec((d, tn), lambda i, j: (0, jb + j))],
        out_specs=pl.BlockSpec((tm, tn), lambda i, j: (i, j)),
        out_shape=jax.ShapeDtypeStruct((m, ncols), out_dtype),
        compiler_params=_params("parallel", "arbitrary"),
        name="proj_" + epilogue,
    )(xn, w)


def _norm_rope_head(a, nw_ref, cos_ref, sin_ref, store):
    rows = a.shape[0]
    rc = min(rows, QK_ROW_CHUNK)
    half = HEAD_DIM // 2
    nw = nw_ref[...]
    nw_swapped = pltpu.roll(nw, half, axis=1)
    ssq = jnp.dot((a * a).astype(BF16), jnp.ones((HEAD_DIM, HEAD_DIM), BF16), preferred_element_type=F32)
    for r0 in range(0, rows, rc):
        ac = a[r0:r0 + rc]
        cs = cos_ref[r0:r0 + rc, :] * nw
        ss = sin_ref[r0:r0 + rc, :] * nw_swapped
        r = lax.rsqrt(ssq[r0:r0 + rc] * (1.0 / HEAD_DIM) + NORM_EPS)
        store(r0, rc, (ac * cs + pltpu.roll(ac, half, axis=1) * ss) * r)


def _project_dilated(x, w_ref, nw_ref, cos_ref, sin_ref, hm_ref, d_ref, dil):
    tm = x.shape[0]
    n_heads = w_ref.shape[1] // HEAD_DIM
    per = tm // dil
    group = n_heads
    for h0 in range(0, n_heads, group):
        acc = jnp.dot(x, w_ref[:, h0 * HEAD_DIM:(h0 + group) * HEAD_DIM], preferred_element_type=F32)
        for h in range(h0, h0 + group):
            a = acc[:, (h - h0) * HEAD_DIM:(h - h0 + 1) * HEAD_DIM]
            if nw_ref is None:
                hm_ref[h * tm:(h + 1) * tm, :] = a
            else:
                def store(r0, rc, chunk, h=h):
                    hm_ref[h * tm + r0:h * tm + r0 + rc, :] = chunk

                _norm_rope_head(a, nw_ref, cos_ref, sin_ref, store)
            for r in range(dil):
                rows = pl.ds(h * tm + r, per, stride=dil) if dil > 1 else pl.ds(h * tm, per)
                d_ref[0, r, :, h * HEAD_DIM:(h + 1) * HEAD_DIM] = hm_ref[rows, :].astype(d_ref.dtype)


def _q_kernel(x_ref, w_ref, nw_ref, cos_ref, sin_ref, *rest):
    qd_refs, hm_ref = rest[:N_DIL], rest[N_DIL]
    for g in range(N_DIL):
        @pl.when(pl.program_id(0) == g)
        def _(g=g):
            _project_dilated(x_ref[...], w_ref, nw_ref, cos_ref, sin_ref, hm_ref, qd_refs[g], DILATIONS[g])


def _q_all(xn, w, nb, seq, q_norm_w, cos, sin):
    m, d = xn.shape
    aw = d // 2
    tm = _row_tile(seq, 1024)
    bps = seq // tm
    out_specs, out_shape = [], []
    for g in range(N_DIL):
        dil = DILATIONS[g]
        assert (tm // dil) % 16 == 0

        def q_map(gi, b, i, g=g):
            return (jnp.where(gi < g, 0, jnp.where(gi == g, b, nb - 1)), 0,
                    jnp.where(gi < g, 0, jnp.where(gi == g, i, bps - 1)), 0)

        out_specs.append(pl.BlockSpec((1, dil, tm // dil, aw), q_map))
        out_shape.append(jax.ShapeDtypeStruct((nb, dil, seq // dil, aw), BF16))
    return pl.pallas_call(
        _q_kernel,
        grid=(N_DIL, nb, bps),
        in_specs=[pl.BlockSpec((tm, d), lambda g, b, i: (b * bps + i, 0)),
                  pl.BlockSpec((d, aw), lambda g, b, i: (0, g)),
                  pl.BlockSpec((1, HEAD_DIM), lambda g, b, i: (0, 0)),
                  pl.BlockSpec((tm, HEAD_DIM), lambda g, b, i: (i, 0)),
                  pl.BlockSpec((tm, HEAD_DIM), lambda g, b, i: (i, 0))],
        out_specs=out_specs,
        out_shape=out_shape,
        scratch_shapes=[pltpu.VMEM((aw // HEAD_DIM * tm, HEAD_DIM), F32)],
        compiler_params=_params("arbitrary", "arbitrary", "arbitrary"),
        name="q_all",
    )(xn, w, q_norm_w.reshape(1, HEAD_DIM), cos, sin)


def _kv_kernel(x_ref, wk_ref, wv_ref, nw_ref, cos_ref, sin_ref, *rest, dil, first_kept):
    kd_ref, vd_ref, kv_ref, hk_ref, hv_ref = rest[-5:]
    half_idx = pl.program_id(2)
    tm = x_ref.shape[0]
    n_heads = wk_ref.shape[1] // HEAD_DIM

    @pl.when(half_idx == 0)
    def _():
        x = x_ref[...]
        _project_dilated(x, wk_ref, nw_ref, cos_ref, sin_ref, hk_ref, kd_ref, dil)
        _project_dilated(x, wv_ref, None, None, None, hv_ref, vd_ref, dil)

    kb = kv_ref.shape[1]
    half = tm // 2

    @pl.when(2 * pl.program_id(1) + half_idx >= first_kept)
    def _():
        base = (half_idx + 1) * half - kb

        def token(t, carry):
            rows = pl.ds(base + t, n_heads, stride=tm)
            kv_ref[0, t, 0] = hk_ref[rows, :]
            kv_ref[0, t, 1] = hv_ref[rows, :]
            return carry

        lax.fori_loop(0, kb, token, 0, unroll=8)


def _kv(xn, w, g, nb, seq, k_norm_w, cos, sin, layer, depth, kv_prev):
    m, d = xn.shape
    aw = d // 2
    hg = aw // HEAD_DIM
    dil = DILATIONS[g]
    keep = min(WINDOWS[g], seq)
    tm = _row_tile(seq, 1024)
    per = tm // dil
    half = tm // 2
    kb = min(keep, half)
    bps = seq // tm
    n_half = 2 * bps
    first_kept = (seq - keep) // half if keep >= half else n_half - 1
    assert per % 16 == 0 and keep % kb == 0 and (seq - keep) % kb == 0
    once = pl.Buffered(1)
    nw = lambda a: a.reshape(1, HEAD_DIM)
    dshape = jax.ShapeDtypeStruct((nb, dil, seq // dil, aw), BF16)
    staging = pltpu.VMEM((hg * tm, HEAD_DIM), F32)

    wspec = lambda col: pl.BlockSpec((d, aw), lambda b, i, h: (0, col), pipeline_mode=once)
    tspec = pl.BlockSpec((tm, HEAD_DIM), lambda b, i, h: (i, 0))
    dspec = pl.BlockSpec((1, dil, per, aw), lambda b, i, h: (b, 0, i, 0))
    kv_shape = jax.ShapeDtypeStruct((depth, nb, keep, 2, hg, HEAD_DIM), F32)
    if kv_prev is None:
        kv_prev = jnp.zeros(kv_shape.shape, kv_shape.dtype)
    kd, vd, kv = pl.pallas_call(
        functools.partial(_kv_kernel, dil=dil, first_kept=first_kept),
        grid=(nb, bps, 2),
        in_specs=[pl.BlockSpec((tm, d), lambda b, i, h: (b * bps + i, 0)),
                  wspec(N_DIL + g), wspec(2 * N_DIL + g),
                  pl.BlockSpec((1, HEAD_DIM), lambda b, i, h: (0, 0)), tspec, tspec,
                  pl.BlockSpec(memory_space=pl.ANY)],
        out_specs=[dspec, dspec,
                   pl.BlockSpec((None, 1, kb, 2, hg, HEAD_DIM),
                                lambda b, i, h: (layer, b, jnp.maximum(2 * i + h - first_kept, 0), 0, 0, 0))],
        out_shape=[dshape, dshape, kv_shape],
        scratch_shapes=[staging, staging],
        input_output_aliases={6: 2},
        compiler_params=_params("parallel", "arbitrary", "arbitrary"),
        name="kv_d%d" % dil,
    )(xn, w, w, nw(k_norm_w), cos, sin, kv_prev)
    return kd, vd, kv


def _rope_tables(pos):
    half = HEAD_DIM // 2
    inv_freq = jnp.power(ROPE_THETA, -jnp.arange(half, dtype=F32) * (2.0 / HEAD_DIM))
    ang = pos[:, None] * inv_freq[None, :]
    cos, sin = jnp.cos(ang), jnp.sin(ang)
    return jnp.concatenate([cos, cos], axis=-1), jnp.concatenate([-sin, sin], axis=-1)


def _attn_prompt_kernel(q_ref, kp_ref, kc_ref, vp_ref, vc_ref, o_ref, lse_ref):
    not_first = pl.program_id(2) > 0
    a = lax.broadcasted_iota(jnp.int32, (SPAN, 2 * SPAN), 0)
    c = lax.broadcasted_iota(jnp.int32, (SPAN, 2 * SPAN), 1)
    band = (c >= a) & (c <= a + SPAN)
    band_first = band & ((c >= SPAN) | not_first)
    lane = lax.broadcasted_iota(jnp.int32, (SPAN, HEAD_DIM), 1)
    scale = HEAD_DIM ** -0.5
    for j in range(q_ref.shape[2] // SPAN):
        rows = slice(j * SPAN, (j + 1) * SPAN)
        lse_tile = jnp.zeros((SPAN, HEAD_DIM), F32)
        for h in range(q_ref.shape[3] // HEAD_DIM):
            sl = slice(h * HEAD_DIM, (h + 1) * HEAD_DIM)
            q = q_ref[0, 0, rows, sl]
            if j == 0:
                k = jnp.concatenate([kp_ref[0, 0, :, sl], kc_ref[0, 0, rows, sl]], axis=0)
                v = jnp.concatenate([vp_ref[0, 0, :, sl], vc_ref[0, 0, rows, sl]], axis=0)
            else:
                k = kc_ref[0, 0, (j - 1) * SPAN:(j + 1) * SPAN, sl]
                v = vc_ref[0, 0, (j - 1) * SPAN:(j + 1) * SPAN, sl]
            s = lax.dot_general(q, k, (((1,), (1,)), ((), ())), preferred_element_type=F32) * scale
            s = jnp.where(band_first if j == 0 else band, s, NEG_BIG)
            m = jnp.max(s, axis=-1, keepdims=True)
            p = jnp.exp(s - m)
            l = jnp.sum(p, axis=-1, keepdims=True)
            o_ref[0, 0, rows, sl] = jnp.dot(p.astype(BF16), v, preferred_element_type=F32) / l
            lse_tile = jnp.where(lane == h, m + jnp.log(l), lse_tile)
        lse_ref[0, 0, rows, :] = lse_tile


def _attn_prompt(q, k, v):
    nb, dil, length, aw = q.shape
    n_blk = length // SPAN
    qb = min(4, n_blk)
    assert length % SPAN == 0 and n_blk % qb == 0 and aw // HEAD_DIM <= HEAD_DIM
    cur = pl.BlockSpec((1, 1, qb * SPAN, aw), lambda b, r, n: (b, r, n, 0))
    prev = pl.BlockSpec((1, 1, SPAN, aw), lambda b, r, n: (b, r, jnp.maximum(qb * n - 1, 0), 0))
    return pl.pallas_call(
        _attn_prompt_kernel,
        grid=(nb, dil, n_blk // qb),
        in_specs=[cur, prev, cur, prev, cur],
        out_specs=[cur, pl.BlockSpec((1, 1, qb * SPAN, HEAD_DIM), lambda b, r, n: (b, r, n, 0))],
        out_shape=[jax.ShapeDtypeStruct(q.shape, F32),
                   jax.ShapeDtypeStruct((nb, dil, length, HEAD_DIM), F32)],
        compiler_params=_params("parallel", "parallel", "arbitrary"),
        name="attn_prompt_d%d" % dil,
    )(q, k, k, v, v)


def _attn_sample_kernel(q_ref, kn_ref, vn_ref, c1_ref, c4_ref, c16_ref, o_ref, lse_ref):
    n_new, n_heads, _ = q_ref.shape
    hg = n_heads // N_DIL
    scale = HEAD_DIM ** -0.5
    caches = (c1_ref, c4_ref, c16_ref)
    for g in range(N_DIL):
        hs = slice(g * hg, (g + 1) * hg)
        for t in range(n_new):
            q = q_ref[t, hs, :]
            if DILATIONS[g] == 1:
                kc, vc = caches[g][:, 0], caches[g][:, 1]
                rows = lax.broadcasted_iota(jnp.int32, (kc.shape[0], hg, 1), 0)
                s = jnp.sum(kc * q[None], axis=-1, keepdims=True) * scale
                s = jnp.where(rows >= t, s, NEG_BIG)
                new = range(t + 1)
            else:
                kc, vc = caches[g][:, t, 0], caches[g][:, t, 1]
                s = jnp.sum(kc * q[None], axis=-1, keepdims=True) * scale
                new = (t,)
            s_new = [jnp.sum(kn_ref[u, hs, :] * q, axis=-1, keepdims=True) * scale for u in new]
            m = jnp.max(s, axis=0)
            for sn in s_new:
                m = jnp.maximum(m, sn)
            p = jnp.exp(s - m[None])
            l = jnp.sum(p, axis=0)
            acc = jnp.sum(p * vc, axis=0)
            for u, sn in zip(new, s_new):
                pn = jnp.exp(sn - m)
                l = l + pn
                acc = acc + pn * vn_ref[u, hs, :]
            o_ref[t, hs, :] = acc / l
            lse_ref[t, hs, :] = jnp.broadcast_to(m + jnp.log(l), (hg, HEAD_DIM))


def _attn_sample(q, kn, vn, caches, layer):
    nb, n_new, n_heads, _ = q.shape
    hg = n_heads // N_DIL
    views, specs = [], []
    for g in range(N_DIL):
        dil = DILATIONS[g]
        depth, _, buf = caches[g].shape[:3]
        assert buf == WINDOWS[g] and (dil == 1 or dil % n_new == 0)
        if dil == 1:
            views.append(caches[g])
            specs.append(pl.BlockSpec((None, None, buf, 2, hg, HEAD_DIM),
                                      lambda b: (layer, b, 0, 0, 0, 0)))
        else:
            views.append(caches[g].reshape(depth, nb, SPAN, dil, 2, hg, HEAD_DIM))
            specs.append(pl.BlockSpec((None, None, SPAN, n_new, 2, hg, HEAD_DIM),
                                      lambda b: (layer, b, 0, 0, 0, 0, 0)))
    tok = pl.BlockSpec((None, n_new, n_heads, HEAD_DIM), lambda b: (b, 0, 0, 0))
    return pl.pallas_call(
        _attn_sample_kernel,
        grid=(nb,),
        in_specs=[tok, tok, tok] + specs,
        out_specs=[tok, tok],
        out_shape=[jax.ShapeDtypeStruct(q.shape, F32)] * 2,
        compiler_params=_params("parallel"),
        name="attn_sample",
    )(q, kn, vn, *views)


def _s5_params_kernel(lre_ref, lim_ref, ldt_ref, bre_ref, bim_ref, ptab_ref, bbre_ref, bbim_ref):
    dt = jnp.exp(ldt_ref[...])
    lre, lim = lre_ref[...], lim_ref[...]
    xr, xi = lre * dt, lim * dt
    mag = jnp.exp(xr)
    ar, ai = mag * jnp.cos(xi), mag * jnp.sin(xi)
    shape = (ROW_TILE, xr.shape[1])
    abr, abi = jnp.broadcast_to(ar, shape), jnp.broadcast_to(ai, shape)
    ptab_ref[0, 0] = abr
    ptab_ref[1, 0] = abi

    def next_power(i, carry):
        pr, pi = carry
        pr, pi = pr * abr - pi * abi, pr * abi + pi * abr
        ptab_ref[0, i] = pr
        ptab_ref[1, i] = pi
        return pr, pi

    lax.fori_loop(1, ptab_ref.shape[1], next_power, (abr, abi))
    nr, ni = ar - 1.0, ai
    den = lre * lre + lim * lim
    fr = (nr * lre + ni * lim) / den
    fi = (ni * lre - nr * lim) / den
    bre, bim = bre_ref[...], bim_ref[...]
    bbre_ref[...] = fr * bre - fi * bim
    bbim_ref[...] = fr * bim + fi * bre


def _s5_params(lam_re, lam_im, log_dt, b_re, b_im, c_re, c_im, seg_len):
    n_grp, n_st = lam_re.shape
    n_ch = b_re.shape[2]
    n = n_grp * n_st
    gc = GROUPS_PER_CHUNK
    n_chunk = n_grp // gc
    sw = gc * n_st
    to_lanes = lambda b: b.transpose(2, 0, 1).reshape(n_ch, n)
    lane = lambda rows: pl.BlockSpec((rows, sw), lambda j: (0, j))
    ptab, bb_re, bb_im = pl.pallas_call(
        _s5_params_kernel,
        grid=(n_chunk,),
        in_specs=[lane(1)] * 3 + [lane(n_ch)] * 2,
        out_specs=[pl.BlockSpec((2, seg_len, ROW_TILE, sw), lambda j: (0, 0, 0, j)), lane(n_ch), lane(n_ch)],
        out_shape=[jax.ShapeDtypeStruct((2, seg_len, ROW_TILE, n), F32),
                   jax.ShapeDtypeStruct((n_ch, n), F32), jax.ShapeDtypeStruct((n_ch, n), F32)],
        compiler_params=_params("parallel"),
        name="s5_params",
    )(lam_re.reshape(1, n), lam_im.reshape(1, n),
      jnp.broadcast_to(log_dt[:, None], (n_grp, n_st)).reshape(1, n), to_lanes(b_re), to_lanes(b_im))

    eye = jnp.eye(gc, dtype=F32)

    def b_blocks(bb):
        bb = bb.reshape(n_ch, n_chunk, gc, n_st)
        blk = jnp.einsum("cjgp,gh->jgchp", bb, eye)
        return blk.reshape(n_chunk, gc * n_ch, gc * n_st).astype(BF16)

    def c_blocks(cc):
        cc = cc.reshape(n_chunk, gc, n_ch, n_st)
        blk = jnp.einsum("jgcp,gh->jgphc", cc, eye)
        return blk.reshape(n_chunk, gc * n_st, gc * n_ch).astype(BF16)

    b_cat = jnp.concatenate([b_blocks(bb_re), b_blocks(bb_im)], axis=2)
    c_cat = jnp.concatenate([c_blocks(c_re), c_blocks(c_im)], axis=1)
    return ptab, b_cat, c_cat


def _cmul_add(xr, xi, ar, ai, br, bi):
    return xr + ar * br - ai * bi, xi + ar * bi + ai * br


def _shift_plan(cache_shape, new_shape, n_steps):
    lb_n, buf = cache_shape[:2]
    keep = buf - new_shape[1]
    row_bytes = 4 * math.prod(cache_shape[2:])
    for per_row in range(1, keep + 1):
        n_jobs = lb_n * per_row
        if keep % per_row == 0 and keep // per_row * row_bytes <= KV_COPY_BYTES and n_steps % n_jobs == 0:
            return keep // per_row, per_row, n_steps // n_jobs
    raise ValueError("window shift does not fit the grid")


def _shift_lag(period):
    return 1 if period == 2 else 2


def _shift_slots(period):
    return 4 if period == 1 else 2


def _window_shift_step(step, n_steps, caches, news, outs, rings, nbufs, in_sem, out_sem, new_sem):
    last = step == n_steps - 1
    for g in range(N_DIL):
        cache, out, ring = caches[g], outs[g], rings[g]
        n_new = news[g].shape[1]
        keep = cache.shape[1] - n_new
        rows, per_row, period = _shift_plan(cache.shape, news[g].shape, n_steps)
        n_jobs = cache.shape[0] * per_row
        slots = ring.shape[0]
        lag = _shift_lag(period)
        assert n_jobs >= slots and (period == 1 or lag < period)

        def load(k, cache=cache, ring=ring, g=g, rows=rows, per_row=per_row, n_new=n_new, slots=slots):
            src = cache.at[lax.div(k, per_row), pl.ds(n_new + lax.rem(k, per_row) * rows, rows)]
            return pltpu.make_async_copy(src, ring.at[lax.rem(k, slots)], in_sem.at[g, lax.rem(k, slots)])

        def store(k, out=out, ring=ring, g=g, rows=rows, per_row=per_row, slots=slots):
            dst = out.at[lax.div(k, per_row), pl.ds(lax.rem(k, per_row) * rows, rows)]
            return pltpu.make_async_copy(ring.at[lax.rem(k, slots)], dst, out_sem.at[g, lax.rem(k, slots)])

        def issue(k, load=load, store=store, slots=slots):
            @pl.when(k >= slots)
            def _():
                store(k - slots).wait()
            load(k).start()

        def forward(k, load=load, store=store):
            load(k).wait()
            store(k).start()

        def drain(k, store=store, slots=slots):
            for back in range(slots - 1, -1, -1):
                store(k - back).wait()

        if period == 1:
            issue(step)

            @pl.when(step >= lag)
            def _(forward=forward, lag=lag):
                forward(step - lag)

            @pl.when(last)
            def _(forward=forward, drain=drain, lag=lag):
                for back in range(lag - 1, -1, -1):
                    forward(step - back)
                drain(step)
        else:
            k = lax.div(step, period)
            phase = lax.rem(step, period)

            @pl.when(phase == 0)
            def _(issue=issue, k=k):
                issue(k)

            @pl.when(phase == lag)
            def _(forward=forward, k=k):
                forward(k)

            @pl.when(last)
            def _(drain=drain, k=k):
                drain(k)

        new_in = pltpu.make_async_copy(news[g], nbufs[g], new_sem.at[g])
        new_out = pltpu.make_async_copy(nbufs[g], out.at[:, pl.ds(keep, n_new)], new_sem.at[N_DIL + g])

        @pl.when(step == 0)
        def _(new_in=new_in, new_out=new_out):
            new_in.start()
            new_in.wait()
            new_out.start()

        @pl.when(last)
        def _(new_out=new_out):
            new_out.wait()


def _s5_scan_kernel(u_ref, b_ref, ptab_ref, c_ref, d_ref, *rest, with_shift):
    if with_shift:
        n = N_DIL
        caches, news = rest[:n], rest[n:2 * n]
        y_ref, hfin_ref = rest[2 * n:2 * n + 2]
        outs = rest[2 * n + 2:3 * n + 2]
        up_ref, s_ref, hb_ref, end_ref = rest[3 * n + 2:3 * n + 6]
        rings, nbufs = rest[3 * n + 6:4 * n + 6], rest[4 * n + 6:5 * n + 6]
        in_sem, out_sem, new_sem = rest[5 * n + 6:]
        _, n_j, n_t = with_shift
        step = (pl.program_id(0) * n_j + pl.program_id(1)) * n_t + pl.program_id(2)
        _window_shift_step(step, math.prod(with_shift), caches, news, outs, rings, nbufs,
                           in_sem, out_sem, new_sem)
    else:
        y_ref, hfin_ref, up_ref, s_ref, hb_ref, end_ref = rest
    tc, sw2 = s_ref.shape
    sw = sw2 // 2
    seg = tc // ROW_TILE
    re, im = slice(0, sw), slice(sw, sw2)

    @pl.when(pl.program_id(2) == 0)
    def _():
        end_ref[...] = jnp.zeros_like(end_ref)

    for i in range(seg):
        up_ref[i * ROW_TILE:(i + 1) * ROW_TILE, :] = u_ref[0, pl.ds(i, ROW_TILE, stride=seg), :]
    s_ref[...] = jnp.dot(up_ref[...].astype(BF16), b_ref[0], preferred_element_type=F32)

    ar, ai = ptab_ref[0, 0], ptab_ref[1, 0]

    def local_step(i, carry):
        hr, hi = carry
        r0 = pl.multiple_of(i * ROW_TILE, ROW_TILE)
        hr, hi = _cmul_add(s_ref[pl.ds(r0, ROW_TILE), re], s_ref[pl.ds(r0, ROW_TILE), im], ar, ai, hr, hi)
        s_ref[pl.ds(r0, ROW_TILE), re] = hr
        s_ref[pl.ds(r0, ROW_TILE), im] = hi
        return hr, hi

    zero = jnp.zeros((ROW_TILE, sw), F32)
    er, ei = lax.fori_loop(0, seg, local_step, (zero, zero), unroll=8)

    row = lax.broadcasted_iota(jnp.int32, (ROW_TILE, sw), 0)
    cr = jnp.where(row == 0, pltpu.roll(end_ref[0], 1, axis=0), pltpu.roll(er, 1, axis=0))
    ci = jnp.where(row == 0, pltpu.roll(end_ref[1], 1, axis=0), pltpu.roll(ei, 1, axis=0))
    mr, mi = ptab_ref[0, seg - 1], ptab_ref[1, seg - 1]
    wr, wi = mr, mi
    for sh in (1, 2, 4):
        gr, gi = jnp.where(row >= sh, wr, 0.0), jnp.where(row >= sh, wi, 0.0)
        cr, ci = _cmul_add(cr, ci, gr, gi, pltpu.roll(cr, sh, axis=0), pltpu.roll(ci, sh, axis=0))
        wr, wi = wr * wr - wi * wi, 2.0 * wr * wi
    fr, fi = _cmul_add(er, ei, mr, mi, cr, ci)
    end_ref[0] = fr
    end_ref[1] = fi
    hfin_ref[0, 0] = fr
    hfin_ref[0, 1] = fi

    def fix_up(k, carry):
        r0 = pl.multiple_of(k * 2 * ROW_TILE, 2 * ROW_TILE)
        hs, ns = [], []
        for half in range(2):
            i = 2 * k + half
            rows = pl.ds(r0 + half * ROW_TILE, ROW_TILE)
            hr, hi = _cmul_add(s_ref[rows, re], s_ref[rows, im], ptab_ref[0, i], ptab_ref[1, i], cr, ci)
            hs.append(hr)
            ns.append(-hi)
        hb_ref[pl.ds(r0, 2 * ROW_TILE), re] = jnp.concatenate(hs, axis=0).astype(BF16)
        hb_ref[pl.ds(r0, 2 * ROW_TILE), im] = jnp.concatenate(ns, axis=0).astype(BF16)
        return carry

    lax.fori_loop(0, seg // 2, fix_up, 0, unroll=4)
    yp = jnp.dot(hb_ref[...], c_ref[0], preferred_element_type=F32) + d_ref[...] * up_ref[...]
    for i in range(seg):
        y_ref[0, pl.ds(i, ROW_TILE, stride=seg), :] = yp[i * ROW_TILE:(i + 1) * ROW_TILE, :]


def _s5_chunk(seq):
    return min(seq, 1024)


def _flat_window(a):
    return a.reshape(a.shape[0] * a.shape[1], a.shape[2], a.shape[3] * a.shape[4], a.shape[5])


def _s5_prompt(u, prm, ssm_d, nb, seq, shift=None):
    ptab, b_cat, c_cat = prm
    n_chunk, cw, sw2 = b_cat.shape
    sw = sw2 // 2
    width = u.shape[1]
    tc = _s5_chunk(seq)
    seg = tc // ROW_TILE
    assert seq % tc == 0 and seg % 2 == 0 and ptab.shape[1] == seg
    grid = (nb, n_chunk, seq // tc)
    in_specs = [pl.BlockSpec((1, tc, cw), lambda b, j, t: (b, t, j)),
                pl.BlockSpec((1, cw, sw2), lambda b, j, t: (j, 0, 0)),
                pl.BlockSpec((2, seg, ROW_TILE, sw), lambda b, j, t: (0, 0, 0, j)),
                pl.BlockSpec((1, sw2, cw), lambda b, j, t: (j, 0, 0)),
                pl.BlockSpec((1, cw), lambda b, j, t: (0, j))]
    out_specs = [pl.BlockSpec((1, tc, cw), lambda b, j, t: (b, t, j)),
                 pl.BlockSpec((1, 2, ROW_TILE, sw), lambda b, j, t: (b, 0, 0, j))]
    out_shape = [jax.ShapeDtypeStruct((nb, seq, width), F32),
                 jax.ShapeDtypeStruct((nb, 2, ROW_TILE, n_chunk * sw), F32)]
    scratch = [pltpu.VMEM((tc, cw), F32), pltpu.VMEM((tc, sw2), F32),
               pltpu.VMEM((tc, sw2), BF16), pltpu.VMEM((2, ROW_TILE, sw), F32)]
    args = [u.reshape(nb, seq, width), b_cat, ptab, c_cat, ssm_d.reshape(1, width)]
    if shift is not None:
        caches, news = shift
        cf, nf = [_flat_window(c) for c in caches], [_flat_window(w) for w in news]
        hbm = pl.BlockSpec(memory_space=pl.ANY)
        in_specs += [hbm] * (2 * N_DIL)
        out_specs += [hbm] * N_DIL
        out_shape += [jax.ShapeDtypeStruct(c.shape, c.dtype) for c in cf]
        args += cf + nf
        max_slots = 0
        for c, w in zip(cf, nf):
            rows, _, period = _shift_plan(c.shape, w.shape, math.prod(grid))
            scratch.append(pltpu.VMEM((_shift_slots(period), rows) + c.shape[2:], c.dtype))
            max_slots = max(max_slots, _shift_slots(period))
        scratch += [pltpu.VMEM(w.shape, w.dtype) for w in nf]
        scratch += [pltpu.SemaphoreType.DMA((N_DIL, max_slots)), pltpu.SemaphoreType.DMA((N_DIL, max_slots)),
                    pltpu.SemaphoreType.DMA((2 * N_DIL,))]
    res = pl.pallas_call(
        functools.partial(_s5_scan_kernel, with_shift=grid if shift is not None else None),
        grid=grid,
        in_specs=in_specs,
        out_specs=out_specs,
        out_shape=out_shape,
        scratch_shapes=scratch,
        compiler_params=_params("arbitrary", "arbitrary", "arbitrary") if shift is not None
        else _params("parallel", "parallel", "arbitrary"),
        name="s5_scan",
    )(*args)
    y, hfin = res[0], res[1]
    last = ROW_TILE - 1
    shifted = None if shift is None else [o.reshape(c.shape) for o, c in zip(res[2:], shift[0])]
    return y.reshape(nb * seq, width), hfin[:, 0, last], hfin[:, 1, last], shifted


def _s5_sample_kernel(u_ref, h0re_ref, h0im_ref, b_ref, ptab_ref, c_ref, d_ref, y_ref, hre_ref, him_ref):
    sw = h0re_ref.shape[1]
    ar, ai = ptab_ref[0, 0, 0:1, :], ptab_ref[1, 0, 0:1, :]
    hr, hi = h0re_ref[...], h0im_ref[...]
    for t in range(u_ref.shape[0]):
        u = u_ref[t]
        bu = jnp.dot(u.astype(BF16), b_ref[0], preferred_element_type=F32)
        hr, hi = _cmul_add(bu[:, :sw], bu[:, sw:], ar, ai, hr, hi)
        hb = jnp.concatenate([hr, -hi], axis=1).astype(BF16)
        y_ref[t] = jnp.dot(hb, c_ref[0], preferred_element_type=F32) + d_ref[...] * u
    hre_ref[...] = hr
    him_ref[...] = hi


def _s5_sample(u, h0re, h0im, prm, ssm_d):
    ptab, b_cat, c_cat = prm
    n_chunk, cw, sw2 = b_cat.shape
    sw = sw2 // 2
    n_new, nb, width = u.shape
    return pl.pallas_call(
        _s5_sample_kernel,
        grid=(n_chunk,),
        in_specs=[pl.BlockSpec((n_new, nb, cw), lambda j: (0, 0, j)),
                  pl.BlockSpec((nb, sw), lambda j: (0, j)),
                  pl.BlockSpec((nb, sw), lambda j: (0, j)),
                  pl.BlockSpec((1, cw, sw2), lambda j: (j, 0, 0)),
                  pl.BlockSpec((2, 1, ROW_TILE, sw), lambda j: (0, 0, 0, j)),
                  pl.BlockSpec((1, sw2, cw), lambda j: (j, 0, 0)),
                  pl.BlockSpec((1, cw), lambda j: (0, j))],
        out_specs=[pl.BlockSpec((n_new, nb, cw), lambda j: (0, 0, j)),
                   pl.BlockSpec((nb, sw), lambda j: (0, j)),
                   pl.BlockSpec((nb, sw), lambda j: (0, j))],
        out_shape=[jax.ShapeDtypeStruct((n_new, nb, width), F32),
                   jax.ShapeDtypeStruct((nb, n_chunk * sw), F32),
                   jax.ShapeDtypeStruct((nb, n_chunk * sw), F32)],
        compiler_params=_params("parallel"),
        name="s5_sample",
    )(u, h0re, h0im, b_cat, ptab, c_cat, ssm_d.reshape(1, width))


def _glu_kernel(y_ref, g_ref, w_ref, b_ref, o_ref):
    y = y_ref[...]
    s = 0.5 * y * (1.0 + jnp.tanh(math.sqrt(2.0 / math.pi) * (y + 0.044715 * (y * y * y))))
    z = jnp.dot(s.astype(BF16), w_ref[...], preferred_element_type=F32) + b_ref[...]
    o_ref[...] = (s * _sigmoid(z) * g_ref[...].astype(F32)).astype(o_ref.dtype)


def _glu(y, gate, w, b):
    m, width = y.shape
    tm = _row_tile(m, 512)
    row = pl.BlockSpec((tm, width), lambda i: (i, 0))
    return pl.pallas_call(
        _glu_kernel,
        grid=(m // tm,),
        in_specs=[row, row, pl.BlockSpec((width, width), lambda i: (0, 0)),
                  pl.BlockSpec((1, width), lambda i: (0, 0))],
        out_specs=row,
        out_shape=jax.ShapeDtypeStruct((m, width), BF16),
        compiler_params=_params("parallel"),
        name="glu",
    )(y, gate, w, b.reshape(1, width))


def _merge_kernel(*refs):
    o_refs, l_refs = refs[:N_DIL], refs[N_DIL:2 * N_DIL]
    ga_ref, ys_ref, ma_ref, ms_ref, wa_ref, ws_ref, out_ref = refs[2 * N_DIL:2 * N_DIL + 7]
    a_ref, otok_ref, ltok_ref = refs[2 * N_DIL + 7:]

    def token_order(ref, cols, stage):
        dil, per = ref.shape[1], ref.shape[2]
        if dil == 1:
            return ref[0, 0, :, cols]
        for r in range(dil):
            stage[pl.ds(r, per, stride=dil), :] = ref[0, r, :, cols]
        return stage[...]

    ls = [token_order(l_refs[g], slice(None), ltok_ref.at[g]) for g in range(N_DIL)]
    mx = jnp.maximum(jnp.maximum(ls[0], ls[1]), ls[2])
    es = [jnp.exp(l - mx) for l in ls]
    den = es[0] + es[1] + es[2]
    wts = [e / den for e in es]
    tm, aw = a_ref.shape
    half = out_ref.shape[1] // 2
    halves = [slice(c * half, (c + 1) * half) for c in range(2)]
    ys = ys_ref[...]
    acc = [ms_ref[:, cs].astype(F32) * jnp.dot(ys, ws_ref[:, cs], preferred_element_type=F32) for cs in halves]
    group = min(aw, MXU_COLS) // HEAD_DIM
    ya = [None, None]
    for h0 in range(0, aw // HEAD_DIM, group):
        for h in range(h0, h0 + group):
            sl = slice(h * HEAD_DIM, (h + 1) * HEAD_DIM)
            attn = None
            for g in range(N_DIL):
                term = (jnp.broadcast_to(wts[g][:, h:h + 1], (tm, HEAD_DIM))
                        * token_order(o_refs[g], sl, otok_ref.at[g]))
                attn = term if attn is None else attn + term
            a_ref[:, sl] = (attn * ga_ref[:, sl].astype(F32)).astype(BF16)
        ks = slice(h0 * HEAD_DIM, (h0 + group) * HEAD_DIM)
        for c, cs in enumerate(halves):
            part = jnp.dot(a_ref[:, ks], wa_ref[ks, cs], preferred_element_type=F32)
            ya[c] = part if ya[c] is None else ya[c] + part
    for c, cs in enumerate(halves):
        out_ref[:, cs] = (ma_ref[:, cs].astype(F32) * ya[c] + acc[c]).astype(out_ref.dtype)


def _merge(outs, lses, g_attn, y_ssm, gates, w_a, w_s):
    m, aw = g_attn.shape
    d = w_a.shape[1]
    nb = outs[0].shape[0]
    seq = m // nb
    tm = _row_tile(seq, 512)
    bps = seq // tm
    row = pl.BlockSpec((tm, aw), lambda i: (i, 0))
    ol_specs = []
    for arr in list(outs) + list(lses):
        dil = arr.shape[1]
        assert tm % (dil * ROW_TILE) == 0
        ol_specs.append(pl.BlockSpec((1, dil, tm // dil, arr.shape[3]),
                                     lambda i: (lax.div(i, bps), 0, lax.rem(i, bps), 0)))
    scratch = [pltpu.VMEM((tm, aw), BF16), pltpu.VMEM((N_DIL, tm, HEAD_DIM), F32),
               pltpu.VMEM((N_DIL, tm, HEAD_DIM), F32)]
    once = pl.Buffered(1)
    return pl.pallas_call(
        _merge_kernel,
        grid=(m // tm,),
        in_specs=ol_specs + [row, row, pl.BlockSpec((tm, d), lambda i: (i, 0)),
                             pl.BlockSpec((tm, d), lambda i: (i, 1)),
                             pl.BlockSpec((aw, d), lambda i: (0, 0), pipeline_mode=once),
                             pl.BlockSpec((aw, d), lambda i: (0, 0), pipeline_mode=once)],
        out_specs=pl.BlockSpec((tm, d), lambda i: (i, 0)),
        out_shape=jax.ShapeDtypeStruct((m, d), BF16),
        scratch_shapes=scratch,
        compiler_params=_params("parallel"),
        name="merge",
    )(*outs, *lses, g_attn, y_ssm, gates, gates, w_a, w_s)


def _out_kernel(x_ref, m_ref, w_ref, o_ref):
    o_ref[...] = x_ref[...] + jnp.dot(m_ref[...], w_ref[...], preferred_element_type=F32)


def _out_proj(x, merged, w):
    m, d = x.shape
    tm = _row_tile(m, 1024)
    tn = min(d, 1024)
    return pl.pallas_call(
        _out_kernel,
        grid=(m // tm, d // tn),
        in_specs=[pl.BlockSpec((tm, tn), lambda i, j: (i, j)),
                  pl.BlockSpec((tm, d), lambda i, j: (i, 0)),
                  pl.BlockSpec((d, tn), lambda i, j: (0, j))],
        out_specs=pl.BlockSpec((tm, tn), lambda i, j: (i, j)),
        out_shape=jax.ShapeDtypeStruct((m, d), F32),
        compiler_params=_params("parallel", "arbitrary"),
        name="out_proj",
    )(x, merged, w)


def _sample_front_kernel(x_ref, gw_ref, w_ref, nwq_ref, nwk_ref, cos_ref, sin_ref, o_ref, nw_ref):
    j = pl.program_id(0)
    x = x_ref[...]
    xn = x * lax.rsqrt(jnp.mean(x * x, axis=-1, keepdims=True) + NORM_EPS) * gw_ref[...]
    acc = jnp.dot(xn.astype(BF16), w_ref[...], preferred_element_type=F32)
    qk, v_end = 2 * N_DIL, 3 * N_DIL

    @pl.when(j < qk)
    def _():
        nw_ref[...] = jnp.where(j < N_DIL, nwq_ref[...], nwk_ref[...])
        o_ref[...] = acc
        _norm_rope_inplace(o_ref, nw_ref, cos_ref, sin_ref)

    @pl.when(((j >= qk) & (j < v_end)) | (j == v_end + 1))
    def _():
        o_ref[...] = acc

    @pl.when((j == v_end) | (j == v_end + 2))
    def _():
        o_ref[...] = acc * _sigmoid(acc)

    @pl.when(j > v_end + 2)
    def _():
        o_ref[...] = _sigmoid(acc)


def _sample_front(x, norm_w, w, q_norm_w, k_norm_w, cos, sin):
    m, d = x.shape
    aw = d // 2
    n_cols = w.shape[1]
    row = pl.BlockSpec((m, HEAD_DIM), lambda j: (0, 0))
    one = pl.BlockSpec((1, HEAD_DIM), lambda j: (0, 0))
    return pl.pallas_call(
        _sample_front_kernel,
        grid=(n_cols // aw,),
        in_specs=[pl.BlockSpec((m, d), lambda j: (0, 0)), pl.BlockSpec((1, d), lambda j: (0, 0)),
                  pl.BlockSpec((d, aw), lambda j: (0, j)), one, one, row, row],
        out_specs=pl.BlockSpec((m, aw), lambda j: (0, j)),
        out_shape=jax.ShapeDtypeStruct((m, n_cols), F32),
        scratch_shapes=[pltpu.VMEM((1, HEAD_DIM), F32)],
        compiler_params=_params("arbitrary"),
        name="sample_front",
    )(x, norm_w.reshape(1, d), w, q_norm_w.reshape(1, HEAD_DIM), k_norm_w.reshape(1, HEAD_DIM), cos, sin)


def _layer(x, cos, sin, lw, nb, seq, layer, depth, kv_prev, sample, shift=None):
    (norm_w, w_in, q_norm_w, k_norm_w, w_glu, b_glu, w_br_attn, w_br_ssm, w_out, ssm_d, s5prm) = lw
    m, d = x.shape
    aw = d // 2
    qkv = N_DIL * aw
    hg = aw // HEAD_DIM

    new_kv = []
    if sample is None:
        xn = _rmsnorm(x, norm_w)
        g_attn = _proj(xn, w_in, 3 * qkv, aw, "silu", BF16)
        u = _proj(xn, w_in, 3 * qkv + aw, aw, "plain", F32)
        g_ssm = _proj(xn, w_in, 3 * qkv + 2 * aw, aw, "silu", BF16)
        gates = _proj(xn, w_in, 3 * qkv + 3 * aw, 2 * d, "sigmoid", BF16)
        qds = _q_all(xn, w_in, nb, seq, q_norm_w, cos, sin)
        outs, lses = [], []
        for g in range(N_DIL):
            kd, vd, kv = _kv(xn, w_in, g, nb, seq, k_norm_w, cos, sin, layer, depth,
                             None if kv_prev is None else kv_prev[g])
            o, lse = _attn_prompt(qds[g], kd, vd)
            outs.append(o)
            lses.append(lse)
            new_kv.append(kv)
        y, hre, him, shifted = _s5_prompt(u, s5prm, ssm_d, nb, seq, shift)
    else:
        caches, h0re, h0im = sample
        front = _sample_front(x, norm_w, w_in, q_norm_w, k_norm_w, cos, sin)
        q, k, v = front[:, :qkv], front[:, qkv:2 * qkv], front[:, 2 * qkv:3 * qkv]
        g_attn, u = front[:, 3 * qkv:3 * qkv + aw], front[:, 3 * qkv + aw:3 * qkv + 2 * aw]
        g_ssm, gates = front[:, 3 * qkv + 2 * aw:3 * qkv + 3 * aw], front[:, 3 * qkv + 3 * aw:]
        tok = (nb, seq, N_DIL * hg, HEAD_DIM)
        o, lse = _attn_sample(q.reshape(tok), k.reshape(tok), v.reshape(tok), caches, layer)
        o = o.reshape(m, qkv)
        lse = jnp.pad(lse[..., 0].reshape(m, N_DIL, hg), ((0, 0), (0, 0), (0, HEAD_DIM - hg)))
        outs = [o[:, g * aw:(g + 1) * aw].reshape(1, 1, m, aw) for g in range(N_DIL)]
        lses = [lse[:, g].reshape(1, 1, m, HEAD_DIM) for g in range(N_DIL)]
        ut = u.reshape(nb, seq, aw).transpose(1, 0, 2)
        yt, hre, him = _s5_sample(ut, h0re, h0im, s5prm, ssm_d)
        y = yt.transpose(1, 0, 2).reshape(m, aw)
        k5, v5 = k.reshape(nb, seq, N_DIL, hg, HEAD_DIM), v.reshape(nb, seq, N_DIL, hg, HEAD_DIM)
        new_kv = [jnp.stack([k5[:, :, g], v5[:, :, g]], axis=2) for g in range(N_DIL)]
        shifted = None

    y_ssm = _glu(y, g_ssm, w_glu, b_glu)
    merged = _merge(outs, lses, g_attn, y_ssm, gates, w_br_attn, w_br_ssm)
    x_out = _out_proj(x, merged, w_out)

    n_grp = aw // SSM_GROUP_CH
    state = jnp.stack([hre, him], axis=-1).reshape(nb, n_grp, SSM_STATE, 2)
    return x_out, new_kv, state, shifted


def kernel(x_prompt, x_sample, cache_kv_d1, cache_kv_d4, cache_kv_d16, state_ssm, norm_w, w_in, q_norm_w, k_norm_w, ssm_lambda_re, ssm_lambda_im, ssm_log_dt, ssm_b_re, ssm_b_im, ssm_c_re, ssm_c_im, ssm_d, w_glu, b_glu, w_br_attn, w_br_ssm, w_out):
    nb_p, seq_p, d = x_prompt.shape
    nb_s, seq_s, _ = x_sample.shape
    depth = norm_w.shape[0]
    caches = (cache_kv_d1, cache_kv_d4, cache_kv_d16)

    cos_p, sin_p = _rope_tables(jnp.arange(seq_p, dtype=F32))
    cos_s, sin_s = _rope_tables(PAST_LEN + jnp.arange(seq_s, dtype=F32))
    cos_s, sin_s = jnp.tile(cos_s, (nb_s, 1)), jnp.tile(sin_s, (nb_s, 1))

    hp = x_prompt.reshape(nb_p * seq_p, d)
    hs = x_sample.reshape(nb_s * seq_s, d)
    lws = []
    for l in range(depth):
        s5prm = _s5_params(ssm_lambda_re[l], ssm_lambda_im[l], ssm_log_dt[l], ssm_b_re[l], ssm_b_im[l],
                           ssm_c_re[l], ssm_c_im[l], _s5_chunk(seq_p) // ROW_TILE)
        lws.append((norm_w[l], _layer_bf16(w_in, l), q_norm_w[l], k_norm_w[l], _layer_bf16(w_glu, l), b_glu[l],
                    _layer_bf16(w_br_attn, l), _layer_bf16(w_br_ssm, l), _layer_bf16(w_out, l), ssm_d[l], s5prm))

    kv_s_new = [[] for _ in range(N_DIL)]
    ssm_s = []
    n_state = state_ssm.shape[2] * state_ssm.shape[3]
    for l in range(depth):
        h0re = state_ssm[l, ..., 0].reshape(nb_s, n_state)
        h0im = state_ssm[l, ..., 1].reshape(nb_s, n_state)
        hs, new_kv_s, st_s, _ = _layer(hs, cos_s, sin_s, lws[l], nb_s, seq_s, l, depth, None,
                                       (caches, h0re, h0im))
        for g in range(N_DIL):
            kv_s_new[g].append(new_kv_s[g])
        ssm_s.append(st_s)

    kv_p, kv_s, ssm_p = None, None, []
    for l in range(depth):
        shift = (caches, [jnp.stack(n) for n in kv_s_new]) if l == 0 else None
        hp, kv_p, st, shifted = _layer(hp, cos_p, sin_p, lws[l], nb_p, seq_p, l, depth, kv_p, None, shift)
        kv_s = shifted if shifted is not None else kv_s
        ssm_p.append(st)

    return (hp.reshape(x_prompt.shape), hs.reshape(x_sample.shape), kv_p[0], kv_p[1], kv_p[2], jnp.stack(ssm_p),
            kv_s[0], kv_s[1], kv_s[2], jnp.stack(ssm_s))
```

```python
import functools
import math

import jax
import jax.numpy as jnp
from jax import lax
from jax.experimental import pallas as pl
from jax.experimental.pallas import tpu as pltpu

F32 = jnp.float32
BF16 = jnp.bfloat16

HEAD_DIM = 128
WINDOWS = (128, 512, 2048)
DILATIONS = (1, 4, 16)
N_DIL = 3
SPAN = 128
SSM_GROUP_CH = 16
SSM_STATE = 64
GROUPS_PER_CHUNK = 8
ROPE_THETA = 10000.0
NORM_EPS = 1e-6
PAST_LEN = 16384
NEG_BIG = -1e30
VMEM_LIMIT = 56 * 1024 * 1024
ROW_TILE = 8
MXU_COLS = 256
QK_ROW_CHUNK = 64
KV_COPY_BYTES = 4 * 1024 * 1024


def _params(*sem):
    return pltpu.CompilerParams(dimension_semantics=sem, vmem_limit_bytes=VMEM_LIMIT)


def _sigmoid(x):
    return 1.0 / (1.0 + jnp.exp(-x))


def _row_tile(m, cap):
    t = min(m, cap)
    assert m % t == 0, (m, t)
    return t


def _rmsnorm_kernel(x_ref, w_ref, o_ref):
    x = x_ref[...]
    y = x * lax.rsqrt(jnp.mean(x * x, axis=-1, keepdims=True) + NORM_EPS)
    o_ref[...] = (y * w_ref[...]).astype(o_ref.dtype)


def _rmsnorm(x, w):
    m, d = x.shape
    tm = _row_tile(m, 512)
    return pl.pallas_call(
        _rmsnorm_kernel,
        grid=(m // tm,),
        in_specs=[pl.BlockSpec((tm, d), lambda i: (i, 0)),
                  pl.BlockSpec((1, d), lambda i: (0, 0))],
        out_specs=pl.BlockSpec((tm, d), lambda i: (i, 0)),
        out_shape=jax.ShapeDtypeStruct((m, d), BF16),
        compiler_params=_params("parallel"),
        name="rmsnorm",
    )(x, w.reshape(1, d))


def _cast_kernel(w_ref, o_ref):
    o_ref[...] = w_ref[...].astype(o_ref.dtype)


def _layer_bf16(w, layer):
    _, rows, cols = w.shape
    tr, tc = min(rows, 512), min(cols, 2048)
    assert rows % tr == 0 and cols % tc == 0
    return pl.pallas_call(
        _cast_kernel,
        grid=(rows // tr, cols // tc),
        in_specs=[pl.BlockSpec((None, tr, tc), lambda i, j: (layer, i, j))],
        out_specs=pl.BlockSpec((tr, tc), lambda i, j: (i, j)),
        out_shape=jax.ShapeDtypeStruct((rows, cols), BF16),
        compiler_params=_params("parallel", "parallel"),
        name="cast_bf16",
    )(w)


def _norm_rope_inplace(t_ref, nw_ref, cos_ref, sin_ref, head_major=False):
    rows = cos_ref.shape[0]
    n_heads = t_ref.shape[0] // rows if head_major else t_ref.shape[1] // HEAD_DIM
    rc = min(rows, QK_ROW_CHUNK)
    half = HEAD_DIM // 2
    nw = nw_ref[...]
    nw_swapped = pltpu.roll(nw, half, axis=1)

    def chunk(c, carry):
        r0 = pl.multiple_of(c * rc, rc)
        cs = cos_ref[pl.ds(r0, rc), :] * nw
        ss = sin_ref[pl.ds(r0, rc), :] * nw_swapped
        for h in range(n_heads):
            idx = (pl.ds(h * rows + r0, rc), slice(None)) if head_major else (
                pl.ds(r0, rc), slice(h * HEAD_DIM, (h + 1) * HEAD_DIM))
            a = t_ref[idx]
            r = lax.rsqrt(jnp.mean(a * a, axis=-1, keepdims=True) + NORM_EPS)
            t_ref[idx] = (a * cs + pltpu.roll(a, half, axis=1) * ss) * r
        return carry

    lax.fori_loop(0, rows // rc, chunk, 0)


def _proj_kernel(x_ref, w_ref, o_ref, *, epilogue):
    acc = jnp.dot(x_ref[...], w_ref[...], preferred_element_type=F32)
    if epilogue == "silu":
        acc = acc * _sigmoid(acc)
    elif epilogue == "sigmoid":
        acc = _sigmoid(acc)
    o_ref[...] = acc.astype(o_ref.dtype)


def _proj(xn, w, col0, ncols, epilogue, out_dtype):
    m, d = xn.shape
    tm = _row_tile(m, 1024)
    tn = min(ncols, 1024)
    assert ncols % tn == 0 and col0 % tn == 0
    jb = col0 // tn
    return pl.pallas_call(
        functools.partial(_proj_kernel, epilogue=epilogue),
        grid=(m // tm, ncols // tn),
        in_specs=[pl.BlockSpec((tm, d), lambda i, j: (i, 0)),
                  pl.BlockSpec((d, tn), lambda i, j: (0, jb + j))],
        out_specs=pl.BlockSpec((tm, tn), lambda i, j: (i, j)),
        out_shape=jax.ShapeDtypeStruct((m, ncols), out_dtype),
        compiler_params=_params("parallel", "arbitrary"),
        name="proj_" + epilogue,
    )(xn, w)


def _norm_rope_head(a, nw_ref, cos_ref, sin_ref, store):
    rows = a.shape[0]
    rc = min(rows, QK_ROW_CHUNK)
    half = HEAD_DIM // 2
    nw = nw_ref[...]
    nw_swapped = pltpu.roll(nw, half, axis=1)
    ssq = jnp.dot((a * a).astype(BF16), jnp.ones((HEAD_DIM, HEAD_DIM), BF16), preferred_element_type=F32)
    for r0 in range(0, rows, rc):
        ac = a[r0:r0 + rc]
        cs = cos_ref[r0:r0 + rc, :] * nw
        ss = sin_ref[r0:r0 + rc, :] * nw_swapped
        r = lax.rsqrt(ssq[r0:r0 + rc] * (1.0 / HEAD_DIM) + NORM_EPS)
        store(r0, rc, (ac * cs + pltpu.roll(ac, half, axis=1) * ss) * r)


def _project_dilated(x, w_ref, nw_ref, cos_ref, sin_ref, hm_ref, d_ref, dil):
    tm = x.shape[0]
    n_heads = w_ref.shape[1] // HEAD_DIM
    per = tm // dil
    group = n_heads
    for h0 in range(0, n_heads, group):
        acc = jnp.dot(x, w_ref[:, h0 * HEAD_DIM:(h0 + group) * HEAD_DIM], preferred_element_type=F32)
        for h in range(h0, h0 + group):
            a = acc[:, (h - h0) * HEAD_DIM:(h - h0 + 1) * HEAD_DIM]
            if nw_ref is None:
                hm_ref[h * tm:(h + 1) * tm, :] = a
            else:
                def store(r0, rc, chunk, h=h):
                    hm_ref[h * tm + r0:h * tm + r0 + rc, :] = chunk

                _norm_rope_head(a, nw_ref, cos_ref, sin_ref, store)
            for r in range(dil):
                rows = pl.ds(h * tm + r, per, stride=dil) if dil > 1 else pl.ds(h * tm, per)
                d_ref[0, r, :, h * HEAD_DIM:(h + 1) * HEAD_DIM] = hm_ref[rows, :].astype(d_ref.dtype)


def _q_kernel(x_ref, w_ref, nw_ref, cos_ref, sin_ref, *rest):
    qd_refs, hm_ref = rest[:N_DIL], rest[N_DIL]
    for g in range(N_DIL):
        @pl.when(pl.program_id(0) == g)
        def _(g=g):
            _project_dilated(x_ref[...], w_ref, nw_ref, cos_ref, sin_ref, hm_ref, qd_refs[g], DILATIONS[g])


def _q_all(xn, w, nb, seq, q_norm_w, cos, sin):
    m, d = xn.shape
    aw = d // 2
    tm = _row_tile(seq, 1024)
    bps = seq // tm
    out_specs, out_shape = [], []
    for g in range(N_DIL):
        dil = DILATIONS[g]
        assert (tm // dil) % 16 == 0

        def q_map(gi, b, i, g=g):
            return (jnp.where(gi < g, 0, jnp.where(gi == g, b, nb - 1)), 0,
                    jnp.where(gi < g, 0, jnp.where(gi == g, i, bps - 1)), 0)

        out_specs.append(pl.BlockSpec((1, dil, tm // dil, aw), q_map))
        out_shape.append(jax.ShapeDtypeStruct((nb, dil, seq // dil, aw), BF16))
    return pl.pallas_call(
        _q_kernel,
        grid=(N_DIL, nb, bps),
        in_specs=[pl.BlockSpec((tm, d), lambda g, b, i: (b * bps + i, 0)),
                  pl.BlockSpec((d, aw), lambda g, b, i: (0, g)),
                  pl.BlockSpec((1, HEAD_DIM), lambda g, b, i: (0, 0)),
                  pl.BlockSpec((tm, HEAD_DIM), lambda g, b, i: (i, 0)),
                  pl.BlockSpec((tm, HEAD_DIM), lambda g, b, i: (i, 0))],
        out_specs=out_specs,
        out_shape=out_shape,
        scratch_shapes=[pltpu.VMEM((aw // HEAD_DIM * tm, HEAD_DIM), F32)],
        compiler_params=_params("arbitrary", "arbitrary", "arbitrary"),
        name="q_all",
    )(xn, w, q_norm_w.reshape(1, HEAD_DIM), cos, sin)


def _kv_kernel(x_ref, wk_ref, wv_ref, nw_ref, cos_ref, sin_ref, *rest, dil, first_kept):
    kd_ref, vd_ref, kv_ref, hk_ref, hv_ref = rest[-5:]
    half_idx = pl.program_id(2)
    tm = x_ref.shape[0]
    n_heads = wk_ref.shape[1] // HEAD_DIM

    @pl.when(half_idx == 0)
    def _():
        x = x_ref[...]
        _project_dilated(x, wk_ref, nw_ref, cos_ref, sin_ref, hk_ref, kd_ref, dil)
        _project_dilated(x, wv_ref, None, None, None, hv_ref, vd_ref, dil)

    kb = kv_ref.shape[1]
    half = tm // 2

    @pl.when(2 * pl.program_id(1) + half_idx >= first_kept)
    def _():
        base = (half_idx + 1) * half - kb

        def token(t, carry):
            rows = pl.ds(base + t, n_heads, stride=tm)
            kv_ref[0, t, 0] = hk_ref[rows, :]
            kv_ref[0, t, 1] = hv_ref[rows, :]
            return carry

        lax.fori_loop(0, kb, token, 0, unroll=8)


def _kv(xn, w, g, nb, seq, k_norm_w, cos, sin, layer, depth, kv_prev):
    m, d = xn.shape
    aw = d // 2
    hg = aw // HEAD_DIM
    dil = DILATIONS[g]
    keep = min(WINDOWS[g], seq)
    tm = _row_tile(seq, 1024)
    per = tm // dil
    half = tm // 2
    kb = min(keep, half)
    bps = seq // tm
    n_half = 2 * bps
    first_kept = (seq - keep) // half if keep >= half else n_half - 1
    assert per % 16 == 0 and keep % kb == 0 and (seq - keep) % kb == 0
    once = pl.Buffered(1)
    nw = lambda a: a.reshape(1, HEAD_DIM)
    dshape = jax.ShapeDtypeStruct((nb, dil, seq // dil, aw), BF16)
    staging = pltpu.VMEM((hg * tm, HEAD_DIM), F32)

    wspec = lambda col: pl.BlockSpec((d, aw), lambda b, i, h: (0, col), pipeline_mode=once)
    tspec = pl.BlockSpec((tm, HEAD_DIM), lambda b, i, h: (i, 0))
    dspec = pl.BlockSpec((1, dil, per, aw), lambda b, i, h: (b, 0, i, 0))
    kv_shape = jax.ShapeDtypeStruct((depth, nb, keep, 2, hg, HEAD_DIM), F32)
    if kv_prev is None:
        kv_prev = jnp.zeros(kv_shape.shape, kv_shape.dtype)
    kd, vd, kv = pl.pallas_call(
        functools.partial(_kv_kernel, dil=dil, first_kept=first_kept),
        grid=(nb, bps, 2),
        in_specs=[pl.BlockSpec((tm, d), lambda b, i, h: (b * bps + i, 0)),
                  wspec(N_DIL + g), wspec(2 * N_DIL + g),
                  pl.BlockSpec((1, HEAD_DIM), lambda b, i, h: (0, 0)), tspec, tspec,
                  pl.BlockSpec(memory_space=pl.ANY)],
        out_specs=[dspec, dspec,
                   pl.BlockSpec((None, 1, kb, 2, hg, HEAD_DIM),
                                lambda b, i, h: (layer, b, jnp.maximum(2 * i + h - first_kept, 0), 0, 0, 0))],
        out_shape=[dshape, dshape, kv_shape],
        scratch_shapes=[staging, staging],
        input_output_aliases={6: 2},
        compiler_params=_params("parallel", "arbitrary", "arbitrary"),
        name="kv_d%d" % dil,
    )(xn, w, w, nw(k_norm_w), cos, sin, kv_prev)
    return kd, vd, kv


def _rope_tables(pos):
    half = HEAD_DIM // 2
    inv_freq = jnp.power(ROPE_THETA, -jnp.arange(half, dtype=F32) * (2.0 / HEAD_DIM))
    ang = pos[:, None] * inv_freq[None, :]
    cos, sin = jnp.cos(ang), jnp.sin(ang)
    return jnp.concatenate([cos, cos], axis=-1), jnp.concatenate([-sin, sin], axis=-1)


def _attn_prompt_kernel(q_ref, kp_ref, kc_ref, vp_ref, vc_ref, o_ref, lse_ref):
    not_first = pl.program_id(2) > 0
    a = lax.broadcasted_iota(jnp.int32, (SPAN, 2 * SPAN), 0)
    c = lax.broadcasted_iota(jnp.int32, (SPAN, 2 * SPAN), 1)
    band = (c >= a) & (c <= a + SPAN)
    band_first = band & ((c >= SPAN) | not_first)
    lane = lax.broadcasted_iota(jnp.int32, (SPAN, HEAD_DIM), 1)
    scale = HEAD_DIM ** -0.5
    for j in range(q_ref.shape[2] // SPAN):
        rows = slice(j * SPAN, (j + 1) * SPAN)
        lse_tile = jnp.zeros((SPAN, HEAD_DIM), F32)
        for h in range(q_ref.shape[3] // HEAD_DIM):
            sl = slice(h * HEAD_DIM, (h + 1) * HEAD_DIM)
            q = q_ref[0, 0, rows, sl]
            if j == 0:
                k = jnp.concatenate([kp_ref[0, 0, :, sl], kc_ref[0, 0, rows, sl]], axis=0)
                v = jnp.concatenate([vp_ref[0, 0, :, sl], vc_ref[0, 0, rows, sl]], axis=0)
            else:
                k = kc_ref[0, 0, (j - 1) * SPAN:(j + 1) * SPAN, sl]
                v = vc_ref[0, 0, (j - 1) * SPAN:(j + 1) * SPAN, sl]
            s = lax.dot_general(q, k, (((1,), (1,)), ((), ())), preferred_element_type=F32) * scale
            s = jnp.where(band_first if j == 0 else band, s, NEG_BIG)
            m = jnp.max(s, axis=-1, keepdims=True)
            p = jnp.exp(s - m)
            l = jnp.sum(p, axis=-1, keepdims=True)
            o = jnp.dot(p.astype(BF16), v, preferred_element_type=F32) / l
            o_ref[0, 0, rows, sl] = o.astype(o_ref.dtype)
            lse_tile = jnp.where(lane == h, m + jnp.log(l), lse_tile)
        lse_ref[0, 0, rows, :] = lse_tile


def _attn_prompt(q, k, v):
    nb, dil, length, aw = q.shape
    n_blk = length // SPAN
    qb = min(4, n_blk)
    assert length % SPAN == 0 and n_blk % qb == 0 and aw // HEAD_DIM <= HEAD_DIM
    cur = pl.BlockSpec((1, 1, qb * SPAN, aw), lambda b, r, n: (b, r, n, 0))
    prev = pl.BlockSpec((1, 1, SPAN, aw), lambda b, r, n: (b, r, jnp.maximum(qb * n - 1, 0), 0))
    return pl.pallas_call(
        _attn_prompt_kernel,
        grid=(nb, dil, n_blk // qb),
        in_specs=[cur, prev, cur, prev, cur],
        out_specs=[cur, pl.BlockSpec((1, 1, qb * SPAN, HEAD_DIM), lambda b, r, n: (b, r, n, 0))],
        out_shape=[jax.ShapeDtypeStruct(q.shape, BF16),
                   jax.ShapeDtypeStruct((nb, dil, length, HEAD_DIM), F32)],
        compiler_params=_params("parallel", "parallel", "arbitrary"),
        name="attn_prompt_d%d" % dil,
    )(q, k, k, v, v)


def _attn_sample_kernel(q_ref, kn_ref, vn_ref, c1_ref, c4_ref, c16_ref, o_ref, lse_ref):
    n_new, n_heads, _ = q_ref.shape
    hg = n_heads // N_DIL
    scale = HEAD_DIM ** -0.5
    caches = (c1_ref, c4_ref, c16_ref)
    for g in range(N_DIL):
        hs = slice(g * hg, (g + 1) * hg)
        for t in range(n_new):
            q = q_ref[t, hs, :]
            if DILATIONS[g] == 1:
                kc, vc = caches[g][:, 0], caches[g][:, 1]
                rows = lax.broadcasted_iota(jnp.int32, (kc.shape[0], hg, 1), 0)
                s = jnp.sum(kc * q[None], axis=-1, keepdims=True) * scale
                s = jnp.where(rows >= t, s, NEG_BIG)
                new = range(t + 1)
            else:
                kc, vc = caches[g][:, t, 0], caches[g][:, t, 1]
                s = jnp.sum(kc * q[None], axis=-1, keepdims=True) * scale
                new = (t,)
            s_new = [jnp.sum(kn_ref[u, hs, :] * q, axis=-1, keepdims=True) * scale for u in new]
            m = jnp.max(s, axis=0)
            for sn in s_new:
                m = jnp.maximum(m, sn)
            p = jnp.exp(s - m[None])
            l = jnp.sum(p, axis=0)
            acc = jnp.sum(p * vc, axis=0)
            for u, sn in zip(new, s_new):
                pn = jnp.exp(sn - m)
                l = l + pn
                acc = acc + pn * vn_ref[u, hs, :]
            o_ref[t, hs, :] = acc / l
            lse_ref[t, hs, :] = jnp.broadcast_to(m + jnp.log(l), (hg, HEAD_DIM))


def _attn_sample(q, kn, vn, caches, layer):
    nb, n_new, n_heads, _ = q.shape
    hg = n_heads // N_DIL
    views, specs = [], []
    for g in range(N_DIL):
        dil = DILATIONS[g]
        depth, _, buf = caches[g].shape[:3]
        assert buf == WINDOWS[g] and (dil == 1 or dil % n_new == 0)
        if dil == 1:
            views.append(caches[g])
            specs.append(pl.BlockSpec((None, None, buf, 2, hg, HEAD_DIM),
                                      lambda b: (layer, b, 0, 0, 0, 0)))
        else:
            views.append(caches[g].reshape(depth, nb, SPAN, dil, 2, hg, HEAD_DIM))
            specs.append(pl.BlockSpec((None, None, SPAN, n_new, 2, hg, HEAD_DIM),
                                      lambda b: (layer, b, 0, 0, 0, 0, 0)))
    tok = pl.BlockSpec((None, n_new, n_heads, HEAD_DIM), lambda b: (b, 0, 0, 0))
    return pl.pallas_call(
        _attn_sample_kernel,
        grid=(nb,),
        in_specs=[tok, tok, tok] + specs,
        out_specs=[tok, tok],
        out_shape=[jax.ShapeDtypeStruct(q.shape, F32)] * 2,
        compiler_params=_params("parallel"),
        name="attn_sample",
    )(q, kn, vn, *views)


def _s5_params_kernel(lre_ref, lim_ref, ldt_ref, bre_ref, bim_ref, ptab_ref, bbre_ref, bbim_ref):
    dt = jnp.exp(ldt_ref[...])
    lre, lim = lre_ref[...], lim_ref[...]
    xr, xi = lre * dt, lim * dt
    mag = jnp.exp(xr)
    ar, ai = mag * jnp.cos(xi), mag * jnp.sin(xi)
    shape = (ROW_TILE, xr.shape[1])
    abr, abi = jnp.broadcast_to(ar, shape), jnp.broadcast_to(ai, shape)
    ptab_ref[0, 0] = abr
    ptab_ref[1, 0] = abi

    def next_power(i, carry):
        pr, pi = carry
        pr, pi = pr * abr - pi * abi, pr * abi + pi * abr
        ptab_ref[0, i] = pr
        ptab_ref[1, i] = pi
        return pr, pi

    lax.fori_loop(1, ptab_ref.shape[1], next_power, (abr, abi))
    nr, ni = ar - 1.0, ai
    den = lre * lre + lim * lim
    fr = (nr * lre + ni * lim) / den
    fi = (ni * lre - nr * lim) / den
    bre, bim = bre_ref[...], bim_ref[...]
    bbre_ref[...] = fr * bre - fi * bim
    bbim_ref[...] = fr * bim + fi * bre


def _s5_params(lam_re, lam_im, log_dt, b_re, b_im, c_re, c_im, seg_len):
    n_grp, n_st = lam_re.shape
    n_ch = b_re.shape[2]
    n = n_grp * n_st
    gc = GROUPS_PER_CHUNK
    n_chunk = n_grp // gc
    sw = gc * n_st
    to_lanes = lambda b: b.transpose(2, 0, 1).reshape(n_ch, n)
    lane = lambda rows: pl.BlockSpec((rows, sw), lambda j: (0, j))
    ptab, bb_re, bb_im = pl.pallas_call(
        _s5_params_kernel,
        grid=(n_chunk,),
        in_specs=[lane(1)] * 3 + [lane(n_ch)] * 2,
        out_specs=[pl.BlockSpec((2, seg_len, ROW_TILE, sw), lambda j: (0, 0, 0, j)), lane(n_ch), lane(n_ch)],
        out_shape=[jax.ShapeDtypeStruct((2, seg_len, ROW_TILE, n), F32),
                   jax.ShapeDtypeStruct((n_ch, n), F32), jax.ShapeDtypeStruct((n_ch, n), F32)],
        compiler_params=_params("parallel"),
        name="s5_params",
    )(lam_re.reshape(1, n), lam_im.reshape(1, n),
      jnp.broadcast_to(log_dt[:, None], (n_grp, n_st)).reshape(1, n), to_lanes(b_re), to_lanes(b_im))

    eye = jnp.eye(gc, dtype=F32)

    def b_blocks(bb):
        bb = bb.reshape(n_ch, n_chunk, gc, n_st)
        blk = jnp.einsum("cjgp,gh->jgchp", bb, eye)
        return blk.reshape(n_chunk, gc * n_ch, gc * n_st).astype(BF16)

    def c_blocks(cc):
        cc = cc.reshape(n_chunk, gc, n_ch, n_st)
        blk = jnp.einsum("jgcp,gh->jgphc", cc, eye)
        return blk.reshape(n_chunk, gc * n_st, gc * n_ch).astype(BF16)

    b_cat = jnp.concatenate([b_blocks(bb_re), b_blocks(bb_im)], axis=2)
    c_cat = jnp.concatenate([c_blocks(c_re), c_blocks(c_im)], axis=1)
    return ptab, b_cat, c_cat


def _cmul_add(xr, xi, ar, ai, br, bi):
    return xr + ar * br - ai * bi, xi + ar * bi + ai * br


def _shift_plan(cache_shape, new_shape, n_steps):
    lb_n, buf = cache_shape[:2]
    keep = buf - new_shape[1]
    row_bytes = 4 * math.prod(cache_shape[2:])
    for per_row in range(1, keep + 1):
        n_jobs = lb_n * per_row
        if keep % per_row == 0 and keep // per_row * row_bytes <= KV_COPY_BYTES and n_steps % n_jobs == 0:
            return keep // per_row, per_row, n_steps // n_jobs
    raise ValueError("window shift does not fit the grid")


def _shift_lag(period):
    return 1 if period == 2 else 2


def _shift_slots(period):
    return 4 if period == 1 else 2


def _window_shift_step(step, n_steps, caches, news, outs, rings, nbufs, in_sem, out_sem, new_sem):
    last = step == n_steps - 1
    for g in range(N_DIL):
        cache, out, ring = caches[g], outs[g], rings[g]
        n_new = news[g].shape[1]
        keep = cache.shape[1] - n_new
        rows, per_row, period = _shift_plan(cache.shape, news[g].shape, n_steps)
        n_jobs = cache.shape[0] * per_row
        slots = ring.shape[0]
        lag = _shift_lag(period)
        assert n_jobs >= slots and (period == 1 or lag < period)

        def load(k, cache=cache, ring=ring, g=g, rows=rows, per_row=per_row, n_new=n_new, slots=slots):
            src = cache.at[lax.div(k, per_row), pl.ds(n_new + lax.rem(k, per_row) * rows, rows)]
            return pltpu.make_async_copy(src, ring.at[lax.rem(k, slots)], in_sem.at[g, lax.rem(k, slots)])

        def store(k, out=out, ring=ring, g=g, rows=rows, per_row=per_row, slots=slots):
            dst = out.at[lax.div(k, per_row), pl.ds(lax.rem(k, per_row) * rows, rows)]
            return pltpu.make_async_copy(ring.at[lax.rem(k, slots)], dst, out_sem.at[g, lax.rem(k, slots)])

        def issue(k, load=load, store=store, slots=slots):
            @pl.when(k >= slots)
            def _():
                store(k - slots).wait()
            load(k).start()

        def forward(k, load=load, store=store):
            load(k).wait()
            store(k).start()

        def drain(k, store=store, slots=slots):
            for back in range(slots - 1, -1, -1):
                store(k - back).wait()

        if period == 1:
            issue(step)

            @pl.when(step >= lag)
            def _(forward=forward, lag=lag):
                forward(step - lag)

            @pl.when(last)
            def _(forward=forward, drain=drain, lag=lag):
                for back in range(lag - 1, -1, -1):
                    forward(step - back)
                drain(step)
        else:
            k = lax.div(step, period)
            phase = lax.rem(step, period)

            @pl.when(phase == 0)
            def _(issue=issue, k=k):
                issue(k)

            @pl.when(phase == lag)
            def _(forward=forward, k=k):
                forward(k)

            @pl.when(last)
            def _(drain=drain, k=k):
                drain(k)

        new_in = pltpu.make_async_copy(news[g], nbufs[g], new_sem.at[g])
        new_out = pltpu.make_async_copy(nbufs[g], out.at[:, pl.ds(keep, n_new)], new_sem.at[N_DIL + g])

        @pl.when(step == 0)
        def _(new_in=new_in, new_out=new_out):
            new_in.start()
            new_in.wait()
            new_out.start()

        @pl.when(last)
        def _(new_out=new_out):
            new_out.wait()


def _s5_scan_kernel(u_ref, b_ref, ptab_ref, c_ref, d_ref, *rest, with_shift):
    if with_shift:
        n = N_DIL
        caches, news = rest[:n], rest[n:2 * n]
        y_ref, hfin_ref = rest[2 * n:2 * n + 2]
        outs = rest[2 * n + 2:3 * n + 2]
        up_ref, s_ref, hb_ref, end_ref = rest[3 * n + 2:3 * n + 6]
        rings, nbufs = rest[3 * n + 6:4 * n + 6], rest[4 * n + 6:5 * n + 6]
        in_sem, out_sem, new_sem = rest[5 * n + 6:]
        _, n_j, n_t = with_shift
        step = (pl.program_id(0) * n_j + pl.program_id(1)) * n_t + pl.program_id(2)
        _window_shift_step(step, math.prod(with_shift), caches, news, outs, rings, nbufs,
                           in_sem, out_sem, new_sem)
    else:
        y_ref, hfin_ref, up_ref, s_ref, hb_ref, end_ref = rest
    tc, sw2 = s_ref.shape
    sw = sw2 // 2
    seg = tc // ROW_TILE
    re, im = slice(0, sw), slice(sw, sw2)

    @pl.when(pl.program_id(2) == 0)
    def _():
        end_ref[...] = jnp.zeros_like(end_ref)

    for i in range(seg):
        up_ref[i * ROW_TILE:(i + 1) * ROW_TILE, :] = u_ref[0, pl.ds(i, ROW_TILE, stride=seg), :]
    s_ref[...] = jnp.dot(up_ref[...].astype(BF16), b_ref[0], preferred_element_type=F32)

    ar, ai = ptab_ref[0, 0], ptab_ref[1, 0]

    def local_step(i, carry):
        hr, hi = carry
        r0 = pl.multiple_of(i * ROW_TILE, ROW_TILE)
        hr, hi = _cmul_add(s_ref[pl.ds(r0, ROW_TILE), re], s_ref[pl.ds(r0, ROW_TILE), im], ar, ai, hr, hi)
        s_ref[pl.ds(r0, ROW_TILE), re] = hr
        s_ref[pl.ds(r0, ROW_TILE), im] = hi
        return hr, hi

    zero = jnp.zeros((ROW_TILE, sw), F32)
    er, ei = lax.fori_loop(0, seg, local_step, (zero, zero), unroll=8)

    row = lax.broadcasted_iota(jnp.int32, (ROW_TILE, sw), 0)
    cr = jnp.where(row == 0, pltpu.roll(end_ref[0], 1, axis=0), pltpu.roll(er, 1, axis=0))
    ci = jnp.where(row == 0, pltpu.roll(end_ref[1], 1, axis=0), pltpu.roll(ei, 1, axis=0))
    mr, mi = ptab_ref[0, seg - 1], ptab_ref[1, seg - 1]
    wr, wi = mr, mi
    for sh in (1, 2, 4):
        gr, gi = jnp.where(row >= sh, wr, 0.0), jnp.where(row >= sh, wi, 0.0)
        cr, ci = _cmul_add(cr, ci, gr, gi, pltpu.roll(cr, sh, axis=0), pltpu.roll(ci, sh, axis=0))
        wr, wi = wr * wr - wi * wi, 2.0 * wr * wi
    fr, fi = _cmul_add(er, ei, mr, mi, cr, ci)
    end_ref[0] = fr
    end_ref[1] = fi
    hfin_ref[0, 0] = fr
    hfin_ref[0, 1] = fi

    def fix_up(k, carry):
        r0 = pl.multiple_of(k * 2 * ROW_TILE, 2 * ROW_TILE)
        hs, ns = [], []
        for half in range(2):
            i = 2 * k + half
            rows = pl.ds(r0 + half * ROW_TILE, ROW_TILE)
            hr, hi = _cmul_add(s_ref[rows, re], s_ref[rows, im], ptab_ref[0, i], ptab_ref[1, i], cr, ci)
            hs.append(hr)
            ns.append(-hi)
        hb_ref[pl.ds(r0, 2 * ROW_TILE), re] = jnp.concatenate(hs, axis=0).astype(BF16)
        hb_ref[pl.ds(r0, 2 * ROW_TILE), im] = jnp.concatenate(ns, axis=0).astype(BF16)
        return carry

    lax.fori_loop(0, seg // 2, fix_up, 0, unroll=4)
    yp = jnp.dot(hb_ref[...], c_ref[0], preferred_element_type=F32) + d_ref[...] * up_ref[...]
    for i in range(seg):
        y_ref[0, pl.ds(i, ROW_TILE, stride=seg), :] = yp[i * ROW_TILE:(i + 1) * ROW_TILE, :]


def _s5_chunk(seq):
    return min(seq, 1024)


def _flat_window(a):
    return a.reshape(a.shape[0] * a.shape[1], a.shape[2], a.shape[3] * a.shape[4], a.shape[5])


def _s5_prompt(u, prm, ssm_d, nb, seq, shift=None):
    ptab, b_cat, c_cat = prm
    n_chunk, cw, sw2 = b_cat.shape
    sw = sw2 // 2
    width = u.shape[1]
    tc = _s5_chunk(seq)
    seg = tc // ROW_TILE
    assert seq % tc == 0 and seg % 2 == 0 and ptab.shape[1] == seg
    grid = (nb, n_chunk, seq // tc)
    in_specs = [pl.BlockSpec((1, tc, cw), lambda b, j, t: (b, t, j)),
                pl.BlockSpec((1, cw, sw2), lambda b, j, t: (j, 0, 0)),
                pl.BlockSpec((2, seg, ROW_TILE, sw), lambda b, j, t: (0, 0, 0, j)),
                pl.BlockSpec((1, sw2, cw), lambda b, j, t: (j, 0, 0)),
                pl.BlockSpec((1, cw), lambda b, j, t: (0, j))]
    out_specs = [pl.BlockSpec((1, tc, cw), lambda b, j, t: (b, t, j)),
                 pl.BlockSpec((1, 2, ROW_TILE, sw), lambda b, j, t: (b, 0, 0, j))]
    out_shape = [jax.ShapeDtypeStruct((nb, seq, width), F32),
                 jax.ShapeDtypeStruct((nb, 2, ROW_TILE, n_chunk * sw), F32)]
    scratch = [pltpu.VMEM((tc, cw), F32), pltpu.VMEM((tc, sw2), F32),
               pltpu.VMEM((tc, sw2), BF16), pltpu.VMEM((2, ROW_TILE, sw), F32)]
    args = [u.reshape(nb, seq, width), b_cat, ptab, c_cat, ssm_d.reshape(1, width)]
    if shift is not None:
        caches, news = shift
        cf, nf = [_flat_window(c) for c in caches], [_flat_window(w) for w in news]
        hbm = pl.BlockSpec(memory_space=pl.ANY)
        in_specs += [hbm] * (2 * N_DIL)
        out_specs += [hbm] * N_DIL
        out_shape += [jax.ShapeDtypeStruct(c.shape, c.dtype) for c in cf]
        args += cf + nf
        max_slots = 0
        for c, w in zip(cf, nf):
            rows, _, period = _shift_plan(c.shape, w.shape, math.prod(grid))
            scratch.append(pltpu.VMEM((_shift_slots(period), rows) + c.shape[2:], c.dtype))
            max_slots = max(max_slots, _shift_slots(period))
        scratch += [pltpu.VMEM(w.shape, w.dtype) for w in nf]
        scratch += [pltpu.SemaphoreType.DMA((N_DIL, max_slots)), pltpu.SemaphoreType.DMA((N_DIL, max_slots)),
                    pltpu.SemaphoreType.DMA((2 * N_DIL,))]
    res = pl.pallas_call(
        functools.partial(_s5_scan_kernel, with_shift=grid if shift is not None else None),
        grid=grid,
        in_specs=in_specs,
        out_specs=out_specs,
        out_shape=out_shape,
        scratch_shapes=scratch,
        compiler_params=_params("arbitrary", "arbitrary", "arbitrary") if shift is not None
        else _params("parallel", "parallel", "arbitrary"),
        name="s5_scan",
    )(*args)
    y, hfin = res[0], res[1]
    last = ROW_TILE - 1
    shifted = None if shift is None else [o.reshape(c.shape) for o, c in zip(res[2:], shift[0])]
    return y.reshape(nb * seq, width), hfin[:, 0, last], hfin[:, 1, last], shifted


def _s5_sample_kernel(u_ref, h0re_ref, h0im_ref, b_ref, ptab_ref, c_ref, d_ref, y_ref, hre_ref, him_ref):
    sw = h0re_ref.shape[1]
    ar, ai = ptab_ref[0, 0, 0:1, :], ptab_ref[1, 0, 0:1, :]
    hr, hi = h0re_ref[...], h0im_ref[...]
    for t in range(u_ref.shape[0]):
        u = u_ref[t]
        bu = jnp.dot(u.astype(BF16), b_ref[0], preferred_element_type=F32)
        hr, hi = _cmul_add(bu[:, :sw], bu[:, sw:], ar, ai, hr, hi)
        hb = jnp.concatenate([hr, -hi], axis=1).astype(BF16)
        y_ref[t] = jnp.dot(hb, c_ref[0], preferred_element_type=F32) + d_ref[...] * u
    hre_ref[...] = hr
    him_ref[...] = hi


def _s5_sample(u, h0re, h0im, prm, ssm_d):
    ptab, b_cat, c_cat = prm
    n_chunk, cw, sw2 = b_cat.shape
    sw = sw2 // 2
    n_new, nb, width = u.shape
    return pl.pallas_call(
        _s5_sample_kernel,
        grid=(n_chunk,),
        in_specs=[pl.BlockSpec((n_new, nb, cw), lambda j: (0, 0, j)),
                  pl.BlockSpec((nb, sw), lambda j: (0, j)),
                  pl.BlockSpec((nb, sw), lambda j: (0, j)),
                  pl.BlockSpec((1, cw, sw2), lambda j: (j, 0, 0)),
                  pl.BlockSpec((2, 1, ROW_TILE, sw), lambda j: (0, 0, 0, j)),
                  pl.BlockSpec((1, sw2, cw), lambda j: (j, 0, 0)),
                  pl.BlockSpec((1, cw), lambda j: (0, j))],
        out_specs=[pl.BlockSpec((n_new, nb, cw), lambda j: (0, 0, j)),
                   pl.BlockSpec((nb, sw), lambda j: (0, j)),
                   pl.BlockSpec((nb, sw), lambda j: (0, j))],
        out_shape=[jax.ShapeDtypeStruct((n_new, nb, width), F32),
                   jax.ShapeDtypeStruct((nb, n_chunk * sw), F32),
                   jax.ShapeDtypeStruct((nb, n_chunk * sw), F32)],
        compiler_params=_params("parallel"),
        name="s5_sample",
    )(u, h0re, h0im, b_cat, ptab, c_cat, ssm_d.reshape(1, width))


def _glu_kernel(y_ref, g_ref, w_ref, b_ref, o_ref):
    y = y_ref[...]
    s = 0.5 * y * (1.0 + jnp.tanh(math.sqrt(2.0 / math.pi) * (y + 0.044715 * (y * y * y))))
    z = jnp.dot(s.astype(BF16), w_ref[...], preferred_element_type=F32) + b_ref[...]
    o_ref[...] = (s * _sigmoid(z) * g_ref[...].astype(F32)).astype(o_ref.dtype)


def _glu(y, gate, w, b):
    m, width = y.shape
    tm = _row_tile(m, 1024)
    row = pl.BlockSpec((tm, width), lambda i: (i, 0))
    return pl.pallas_call(
        _glu_kernel,
        grid=(m // tm,),
        in_specs=[row, row, pl.BlockSpec((width, width), lambda i: (0, 0)),
                  pl.BlockSpec((1, width), lambda i: (0, 0))],
        out_specs=row,
        out_shape=jax.ShapeDtypeStruct((m, width), BF16),
        compiler_params=_params("parallel"),
        name="glu",
    )(y, gate, w, b.reshape(1, width))


def _merge_kernel(*refs):
    o_refs, l_refs = refs[:N_DIL], refs[N_DIL:2 * N_DIL]
    ga_ref, ys_ref, ma_ref, ms_ref, wa_ref, ws_ref, out_ref = refs[2 * N_DIL:2 * N_DIL + 7]
    a_ref, otok_ref, ltok_ref = refs[2 * N_DIL + 7:]

    def token_order(ref, cols, stage):
        dil, per = ref.shape[1], ref.shape[2]
        if dil == 1:
            return ref[0, 0, :, cols].astype(F32)
        for r in range(dil):
            stage[pl.ds(r, per, stride=dil), :] = ref[0, r, :, cols].astype(F32)
        return stage[...]

    ls = [token_order(l_refs[g], slice(None), ltok_ref.at[g]) for g in range(N_DIL)]
    mx = jnp.maximum(jnp.maximum(ls[0], ls[1]), ls[2])
    es = [jnp.exp(l - mx) for l in ls]
    den = es[0] + es[1] + es[2]
    wts = [e / den for e in es]
    tm, aw = a_ref.shape
    half = out_ref.shape[1] // 2
    halves = [slice(c * half, (c + 1) * half) for c in range(2)]
    ys = ys_ref[...]
    acc = [ms_ref[:, cs].astype(F32) * jnp.dot(ys, ws_ref[:, cs], preferred_element_type=F32) for cs in halves]
    group = min(aw, MXU_COLS) // HEAD_DIM
    ya = [None, None]
    for h0 in range(0, aw // HEAD_DIM, group):
        for h in range(h0, h0 + group):
            sl = slice(h * HEAD_DIM, (h + 1) * HEAD_DIM)
            attn = None
            for g in range(N_DIL):
                term = (jnp.broadcast_to(wts[g][:, h:h + 1], (tm, HEAD_DIM))
                        * token_order(o_refs[g], sl, otok_ref.at[g]))
                attn = term if attn is None else attn + term
            a_ref[:, sl] = (attn * ga_ref[:, sl].astype(F32)).astype(BF16)
        ks = slice(h0 * HEAD_DIM, (h0 + group) * HEAD_DIM)
        for c, cs in enumerate(halves):
            part = jnp.dot(a_ref[:, ks], wa_ref[ks, cs], preferred_element_type=F32)
            ya[c] = part if ya[c] is None else ya[c] + part
    for c, cs in enumerate(halves):
        out_ref[:, cs] = (ma_ref[:, cs].astype(F32) * ya[c] + acc[c]).astype(out_ref.dtype)


def _merge(outs, lses, g_attn, y_ssm, gates, w_a, w_s):
    m, aw = g_attn.shape
    d = w_a.shape[1]
    nb = outs[0].shape[0]
    seq = m // nb
    tm = _row_tile(seq, 512)
    bps = seq // tm
    row = pl.BlockSpec((tm, aw), lambda i: (i, 0))
    ol_specs = []
    for arr in list(outs) + list(lses):
        dil = arr.shape[1]
        assert tm % (dil * ROW_TILE) == 0
        ol_specs.append(pl.BlockSpec((1, dil, tm // dil, arr.shape[3]),
                                     lambda i: (lax.div(i, bps), 0, lax.rem(i, bps), 0)))
    scratch = [pltpu.VMEM((tm, aw), BF16), pltpu.VMEM((N_DIL, tm, HEAD_DIM), F32),
               pltpu.VMEM((N_DIL, tm, HEAD_DIM), F32)]
    once = pl.Buffered(1)
    return pl.pallas_call(
        _merge_kernel,
        grid=(m // tm,),
        in_specs=ol_specs + [row, row, pl.BlockSpec((tm, d), lambda i: (i, 0)),
                             pl.BlockSpec((tm, d), lambda i: (i, 1)),
                             pl.BlockSpec((aw, d), lambda i: (0, 0), pipeline_mode=once),
                             pl.BlockSpec((aw, d), lambda i: (0, 0), pipeline_mode=once)],
        out_specs=pl.BlockSpec((tm, d), lambda i: (i, 0)),
        out_shape=jax.ShapeDtypeStruct((m, d), BF16),
        scratch_shapes=scratch,
        compiler_params=_params("parallel"),
        name="merge",
    )(*outs, *lses, g_attn, y_ssm, gates, gates, w_a, w_s)


def _out_kernel(x_ref, m_ref, w_ref, o_ref):
    o_ref[...] = x_ref[...] + jnp.dot(m_ref[...], w_ref[...], preferred_element_type=F32)


def _out_proj(x, merged, w):
    m, d = x.shape
    tm = _row_tile(m, 1024)
    tn = min(d, 1024)
    return pl.pallas_call(
        _out_kernel,
        grid=(m // tm, d // tn),
        in_specs=[pl.BlockSpec((tm, tn), lambda i, j: (i, j)),
                  pl.BlockSpec((tm, d), lambda i, j: (i, 0)),
                  pl.BlockSpec((d, tn), lambda i, j: (0, j))],
        out_specs=pl.BlockSpec((tm, tn), lambda i, j: (i, j)),
        out_shape=jax.ShapeDtypeStruct((m, d), F32),
        compiler_params=_params("parallel", "arbitrary"),
        name="out_proj",
    )(x, merged, w)


def _sample_front_kernel(x_ref, gw_ref, w_ref, nwq_ref, nwk_ref, cos_ref, sin_ref, o_ref, nw_ref):
    j = pl.program_id(0)
    x = x_ref[...]
    xn = x * lax.rsqrt(jnp.mean(x * x, axis=-1, keepdims=True) + NORM_EPS) * gw_ref[...]
    acc = jnp.dot(xn.astype(BF16), w_ref[...], preferred_element_type=F32)
    qk, v_end = 2 * N_DIL, 3 * N_DIL

    @pl.when(j < qk)
    def _():
        nw_ref[...] = jnp.where(j < N_DIL, nwq_ref[...], nwk_ref[...])
        o_ref[...] = acc
        _norm_rope_inplace(o_ref, nw_ref, cos_ref, sin_ref)

    @pl.when(((j >= qk) & (j < v_end)) | (j == v_end + 1))
    def _():
        o_ref[...] = acc

    @pl.when((j == v_end) | (j == v_end + 2))
    def _():
        o_ref[...] = acc * _sigmoid(acc)

    @pl.when(j > v_end + 2)
    def _():
        o_ref[...] = _sigmoid(acc)


def _sample_front(x, norm_w, w, q_norm_w, k_norm_w, cos, sin):
    m, d = x.shape
    aw = d // 2
    n_cols = w.shape[1]
    row = pl.BlockSpec((m, HEAD_DIM), lambda j: (0, 0))
    one = pl.BlockSpec((1, HEAD_DIM), lambda j: (0, 0))
    return pl.pallas_call(
        _sample_front_kernel,
        grid=(n_cols // aw,),
        in_specs=[pl.BlockSpec((m, d), lambda j: (0, 0)), pl.BlockSpec((1, d), lambda j: (0, 0)),
                  pl.BlockSpec((d, aw), lambda j: (0, j)), one, one, row, row],
        out_specs=pl.BlockSpec((m, aw), lambda j: (0, j)),
        out_shape=jax.ShapeDtypeStruct((m, n_cols), F32),
        scratch_shapes=[pltpu.VMEM((1, HEAD_DIM), F32)],
        compiler_params=_params("arbitrary"),
        name="sample_front",
    )(x, norm_w.reshape(1, d), w, q_norm_w.reshape(1, HEAD_DIM), k_norm_w.reshape(1, HEAD_DIM), cos, sin)


def _layer(x, cos, sin, lw, nb, seq, layer, depth, kv_prev, sample, shift=None):
    (norm_w, w_in, q_norm_w, k_norm_w, w_glu, b_glu, w_br_attn, w_br_ssm, w_out, ssm_d, s5prm) = lw
    m, d = x.shape
    aw = d // 2
    qkv = N_DIL * aw
    hg = aw // HEAD_DIM

    new_kv = []
    if sample is None:
        xn = _rmsnorm(x, norm_w)
        g_attn = _proj(xn, w_in, 3 * qkv, aw, "silu", BF16)
        u = _proj(xn, w_in, 3 * qkv + aw, aw, "plain", F32)
        g_ssm = _proj(xn, w_in, 3 * qkv + 2 * aw, aw, "silu", BF16)
        gates = _proj(xn, w_in, 3 * qkv + 3 * aw, 2 * d, "sigmoid", BF16)
        qds = _q_all(xn, w_in, nb, seq, q_norm_w, cos, sin)
        outs, lses = [], []
        for g in range(N_DIL):
            kd, vd, kv = _kv(xn, w_in, g, nb, seq, k_norm_w, cos, sin, layer, depth,
                             None if kv_prev is None else kv_prev[g])
            o, lse = _attn_prompt(qds[g], kd, vd)
            outs.append(o)
            lses.append(lse)
            new_kv.append(kv)
        y, hre, him, shifted = _s5_prompt(u, s5prm, ssm_d, nb, seq, shift)
    else:
        caches, h0re, h0im = sample
        front = _sample_front(x, norm_w, w_in, q_norm_w, k_norm_w, cos, sin)
        q, k, v = front[:, :qkv], front[:, qkv:2 * qkv], front[:, 2 * qkv:3 * qkv]
        g_attn, u = front[:, 3 * qkv:3 * qkv + aw], front[:, 3 * qkv + aw:3 * qkv + 2 * aw]
        g_ssm, gates = front[:, 3 * qkv + 2 * aw:3 * qkv + 3 * aw], front[:, 3 * qkv + 3 * aw:]
        tok = (nb, seq, N_DIL * hg, HEAD_DIM)
        o, lse = _attn_sample(q.reshape(tok), k.reshape(tok), v.reshape(tok), caches, layer)
        o = o.reshape(m, qkv)
        lse = jnp.pad(lse[..., 0].reshape(m, N_DIL, hg), ((0, 0), (0, 0), (0, HEAD_DIM - hg)))
        outs = [o[:, g * aw:(g + 1) * aw].reshape(1, 1, m, aw) for g in range(N_DIL)]
        lses = [lse[:, g].reshape(1, 1, m, HEAD_DIM) for g in range(N_DIL)]
        ut = u.reshape(nb, seq, aw).transpose(1, 0, 2)
        yt, hre, him = _s5_sample(ut, h0re, h0im, s5prm, ssm_d)
        y = yt.transpose(1, 0, 2).reshape(m, aw)
        k5, v5 = k.reshape(nb, seq, N_DIL, hg, HEAD_DIM), v.reshape(nb, seq, N_DIL, hg, HEAD_DIM)
        new_kv = [jnp.stack([k5[:, :, g], v5[:, :, g]], axis=2) for g in range(N_DIL)]
        shifted = None

    y_ssm = _glu(y, g_ssm, w_glu, b_glu)
    merged = _merge(outs, lses, g_attn, y_ssm, gates, w_br_attn, w_br_ssm)
    x_out = _out_proj(x, merged, w_out)

    n_grp = aw // SSM_GROUP_CH
    state = jnp.stack([hre, him], axis=-1).reshape(nb, n_grp, SSM_STATE, 2)
    return x_out, new_kv, state, shifted


def kernel(x_prompt, x_sample, cache_kv_d1, cache_kv_d4, cache_kv_d16, state_ssm, norm_w, w_in, q_norm_w, k_norm_w, ssm_lambda_re, ssm_lambda_im, ssm_log_dt, ssm_b_re, ssm_b_im, ssm_c_re, ssm_c_im, ssm_d, w_glu, b_glu, w_br_attn, w_br_ssm, w_out):
    nb_p, seq_p, d = x_prompt.shape
    nb_s, seq_s, _ = x_sample.shape
    depth = norm_w.shape[0]
    caches = (cache_kv_d1, cache_kv_d4, cache_kv_d16)

    cos_p, sin_p = _rope_tables(jnp.arange(seq_p, dtype=F32))
    cos_s, sin_s = _rope_tables(PAST_LEN + jnp.arange(seq_s, dtype=F32))
    cos_s, sin_s = jnp.tile(cos_s, (nb_s, 1)), jnp.tile(sin_s, (nb_s, 1))

    hp = x_prompt.reshape(nb_p * seq_p, d)
    hs = x_sample.reshape(nb_s * seq_s, d)
    lws = []
    for l in range(depth):
        s5prm = _s5_params(ssm_lambda_re[l], ssm_lambda_im[l], ssm_log_dt[l], ssm_b_re[l], ssm_b_im[l],
                           ssm_c_re[l], ssm_c_im[l], _s5_chunk(seq_p) // ROW_TILE)
        lws.append((norm_w[l], _layer_bf16(w_in, l), q_norm_w[l], k_norm_w[l], _layer_bf16(w_glu, l), b_glu[l],
                    _layer_bf16(w_br_attn, l), _layer_bf16(w_br_ssm, l), _layer_bf16(w_out, l), ssm_d[l], s5prm))

    kv_s_new = [[] for _ in range(N_DIL)]
    ssm_s = []
    n_state = state_ssm.shape[2] * state_ssm.shape[3]
    for l in range(depth):
        h0re = state_ssm[l, ..., 0].reshape(nb_s, n_state)
        h0im = state_ssm[l, ..., 1].reshape(nb_s, n_state)
        hs, new_kv_s, st_s, _ = _layer(hs, cos_s, sin_s, lws[l], nb_s, seq_s, l, depth, None,
                                       (caches, h0re, h0im))
        for g in range(N_DIL):
            kv_s_new[g].append(new_kv_s[g])
        ssm_s.append(st_s)

    kv_p, kv_s, ssm_p = None, None, []
    for l in range(depth):
        shift = (caches, [jnp.stack(n) for n in kv_s_new]) if l == 0 else None
        hp, kv_p, st, shifted = _layer(hp, cos_p, sin_p, lws[l], nb_p, seq_p, l, depth, kv_p, None, shift)
        kv_s = shifted if shifted is not None else kv_s
        ssm_p.append(st)

    return (hp.reshape(x_prompt.shape), hs.reshape(x_sample.shape), kv_p[0], kv_p[1], kv_p[2], jnp.stack(ssm_p),
            kv_s[0], kv_s[1], kv_s[2], jnp.stack(ssm_s))
```

```python
import functools
import math

import jax
import jax.numpy as jnp
from jax import lax
from jax.experimental import pallas as pl
from jax.experimental.pallas import tpu as pltpu

F32 = jnp.float32
BF16 = jnp.bfloat16

HEAD_DIM = 128
WINDOWS = (128, 512, 2048)
DILATIONS = (1, 4, 16)
N_DIL = 3
SPAN = 128
SSM_GROUP_CH = 16
SSM_STATE = 64
GROUPS_PER_CHUNK = 8
ROPE_THETA = 10000.0
NORM_EPS = 1e-6
PAST_LEN = 16384
NEG_BIG = -1e30
VMEM_LIMIT = 56 * 1024 * 1024
ROW_TILE = 8
MXU_COLS = 256
QK_ROW_CHUNK = 64
KV_COPY_BYTES = 4 * 1024 * 1024


def _params(*sem):
    return pltpu.CompilerParams(dimension_semantics=sem, vmem_limit_bytes=VMEM_LIMIT)


def _sigmoid(x):
    return 1.0 / (1.0 + jnp.exp(-x))


def _row_tile(m, cap):
    t = min(m, cap)
    assert m % t == 0, (m, t)
    return t


def _rmsnorm_kernel(x_ref, w_ref, o_ref):
    x = x_ref[...]
    y = x * lax.rsqrt(jnp.mean(x * x, axis=-1, keepdims=True) + NORM_EPS)
    o_ref[...] = (y * w_ref[...]).astype(o_ref.dtype)


def _rmsnorm(x, w):
    m, d = x.shape
    tm = _row_tile(m, 512)
    return pl.pallas_call(
        _rmsnorm_kernel,
        grid=(m // tm,),
        in_specs=[pl.BlockSpec((tm, d), lambda i: (i, 0)),
                  pl.BlockSpec((1, d), lambda i: (0, 0))],
        out_specs=pl.BlockSpec((tm, d), lambda i: (i, 0)),
        out_shape=jax.ShapeDtypeStruct((m, d), BF16),
        compiler_params=_params("parallel"),
        name="rmsnorm",
    )(x, w.reshape(1, d))


def _cast_kernel(w_ref, o_ref):
    o_ref[...] = w_ref[...].astype(o_ref.dtype)


def _layer_bf16(w, layer):
    _, rows, cols = w.shape
    tr, tc = min(rows, 512), min(cols, 2048)
    assert rows % tr == 0 and cols % tc == 0
    return pl.pallas_call(
        _cast_kernel,
        grid=(rows // tr, cols // tc),
        in_specs=[pl.BlockSpec((None, tr, tc), lambda i, j: (layer, i, j))],
        out_specs=pl.BlockSpec((tr, tc), lambda i, j: (i, j)),
        out_shape=jax.ShapeDtypeStruct((rows, cols), BF16),
        compiler_params=_params("parallel", "parallel"),
        name="cast_bf16",
    )(w)


def _norm_rope_inplace(t_ref, nw_ref, cos_ref, sin_ref, head_major=False):
    rows = cos_ref.shape[0]
    n_heads = t_ref.shape[0] // rows if head_major else t_ref.shape[1] // HEAD_DIM
    rc = min(rows, QK_ROW_CHUNK)
    half = HEAD_DIM // 2
    nw = nw_ref[...]
    nw_swapped = pltpu.roll(nw, half, axis=1)

    def chunk(c, carry):
        r0 = pl.multiple_of(c * rc, rc)
        cs = cos_ref[pl.ds(r0, rc), :] * nw
        ss = sin_ref[pl.ds(r0, rc), :] * nw_swapped
        for h in range(n_heads):
            idx = (pl.ds(h * rows + r0, rc), slice(None)) if head_major else (
                pl.ds(r0, rc), slice(h * HEAD_DIM, (h + 1) * HEAD_DIM))
            a = t_ref[idx]
            r = lax.rsqrt(jnp.mean(a * a, axis=-1, keepdims=True) + NORM_EPS)
            t_ref[idx] = (a * cs + pltpu.roll(a, half, axis=1) * ss) * r
        return carry

    lax.fori_loop(0, rows // rc, chunk, 0)


def _proj_kernel(x_ref, w_ref, o_ref, *, epilogue):
    acc = jnp.dot(x_ref[...], w_ref[...], preferred_element_type=F32)
    if epilogue == "silu":
        acc = acc * _sigmoid(acc)
    elif epilogue == "sigmoid":
        acc = _sigmoid(acc)
    o_ref[...] = acc.astype(o_ref.dtype)


def _proj(xn, w, col0, ncols, epilogue, out_dtype):
    m, d = xn.shape
    tm = _row_tile(m, 1024)
    tn = min(ncols, 1024)
    assert ncols % tn == 0 and col0 % tn == 0
    jb = col0 // tn
    return pl.pallas_call(
        functools.partial(_proj_kernel, epilogue=epilogue),
        grid=(m // tm, ncols // tn),
        in_specs=[pl.BlockSpec((tm, d), lambda i, j: (i, 0)),
                  pl.BlockSpec((d, tn), lambda i, j: (0, jb + j))],
        out_specs=pl.BlockSpec((tm, tn), lambda i, j: (i, j)),
        out_shape=jax.ShapeDtypeStruct((m, ncols), out_dtype),
        compiler_params=_params("parallel", "arbitrary"),
        name="proj_" + epilogue,
    )(xn, w)


def _norm_rope_head(a, nw_ref, cos_ref, sin_ref, store):
    rows = a.shape[0]
    rc = min(rows, QK_ROW_CHUNK)
    half = HEAD_DIM // 2
    nw = nw_ref[...]
    nw_swapped = pltpu.roll(nw, half, axis=1)
    ssq = jnp.dot((a * a).astype(BF16), jnp.ones((HEAD_DIM, HEAD_DIM), BF16), preferred_element_type=F32)
    for r0 in range(0, rows, rc):
        ac = a[r0:r0 + rc]
        cs = cos_ref[r0:r0 + rc, :] * nw
        ss = sin_ref[r0:r0 + rc, :] * nw_swapped
        r = lax.rsqrt(ssq[r0:r0 + rc] * (1.0 / HEAD_DIM) + NORM_EPS)
        store(r0, rc, (ac * cs + pltpu.roll(ac, half, axis=1) * ss) * r)


def _project_dilated(x, w_ref, nw_ref, cos_ref, sin_ref, hm_ref, d_ref, dil):
    tm = x.shape[0]
    n_heads = w_ref.shape[1] // HEAD_DIM
    per = tm // dil
    group = n_heads
    for h0 in range(0, n_heads, group):
        acc = jnp.dot(x, w_ref[:, h0 * HEAD_DIM:(h0 + group) * HEAD_DIM], preferred_element_type=F32)
        for h in range(h0, h0 + group):
            a = acc[:, (h - h0) * HEAD_DIM:(h - h0 + 1) * HEAD_DIM]
            if nw_ref is None:
                hm_ref[h * tm:(h + 1) * tm, :] = a
            else:
                def store(r0, rc, chunk, h=h):
                    hm_ref[h * tm + r0:h * tm + r0 + rc, :] = chunk

                _norm_rope_head(a, nw_ref, cos_ref, sin_ref, store)
            for r in range(dil):
                rows = pl.ds(h * tm + r, per, stride=dil) if dil > 1 else pl.ds(h * tm, per)
                d_ref[0, r, :, h * HEAD_DIM:(h + 1) * HEAD_DIM] = hm_ref[rows, :].astype(d_ref.dtype)


def _q_kernel(x_ref, w_ref, nw_ref, cos_ref, sin_ref, *rest):
    qd_refs, hm_ref = rest[:N_DIL], rest[N_DIL]
    for g in range(N_DIL):
        @pl.when(pl.program_id(0) == g)
        def _(g=g):
            _project_dilated(x_ref[...], w_ref, nw_ref, cos_ref, sin_ref, hm_ref, qd_refs[g], DILATIONS[g])


def _q_all(xn, w, nb, seq, q_norm_w, cos, sin):
    m, d = xn.shape
    aw = d // 2
    tm = _row_tile(seq, 1024)
    bps = seq // tm
    out_specs, out_shape = [], []
    for g in range(N_DIL):
        dil = DILATIONS[g]
        assert (tm // dil) % 16 == 0

        def q_map(gi, b, i, g=g):
            return (jnp.where(gi < g, 0, jnp.where(gi == g, b, nb - 1)), 0,
                    jnp.where(gi < g, 0, jnp.where(gi == g, i, bps - 1)), 0)

        out_specs.append(pl.BlockSpec((1, dil, tm // dil, aw), q_map))
        out_shape.append(jax.ShapeDtypeStruct((nb, dil, seq // dil, aw), BF16))
    return pl.pallas_call(
        _q_kernel,
        grid=(N_DIL, nb, bps),
        in_specs=[pl.BlockSpec((tm, d), lambda g, b, i: (b * bps + i, 0)),
                  pl.BlockSpec((d, aw), lambda g, b, i: (0, g)),
                  pl.BlockSpec((1, HEAD_DIM), lambda g, b, i: (0, 0)),
                  pl.BlockSpec((tm, HEAD_DIM), lambda g, b, i: (i, 0)),
                  pl.BlockSpec((tm, HEAD_DIM), lambda g, b, i: (i, 0))],
        out_specs=out_specs,
        out_shape=out_shape,
        scratch_shapes=[pltpu.VMEM((aw // HEAD_DIM * tm, HEAD_DIM), F32)],
        compiler_params=_params("arbitrary", "arbitrary", "arbitrary"),
        name="q_all",
    )(xn, w, q_norm_w.reshape(1, HEAD_DIM), cos, sin)


def _kv_kernel(x_ref, wk_ref, wv_ref, nw_ref, cos_ref, sin_ref, *rest, dil, first_kept):
    kd_ref, vd_ref, kv_ref, hk_ref, hv_ref = rest[-5:]
    half_idx = pl.program_id(2)
    tm = x_ref.shape[0]
    n_heads = wk_ref.shape[1] // HEAD_DIM

    @pl.when(half_idx == 0)
    def _():
        x = x_ref[...]
        _project_dilated(x, wk_ref, nw_ref, cos_ref, sin_ref, hk_ref, kd_ref, dil)
        _project_dilated(x, wv_ref, None, None, None, hv_ref, vd_ref, dil)

    kb = kv_ref.shape[1]
    half = tm // 2

    @pl.when(2 * pl.program_id(1) + half_idx >= first_kept)
    def _():
        base = (half_idx + 1) * half - kb

        def tokens(i, carry):
            t0 = pl.multiple_of(i * ROW_TILE, ROW_TILE)
            for which, hm_ref in ((0, hk_ref), (1, hv_ref)):
                tiles = [hm_ref[pl.ds(h * tm + base + t0, ROW_TILE), :] for h in range(n_heads)]
                kv_ref[0, pl.ds(t0, ROW_TILE), which] = jnp.swapaxes(jnp.stack(tiles, axis=0), 0, 1)
            return carry

        lax.fori_loop(0, kb // ROW_TILE, tokens, 0, unroll=2)


def _kv(xn, w, g, nb, seq, k_norm_w, cos, sin, layer, depth, kv_prev):
    m, d = xn.shape
    aw = d // 2
    hg = aw // HEAD_DIM
    dil = DILATIONS[g]
    keep = min(WINDOWS[g], seq)
    tm = _row_tile(seq, 1024)
    per = tm // dil
    half = tm // 2
    kb = min(keep, half)
    bps = seq // tm
    n_half = 2 * bps
    first_kept = (seq - keep) // half if keep >= half else n_half - 1
    assert per % 16 == 0 and keep % kb == 0 and (seq - keep) % kb == 0
    once = pl.Buffered(1)
    nw = lambda a: a.reshape(1, HEAD_DIM)
    dshape = jax.ShapeDtypeStruct((nb, dil, seq // dil, aw), BF16)
    staging = pltpu.VMEM((hg * tm, HEAD_DIM), F32)

    wspec = lambda col: pl.BlockSpec((d, aw), lambda b, i, h: (0, col), pipeline_mode=once)
    tspec = pl.BlockSpec((tm, HEAD_DIM), lambda b, i, h: (i, 0))
    dspec = pl.BlockSpec((1, dil, per, aw), lambda b, i, h: (b, 0, i, 0))
    kv_shape = jax.ShapeDtypeStruct((depth, nb, keep, 2, hg, HEAD_DIM), F32)
    if kv_prev is None:
        kv_prev = jnp.zeros(kv_shape.shape, kv_shape.dtype)
    kd, vd, kv = pl.pallas_call(
        functools.partial(_kv_kernel, dil=dil, first_kept=first_kept),
        grid=(nb, bps, 2),
        in_specs=[pl.BlockSpec((tm, d), lambda b, i, h: (b * bps + i, 0)),
                  wspec(N_DIL + g), wspec(2 * N_DIL + g),
                  pl.BlockSpec((1, HEAD_DIM), lambda b, i, h: (0, 0)), tspec, tspec,
                  pl.BlockSpec(memory_space=pl.ANY)],
        out_specs=[dspec, dspec,
                   pl.BlockSpec((None, 1, kb, 2, hg, HEAD_DIM),
                                lambda b, i, h: (layer, b, jnp.maximum(2 * i + h - first_kept, 0), 0, 0, 0))],
        out_shape=[dshape, dshape, kv_shape],
        scratch_shapes=[staging, staging],
        input_output_aliases={6: 2},
        compiler_params=_params("parallel", "arbitrary", "arbitrary"),
        name="kv_d%d" % dil,
    )(xn, w, w, nw(k_norm_w), cos, sin, kv_prev)
    return kd, vd, kv


def _rope_tables(pos):
    half = HEAD_DIM // 2
    inv_freq = jnp.power(ROPE_THETA, -jnp.arange(half, dtype=F32) * (2.0 / HEAD_DIM))
    ang = pos[:, None] * inv_freq[None, :]
    cos, sin = jnp.cos(ang), jnp.sin(ang)
    return jnp.concatenate([cos, cos], axis=-1), jnp.concatenate([-sin, sin], axis=-1)


def _attn_prompt_kernel(q_ref, kp_ref, kc_ref, vp_ref, vc_ref, o_ref, lse_ref):
    not_first = pl.program_id(2) > 0
    a = lax.broadcasted_iota(jnp.int32, (SPAN, 2 * SPAN), 0)
    c = lax.broadcasted_iota(jnp.int32, (SPAN, 2 * SPAN), 1)
    band = (c >= a) & (c <= a + SPAN)
    band_first = band & ((c >= SPAN) | not_first)
    lane = lax.broadcasted_iota(jnp.int32, (SPAN, HEAD_DIM), 1)
    scale = HEAD_DIM ** -0.5
    for j in range(q_ref.shape[2] // SPAN):
        rows = slice(j * SPAN, (j + 1) * SPAN)
        lse_tile = jnp.zeros((SPAN, HEAD_DIM), F32)
        for h in range(q_ref.shape[3] // HEAD_DIM):
            sl = slice(h * HEAD_DIM, (h + 1) * HEAD_DIM)
            q = q_ref[0, 0, rows, sl]
            if j == 0:
                k = jnp.concatenate([kp_ref[0, 0, :, sl], kc_ref[0, 0, rows, sl]], axis=0)
                v = jnp.concatenate([vp_ref[0, 0, :, sl], vc_ref[0, 0, rows, sl]], axis=0)
            else:
                k = kc_ref[0, 0, (j - 1) * SPAN:(j + 1) * SPAN, sl]
                v = vc_ref[0, 0, (j - 1) * SPAN:(j + 1) * SPAN, sl]
            s = lax.dot_general(q, k, (((1,), (1,)), ((), ())), preferred_element_type=F32) * scale
            s = jnp.where(band_first if j == 0 else band, s, NEG_BIG)
            m = jnp.max(s, axis=-1, keepdims=True)
            p = jnp.exp(s - m)
            l = jnp.sum(p, axis=-1, keepdims=True)
            o = jnp.dot(p.astype(BF16), v, preferred_element_type=F32) / l
            o_ref[0, 0, rows, sl] = o.astype(o_ref.dtype)
            lse_tile = jnp.where(lane == h, m + jnp.log(l), lse_tile)
        lse_ref[0, 0, rows, :] = lse_tile


def _attn_prompt(q, k, v):
    nb, dil, length, aw = q.shape
    n_blk = length // SPAN
    qb = min(4, n_blk)
    assert length % SPAN == 0 and n_blk % qb == 0 and aw // HEAD_DIM <= HEAD_DIM
    cur = pl.BlockSpec((1, 1, qb * SPAN, aw), lambda b, r, n: (b, r, n, 0))
    prev = pl.BlockSpec((1, 1, SPAN, aw), lambda b, r, n: (b, r, jnp.maximum(qb * n - 1, 0), 0))
    return pl.pallas_call(
        _attn_prompt_kernel,
        grid=(nb, dil, n_blk // qb),
        in_specs=[cur, prev, cur, prev, cur],
        out_specs=[cur, pl.BlockSpec((1, 1, qb * SPAN, HEAD_DIM), lambda b, r, n: (b, r, n, 0))],
        out_shape=[jax.ShapeDtypeStruct(q.shape, BF16),
                   jax.ShapeDtypeStruct((nb, dil, length, HEAD_DIM), F32)],
        compiler_params=_params("parallel", "parallel", "arbitrary"),
        name="attn_prompt_d%d" % dil,
    )(q, k, k, v, v)


def _attn_sample_kernel(q_ref, kn_ref, vn_ref, c1_ref, c4_ref, c16_ref, o_ref, lse_ref):
    n_new, n_heads, _ = q_ref.shape
    hg = n_heads // N_DIL
    scale = HEAD_DIM ** -0.5
    caches = (c1_ref, c4_ref, c16_ref)
    for g in range(N_DIL):
        hs = slice(g * hg, (g + 1) * hg)
        for t in range(n_new):
            q = q_ref[t, hs, :]
            if DILATIONS[g] == 1:
                kc, vc = caches[g][:, 0], caches[g][:, 1]
                rows = lax.broadcasted_iota(jnp.int32, (kc.shape[0], hg, 1), 0)
                s = jnp.sum(kc * q[None], axis=-1, keepdims=True) * scale
                s = jnp.where(rows >= t, s, NEG_BIG)
                new = range(t + 1)
            else:
                kc, vc = caches[g][:, t, 0], caches[g][:, t, 1]
                s = jnp.sum(kc * q[None], axis=-1, keepdims=True) * scale
                new = (t,)
            s_new = [jnp.sum(kn_ref[u, hs, :] * q, axis=-1, keepdims=True) * scale for u in new]
            m = jnp.max(s, axis=0)
            for sn in s_new:
                m = jnp.maximum(m, sn)
            p = jnp.exp(s - m[None])
            l = jnp.sum(p, axis=0)
            acc = jnp.sum(p * vc, axis=0)
            for u, sn in zip(new, s_new):
                pn = jnp.exp(sn - m)
                l = l + pn
                acc = acc + pn * vn_ref[u, hs, :]
            o_ref[t, hs, :] = acc / l
            lse_ref[t, hs, :] = jnp.broadcast_to(m + jnp.log(l), (hg, HEAD_DIM))


def _attn_sample(q, kn, vn, caches, layer):
    nb, n_new, n_heads, _ = q.shape
    hg = n_heads // N_DIL
    views, specs = [], []
    for g in range(N_DIL):
        dil = DILATIONS[g]
        depth, _, buf = caches[g].shape[:3]
        assert buf == WINDOWS[g] and (dil == 1 or dil % n_new == 0)
        if dil == 1:
            views.append(caches[g])
            specs.append(pl.BlockSpec((None, None, buf, 2, hg, HEAD_DIM),
                                      lambda b: (layer, b, 0, 0, 0, 0)))
        else:
            views.append(caches[g].reshape(depth, nb, SPAN, dil, 2, hg, HEAD_DIM))
            specs.append(pl.BlockSpec((None, None, SPAN, n_new, 2, hg, HEAD_DIM),
                                      lambda b: (layer, b, 0, 0, 0, 0, 0)))
    tok = pl.BlockSpec((None, n_new, n_heads, HEAD_DIM), lambda b: (b, 0, 0, 0))
    return pl.pallas_call(
        _attn_sample_kernel,
        grid=(nb,),
        in_specs=[tok, tok, tok] + specs,
        out_specs=[tok, tok],
        out_shape=[jax.ShapeDtypeStruct(q.shape, F32)] * 2,
        compiler_params=_params("parallel"),
        name="attn_sample",
    )(q, kn, vn, *views)


def _s5_params_kernel(lre_ref, lim_ref, ldt_ref, bre_ref, bim_ref, ptab_ref, bbre_ref, bbim_ref):
    dt = jnp.exp(ldt_ref[...])
    lre, lim = lre_ref[...], lim_ref[...]
    xr, xi = lre * dt, lim * dt
    mag = jnp.exp(xr)
    ar, ai = mag * jnp.cos(xi), mag * jnp.sin(xi)
    shape = (ROW_TILE, xr.shape[1])
    abr, abi = jnp.broadcast_to(ar, shape), jnp.broadcast_to(ai, shape)
    ptab_ref[0, 0] = abr
    ptab_ref[1, 0] = abi

    def next_power(i, carry):
        pr, pi = carry
        pr, pi = pr * abr - pi * abi, pr * abi + pi * abr
        ptab_ref[0, i] = pr
        ptab_ref[1, i] = pi
        return pr, pi

    lax.fori_loop(1, ptab_ref.shape[1], next_power, (abr, abi))
    nr, ni = ar - 1.0, ai
    den = lre * lre + lim * lim
    fr = (nr * lre + ni * lim) / den
    fi = (ni * lre - nr * lim) / den
    bre, bim = bre_ref[...], bim_ref[...]
    bbre_ref[...] = fr * bre - fi * bim
    bbim_ref[...] = fr * bim + fi * bre


def _s5_params(lam_re, lam_im, log_dt, b_re, b_im, c_re, c_im, seg_len):
    n_grp, n_st = lam_re.shape
    n_ch = b_re.shape[2]
    n = n_grp * n_st
    gc = GROUPS_PER_CHUNK
    n_chunk = n_grp // gc
    sw = gc * n_st
    to_lanes = lambda b: b.transpose(2, 0, 1).reshape(n_ch, n)
    lane = lambda rows: pl.BlockSpec((rows, sw), lambda j: (0, j))
    ptab, bb_re, bb_im = pl.pallas_call(
        _s5_params_kernel,
        grid=(n_chunk,),
        in_specs=[lane(1)] * 3 + [lane(n_ch)] * 2,
        out_specs=[pl.BlockSpec((2, seg_len, ROW_TILE, sw), lambda j: (0, 0, 0, j)), lane(n_ch), lane(n_ch)],
        out_shape=[jax.ShapeDtypeStruct((2, seg_len, ROW_TILE, n), F32),
                   jax.ShapeDtypeStruct((n_ch, n), F32), jax.ShapeDtypeStruct((n_ch, n), F32)],
        compiler_params=_params("parallel"),
        name="s5_params",
    )(lam_re.reshape(1, n), lam_im.reshape(1, n),
      jnp.broadcast_to(log_dt[:, None], (n_grp, n_st)).reshape(1, n), to_lanes(b_re), to_lanes(b_im))

    eye = jnp.eye(gc, dtype=F32)

    def b_blocks(bb):
        bb = bb.reshape(n_ch, n_chunk, gc, n_st)
        blk = jnp.einsum("cjgp,gh->jgchp", bb, eye)
        return blk.reshape(n_chunk, gc * n_ch, gc * n_st).astype(BF16)

    def c_blocks(cc):
        cc = cc.reshape(n_chunk, gc, n_ch, n_st)
        blk = jnp.einsum("jgcp,gh->jgphc", cc, eye)
        return blk.reshape(n_chunk, gc * n_st, gc * n_ch).astype(BF16)

    b_cat = jnp.concatenate([b_blocks(bb_re), b_blocks(bb_im)], axis=2)
    c_cat = jnp.concatenate([c_blocks(c_re), c_blocks(c_im)], axis=1)
    return ptab, b_cat, c_cat


def _cmul_add(xr, xi, ar, ai, br, bi):
    return xr + ar * br - ai * bi, xi + ar * bi + ai * br


def _shift_plan(cache_shape, new_shape, n_steps):
    lb_n, buf = cache_shape[:2]
    keep = buf - new_shape[1]
    row_bytes = 4 * math.prod(cache_shape[2:])
    for per_row in range(1, keep + 1):
        n_jobs = lb_n * per_row
        if keep % per_row == 0 and keep // per_row * row_bytes <= KV_COPY_BYTES and n_steps % n_jobs == 0:
            return keep // per_row, per_row, n_steps // n_jobs
    raise ValueError("window shift does not fit the grid")


def _shift_lag(period):
    return 1 if period == 2 else 2


def _shift_slots(period):
    return 4 if period == 1 else 2


def _window_shift_step(step, n_steps, caches, news, outs, rings, nbufs, in_sem, out_sem, new_sem):
    last = step == n_steps - 1
    for g in range(N_DIL):
        cache, out, ring = caches[g], outs[g], rings[g]
        n_new = news[g].shape[1]
        keep = cache.shape[1] - n_new
        rows, per_row, period = _shift_plan(cache.shape, news[g].shape, n_steps)
        n_jobs = cache.shape[0] * per_row
        slots = ring.shape[0]
        lag = _shift_lag(period)
        assert n_jobs >= slots and (period == 1 or lag < period)

        def load(k, cache=cache, ring=ring, g=g, rows=rows, per_row=per_row, n_new=n_new, slots=slots):
            src = cache.at[lax.div(k, per_row), pl.ds(n_new + lax.rem(k, per_row) * rows, rows)]
            return pltpu.make_async_copy(src, ring.at[lax.rem(k, slots)], in_sem.at[g, lax.rem(k, slots)])

        def store(k, out=out, ring=ring, g=g, rows=rows, per_row=per_row, slots=slots):
            dst = out.at[lax.div(k, per_row), pl.ds(lax.rem(k, per_row) * rows, rows)]
            return pltpu.make_async_copy(ring.at[lax.rem(k, slots)], dst, out_sem.at[g, lax.rem(k, slots)])

        def issue(k, load=load, store=store, slots=slots):
            @pl.when(k >= slots)
            def _():
                store(k - slots).wait()
            load(k).start()

        def forward(k, load=load, store=store):
            load(k).wait()
            store(k).start()

        def drain(k, store=store, slots=slots):
            for back in range(slots - 1, -1, -1):
                store(k - back).wait()

        if period == 1:
            issue(step)

            @pl.when(step >= lag)
            def _(forward=forward, lag=lag):
                forward(step - lag)

            @pl.when(last)
            def _(forward=forward, drain=drain, lag=lag):
                for back in range(lag - 1, -1, -1):
                    forward(step - back)
                drain(step)
        else:
            k = lax.div(step, period)
            phase = lax.rem(step, period)

            @pl.when(phase == 0)
            def _(issue=issue, k=k):
                issue(k)

            @pl.when(phase == lag)
            def _(forward=forward, k=k):
                forward(k)

            @pl.when(last)
            def _(drain=drain, k=k):
                drain(k)

        new_in = pltpu.make_async_copy(news[g], nbufs[g], new_sem.at[g])
        new_out = pltpu.make_async_copy(nbufs[g], out.at[:, pl.ds(keep, n_new)], new_sem.at[N_DIL + g])

        @pl.when(step == 0)
        def _(new_in=new_in, new_out=new_out):
            new_in.start()
            new_in.wait()
            new_out.start()

        @pl.when(last)
        def _(new_out=new_out):
            new_out.wait()


def _s5_scan_kernel(u_ref, b_ref, ptab_ref, c_ref, d_ref, *rest, with_shift):
    if with_shift:
        n = N_DIL
        caches, news = rest[:n], rest[n:2 * n]
        y_ref, hfin_ref = rest[2 * n:2 * n + 2]
        outs = rest[2 * n + 2:3 * n + 2]
        up_ref, s_ref, hb_ref, end_ref = rest[3 * n + 2:3 * n + 6]
        rings, nbufs = rest[3 * n + 6:4 * n + 6], rest[4 * n + 6:5 * n + 6]
        in_sem, out_sem, new_sem = rest[5 * n + 6:]
        _, n_j, n_t = with_shift
        step = (pl.program_id(0) * n_j + pl.program_id(1)) * n_t + pl.program_id(2)
        _window_shift_step(step, math.prod(with_shift), caches, news, outs, rings, nbufs,
                           in_sem, out_sem, new_sem)
    else:
        y_ref, hfin_ref, up_ref, s_ref, hb_ref, end_ref = rest
    tc, sw2 = s_ref.shape
    sw = sw2 // 2
    seg = tc // ROW_TILE
    re, im = slice(0, sw), slice(sw, sw2)

    @pl.when(pl.program_id(2) == 0)
    def _():
        end_ref[...] = jnp.zeros_like(end_ref)

    for i in range(seg):
        up_ref[i * ROW_TILE:(i + 1) * ROW_TILE, :] = u_ref[0, pl.ds(i, ROW_TILE, stride=seg), :]
    s_ref[...] = jnp.dot(up_ref[...].astype(BF16), b_ref[0], preferred_element_type=F32)

    ar, ai = ptab_ref[0, 0], ptab_ref[1, 0]

    def local_step(i, carry):
        hr, hi = carry
        r0 = pl.multiple_of(i * ROW_TILE, ROW_TILE)
        hr, hi = _cmul_add(s_ref[pl.ds(r0, ROW_TILE), re], s_ref[pl.ds(r0, ROW_TILE), im], ar, ai, hr, hi)
        s_ref[pl.ds(r0, ROW_TILE), re] = hr
        s_ref[pl.ds(r0, ROW_TILE), im] = hi
        return hr, hi

    zero = jnp.zeros((ROW_TILE, sw), F32)
    er, ei = lax.fori_loop(0, seg, local_step, (zero, zero), unroll=8)

    row = lax.broadcasted_iota(jnp.int32, (ROW_TILE, sw), 0)
    cr = jnp.where(row == 0, pltpu.roll(end_ref[0], 1, axis=0), pltpu.roll(er, 1, axis=0))
    ci = jnp.where(row == 0, pltpu.roll(end_ref[1], 1, axis=0), pltpu.roll(ei, 1, axis=0))
    mr, mi = ptab_ref[0, seg - 1], ptab_ref[1, seg - 1]
    wr, wi = mr, mi
    for sh in (1, 2, 4):
        gr, gi = jnp.where(row >= sh, wr, 0.0), jnp.where(row >= sh, wi, 0.0)
        cr, ci = _cmul_add(cr, ci, gr, gi, pltpu.roll(cr, sh, axis=0), pltpu.roll(ci, sh, axis=0))
        wr, wi = wr * wr - wi * wi, 2.0 * wr * wi
    fr, fi = _cmul_add(er, ei, mr, mi, cr, ci)
    end_ref[0] = fr
    end_ref[1] = fi
    hfin_ref[0, 0] = fr
    hfin_ref[0, 1] = fi

    def fix_up(k, carry):
        r0 = pl.multiple_of(k * 2 * ROW_TILE, 2 * ROW_TILE)
        hs, ns = [], []
        for half in range(2):
            i = 2 * k + half
            rows = pl.ds(r0 + half * ROW_TILE, ROW_TILE)
            hr, hi = _cmul_add(s_ref[rows, re], s_ref[rows, im], ptab_ref[0, i], ptab_ref[1, i], cr, ci)
            hs.append(hr)
            ns.append(-hi)
        hb_ref[pl.ds(r0, 2 * ROW_TILE), re] = jnp.concatenate(hs, axis=0).astype(BF16)
        hb_ref[pl.ds(r0, 2 * ROW_TILE), im] = jnp.concatenate(ns, axis=0).astype(BF16)
        return carry

    lax.fori_loop(0, seg // 2, fix_up, 0, unroll=4)
    yp = jnp.dot(hb_ref[...], c_ref[0], preferred_element_type=F32) + d_ref[...] * up_ref[...]
    for i in range(seg):
        y_ref[0, pl.ds(i, ROW_TILE, stride=seg), :] = yp[i * ROW_TILE:(i + 1) * ROW_TILE, :]


def _s5_chunk(seq):
    return min(seq, 1024)


def _flat_window(a):
    return a.reshape(a.shape[0] * a.shape[1], a.shape[2], a.shape[3] * a.shape[4], a.shape[5])


def _s5_prompt(u, prm, ssm_d, nb, seq, shift=None):
    ptab, b_cat, c_cat = prm
    n_chunk, cw, sw2 = b_cat.shape
    sw = sw2 // 2
    width = u.shape[1]
    tc = _s5_chunk(seq)
    seg = tc // ROW_TILE
    assert seq % tc == 0 and seg % 2 == 0 and ptab.shape[1] == seg
    grid = (nb, n_chunk, seq // tc)
    in_specs = [pl.BlockSpec((1, tc, cw), lambda b, j, t: (b, t, j)),
                pl.BlockSpec((1, cw, sw2), lambda b, j, t: (j, 0, 0)),
                pl.BlockSpec((2, seg, ROW_TILE, sw), lambda b, j, t: (0, 0, 0, j)),
                pl.BlockSpec((1, sw2, cw), lambda b, j, t: (j, 0, 0)),
                pl.BlockSpec((1, cw), lambda b, j, t: (0, j))]
    out_specs = [pl.BlockSpec((1, tc, cw), lambda b, j, t: (b, t, j)),
                 pl.BlockSpec((1, 2, ROW_TILE, sw), lambda b, j, t: (b, 0, 0, j))]
    out_shape = [jax.ShapeDtypeStruct((nb, seq, width), F32),
                 jax.ShapeDtypeStruct((nb, 2, ROW_TILE, n_chunk * sw), F32)]
    scratch = [pltpu.VMEM((tc, cw), F32), pltpu.VMEM((tc, sw2), F32),
               pltpu.VMEM((tc, sw2), BF16), pltpu.VMEM((2, ROW_TILE, sw), F32)]
    args = [u.reshape(nb, seq, width), b_cat, ptab, c_cat, ssm_d.reshape(1, width)]
    if shift is not None:
        caches, news = shift
        cf, nf = [_flat_window(c) for c in caches], [_flat_window(w) for w in news]
        hbm = pl.BlockSpec(memory_space=pl.ANY)
        in_specs += [hbm] * (2 * N_DIL)
        out_specs += [hbm] * N_DIL
        out_shape += [jax.ShapeDtypeStruct(c.shape, c.dtype) for c in cf]
        args += cf + nf
        max_slots = 0
        for c, w in zip(cf, nf):
            rows, _, period = _shift_plan(c.shape, w.shape, math.prod(grid))
            scratch.append(pltpu.VMEM((_shift_slots(period), rows) + c.shape[2:], c.dtype))
            max_slots = max(max_slots, _shift_slots(period))
        scratch += [pltpu.VMEM(w.shape, w.dtype) for w in nf]
        scratch += [pltpu.SemaphoreType.DMA((N_DIL, max_slots)), pltpu.SemaphoreType.DMA((N_DIL, max_slots)),
                    pltpu.SemaphoreType.DMA((2 * N_DIL,))]
    res = pl.pallas_call(
        functools.partial(_s5_scan_kernel, with_shift=grid if shift is not None else None),
        grid=grid,
        in_specs=in_specs,
        out_specs=out_specs,
        out_shape=out_shape,
        scratch_shapes=scratch,
        compiler_params=_params("arbitrary", "arbitrary", "arbitrary") if shift is not None
        else _params("parallel", "parallel", "arbitrary"),
        name="s5_scan",
    )(*args)
    y, hfin = res[0], res[1]
    last = ROW_TILE - 1
    shifted = None if shift is None else [o.reshape(c.shape) for o, c in zip(res[2:], shift[0])]
    return y.reshape(nb * seq, width), hfin[:, 0, last], hfin[:, 1, last], shifted


def _s5_sample_kernel(u_ref, h0re_ref, h0im_ref, b_ref, ptab_ref, c_ref, d_ref, y_ref, hre_ref, him_ref):
    sw = h0re_ref.shape[1]
    ar, ai = ptab_ref[0, 0, 0:1, :], ptab_ref[1, 0, 0:1, :]
    hr, hi = h0re_ref[...], h0im_ref[...]
    for t in range(u_ref.shape[0]):
        u = u_ref[t]
        bu = jnp.dot(u.astype(BF16), b_ref[0], preferred_element_type=F32)
        hr, hi = _cmul_add(bu[:, :sw], bu[:, sw:], ar, ai, hr, hi)
        hb = jnp.concatenate([hr, -hi], axis=1).astype(BF16)
        y_ref[t] = jnp.dot(hb, c_ref[0], preferred_element_type=F32) + d_ref[...] * u
    hre_ref[...] = hr
    him_ref[...] = hi


def _s5_sample(u, h0re, h0im, prm, ssm_d):
    ptab, b_cat, c_cat = prm
    n_chunk, cw, sw2 = b_cat.shape
    sw = sw2 // 2
    n_new, nb, width = u.shape
    return pl.pallas_call(
        _s5_sample_kernel,
        grid=(n_chunk,),
        in_specs=[pl.BlockSpec((n_new, nb, cw), lambda j: (0, 0, j)),
                  pl.BlockSpec((nb, sw), lambda j: (0, j)),
                  pl.BlockSpec((nb, sw), lambda j: (0, j)),
                  pl.BlockSpec((1, cw, sw2), lambda j: (j, 0, 0)),
                  pl.BlockSpec((2, 1, ROW_TILE, sw), lambda j: (0, 0, 0, j)),
                  pl.BlockSpec((1, sw2, cw), lambda j: (j, 0, 0)),
                  pl.BlockSpec((1, cw), lambda j: (0, j))],
        out_specs=[pl.BlockSpec((n_new, nb, cw), lambda j: (0, 0, j)),
                   pl.BlockSpec((nb, sw), lambda j: (0, j)),
                   pl.BlockSpec((nb, sw), lambda j: (0, j))],
        out_shape=[jax.ShapeDtypeStruct((n_new, nb, width), F32),
                   jax.ShapeDtypeStruct((nb, n_chunk * sw), F32),
                   jax.ShapeDtypeStruct((nb, n_chunk * sw), F32)],
        compiler_params=_params("parallel"),
        name="s5_sample",
    )(u, h0re, h0im, b_cat, ptab, c_cat, ssm_d.reshape(1, width))


def _glu_kernel(y_ref, g_ref, w_ref, b_ref, o_ref):
    y = y_ref[...]
    s = 0.5 * y * (1.0 + jnp.tanh(math.sqrt(2.0 / math.pi) * (y + 0.044715 * (y * y * y))))
    z = jnp.dot(s.astype(BF16), w_ref[...], preferred_element_type=F32) + b_ref[...]
    o_ref[...] = (s * _sigmoid(z) * g_ref[...].astype(F32)).astype(o_ref.dtype)


def _glu(y, gate, w, b):
    m, width = y.shape
    tm = _row_tile(m, 1024)
    row = pl.BlockSpec((tm, width), lambda i: (i, 0))
    return pl.pallas_call(
        _glu_kernel,
        grid=(m // tm,),
        in_specs=[row, row, pl.BlockSpec((width, width), lambda i: (0, 0)),
                  pl.BlockSpec((1, width), lambda i: (0, 0))],
        out_specs=row,
        out_shape=jax.ShapeDtypeStruct((m, width), BF16),
        compiler_params=_params("parallel"),
        name="glu",
    )(y, gate, w, b.reshape(1, width))


def _merge_kernel(*refs):
    o_refs, l_refs = refs[:N_DIL], refs[N_DIL:2 * N_DIL]
    ga_ref, ys_ref, ma_ref, ms_ref, wa_ref, ws_ref, out_ref = refs[2 * N_DIL:2 * N_DIL + 7]
    a_ref, otok_ref, ltok_ref = refs[2 * N_DIL + 7:]

    def token_order(ref, cols, stage):
        dil, per = ref.shape[1], ref.shape[2]
        if dil == 1:
            return ref[0, 0, :, cols].astype(F32)
        for r in range(dil):
            stage[pl.ds(r, per, stride=dil), :] = ref[0, r, :, cols].astype(F32)
        return stage[...]

    ls = [token_order(l_refs[g], slice(None), ltok_ref.at[g]) for g in range(N_DIL)]
    mx = jnp.maximum(jnp.maximum(ls[0], ls[1]), ls[2])
    es = [jnp.exp(l - mx) for l in ls]
    den = es[0] + es[1] + es[2]
    wts = [e / den for e in es]
    tm, aw = a_ref.shape
    half = out_ref.shape[1] // 2
    halves = [slice(c * half, (c + 1) * half) for c in range(2)]
    ys = ys_ref[...]
    acc = [ms_ref[:, cs].astype(F32) * jnp.dot(ys, ws_ref[:, cs], preferred_element_type=F32) for cs in halves]
    group = min(aw, MXU_COLS) // HEAD_DIM
    ya = [None, None]
    for h0 in range(0, aw // HEAD_DIM, group):
        for h in range(h0, h0 + group):
            sl = slice(h * HEAD_DIM, (h + 1) * HEAD_DIM)
            attn = None
            for g in range(N_DIL):
                term = (jnp.broadcast_to(wts[g][:, h:h + 1], (tm, HEAD_DIM))
                        * token_order(o_refs[g], sl, otok_ref.at[g]))
                attn = term if attn is None else attn + term
            a_ref[:, sl] = (attn * ga_ref[:, sl].astype(F32)).astype(BF16)
        ks = slice(h0 * HEAD_DIM, (h0 + group) * HEAD_DIM)
        for c, cs in enumerate(halves):
            part = jnp.dot(a_ref[:, ks], wa_ref[ks, cs], preferred_element_type=F32)
            ya[c] = part if ya[c] is None else ya[c] + part
    for c, cs in enumerate(halves):
        out_ref[:, cs] = (ma_ref[:, cs].astype(F32) * ya[c] + acc[c]).astype(out_ref.dtype)


def _merge(outs, lses, g_attn, y_ssm, gates, w_a, w_s):
    m, aw = g_attn.shape
    d = w_a.shape[1]
    nb = outs[0].shape[0]
    seq = m // nb
    tm = _row_tile(seq, 512)
    bps = seq // tm
    row = pl.BlockSpec((tm, aw), lambda i: (i, 0))
    ol_specs = []
    for arr in list(outs) + list(lses):
        dil = arr.shape[1]
        assert tm % (dil * ROW_TILE) == 0
        ol_specs.append(pl.BlockSpec((1, dil, tm // dil, arr.shape[3]),
                                     lambda i: (lax.div(i, bps), 0, lax.rem(i, bps), 0)))
    scratch = [pltpu.VMEM((tm, aw), BF16), pltpu.VMEM((N_DIL, tm, HEAD_DIM), F32),
               pltpu.VMEM((N_DIL, tm, HEAD_DIM), F32)]
    once = pl.Buffered(1)
    return pl.pallas_call(
        _merge_kernel,
        grid=(m // tm,),
        in_specs=ol_specs + [row, row, pl.BlockSpec((tm, d), lambda i: (i, 0)),
                             pl.BlockSpec((tm, d), lambda i: (i, 1)),
                             pl.BlockSpec((aw, d), lambda i: (0, 0), pipeline_mode=once),
                             pl.BlockSpec((aw, d), lambda i: (0, 0), pipeline_mode=once)],
        out_specs=pl.BlockSpec((tm, d), lambda i: (i, 0)),
        out_shape=jax.ShapeDtypeStruct((m, d), BF16),
        scratch_shapes=scratch,
        compiler_params=_params("parallel"),
        name="merge",
    )(*outs, *lses, g_attn, y_ssm, gates, gates, w_a, w_s)


def _out_kernel(x_ref, m_ref, w_ref, o_ref):
    o_ref[...] = x_ref[...] + jnp.dot(m_ref[...], w_ref[...], preferred_element_type=F32)


def _out_proj(x, merged, w):
    m, d = x.shape
    tm = _row_tile(m, 1024)
    tn = min(d, 1024)
    return pl.pallas_call(
        _out_kernel,
        grid=(d // tn, m // tm),
        in_specs=[pl.BlockSpec((tm, tn), lambda j, i: (i, j)),
                  pl.BlockSpec((tm, d), lambda j, i: (i, 0)),
                  pl.BlockSpec((d, tn), lambda j, i: (0, j))],
        out_specs=pl.BlockSpec((tm, tn), lambda j, i: (i, j)),
        out_shape=jax.ShapeDtypeStruct((m, d), F32),
        compiler_params=_params("parallel", "parallel"),
        name="out_proj",
    )(x, merged, w)


def _sample_front_kernel(x_ref, gw_ref, w_ref, nwq_ref, nwk_ref, cos_ref, sin_ref, o_ref, nw_ref):
    j = pl.program_id(0)
    x = x_ref[...]
    xn = x * lax.rsqrt(jnp.mean(x * x, axis=-1, keepdims=True) + NORM_EPS) * gw_ref[...]
    acc = jnp.dot(xn.astype(BF16), w_ref[...], preferred_element_type=F32)
    qk, v_end = 2 * N_DIL, 3 * N_DIL

    @pl.when(j < qk)
    def _():
        nw_ref[...] = jnp.where(j < N_DIL, nwq_ref[...], nwk_ref[...])
        o_ref[...] = acc
        _norm_rope_inplace(o_ref, nw_ref, cos_ref, sin_ref)

    @pl.when(((j >= qk) & (j < v_end)) | (j == v_end + 1))
    def _():
        o_ref[...] = acc

    @pl.when((j == v_end) | (j == v_end + 2))
    def _():
        o_ref[...] = acc * _sigmoid(acc)

    @pl.when(j > v_end + 2)
    def _():
        o_ref[...] = _sigmoid(acc)


def _sample_front(x, norm_w, w, q_norm_w, k_norm_w, cos, sin):
    m, d = x.shape
    aw = d // 2
    n_cols = w.shape[1]
    row = pl.BlockSpec((m, HEAD_DIM), lambda j: (0, 0))
    one = pl.BlockSpec((1, HEAD_DIM), lambda j: (0, 0))
    return pl.pallas_call(
        _sample_front_kernel,
        grid=(n_cols // aw,),
        in_specs=[pl.BlockSpec((m, d), lambda j: (0, 0)), pl.BlockSpec((1, d), lambda j: (0, 0)),
                  pl.BlockSpec((d, aw), lambda j: (0, j)), one, one, row, row],
        out_specs=pl.BlockSpec((m, aw), lambda j: (0, j)),
        out_shape=jax.ShapeDtypeStruct((m, n_cols), F32),
        scratch_shapes=[pltpu.VMEM((1, HEAD_DIM), F32)],
        compiler_params=_params("arbitrary"),
        name="sample_front",
    )(x, norm_w.reshape(1, d), w, q_norm_w.reshape(1, HEAD_DIM), k_norm_w.reshape(1, HEAD_DIM), cos, sin)


def _layer(x, cos, sin, lw, nb, seq, layer, depth, kv_prev, sample, shift=None):
    (norm_w, w_in, q_norm_w, k_norm_w, w_glu, b_glu, w_br_attn, w_br_ssm, w_out, ssm_d, s5prm) = lw
    m, d = x.shape
    aw = d // 2
    qkv = N_DIL * aw
    hg = aw // HEAD_DIM

    new_kv = []
    if sample is None:
        xn = _rmsnorm(x, norm_w)
        g_attn = _proj(xn, w_in, 3 * qkv, aw, "silu", BF16)
        u = _proj(xn, w_in, 3 * qkv + aw, aw, "plain", F32)
        g_ssm = _proj(xn, w_in, 3 * qkv + 2 * aw, aw, "silu", BF16)
        gates = _proj(xn, w_in, 3 * qkv + 3 * aw, 2 * d, "sigmoid", BF16)
        qds = _q_all(xn, w_in, nb, seq, q_norm_w, cos, sin)
        outs, lses = [], []
        for g in range(N_DIL):
            kd, vd, kv = _kv(xn, w_in, g, nb, seq, k_norm_w, cos, sin, layer, depth,
                             None if kv_prev is None else kv_prev[g])
            o, lse = _attn_prompt(qds[g], kd, vd)
            outs.append(o)
            lses.append(lse)
            new_kv.append(kv)
        y, hre, him, shifted = _s5_prompt(u, s5prm, ssm_d, nb, seq, shift)
    else:
        caches, h0re, h0im = sample
        front = _sample_front(x, norm_w, w_in, q_norm_w, k_norm_w, cos, sin)
        q, k, v = front[:, :qkv], front[:, qkv:2 * qkv], front[:, 2 * qkv:3 * qkv]
        g_attn, u = front[:, 3 * qkv:3 * qkv + aw], front[:, 3 * qkv + aw:3 * qkv + 2 * aw]
        g_ssm, gates = front[:, 3 * qkv + 2 * aw:3 * qkv + 3 * aw], front[:, 3 * qkv + 3 * aw:]
        tok = (nb, seq, N_DIL * hg, HEAD_DIM)
        o, lse = _attn_sample(q.reshape(tok), k.reshape(tok), v.reshape(tok), caches, layer)
        o = o.reshape(m, qkv)
        lse = jnp.pad(lse[..., 0].reshape(m, N_DIL, hg), ((0, 0), (0, 0), (0, HEAD_DIM - hg)))
        outs = [o[:, g * aw:(g + 1) * aw].reshape(1, 1, m, aw) for g in range(N_DIL)]
        lses = [lse[:, g].reshape(1, 1, m, HEAD_DIM) for g in range(N_DIL)]
        ut = u.reshape(nb, seq, aw).transpose(1, 0, 2)
        yt, hre, him = _s5_sample(ut, h0re, h0im, s5prm, ssm_d)
        y = yt.transpose(1, 0, 2).reshape(m, aw)
        k5, v5 = k.reshape(nb, seq, N_DIL, hg, HEAD_DIM), v.reshape(nb, seq, N_DIL, hg, HEAD_DIM)
        new_kv = [jnp.stack([k5[:, :, g], v5[:, :, g]], axis=2) for g in range(N_DIL)]
        shifted = None

    y_ssm = _glu(y, g_ssm, w_glu, b_glu)
    merged = _merge(outs, lses, g_attn, y_ssm, gates, w_br_attn, w_br_ssm)
    x_out = _out_proj(x, merged, w_out)

    n_grp = aw // SSM_GROUP_CH
    state = jnp.stack([hre, him], axis=-1).reshape(nb, n_grp, SSM_STATE, 2)
    return x_out, new_kv, state, shifted


def kernel(x_prompt, x_sample, cache_kv_d1, cache_kv_d4, cache_kv_d16, state_ssm, norm_w, w_in, q_norm_w, k_norm_w, ssm_lambda_re, ssm_lambda_im, ssm_log_dt, ssm_b_re, ssm_b_im, ssm_c_re, ssm_c_im, ssm_d, w_glu, b_glu, w_br_attn, w_br_ssm, w_out):
    nb_p, seq_p, d = x_prompt.shape
    nb_s, seq_s, _ = x_sample.shape
    depth = norm_w.shape[0]
    caches = (cache_kv_d1, cache_kv_d4, cache_kv_d16)

    cos_p, sin_p = _rope_tables(jnp.arange(seq_p, dtype=F32))
    cos_s, sin_s = _rope_tables(PAST_LEN + jnp.arange(seq_s, dtype=F32))
    cos_s, sin_s = jnp.tile(cos_s, (nb_s, 1)), jnp.tile(sin_s, (nb_s, 1))

    hp = x_prompt.reshape(nb_p * seq_p, d)
    hs = x_sample.reshape(nb_s * seq_s, d)
    lws = []
    for l in range(depth):
        s5prm = _s5_params(ssm_lambda_re[l], ssm_lambda_im[l], ssm_log_dt[l], ssm_b_re[l], ssm_b_im[l],
                           ssm_c_re[l], ssm_c_im[l], _s5_chunk(seq_p) // ROW_TILE)
        lws.append((norm_w[l], _layer_bf16(w_in, l), q_norm_w[l], k_norm_w[l], _layer_bf16(w_glu, l), b_glu[l],
                    _layer_bf16(w_br_attn, l), _layer_bf16(w_br_ssm, l), _layer_bf16(w_out, l), ssm_d[l], s5prm))

    kv_s_new = [[] for _ in range(N_DIL)]
    ssm_s = []
    n_state = state_ssm.shape[2] * state_ssm.shape[3]
    for l in range(depth):
        h0re = state_ssm[l, ..., 0].reshape(nb_s, n_state)
        h0im = state_ssm[l, ..., 1].reshape(nb_s, n_state)
        hs, new_kv_s, st_s, _ = _layer(hs, cos_s, sin_s, lws[l], nb_s, seq_s, l, depth, None,
                                       (caches, h0re, h0im))
        for g in range(N_DIL):
            kv_s_new[g].append(new_kv_s[g])
        ssm_s.append(st_s)

    kv_p, kv_s, ssm_p = None, None, []
    for l in range(depth):
        shift = (caches, [jnp.stack(n) for n in kv_s_new]) if l == 0 else None
        hp, kv_p, st, shifted = _layer(hp, cos_p, sin_p, lws[l], nb_p, seq_p, l, depth, kv_p, None, shift)
        kv_s = shifted if shifted is not None else kv_s
        ssm_p.append(st)

    return (hp.reshape(x_prompt.shape), hs.reshape(x_sample.shape), kv_p[0], kv_p[1], kv_p[2], jnp.stack(ssm_p),
            kv_s[0], kv_s[1], kv_s[2], jnp.stack(ssm_s))
```

```python
import functools
import math

import jax
import jax.numpy as jnp
from jax import lax
from jax.experimental import pallas as pl
from jax.experimental.pallas import tpu as pltpu

F32 = jnp.float32
BF16 = jnp.bfloat16

HEAD_DIM = 128
WINDOWS = (128, 512, 2048)
DILATIONS = (1, 4, 16)
N_DIL = 3
SPAN = 128
SSM_GROUP_CH = 16
SSM_STATE = 64
GROUPS_PER_CHUNK = 8
ROPE_THETA = 10000.0
NORM_EPS = 1e-6
PAST_LEN = 16384
NEG_BIG = -1e30
VMEM_LIMIT = 56 * 1024 * 1024
ROW_TILE = 8
MXU_COLS = 256
QK_ROW_CHUNK = 64
KV_COPY_BYTES = 4 * 1024 * 1024


def _params(*sem):
    return pltpu.CompilerParams(dimension_semantics=sem, vmem_limit_bytes=VMEM_LIMIT)


def _sigmoid(x):
    return 1.0 / (1.0 + jnp.exp(-x))


def _row_tile(m, cap):
    t = min(m, cap)
    assert m % t == 0, (m, t)
    return t


def _rmsnorm_kernel(x_ref, w_ref, o_ref):
    x = x_ref[...]
    y = x * lax.rsqrt(jnp.mean(x * x, axis=-1, keepdims=True) + NORM_EPS)
    o_ref[...] = (y * w_ref[...]).astype(o_ref.dtype)


def _rmsnorm(x, w):
    m, d = x.shape
    tm = _row_tile(m, 512)
    return pl.pallas_call(
        _rmsnorm_kernel,
        grid=(m // tm,),
        in_specs=[pl.BlockSpec((tm, d), lambda i: (i, 0)),
                  pl.BlockSpec((1, d), lambda i: (0, 0))],
        out_specs=pl.BlockSpec((tm, d), lambda i: (i, 0)),
        out_shape=jax.ShapeDtypeStruct((m, d), BF16),
        compiler_params=_params("parallel"),
        name="rmsnorm",
    )(x, w.reshape(1, d))


def _cast_kernel(w_ref, o_ref):
    o_ref[...] = w_ref[...].astype(o_ref.dtype)


def _layer_bf16(w, layer):
    _, rows, cols = w.shape
    tr, tc = min(rows, 512), min(cols, 2048)
    assert rows % tr == 0 and cols % tc == 0
    return pl.pallas_call(
        _cast_kernel,
        grid=(rows // tr, cols // tc),
        in_specs=[pl.BlockSpec((None, tr, tc), lambda i, j: (layer, i, j))],
        out_specs=pl.BlockSpec((tr, tc), lambda i, j: (i, j)),
        out_shape=jax.ShapeDtypeStruct((rows, cols), BF16),
        compiler_params=_params("parallel", "parallel"),
        name="cast_bf16",
    )(w)


def _norm_rope_inplace(t_ref, nw_ref, cos_ref, sin_ref, head_major=False):
    rows = cos_ref.shape[0]
    n_heads = t_ref.shape[0] // rows if head_major else t_ref.shape[1] // HEAD_DIM
    rc = min(rows, QK_ROW_CHUNK)
    half = HEAD_DIM // 2
    nw = nw_ref[...]
    nw_swapped = pltpu.roll(nw, half, axis=1)

    def chunk(c, carry):
        r0 = pl.multiple_of(c * rc, rc)
        cs = cos_ref[pl.ds(r0, rc), :] * nw
        ss = sin_ref[pl.ds(r0, rc), :] * nw_swapped
        for h in range(n_heads):
            idx = (pl.ds(h * rows + r0, rc), slice(None)) if head_major else (
                pl.ds(r0, rc), slice(h * HEAD_DIM, (h + 1) * HEAD_DIM))
            a = t_ref[idx]
            r = lax.rsqrt(jnp.mean(a * a, axis=-1, keepdims=True) + NORM_EPS)
            t_ref[idx] = (a * cs + pltpu.roll(a, half, axis=1) * ss) * r
        return carry

    lax.fori_loop(0, rows // rc, chunk, 0)


def _proj_kernel(x_ref, w_ref, o_ref, *, epilogue):
    acc = jnp.dot(x_ref[...], w_ref[...], preferred_element_type=F32)
    if epilogue == "silu":
        acc = acc * _sigmoid(acc)
    elif epilogue == "sigmoid":
        acc = _sigmoid(acc)
    o_ref[...] = acc.astype(o_ref.dtype)


def _proj(xn, w, col0, ncols, epilogue, out_dtype):
    m, d = xn.shape
    tm = _row_tile(m, 1024)
    tn = min(ncols, 1024)
    assert ncols % tn == 0 and col0 % tn == 0
    jb = col0 // tn
    return pl.pallas_call(
        functools.partial(_proj_kernel, epilogue=epilogue),
        grid=(m // tm, ncols // tn),
        in_specs=[pl.BlockSpec((tm, d), lambda i, j: (i, 0)),
                  pl.BlockSpec((d, tn), lambda i, j: (0, jb + j))],
        out_specs=pl.BlockSpec((tm, tn), lambda i, j: (i, j)),
        out_shape=jax.ShapeDtypeStruct((m, ncols), out_dtype),
        compiler_params=_params("parallel", "arbitrary"),
        name="proj_" + epilogue,
    )(xn, w)


def _norm_rope_head(a, ssq, nw_ref, cos_ref, sin_ref, store):
    rows = a.shape[0]
    rc = min(rows, QK_ROW_CHUNK)
    half = HEAD_DIM // 2
    nw = nw_ref[...]
    nw_swapped = pltpu.roll(nw, half, axis=1)
    for r0 in range(0, rows, rc):
        ac = a[r0:r0 + rc]
        cs = cos_ref[r0:r0 + rc, :] * nw
        ss = sin_ref[r0:r0 + rc, :] * nw_swapped
        r = lax.rsqrt(ssq[r0:r0 + rc] * (1.0 / HEAD_DIM) + NORM_EPS)
        store(r0, rc, (ac * cs + pltpu.roll(ac, half, axis=1) * ss) * r)


def _project_dilated(x, w_ref, nw_ref, cos_ref, sin_ref, hm_ref, d_ref, dil):
    tm = x.shape[0]
    n_heads = w_ref.shape[1] // HEAD_DIM
    per = tm // dil
    acc = jnp.dot(x, w_ref[...], preferred_element_type=F32)
    ones = jnp.ones((HEAD_DIM, HEAD_DIM), BF16)
    for h in range(n_heads):
        sl = slice(h * HEAD_DIM, (h + 1) * HEAD_DIM)
        a = acc[:, sl]
        if nw_ref is None:
            hm_ref[h * tm:(h + 1) * tm, :] = a
        else:
            def store(r0, rc, chunk, h=h):
                hm_ref[h * tm + r0:h * tm + r0 + rc, :] = chunk

            ssq = jnp.dot((a * a).astype(BF16), ones, preferred_element_type=F32)
            _norm_rope_head(a, ssq, nw_ref, cos_ref, sin_ref, store)
        for r in range(dil):
            rows = pl.ds(h * tm + r, per, stride=dil) if dil > 1 else pl.ds(h * tm, per)
            d_ref[0, r, :, sl] = hm_ref[rows, :].astype(d_ref.dtype)


def _q_kernel(x_ref, w_ref, nw_ref, cos_ref, sin_ref, *rest):
    qd_refs, hm_ref = rest[:N_DIL], rest[N_DIL]
    for g in range(N_DIL):
        @pl.when(pl.program_id(0) == g)
        def _(g=g):
            _project_dilated(x_ref[...], w_ref, nw_ref, cos_ref, sin_ref, hm_ref, qd_refs[g], DILATIONS[g])


def _q_all(xn, w, nb, seq, q_norm_w, cos, sin):
    m, d = xn.shape
    aw = d // 2
    tm = _row_tile(seq, 1024)
    bps = seq // tm
    out_specs, out_shape = [], []
    for g in range(N_DIL):
        dil = DILATIONS[g]
        assert (tm // dil) % 16 == 0

        def q_map(gi, b, i, g=g):
            return (jnp.where(gi < g, 0, jnp.where(gi == g, b, nb - 1)), 0,
                    jnp.where(gi < g, 0, jnp.where(gi == g, i, bps - 1)), 0)

        out_specs.append(pl.BlockSpec((1, dil, tm // dil, aw), q_map))
        out_shape.append(jax.ShapeDtypeStruct((nb, dil, seq // dil, aw), BF16))
    return pl.pallas_call(
        _q_kernel,
        grid=(N_DIL, nb, bps),
        in_specs=[pl.BlockSpec((tm, d), lambda g, b, i: (b * bps + i, 0)),
                  pl.BlockSpec((d, aw), lambda g, b, i: (0, g)),
                  pl.BlockSpec((1, HEAD_DIM), lambda g, b, i: (0, 0)),
                  pl.BlockSpec((tm, HEAD_DIM), lambda g, b, i: (i, 0)),
                  pl.BlockSpec((tm, HEAD_DIM), lambda g, b, i: (i, 0))],
        out_specs=out_specs,
        out_shape=out_shape,
        scratch_shapes=[pltpu.VMEM((aw // HEAD_DIM * tm, HEAD_DIM), F32)],
        compiler_params=_params("arbitrary", "arbitrary", "arbitrary"),
        name="q_all",
    )(xn, w, q_norm_w.reshape(1, HEAD_DIM), cos, sin)


def _kv_kernel(x_ref, wk_ref, wv_ref, nw_ref, cos_ref, sin_ref, kv_in_ref, kd_ref, vd_ref, kv_ref,
               hk_ref, hv_ref, rows_ref, sem, *, dil, layer, first_kept):
    b, i = pl.program_id(0), pl.program_id(1)
    tm = x_ref.shape[0]
    n_heads = wk_ref.shape[1] // HEAD_DIM
    kb = rows_ref.shape[0]
    x = x_ref[...]
    _project_dilated(x, wk_ref, nw_ref, cos_ref, sin_ref, hk_ref, kd_ref, dil)
    _project_dilated(x, wv_ref, None, None, None, hv_ref, vd_ref, dil)

    def copy_out(bb, ii):
        dst = kv_ref.at[layer, bb, pl.ds((ii - first_kept) * kb, kb)]
        return pltpu.make_async_copy(rows_ref, dst, sem.at[0])

    @pl.when(i >= first_kept)
    def _():
        @pl.when((i > first_kept) | (b > 0))
        def _():
            prev_same = i > first_kept
            copy_out(jnp.where(prev_same, b, b - 1),
                     jnp.where(prev_same, i - 1, pl.num_programs(1) - 1)).wait()

        def tokens(j, carry):
            t0 = pl.multiple_of(j * ROW_TILE, ROW_TILE)
            for which, hm_ref in ((0, hk_ref), (1, hv_ref)):
                tiles = [hm_ref[pl.ds(h * tm + tm - kb + t0, ROW_TILE), :] for h in range(n_heads)]
                rows_ref[pl.ds(t0, ROW_TILE), which] = jnp.swapaxes(jnp.stack(tiles, axis=0), 0, 1)
            return carry

        lax.fori_loop(0, kb // ROW_TILE, tokens, 0, unroll=2)
        copy_out(b, i).start()

    @pl.when((b == pl.num_programs(0) - 1) & (i == pl.num_programs(1) - 1))
    def _():
        copy_out(b, i).wait()


def _kv(xn, w, g, nb, seq, k_norm_w, cos, sin, layer, depth, kv_prev):
    m, d = xn.shape
    aw = d // 2
    hg = aw // HEAD_DIM
    dil = DILATIONS[g]
    keep = min(WINDOWS[g], seq)
    tm = _row_tile(seq, 1024)
    per = tm // dil
    kb = min(keep, tm)
    bps = seq // tm
    first_kept = (seq - keep) // tm if keep >= tm else bps - 1
    assert per % 16 == 0 and keep % kb == 0 and (seq - keep) % kb == 0 and kb % ROW_TILE == 0
    once = pl.Buffered(1)
    nw = lambda a: a.reshape(1, HEAD_DIM)
    dshape = jax.ShapeDtypeStruct((nb, dil, seq // dil, aw), BF16)
    staging = pltpu.VMEM((hg * tm, HEAD_DIM), F32)

    wspec = lambda col: pl.BlockSpec((d, aw), lambda b, i: (0, col), pipeline_mode=once)
    tspec = pl.BlockSpec((tm, HEAD_DIM), lambda b, i: (i, 0))
    dspec = pl.BlockSpec((1, dil, per, aw), lambda b, i: (b, 0, i, 0))
    hbm = pl.BlockSpec(memory_space=pl.ANY)
    kv_shape = jax.ShapeDtypeStruct((depth, nb, keep, 2, hg, HEAD_DIM), F32)
    if kv_prev is None:
        kv_prev = jnp.zeros(kv_shape.shape, kv_shape.dtype)
    kd, vd, kv = pl.pallas_call(
        functools.partial(_kv_kernel, dil=dil, layer=layer, first_kept=first_kept),
        grid=(nb, bps),
        in_specs=[pl.BlockSpec((tm, d), lambda b, i: (b * bps + i, 0)),
                  wspec(N_DIL + g), wspec(2 * N_DIL + g),
                  pl.BlockSpec((1, HEAD_DIM), lambda b, i: (0, 0)), tspec, tspec, hbm],
        out_specs=[dspec, dspec, hbm],
        out_shape=[dshape, dshape, kv_shape],
        scratch_shapes=[staging, staging, pltpu.VMEM((kb, 2, hg, HEAD_DIM), F32),
                        pltpu.SemaphoreType.DMA((1,))],
        input_output_aliases={6: 2},
        compiler_params=_params("arbitrary", "arbitrary"),
        name="kv_d%d" % dil,
    )(xn, w, w, nw(k_norm_w), cos, sin, kv_prev)
    return kd, vd, kv


def _rope_tables(pos):
    half = HEAD_DIM // 2
    inv_freq = jnp.power(ROPE_THETA, -jnp.arange(half, dtype=F32) * (2.0 / HEAD_DIM))
    ang = pos[:, None] * inv_freq[None, :]
    cos, sin = jnp.cos(ang), jnp.sin(ang)
    return jnp.concatenate([cos, cos], axis=-1), jnp.concatenate([-sin, sin], axis=-1)


def _attn_prompt_kernel(q_ref, kp_ref, kc_ref, vp_ref, vc_ref, o_ref, lse_ref):
    not_first = pl.program_id(2) > 0
    a = lax.broadcasted_iota(jnp.int32, (SPAN, 2 * SPAN), 0)
    c = lax.broadcasted_iota(jnp.int32, (SPAN, 2 * SPAN), 1)
    band = (c >= a) & (c <= a + SPAN)
    band_first = band & ((c >= SPAN) | not_first)
    lane = lax.broadcasted_iota(jnp.int32, (SPAN, HEAD_DIM), 1)
    scale = HEAD_DIM ** -0.5
    for j in range(q_ref.shape[2] // SPAN):
        rows = slice(j * SPAN, (j + 1) * SPAN)
        lse_tile = jnp.zeros((SPAN, HEAD_DIM), F32)
        for h in range(q_ref.shape[3] // HEAD_DIM):
            sl = slice(h * HEAD_DIM, (h + 1) * HEAD_DIM)
            q = q_ref[0, 0, rows, sl]
            if j == 0:
                k = jnp.concatenate([kp_ref[0, 0, :, sl], kc_ref[0, 0, rows, sl]], axis=0)
                v = jnp.concatenate([vp_ref[0, 0, :, sl], vc_ref[0, 0, rows, sl]], axis=0)
            else:
                k = kc_ref[0, 0, (j - 1) * SPAN:(j + 1) * SPAN, sl]
                v = vc_ref[0, 0, (j - 1) * SPAN:(j + 1) * SPAN, sl]
            s = lax.dot_general(q, k, (((1,), (1,)), ((), ())), preferred_element_type=F32) * scale
            s = jnp.where(band_first if j == 0 else band, s, NEG_BIG)
            m = jnp.max(s, axis=-1, keepdims=True)
            p = jnp.exp(s - m)
            l = jnp.sum(p, axis=-1, keepdims=True)
            o = jnp.dot(p.astype(BF16), v, preferred_element_type=F32) / l
            o_ref[0, 0, rows, sl] = o.astype(o_ref.dtype)
            lse_tile = jnp.where(lane == h, m + jnp.log(l), lse_tile)
        lse_ref[0, 0, rows, :] = lse_tile


def _attn_prompt(q, k, v):
    nb, dil, length, aw = q.shape
    n_blk = length // SPAN
    qb = min(4, n_blk)
    assert length % SPAN == 0 and n_blk % qb == 0 and aw // HEAD_DIM <= HEAD_DIM
    cur = pl.BlockSpec((1, 1, qb * SPAN, aw), lambda b, r, n: (b, r, n, 0))
    prev = pl.BlockSpec((1, 1, SPAN, aw), lambda b, r, n: (b, r, jnp.maximum(qb * n - 1, 0), 0))
    return pl.pallas_call(
        _attn_prompt_kernel,
        grid=(nb, dil, n_blk // qb),
        in_specs=[cur, prev, cur, prev, cur],
        out_specs=[cur, pl.BlockSpec((1, 1, qb * SPAN, HEAD_DIM), lambda b, r, n: (b, r, n, 0))],
        out_shape=[jax.ShapeDtypeStruct(q.shape, BF16),
                   jax.ShapeDtypeStruct((nb, dil, length, HEAD_DIM), F32)],
        compiler_params=_params("parallel", "parallel", "arbitrary"),
        name="attn_prompt_d%d" % dil,
    )(q, k, k, v, v)


def _attn_sample_kernel(q_ref, kn_ref, vn_ref, c1_ref, c4_ref, c16_ref, o_ref, lse_ref):
    n_new, n_heads, _ = q_ref.shape
    hg = n_heads // N_DIL
    scale = HEAD_DIM ** -0.5
    caches = (c1_ref, c4_ref, c16_ref)
    for g in range(N_DIL):
        hs = slice(g * hg, (g + 1) * hg)
        for t in range(n_new):
            q = q_ref[t, hs, :]
            if DILATIONS[g] == 1:
                kc, vc = caches[g][:, 0], caches[g][:, 1]
                rows = lax.broadcasted_iota(jnp.int32, (kc.shape[0], hg, 1), 0)
                s = jnp.sum(kc * q[None], axis=-1, keepdims=True) * scale
                s = jnp.where(rows >= t, s, NEG_BIG)
                new = range(t + 1)
            else:
                kc, vc = caches[g][:, t, 0], caches[g][:, t, 1]
                s = jnp.sum(kc * q[None], axis=-1, keepdims=True) * scale
                new = (t,)
            s_new = [jnp.sum(kn_ref[u, hs, :] * q, axis=-1, keepdims=True) * scale for u in new]
            m = jnp.max(s, axis=0)
            for sn in s_new:
                m = jnp.maximum(m, sn)
            p = jnp.exp(s - m[None])
            l = jnp.sum(p, axis=0)
            acc = jnp.sum(p * vc, axis=0)
            for u, sn in zip(new, s_new):
                pn = jnp.exp(sn - m)
                l = l + pn
                acc = acc + pn * vn_ref[u, hs, :]
            o_ref[t, hs, :] = acc / l
            lse_ref[t, hs, :] = jnp.broadcast_to(m + jnp.log(l), (hg, HEAD_DIM))


def _attn_sample(q, kn, vn, caches, layer):
    nb, n_new, n_heads, _ = q.shape
    hg = n_heads // N_DIL
    views, specs = [], []
    for g in range(N_DIL):
        dil = DILATIONS[g]
        depth, _, buf = caches[g].shape[:3]
        assert buf == WINDOWS[g] and (dil == 1 or dil % n_new == 0)
        if dil == 1:
            views.append(caches[g])
            specs.append(pl.BlockSpec((None, None, buf, 2, hg, HEAD_DIM),
                                      lambda b: (layer, b, 0, 0, 0, 0)))
        else:
            views.append(caches[g].reshape(depth, nb, SPAN, dil, 2, hg, HEAD_DIM))
            specs.append(pl.BlockSpec((None, None, SPAN, n_new, 2, hg, HEAD_DIM),
                                      lambda b: (layer, b, 0, 0, 0, 0, 0)))
    tok = pl.BlockSpec((None, n_new, n_heads, HEAD_DIM), lambda b: (b, 0, 0, 0))
    return pl.pallas_call(
        _attn_sample_kernel,
        grid=(nb,),
        in_specs=[tok, tok, tok] + specs,
        out_specs=[tok, tok],
        out_shape=[jax.ShapeDtypeStruct(q.shape, F32)] * 2,
        compiler_params=_params("parallel"),
        name="attn_sample",
    )(q, kn, vn, *views)


def _s5_params_kernel(lre_ref, lim_ref, ldt_ref, bre_ref, bim_ref, ptab_ref, bbre_ref, bbim_ref):
    dt = jnp.exp(ldt_ref[...])
    lre, lim = lre_ref[...], lim_ref[...]
    xr, xi = lre * dt, lim * dt
    mag = jnp.exp(xr)
    ar, ai = mag * jnp.cos(xi), mag * jnp.sin(xi)
    shape = (ROW_TILE, xr.shape[1])
    abr, abi = jnp.broadcast_to(ar, shape), jnp.broadcast_to(ai, shape)
    ptab_ref[0, 0] = abr
    ptab_ref[1, 0] = abi

    def next_power(i, carry):
        pr, pi = carry
        pr, pi = pr * abr - pi * abi, pr * abi + pi * abr
        ptab_ref[0, i] = pr
        ptab_ref[1, i] = pi
        return pr, pi

    lax.fori_loop(1, ptab_ref.shape[1], next_power, (abr, abi))
    nr, ni = ar - 1.0, ai
    den = lre * lre + lim * lim
    fr = (nr * lre + ni * lim) / den
    fi = (ni * lre - nr * lim) / den
    bre, bim = bre_ref[...], bim_ref[...]
    bbre_ref[...] = fr * bre - fi * bim
    bbim_ref[...] = fr * bim + fi * bre


def _s5_params(lam_re, lam_im, log_dt, b_re, b_im, c_re, c_im, seg_len):
    n_grp, n_st = lam_re.shape
    n_ch = b_re.shape[2]
    n = n_grp * n_st
    gc = GROUPS_PER_CHUNK
    n_chunk = n_grp // gc
    sw = gc * n_st
    to_lanes = lambda b: b.transpose(2, 0, 1).reshape(n_ch, n)
    lane = lambda rows: pl.BlockSpec((rows, sw), lambda j: (0, j))
    ptab, bb_re, bb_im = pl.pallas_call(
        _s5_params_kernel,
        grid=(n_chunk,),
        in_specs=[lane(1)] * 3 + [lane(n_ch)] * 2,
        out_specs=[pl.BlockSpec((2, seg_len, ROW_TILE, sw), lambda j: (0, 0, 0, j)), lane(n_ch), lane(n_ch)],
        out_shape=[jax.ShapeDtypeStruct((2, seg_len, ROW_TILE, n), F32),
                   jax.ShapeDtypeStruct((n_ch, n), F32), jax.ShapeDtypeStruct((n_ch, n), F32)],
        compiler_params=_params("parallel"),
        name="s5_params",
    )(lam_re.reshape(1, n), lam_im.reshape(1, n),
      jnp.broadcast_to(log_dt[:, None], (n_grp, n_st)).reshape(1, n), to_lanes(b_re), to_lanes(b_im))

    eye = jnp.eye(gc, dtype=F32)

    def b_blocks(bb):
        bb = bb.reshape(n_ch, n_chunk, gc, n_st)
        blk = jnp.einsum("cjgp,gh->jgchp", bb, eye)
        return blk.reshape(n_chunk, gc * n_ch, gc * n_st).astype(BF16)

    def c_blocks(cc):
        cc = cc.reshape(n_chunk, gc, n_ch, n_st)
        blk = jnp.einsum("jgcp,gh->jgphc", cc, eye)
        return blk.reshape(n_chunk, gc * n_st, gc * n_ch).astype(BF16)

    b_cat = jnp.concatenate([b_blocks(bb_re), b_blocks(bb_im)], axis=2)
    c_cat = jnp.concatenate([c_blocks(c_re), c_blocks(c_im)], axis=1)
    return ptab, b_cat, c_cat


def _cmul_add(xr, xi, ar, ai, br, bi):
    return xr + ar * br - ai * bi, xi + ar * bi + ai * br


def _shift_plan(cache_shape, new_shape, n_steps):
    lb_n, buf = cache_shape[:2]
    keep = buf - new_shape[1]
    row_bytes = 4 * math.prod(cache_shape[2:])
    for per_row in range(1, keep + 1):
        n_jobs = lb_n * per_row
        if keep % per_row == 0 and keep // per_row * row_bytes <= KV_COPY_BYTES and n_steps % n_jobs == 0:
            return keep // per_row, per_row, n_steps // n_jobs
    raise ValueError("window shift does not fit the grid")


def _shift_lag(period):
    return 1 if period == 2 else 2


def _shift_slots(period):
    return 4 if period == 1 else 2


def _window_shift_step(step, n_steps, caches, news, outs, rings, nbufs, in_sem, out_sem, new_sem):
    last = step == n_steps - 1
    for g in range(N_DIL):
        cache, out, ring = caches[g], outs[g], rings[g]
        n_new = news[g].shape[1]
        keep = cache.shape[1] - n_new
        rows, per_row, period = _shift_plan(cache.shape, news[g].shape, n_steps)
        n_jobs = cache.shape[0] * per_row
        slots = ring.shape[0]
        lag = _shift_lag(period)
        assert n_jobs >= slots and (period == 1 or lag < period)

        def load(k, cache=cache, ring=ring, g=g, rows=rows, per_row=per_row, n_new=n_new, slots=slots):
            src = cache.at[lax.div(k, per_row), pl.ds(n_new + lax.rem(k, per_row) * rows, rows)]
            return pltpu.make_async_copy(src, ring.at[lax.rem(k, slots)], in_sem.at[g, lax.rem(k, slots)])

        def store(k, out=out, ring=ring, g=g, rows=rows, per_row=per_row, slots=slots):
            dst = out.at[lax.div(k, per_row), pl.ds(lax.rem(k, per_row) * rows, rows)]
            return pltpu.make_async_copy(ring.at[lax.rem(k, slots)], dst, out_sem.at[g, lax.rem(k, slots)])

        def issue(k, load=load, store=store, slots=slots):
            @pl.when(k >= slots)
            def _():
                store(k - slots).wait()
            load(k).start()

        def forward(k, load=load, store=store):
            load(k).wait()
            store(k).start()

        def drain(k, store=store, slots=slots):
            for back in range(slots - 1, -1, -1):
                store(k - back).wait()

        if period == 1:
            issue(step)

            @pl.when(step >= lag)
            def _(forward=forward, lag=lag):
                forward(step - lag)

            @pl.when(last)
            def _(forward=forward, drain=drain, lag=lag):
                for back in range(lag - 1, -1, -1):
                    forward(step - back)
                drain(step)
        else:
            k = lax.div(step, period)
            phase = lax.rem(step, period)

            @pl.when(phase == 0)
            def _(issue=issue, k=k):
                issue(k)

            @pl.when(phase == lag)
            def _(forward=forward, k=k):
                forward(k)

            @pl.when(last)
            def _(drain=drain, k=k):
                drain(k)

        new_in = pltpu.make_async_copy(news[g], nbufs[g], new_sem.at[g])
        new_out = pltpu.make_async_copy(nbufs[g], out.at[:, pl.ds(keep, n_new)], new_sem.at[N_DIL + g])

        @pl.when(step == 0)
        def _(new_in=new_in, new_out=new_out):
            new_in.start()
            new_in.wait()
            new_out.start()

        @pl.when(last)
        def _(new_out=new_out):
            new_out.wait()


def _s5_scan_kernel(u_ref, b_ref, ptab_ref, c_ref, d_ref, *rest, with_shift):
    if with_shift:
        n = N_DIL
        caches, news = rest[:n], rest[n:2 * n]
        y_ref, hfin_ref = rest[2 * n:2 * n + 2]
        outs = rest[2 * n + 2:3 * n + 2]
        up_ref, s_ref, hb_ref, end_ref = rest[3 * n + 2:3 * n + 6]
        rings, nbufs = rest[3 * n + 6:4 * n + 6], rest[4 * n + 6:5 * n + 6]
        in_sem, out_sem, new_sem = rest[5 * n + 6:]
        _, n_j, n_t = with_shift
        step = (pl.program_id(0) * n_j + pl.program_id(1)) * n_t + pl.program_id(2)
        _window_shift_step(step, math.prod(with_shift), caches, news, outs, rings, nbufs,
                           in_sem, out_sem, new_sem)
    else:
        y_ref, hfin_ref, up_ref, s_ref, hb_ref, end_ref = rest
    tc, sw2 = s_ref.shape
    sw = sw2 // 2
    seg = tc // ROW_TILE
    re, im = slice(0, sw), slice(sw, sw2)

    @pl.when(pl.program_id(2) == 0)
    def _():
        end_ref[...] = jnp.zeros_like(end_ref)

    for i in range(seg):
        up_ref[i * ROW_TILE:(i + 1) * ROW_TILE, :] = u_ref[0, pl.ds(i, ROW_TILE, stride=seg), :]
    s_ref[...] = jnp.dot(up_ref[...].astype(BF16), b_ref[0], preferred_element_type=F32)

    ar, ai = ptab_ref[0, 0], ptab_ref[1, 0]

    def local_step(i, carry):
        hr, hi = carry
        r0 = pl.multiple_of(i * ROW_TILE, ROW_TILE)
        hr, hi = _cmul_add(s_ref[pl.ds(r0, ROW_TILE), re], s_ref[pl.ds(r0, ROW_TILE), im], ar, ai, hr, hi)
        s_ref[pl.ds(r0, ROW_TILE), re] = hr
        s_ref[pl.ds(r0, ROW_TILE), im] = hi
        return hr, hi

    zero = jnp.zeros((ROW_TILE, sw), F32)
    er, ei = lax.fori_loop(0, seg, local_step, (zero, zero), unroll=8)

    row = lax.broadcasted_iota(jnp.int32, (ROW_TILE, sw), 0)
    cr = jnp.where(row == 0, pltpu.roll(end_ref[0], 1, axis=0), pltpu.roll(er, 1, axis=0))
    ci = jnp.where(row == 0, pltpu.roll(end_ref[1], 1, axis=0), pltpu.roll(ei, 1, axis=0))
    mr, mi = ptab_ref[0, seg - 1], ptab_ref[1, seg - 1]
    wr, wi = mr, mi
    for sh in (1, 2, 4):
        gr, gi = jnp.where(row >= sh, wr, 0.0), jnp.where(row >= sh, wi, 0.0)
        cr, ci = _cmul_add(cr, ci, gr, gi, pltpu.roll(cr, sh, axis=0), pltpu.roll(ci, sh, axis=0))
        wr, wi = wr * wr - wi * wi, 2.0 * wr * wi
    fr, fi = _cmul_add(er, ei, mr, mi, cr, ci)
    end_ref[0] = fr
    end_ref[1] = fi
    hfin_ref[0, 0] = fr
    hfin_ref[0, 1] = fi

    def fix_up(k, carry):
        r0 = pl.multiple_of(k * 2 * ROW_TILE, 2 * ROW_TILE)
        hs, ns = [], []
        for half in range(2):
            i = 2 * k + half
            rows = pl.ds(r0 + half * ROW_TILE, ROW_TILE)
            hr, hi = _cmul_add(s_ref[rows, re], s_ref[rows, im], ptab_ref[0, i], ptab_ref[1, i], cr, ci)
            hs.append(hr)
            ns.append(-hi)
        hb_ref[pl.ds(r0, 2 * ROW_TILE), re] = jnp.concatenate(hs, axis=0).astype(BF16)
        hb_ref[pl.ds(r0, 2 * ROW_TILE), im] = jnp.concatenate(ns, axis=0).astype(BF16)
        return carry

    lax.fori_loop(0, seg // 2, fix_up, 0, unroll=4)
    yp = jnp.dot(hb_ref[...], c_ref[0], preferred_element_type=F32) + d_ref[...] * up_ref[...]
    for i in range(seg):
        y_ref[0, pl.ds(i, ROW_TILE, stride=seg), :] = yp[i * ROW_TILE:(i + 1) * ROW_TILE, :]


def _s5_chunk(seq):
    return min(seq, 1024)


def _flat_window(a):
    return a.reshape(a.shape[0] * a.shape[1], a.shape[2], a.shape[3] * a.shape[4], a.shape[5])


def _s5_prompt(u, prm, ssm_d, nb, seq, shift=None):
    ptab, b_cat, c_cat = prm
    n_chunk, cw, sw2 = b_cat.shape
    sw = sw2 // 2
    width = u.shape[1]
    tc = _s5_chunk(seq)
    seg = tc // ROW_TILE
    assert seq % tc == 0 and seg % 2 == 0 and ptab.shape[1] == seg
    grid = (nb, n_chunk, seq // tc)
    in_specs = [pl.BlockSpec((1, tc, cw), lambda b, j, t: (b, t, j)),
                pl.BlockSpec((1, cw, sw2), lambda b, j, t: (j, 0, 0)),
                pl.BlockSpec((2, seg, ROW_TILE, sw), lambda b, j, t: (0, 0, 0, j)),
                pl.BlockSpec((1, sw2, cw), lambda b, j, t: (j, 0, 0)),
                pl.BlockSpec((1, cw), lambda b, j, t: (0, j))]
    out_specs = [pl.BlockSpec((1, tc, cw), lambda b, j, t: (b, t, j)),
                 pl.BlockSpec((1, 2, ROW_TILE, sw), lambda b, j, t: (b, 0, 0, j))]
    out_shape = [jax.ShapeDtypeStruct((nb, seq, width), F32),
                 jax.ShapeDtypeStruct((nb, 2, ROW_TILE, n_chunk * sw), F32)]
    scratch = [pltpu.VMEM((tc, cw), F32), pltpu.VMEM((tc, sw2), F32),
               pltpu.VMEM((tc, sw2), BF16), pltpu.VMEM((2, ROW_TILE, sw), F32)]
    args = [u.reshape(nb, seq, width), b_cat, ptab, c_cat, ssm_d.reshape(1, width)]
    if shift is not None:
        caches, news = shift
        cf, nf = [_flat_window(c) for c in caches], [_flat_window(w) for w in news]
        hbm = pl.BlockSpec(memory_space=pl.ANY)
        in_specs += [hbm] * (2 * N_DIL)
        out_specs += [hbm] * N_DIL
        out_shape += [jax.ShapeDtypeStruct(c.shape, c.dtype) for c in cf]
        args += cf + nf
        max_slots = 0
        for c, w in zip(cf, nf):
            rows, _, period = _shift_plan(c.shape, w.shape, math.prod(grid))
            scratch.append(pltpu.VMEM((_shift_slots(period), rows) + c.shape[2:], c.dtype))
            max_slots = max(max_slots, _shift_slots(period))
        scratch += [pltpu.VMEM(w.shape, w.dtype) for w in nf]
        scratch += [pltpu.SemaphoreType.DMA((N_DIL, max_slots)), pltpu.SemaphoreType.DMA((N_DIL, max_slots)),
                    pltpu.SemaphoreType.DMA((2 * N_DIL,))]
    res = pl.pallas_call(
        functools.partial(_s5_scan_kernel, with_shift=grid if shift is not None else None),
        grid=grid,
        in_specs=in_specs,
        out_specs=out_specs,
        out_shape=out_shape,
        scratch_shapes=scratch,
        compiler_params=_params("arbitrary", "arbitrary", "arbitrary") if shift is not None
        else _params("parallel", "parallel", "arbitrary"),
        name="s5_scan",
    )(*args)
    y, hfin = res[0], res[1]
    last = ROW_TILE - 1
    shifted = None if shift is None else [o.reshape(c.shape) for o, c in zip(res[2:], shift[0])]
    return y.reshape(nb * seq, width), hfin[:, 0, last], hfin[:, 1, last], shifted


def _s5_sample_kernel(u_ref, h0re_ref, h0im_ref, b_ref, ptab_ref, c_ref, d_ref, y_ref, hre_ref, him_ref):
    sw = h0re_ref.shape[1]
    ar, ai = ptab_ref[0, 0, 0:1, :], ptab_ref[1, 0, 0:1, :]
    hr, hi = h0re_ref[...], h0im_ref[...]
    for t in range(u_ref.shape[0]):
        u = u_ref[t]
        bu = jnp.dot(u.astype(BF16), b_ref[0], preferred_element_type=F32)
        hr, hi = _cmul_add(bu[:, :sw], bu[:, sw:], ar, ai, hr, hi)
        hb = jnp.concatenate([hr, -hi], axis=1).astype(BF16)
        y_ref[t] = jnp.dot(hb, c_ref[0], preferred_element_type=F32) + d_ref[...] * u
    hre_ref[...] = hr
    him_ref[...] = hi


def _s5_sample(u, h0re, h0im, prm, ssm_d):
    ptab, b_cat, c_cat = prm
    n_chunk, cw, sw2 = b_cat.shape
    sw = sw2 // 2
    n_new, nb, width = u.shape
    return pl.pallas_call(
        _s5_sample_kernel,
        grid=(n_chunk,),
        in_specs=[pl.BlockSpec((n_new, nb, cw), lambda j: (0, 0, j)),
                  pl.BlockSpec((nb, sw), lambda j: (0, j)),
                  pl.BlockSpec((nb, sw), lambda j: (0, j)),
                  pl.BlockSpec((1, cw, sw2), lambda j: (j, 0, 0)),
                  pl.BlockSpec((2, 1, ROW_TILE, sw), lambda j: (0, 0, 0, j)),
                  pl.BlockSpec((1, sw2, cw), lambda j: (j, 0, 0)),
                  pl.BlockSpec((1, cw), lambda j: (0, j))],
        out_specs=[pl.BlockSpec((n_new, nb, cw), lambda j: (0, 0, j)),
                   pl.BlockSpec((nb, sw), lambda j: (0, j)),
                   pl.BlockSpec((nb, sw), lambda j: (0, j))],
        out_shape=[jax.ShapeDtypeStruct((n_new, nb, width), F32),
                   jax.ShapeDtypeStruct((nb, n_chunk * sw), F32),
                   jax.ShapeDtypeStruct((nb, n_chunk * sw), F32)],
        compiler_params=_params("parallel"),
        name="s5_sample",
    )(u, h0re, h0im, b_cat, ptab, c_cat, ssm_d.reshape(1, width))


def _glu_kernel(y_ref, g_ref, w_ref, b_ref, o_ref):
    y = y_ref[...]
    s = 0.5 * y * (1.0 + jnp.tanh(math.sqrt(2.0 / math.pi) * (y + 0.044715 * (y * y * y))))
    z = jnp.dot(s.astype(BF16), w_ref[...], preferred_element_type=F32) + b_ref[...]
    o_ref[...] = (s * _sigmoid(z) * g_ref[...].astype(F32)).astype(o_ref.dtype)


def _glu(y, gate, w, b):
    m, width = y.shape
    tm = _row_tile(m, 1024)
    row = pl.BlockSpec((tm, width), lambda i: (i, 0))
    return pl.pallas_call(
        _glu_kernel,
        grid=(m // tm,),
        in_specs=[row, row, pl.BlockSpec((width, width), lambda i: (0, 0)),
                  pl.BlockSpec((1, width), lambda i: (0, 0))],
        out_specs=row,
        out_shape=jax.ShapeDtypeStruct((m, width), BF16),
        compiler_params=_params("parallel"),
        name="glu",
    )(y, gate, w, b.reshape(1, width))


def _merge_kernel(*refs):
    o_refs, l_refs = refs[:N_DIL], refs[N_DIL:2 * N_DIL]
    ga_ref, ys_ref, ma_ref, ms_ref, wa_ref, ws_ref, out_ref = refs[2 * N_DIL:2 * N_DIL + 7]
    a_ref, otok_ref, ltok_ref = refs[2 * N_DIL + 7:]

    def token_order(ref, cols, stage):
        dil, per = ref.shape[1], ref.shape[2]
        if dil == 1:
            return ref[0, 0, :, cols].astype(F32)
        for r in range(dil):
            stage[pl.ds(r, per, stride=dil), :] = ref[0, r, :, cols].astype(F32)
        return stage[...]

    ls = [token_order(l_refs[g], slice(None), ltok_ref.at[g]) for g in range(N_DIL)]
    mx = jnp.maximum(jnp.maximum(ls[0], ls[1]), ls[2])
    es = [jnp.exp(l - mx) for l in ls]
    den = es[0] + es[1] + es[2]
    wts = [e / den for e in es]
    tm, aw = a_ref.shape
    half = out_ref.shape[1] // 2
    halves = [slice(c * half, (c + 1) * half) for c in range(2)]
    ys = ys_ref[...]
    acc = [ms_ref[:, cs].astype(F32) * jnp.dot(ys, ws_ref[:, cs], preferred_element_type=F32) for cs in halves]
    group = min(aw, MXU_COLS) // HEAD_DIM
    ya = [None, None]
    for h0 in range(0, aw // HEAD_DIM, group):
        for h in range(h0, h0 + group):
            sl = slice(h * HEAD_DIM, (h + 1) * HEAD_DIM)
            attn = None
            for g in range(N_DIL):
                term = (jnp.broadcast_to(wts[g][:, h:h + 1], (tm, HEAD_DIM))
                        * token_order(o_refs[g], sl, otok_ref.at[g]))
                attn = term if attn is None else attn + term
            a_ref[:, sl] = (attn * ga_ref[:, sl].astype(F32)).astype(BF16)
        ks = slice(h0 * HEAD_DIM, (h0 + group) * HEAD_DIM)
        for c, cs in enumerate(halves):
            part = jnp.dot(a_ref[:, ks], wa_ref[ks, cs], preferred_element_type=F32)
            ya[c] = part if ya[c] is None else ya[c] + part
    for c, cs in enumerate(halves):
        out_ref[:, cs] = (ma_ref[:, cs].astype(F32) * ya[c] + acc[c]).astype(out_ref.dtype)


def _merge(outs, lses, g_attn, y_ssm, gates, w_a, w_s):
    m, aw = g_attn.shape
    d = w_a.shape[1]
    nb = outs[0].shape[0]
    seq = m // nb
    tm = _row_tile(seq, 512)
    bps = seq // tm
    row = pl.BlockSpec((tm, aw), lambda i: (i, 0))
    ol_specs = []
    for arr in list(outs) + list(lses):
        dil = arr.shape[1]
        assert tm % (dil * ROW_TILE) == 0
        ol_specs.append(pl.BlockSpec((1, dil, tm // dil, arr.shape[3]),
                                     lambda i: (lax.div(i, bps), 0, lax.rem(i, bps), 0)))
    scratch = [pltpu.VMEM((tm, aw), BF16), pltpu.VMEM((N_DIL, tm, HEAD_DIM), F32),
               pltpu.VMEM((N_DIL, tm, HEAD_DIM), F32)]
    once = pl.Buffered(1)
    return pl.pallas_call(
        _merge_kernel,
        grid=(m // tm,),
        in_specs=ol_specs + [row, row, pl.BlockSpec((tm, d), lambda i: (i, 0)),
                             pl.BlockSpec((tm, d), lambda i: (i, 1)),
                             pl.BlockSpec((aw, d), lambda i: (0, 0), pipeline_mode=once),
                             pl.BlockSpec((aw, d), lambda i: (0, 0), pipeline_mode=once)],
        out_specs=pl.BlockSpec((tm, d), lambda i: (i, 0)),
        out_shape=jax.ShapeDtypeStruct((m, d), BF16),
        scratch_shapes=scratch,
        compiler_params=_params("parallel"),
        name="merge",
    )(*outs, *lses, g_attn, y_ssm, gates, gates, w_a, w_s)


def _out_kernel(x_ref, m_ref, w_ref, o_ref):
    o_ref[...] = x_ref[...] + jnp.dot(m_ref[...], w_ref[...], preferred_element_type=F32)


def _out_proj(x, merged, w):
    m, d = x.shape
    tm = _row_tile(m, 1024)
    tn = min(d, 1024)
    return pl.pallas_call(
        _out_kernel,
        grid=(d // tn, m // tm),
        in_specs=[pl.BlockSpec((tm, tn), lambda j, i: (i, j)),
                  pl.BlockSpec((tm, d), lambda j, i: (i, 0)),
                  pl.BlockSpec((d, tn), lambda j, i: (0, j))],
        out_specs=pl.BlockSpec((tm, tn), lambda j, i: (i, j)),
        out_shape=jax.ShapeDtypeStruct((m, d), F32),
        compiler_params=_params("parallel", "parallel"),
        name="out_proj",
    )(x, merged, w)


def _sample_front_kernel(x_ref, gw_ref, w_ref, nwq_ref, nwk_ref, cos_ref, sin_ref, o_ref, nw_ref):
    j = pl.program_id(0)
    x = x_ref[...]
    xn = x * lax.rsqrt(jnp.mean(x * x, axis=-1, keepdims=True) + NORM_EPS) * gw_ref[...]
    acc = jnp.dot(xn.astype(BF16), w_ref[...], preferred_element_type=F32)
    qk, v_end = 2 * N_DIL, 3 * N_DIL

    @pl.when(j < qk)
    def _():
        nw_ref[...] = jnp.where(j < N_DIL, nwq_ref[...], nwk_ref[...])
        o_ref[...] = acc
        _norm_rope_inplace(o_ref, nw_ref, cos_ref, sin_ref)

    @pl.when(((j >= qk) & (j < v_end)) | (j == v_end + 1))
    def _():
        o_ref[...] = acc

    @pl.when((j == v_end) | (j == v_end + 2))
    def _():
        o_ref[...] = acc * _sigmoid(acc)

    @pl.when(j > v_end + 2)
    def _():
        o_ref[...] = _sigmoid(acc)


def _sample_front(x, norm_w, w, q_norm_w, k_norm_w, cos, sin):
    m, d = x.shape
    aw = d // 2
    n_cols = w.shape[1]
    row = pl.BlockSpec((m, HEAD_DIM), lambda j: (0, 0))
    one = pl.BlockSpec((1, HEAD_DIM), lambda j: (0, 0))
    return pl.pallas_call(
        _sample_front_kernel,
        grid=(n_cols // aw,),
        in_specs=[pl.BlockSpec((m, d), lambda j: (0, 0)), pl.BlockSpec((1, d), lambda j: (0, 0)),
                  pl.BlockSpec((d, aw), lambda j: (0, j)), one, one, row, row],
        out_specs=pl.BlockSpec((m, aw), lambda j: (0, j)),
        out_shape=jax.ShapeDtypeStruct((m, n_cols), F32),
        scratch_shapes=[pltpu.VMEM((1, HEAD_DIM), F32)],
        compiler_params=_params("arbitrary"),
        name="sample_front",
    )(x, norm_w.reshape(1, d), w, q_norm_w.reshape(1, HEAD_DIM), k_norm_w.reshape(1, HEAD_DIM), cos, sin)


def _layer(x, cos, sin, lw, nb, seq, layer, depth, kv_prev, sample, shift=None):
    (norm_w, w_in, q_norm_w, k_norm_w, w_glu, b_glu, w_br_attn, w_br_ssm, w_out, ssm_d, s5prm) = lw
    m, d = x.shape
    aw = d // 2
    qkv = N_DIL * aw
    hg = aw // HEAD_DIM

    new_kv = []
    if sample is None:
        xn = _rmsnorm(x, norm_w)
        g_attn = _proj(xn, w_in, 3 * qkv, aw, "silu", BF16)
        u = _proj(xn, w_in, 3 * qkv + aw, aw, "plain", F32)
        g_ssm = _proj(xn, w_in, 3 * qkv + 2 * aw, aw, "silu", BF16)
        gates = _proj(xn, w_in, 3 * qkv + 3 * aw, 2 * d, "sigmoid", BF16)
        qds = _q_all(xn, w_in, nb, seq, q_norm_w, cos, sin)
        outs, lses = [], []
        for g in range(N_DIL):
            kd, vd, kv = _kv(xn, w_in, g, nb, seq, k_norm_w, cos, sin, layer, depth,
                             None if kv_prev is None else kv_prev[g])
            o, lse = _attn_prompt(qds[g], kd, vd)
            outs.append(o)
            lses.append(lse)
            new_kv.append(kv)
        y, hre, him, shifted = _s5_prompt(u, s5prm, ssm_d, nb, seq, shift)
    else:
        caches, h0re, h0im = sample
        front = _sample_front(x, norm_w, w_in, q_norm_w, k_norm_w, cos, sin)
        q, k, v = front[:, :qkv], front[:, qkv:2 * qkv], front[:, 2 * qkv:3 * qkv]
        g_attn, u = front[:, 3 * qkv:3 * qkv + aw], front[:, 3 * qkv + aw:3 * qkv + 2 * aw]
        g_ssm, gates = front[:, 3 * qkv + 2 * aw:3 * qkv + 3 * aw], front[:, 3 * qkv + 3 * aw:]
        tok = (nb, seq, N_DIL * hg, HEAD_DIM)
        o, lse = _attn_sample(q.reshape(tok), k.reshape(tok), v.reshape(tok), caches, layer)
        o = o.reshape(m, qkv)
        lse = jnp.pad(lse[..., 0].reshape(m, N_DIL, hg), ((0, 0), (0, 0), (0, HEAD_DIM - hg)))
        outs = [o[:, g * aw:(g + 1) * aw].reshape(1, 1, m, aw) for g in range(N_DIL)]
        lses = [lse[:, g].reshape(1, 1, m, HEAD_DIM) for g in range(N_DIL)]
        ut = u.reshape(nb, seq, aw).transpose(1, 0, 2)
        yt, hre, him = _s5_sample(ut, h0re, h0im, s5prm, ssm_d)
        y = yt.transpose(1, 0, 2).reshape(m, aw)
        k5, v5 = k.reshape(nb, seq, N_DIL, hg, HEAD_DIM), v.reshape(nb, seq, N_DIL, hg, HEAD_DIM)
        new_kv = [jnp.stack([k5[:, :, g], v5[:, :, g]], axis=2) for g in range(N_DIL)]
        shifted = None

    y_ssm = _glu(y, g_ssm, w_glu, b_glu)
    merged = _merge(outs, lses, g_attn, y_ssm, gates, w_br_attn, w_br_ssm)
    x_out = _out_proj(x, merged, w_out)

    n_grp = aw // SSM_GROUP_CH
    state = jnp.stack([hre, him], axis=-1).reshape(nb, n_grp, SSM_STATE, 2)
    return x_out, new_kv, state, shifted


def kernel(x_prompt, x_sample, cache_kv_d1, cache_kv_d4, cache_kv_d16, state_ssm, norm_w, w_in, q_norm_w, k_norm_w, ssm_lambda_re, ssm_lambda_im, ssm_log_dt, ssm_b_re, ssm_b_im, ssm_c_re, ssm_c_im, ssm_d, w_glu, b_glu, w_br_attn, w_br_ssm, w_out):
    nb_p, seq_p, d = x_prompt.shape
    nb_s, seq_s, _ = x_sample.shape
    depth = norm_w.shape[0]
    caches = (cache_kv_d1, cache_kv_d4, cache_kv_d16)

    cos_p, sin_p = _rope_tables(jnp.arange(seq_p, dtype=F32))
    cos_s, sin_s = _rope_tables(PAST_LEN + jnp.arange(seq_s, dtype=F32))
    cos_s, sin_s = jnp.tile(cos_s, (nb_s, 1)), jnp.tile(sin_s, (nb_s, 1))

    hp = x_prompt.reshape(nb_p * seq_p, d)
    hs = x_sample.reshape(nb_s * seq_s, d)
    lws = []
    for l in range(depth):
        s5prm = _s5_params(ssm_lambda_re[l], ssm_lambda_im[l], ssm_log_dt[l], ssm_b_re[l], ssm_b_im[l],
                           ssm_c_re[l], ssm_c_im[l], _s5_chunk(seq_p) // ROW_TILE)
        lws.append((norm_w[l], _layer_bf16(w_in, l), q_norm_w[l], k_norm_w[l], _layer_bf16(w_glu, l), b_glu[l],
                    _layer_bf16(w_br_attn, l), _layer_bf16(w_br_ssm, l), _layer_bf16(w_out, l), ssm_d[l], s5prm))

    kv_s_new = [[] for _ in range(N_DIL)]
    ssm_s = []
    n_state = state_ssm.shape[2] * state_ssm.shape[3]
    for l in range(depth):
        h0re = state_ssm[l, ..., 0].reshape(nb_s, n_state)
        h0im = state_ssm[l, ..., 1].reshape(nb_s, n_state)
        hs, new_kv_s, st_s, _ = _layer(hs, cos_s, sin_s, lws[l], nb_s, seq_s, l, depth, None,
                                       (caches, h0re, h0im))
        for g in range(N_DIL):
            kv_s_new[g].append(new_kv_s[g])
        ssm_s.append(st_s)

    kv_p, kv_s, ssm_p = None, None, []
    for l in range(depth):
        shift = (caches, [jnp.stack(n) for n in kv_s_new]) if l == 0 else None
        hp, kv_p, st, shifted = _layer(hp, cos_p, sin_p, lws[l], nb_p, seq_p, l, depth, kv_p, None, shift)
        kv_s = shifted if shifted is not None else kv_s
        ssm_p.append(st)

    return (hp.reshape(x_prompt.shape), hs.reshape(x_sample.shape), kv_p[0], kv_p[1], kv_p[2], jnp.stack(ssm_p),
            kv_s[0], kv_s[1], kv_s[2], jnp.stack(ssm_s))
```

```python
import functools
import math

import jax
import jax.numpy as jnp
from jax import lax
from jax.experimental import pallas as pl
from jax.experimental.pallas import tpu as pltpu

F32 = jnp.float32
BF16 = jnp.bfloat16

HEAD_DIM = 128
WINDOWS = (128, 512, 2048)
DILATIONS = (1, 4, 16)
N_DIL = 3
SPAN = 128
SSM_GROUP_CH = 16
SSM_STATE = 64
GROUPS_PER_CHUNK = 8
ROPE_THETA = 10000.0
NORM_EPS = 1e-6
PAST_LEN = 16384
NEG_BIG = -1e30
VMEM_LIMIT = 56 * 1024 * 1024
ROW_TILE = 8
MXU_COLS = 256
QK_ROW_CHUNK = 64
KV_COPY_BYTES = 4 * 1024 * 1024


def _params(*sem):
    return pltpu.CompilerParams(dimension_semantics=sem, vmem_limit_bytes=VMEM_LIMIT)


def _sigmoid(x):
    return 1.0 / (1.0 + jnp.exp(-x))


def _row_tile(m, cap):
    t = min(m, cap)
    assert m % t == 0, (m, t)
    return t


def _rmsnorm_kernel(x_ref, w_ref, o_ref):
    x = x_ref[...]
    y = x * lax.rsqrt(jnp.mean(x * x, axis=-1, keepdims=True) + NORM_EPS)
    o_ref[...] = (y * w_ref[...]).astype(o_ref.dtype)


def _rmsnorm(x, w):
    m, d = x.shape
    tm = _row_tile(m, 512)
    return pl.pallas_call(
        _rmsnorm_kernel,
        grid=(m // tm,),
        in_specs=[pl.BlockSpec((tm, d), lambda i: (i, 0)),
                  pl.BlockSpec((1, d), lambda i: (0, 0))],
        out_specs=pl.BlockSpec((tm, d), lambda i: (i, 0)),
        out_shape=jax.ShapeDtypeStruct((m, d), BF16),
        compiler_params=_params("parallel"),
        name="rmsnorm",
    )(x, w.reshape(1, d))


def _cast_kernel(w_ref, o_ref):
    o_ref[...] = w_ref[...].astype(o_ref.dtype)


def _layer_bf16(w, layer):
    _, rows, cols = w.shape
    tr, tc = min(rows, 512), min(cols, 2048)
    assert rows % tr == 0 and cols % tc == 0
    return pl.pallas_call(
        _cast_kernel,
        grid=(rows // tr, cols // tc),
        in_specs=[pl.BlockSpec((None, tr, tc), lambda i, j: (layer, i, j))],
        out_specs=pl.BlockSpec((tr, tc), lambda i, j: (i, j)),
        out_shape=jax.ShapeDtypeStruct((rows, cols), BF16),
        compiler_params=_params("parallel", "parallel"),
        name="cast_bf16",
    )(w)


def _norm_rope_inplace(t_ref, nw_ref, cos_ref, sin_ref, head_major=False):
    rows = cos_ref.shape[0]
    n_heads = t_ref.shape[0] // rows if head_major else t_ref.shape[1] // HEAD_DIM
    rc = min(rows, QK_ROW_CHUNK)
    half = HEAD_DIM // 2
    nw = nw_ref[...]
    nw_swapped = pltpu.roll(nw, half, axis=1)

    def chunk(c, carry):
        r0 = pl.multiple_of(c * rc, rc)
        cs = cos_ref[pl.ds(r0, rc), :] * nw
        ss = sin_ref[pl.ds(r0, rc), :] * nw_swapped
        for h in range(n_heads):
            idx = (pl.ds(h * rows + r0, rc), slice(None)) if head_major else (
                pl.ds(r0, rc), slice(h * HEAD_DIM, (h + 1) * HEAD_DIM))
            a = t_ref[idx]
            r = lax.rsqrt(jnp.mean(a * a, axis=-1, keepdims=True) + NORM_EPS)
            t_ref[idx] = (a * cs + pltpu.roll(a, half, axis=1) * ss) * r
        return carry

    lax.fori_loop(0, rows // rc, chunk, 0)


def _proj_kernel(x_ref, w_ref, o_ref, *, epilogue):
    acc = jnp.dot(x_ref[...], w_ref[...], preferred_element_type=F32)
    if epilogue == "silu":
        acc = acc * _sigmoid(acc)
    elif epilogue == "sigmoid":
        acc = _sigmoid(acc)
    o_ref[...] = acc.astype(o_ref.dtype)


def _proj(xn, w, col0, ncols, epilogue, out_dtype):
    m, d = xn.shape
    tm = _row_tile(m, 1024)
    tn = min(ncols, 1024)
    assert ncols % tn == 0 and col0 % tn == 0
    jb = col0 // tn
    return pl.pallas_call(
        functools.partial(_proj_kernel, epilogue=epilogue),
        grid=(m // tm, ncols // tn),
        in_specs=[pl.BlockSpec((tm, d), lambda i, j: (i, 0)),
                  pl.BlockSpec((d, tn), lambda i, j: (0, jb + j))],
        out_specs=pl.BlockSpec((tm, tn), lambda i, j: (i, j)),
        out_shape=jax.ShapeDtypeStruct((m, ncols), out_dtype),
        compiler_params=_params("parallel", "arbitrary"),
        name="proj_" + epilogue,
    )(xn, w)


def _norm_rope_head(a, ssq, nw_ref, cos_ref, sin_ref, store):
    rows = a.shape[0]
    rc = min(rows, QK_ROW_CHUNK)
    half = HEAD_DIM // 2
    nw = nw_ref[...]
    nw_swapped = pltpu.roll(nw, half, axis=1)
    for r0 in range(0, rows, rc):
        ac = a[r0:r0 + rc]
        cs = cos_ref[r0:r0 + rc, :] * nw
        ss = sin_ref[r0:r0 + rc, :] * nw_swapped
        r = lax.rsqrt(ssq[r0:r0 + rc] * (1.0 / HEAD_DIM) + NORM_EPS)
        store(r0, rc, (ac * cs + pltpu.roll(ac, half, axis=1) * ss) * r)


def _project_dilated(x, w_ref, nw_ref, cos_ref, sin_ref, hm_ref, d_ref, dil):
    tm = x.shape[0]
    n_heads = w_ref.shape[1] // HEAD_DIM
    per = tm // dil
    acc = jnp.dot(x, w_ref[...], preferred_element_type=F32)
    ones = jnp.ones((HEAD_DIM, HEAD_DIM), BF16)
    for h in range(n_heads):
        sl = slice(h * HEAD_DIM, (h + 1) * HEAD_DIM)
        a = acc[:, sl]
        if nw_ref is None:
            hm_ref[h * tm:(h + 1) * tm, :] = a
        else:
            def store(r0, rc, chunk, h=h):
                hm_ref[h * tm + r0:h * tm + r0 + rc, :] = chunk

            ssq = jnp.dot((a * a).astype(BF16), ones, preferred_element_type=F32)
            _norm_rope_head(a, ssq, nw_ref, cos_ref, sin_ref, store)
        for r in range(dil):
            rows = pl.ds(h * tm + r, per, stride=dil) if dil > 1 else pl.ds(h * tm, per)
            d_ref[0, r, :, sl] = hm_ref[rows, :].astype(d_ref.dtype)


def _q_kernel(x_ref, w_ref, nw_ref, cos_ref, sin_ref, qd_ref, hm_ref, *, dil):
    _project_dilated(x_ref[...], w_ref, nw_ref, cos_ref, sin_ref, hm_ref, qd_ref, dil)


def _q(xn, w, g, nb, seq, q_norm_w, cos, sin):
    m, d = xn.shape
    aw = d // 2
    dil = DILATIONS[g]
    tm = _row_tile(seq, 1024)
    bps = seq // tm
    assert (tm // dil) % 16 == 0
    return pl.pallas_call(
        functools.partial(_q_kernel, dil=dil),
        grid=(nb, bps),
        in_specs=[pl.BlockSpec((tm, d), lambda b, i: (b * bps + i, 0)),
                  pl.BlockSpec((d, aw), lambda b, i: (0, g), pipeline_mode=pl.Buffered(1)),
                  pl.BlockSpec((1, HEAD_DIM), lambda b, i: (0, 0)),
                  pl.BlockSpec((tm, HEAD_DIM), lambda b, i: (i, 0)),
                  pl.BlockSpec((tm, HEAD_DIM), lambda b, i: (i, 0))],
        out_specs=pl.BlockSpec((1, dil, tm // dil, aw), lambda b, i: (b, 0, i, 0)),
        out_shape=jax.ShapeDtypeStruct((nb, dil, seq // dil, aw), BF16),
        scratch_shapes=[pltpu.VMEM((aw // HEAD_DIM * tm, HEAD_DIM), F32)],
        compiler_params=_params("parallel", "parallel"),
        name="q_d%d" % dil,
    )(xn, w, q_norm_w.reshape(1, HEAD_DIM), cos, sin)


def _kv_kernel(x_ref, wk_ref, wv_ref, nw_ref, cos_ref, sin_ref, kv_in_ref, kd_ref, vd_ref, kv_ref,
               hk_ref, hv_ref, rows_ref, sem, *, dil, layer, first_kept):
    b, i = pl.program_id(0), pl.program_id(1)
    tm = x_ref.shape[0]
    n_heads = wk_ref.shape[1] // HEAD_DIM
    kb = rows_ref.shape[0]
    x = x_ref[...]
    _project_dilated(x, wk_ref, nw_ref, cos_ref, sin_ref, hk_ref, kd_ref, dil)
    _project_dilated(x, wv_ref, None, None, None, hv_ref, vd_ref, dil)

    def copy_out(bb, ii):
        dst = kv_ref.at[layer, bb, pl.ds((ii - first_kept) * kb, kb)]
        return pltpu.make_async_copy(rows_ref, dst, sem.at[0])

    @pl.when(i >= first_kept)
    def _():
        @pl.when((i > first_kept) | (b > 0))
        def _():
            prev_same = i > first_kept
            copy_out(jnp.where(prev_same, b, b - 1),
                     jnp.where(prev_same, i - 1, pl.num_programs(1) - 1)).wait()

        def tokens(j, carry):
            t0 = pl.multiple_of(j * ROW_TILE, ROW_TILE)
            for which, hm_ref in ((0, hk_ref), (1, hv_ref)):
                tiles = [hm_ref[pl.ds(h * tm + tm - kb + t0, ROW_TILE), :] for h in range(n_heads)]
                rows_ref[pl.ds(t0, ROW_TILE), which] = jnp.swapaxes(jnp.stack(tiles, axis=0), 0, 1)
            return carry

        lax.fori_loop(0, kb // ROW_TILE, tokens, 0, unroll=2)
        copy_out(b, i).start()

    @pl.when((b == pl.num_programs(0) - 1) & (i == pl.num_programs(1) - 1))
    def _():
        copy_out(b, i).wait()


def _kv(xn, w, g, nb, seq, k_norm_w, cos, sin, layer, depth, kv_prev):
    m, d = xn.shape
    aw = d // 2
    hg = aw // HEAD_DIM
    dil = DILATIONS[g]
    keep = min(WINDOWS[g], seq)
    tm = _row_tile(seq, 1024)
    per = tm // dil
    kb = min(keep, tm)
    bps = seq // tm
    first_kept = (seq - keep) // tm if keep >= tm else bps - 1
    assert per % 16 == 0 and keep % kb == 0 and (seq - keep) % kb == 0 and kb % ROW_TILE == 0
    once = pl.Buffered(1)
    nw = lambda a: a.reshape(1, HEAD_DIM)
    dshape = jax.ShapeDtypeStruct((nb, dil, seq // dil, aw), BF16)
    staging = pltpu.VMEM((hg * tm, HEAD_DIM), F32)

    wspec = lambda col: pl.BlockSpec((d, aw), lambda b, i: (0, col), pipeline_mode=once)
    tspec = pl.BlockSpec((tm, HEAD_DIM), lambda b, i: (i, 0))
    dspec = pl.BlockSpec((1, dil, per, aw), lambda b, i: (b, 0, i, 0))
    hbm = pl.BlockSpec(memory_space=pl.ANY)
    kv_shape = jax.ShapeDtypeStruct((depth, nb, keep, 2, hg, HEAD_DIM), F32)
    if kv_prev is None:
        kv_prev = jnp.zeros(kv_shape.shape, kv_shape.dtype)
    kd, vd, kv = pl.pallas_call(
        functools.partial(_kv_kernel, dil=dil, layer=layer, first_kept=first_kept),
        grid=(nb, bps),
        in_specs=[pl.BlockSpec((tm, d), lambda b, i: (b * bps + i, 0)),
                  wspec(N_DIL + g), wspec(2 * N_DIL + g),
                  pl.BlockSpec((1, HEAD_DIM), lambda b, i: (0, 0)), tspec, tspec, hbm],
        out_specs=[dspec, dspec, hbm],
        out_shape=[dshape, dshape, kv_shape],
        scratch_shapes=[staging, staging, pltpu.VMEM((kb, 2, hg, HEAD_DIM), F32),
                        pltpu.SemaphoreType.DMA((1,))],
        input_output_aliases={6: 2},
        compiler_params=_params("arbitrary", "arbitrary"),
        name="kv_d%d" % dil,
    )(xn, w, w, nw(k_norm_w), cos, sin, kv_prev)
    return kd, vd, kv


def _rope_tables(pos):
    half = HEAD_DIM // 2
    inv_freq = jnp.power(ROPE_THETA, -jnp.arange(half, dtype=F32) * (2.0 / HEAD_DIM))
    ang = pos[:, None] * inv_freq[None, :]
    cos, sin = jnp.cos(ang), jnp.sin(ang)
    return jnp.concatenate([cos, cos], axis=-1), jnp.concatenate([-sin, sin], axis=-1)


def _attn_prompt_kernel(q_ref, kp_ref, kc_ref, vp_ref, vc_ref, o_ref, lse_ref):
    not_first = pl.program_id(2) > 0
    a = lax.broadcasted_iota(jnp.int32, (SPAN, 2 * SPAN), 0)
    c = lax.broadcasted_iota(jnp.int32, (SPAN, 2 * SPAN), 1)
    band = (c >= a) & (c <= a + SPAN)
    band_first = band & ((c >= SPAN) | not_first)
    lane = lax.broadcasted_iota(jnp.int32, (SPAN, HEAD_DIM), 1)
    scale = HEAD_DIM ** -0.5
    for j in range(q_ref.shape[2] // SPAN):
        rows = slice(j * SPAN, (j + 1) * SPAN)
        lse_tile = jnp.zeros((SPAN, HEAD_DIM), F32)
        for h in range(q_ref.shape[3] // HEAD_DIM):
            sl = slice(h * HEAD_DIM, (h + 1) * HEAD_DIM)
            q = q_ref[0, 0, rows, sl]
            if j == 0:
                k = jnp.concatenate([kp_ref[0, 0, :, sl], kc_ref[0, 0, rows, sl]], axis=0)
                v = jnp.concatenate([vp_ref[0, 0, :, sl], vc_ref[0, 0, rows, sl]], axis=0)
            else:
                k = kc_ref[0, 0, (j - 1) * SPAN:(j + 1) * SPAN, sl]
                v = vc_ref[0, 0, (j - 1) * SPAN:(j + 1) * SPAN, sl]
            s = lax.dot_general(q, k, (((1,), (1,)), ((), ())), preferred_element_type=F32) * scale
            s = jnp.where(band_first if j == 0 else band, s, NEG_BIG)
            m = jnp.max(s, axis=-1, keepdims=True)
            p = jnp.exp(s - m)
            l = jnp.sum(p, axis=-1, keepdims=True)
            o = jnp.dot(p.astype(BF16), v, preferred_element_type=F32) / l
            o_ref[0, 0, rows, sl] = o.astype(o_ref.dtype)
            lse_tile = jnp.where(lane == h, m + jnp.log(l), lse_tile)
        lse_ref[0, 0, rows, :] = lse_tile


def _attn_prompt(q, k, v):
    nb, dil, length, aw = q.shape
    n_blk = length // SPAN
    qb = min(4, n_blk)
    assert length % SPAN == 0 and n_blk % qb == 0 and aw // HEAD_DIM <= HEAD_DIM
    cur = pl.BlockSpec((1, 1, qb * SPAN, aw), lambda b, r, n: (b, r, n, 0))
    prev = pl.BlockSpec((1, 1, SPAN, aw), lambda b, r, n: (b, r, jnp.maximum(qb * n - 1, 0), 0))
    return pl.pallas_call(
        _attn_prompt_kernel,
        grid=(nb, dil, n_blk // qb),
        in_specs=[cur, prev, cur, prev, cur],
        out_specs=[cur, pl.BlockSpec((1, 1, qb * SPAN, HEAD_DIM), lambda b, r, n: (b, r, n, 0))],
        out_shape=[jax.ShapeDtypeStruct(q.shape, BF16),
                   jax.ShapeDtypeStruct((nb, dil, length, HEAD_DIM), F32)],
        compiler_params=_params("parallel", "parallel", "arbitrary"),
        name="attn_prompt_d%d" % dil,
    )(q, k, k, v, v)


def _attn_sample_kernel(q_ref, kn_ref, vn_ref, c1_ref, c4_ref, c16_ref, o_ref, lse_ref):
    n_new, n_heads, _ = q_ref.shape
    hg = n_heads // N_DIL
    scale = HEAD_DIM ** -0.5
    caches = (c1_ref, c4_ref, c16_ref)
    for g in range(N_DIL):
        hs = slice(g * hg, (g + 1) * hg)
        for t in range(n_new):
            q = q_ref[t, hs, :]
            if DILATIONS[g] == 1:
                kc, vc = caches[g][:, 0], caches[g][:, 1]
                rows = lax.broadcasted_iota(jnp.int32, (kc.shape[0], hg, 1), 0)
                s = jnp.sum(kc * q[None], axis=-1, keepdims=True) * scale
                s = jnp.where(rows >= t, s, NEG_BIG)
                new = range(t + 1)
            else:
                kc, vc = caches[g][:, t, 0], caches[g][:, t, 1]
                s = jnp.sum(kc * q[None], axis=-1, keepdims=True) * scale
                new = (t,)
            s_new = [jnp.sum(kn_ref[u, hs, :] * q, axis=-1, keepdims=True) * scale for u in new]
            m = jnp.max(s, axis=0)
            for sn in s_new:
                m = jnp.maximum(m, sn)
            p = jnp.exp(s - m[None])
            l = jnp.sum(p, axis=0)
            acc = jnp.sum(p * vc, axis=0)
            for u, sn in zip(new, s_new):
                pn = jnp.exp(sn - m)
                l = l + pn
                acc = acc + pn * vn_ref[u, hs, :]
            o_ref[t, hs, :] = acc / l
            lse_ref[t, hs, :] = jnp.broadcast_to(m + jnp.log(l), (hg, HEAD_DIM))


def _attn_sample(q, kn, vn, caches, layer):
    nb, n_new, n_heads, _ = q.shape
    hg = n_heads // N_DIL
    views, specs = [], []
    for g in range(N_DIL):
        dil = DILATIONS[g]
        depth, _, buf = caches[g].shape[:3]
        assert buf == WINDOWS[g] and (dil == 1 or dil % n_new == 0)
        if dil == 1:
            views.append(caches[g])
            specs.append(pl.BlockSpec((None, None, buf, 2, hg, HEAD_DIM),
                                      lambda b: (layer, b, 0, 0, 0, 0)))
        else:
            views.append(caches[g].reshape(depth, nb, SPAN, dil, 2, hg, HEAD_DIM))
            specs.append(pl.BlockSpec((None, None, SPAN, n_new, 2, hg, HEAD_DIM),
                                      lambda b: (layer, b, 0, 0, 0, 0, 0)))
    tok = pl.BlockSpec((None, n_new, n_heads, HEAD_DIM), lambda b: (b, 0, 0, 0))
    return pl.pallas_call(
        _attn_sample_kernel,
        grid=(nb,),
        in_specs=[tok, tok, tok] + specs,
        out_specs=[tok, tok],
        out_shape=[jax.ShapeDtypeStruct(q.shape, F32)] * 2,
        compiler_params=_params("parallel"),
        name="attn_sample",
    )(q, kn, vn, *views)


def _s5_params_kernel(lre_ref, lim_ref, ldt_ref, bre_ref, bim_ref, ptab_ref, bbre_ref, bbim_ref):
    dt = jnp.exp(ldt_ref[...])
    lre, lim = lre_ref[...], lim_ref[...]
    xr, xi = lre * dt, lim * dt
    mag = jnp.exp(xr)
    ar, ai = mag * jnp.cos(xi), mag * jnp.sin(xi)
    shape = (ROW_TILE, xr.shape[1])
    abr, abi = jnp.broadcast_to(ar, shape), jnp.broadcast_to(ai, shape)
    ptab_ref[0, 0] = abr
    ptab_ref[1, 0] = abi

    def next_power(i, carry):
        pr, pi = carry
        pr, pi = pr * abr - pi * abi, pr * abi + pi * abr
        ptab_ref[0, i] = pr
        ptab_ref[1, i] = pi
        return pr, pi

    lax.fori_loop(1, ptab_ref.shape[1], next_power, (abr, abi))
    nr, ni = ar - 1.0, ai
    den = lre * lre + lim * lim
    fr = (nr * lre + ni * lim) / den
    fi = (ni * lre - nr * lim) / den
    bre, bim = bre_ref[...], bim_ref[...]
    bbre_ref[...] = fr * bre - fi * bim
    bbim_ref[...] = fr * bim + fi * bre


def _s5_params(lam_re, lam_im, log_dt, b_re, b_im, c_re, c_im, seg_len):
    n_grp, n_st = lam_re.shape
    n_ch = b_re.shape[2]
    n = n_grp * n_st
    gc = GROUPS_PER_CHUNK
    n_chunk = n_grp // gc
    sw = gc * n_st
    to_lanes = lambda b: b.transpose(2, 0, 1).reshape(n_ch, n)
    lane = lambda rows: pl.BlockSpec((rows, sw), lambda j: (0, j))
    ptab, bb_re, bb_im = pl.pallas_call(
        _s5_params_kernel,
        grid=(n_chunk,),
        in_specs=[lane(1)] * 3 + [lane(n_ch)] * 2,
        out_specs=[pl.BlockSpec((2, seg_len, ROW_TILE, sw), lambda j: (0, 0, 0, j)), lane(n_ch), lane(n_ch)],
        out_shape=[jax.ShapeDtypeStruct((2, seg_len, ROW_TILE, n), F32),
                   jax.ShapeDtypeStruct((n_ch, n), F32), jax.ShapeDtypeStruct((n_ch, n), F32)],
        compiler_params=_params("parallel"),
        name="s5_params",
    )(lam_re.reshape(1, n), lam_im.reshape(1, n),
      jnp.broadcast_to(log_dt[:, None], (n_grp, n_st)).reshape(1, n), to_lanes(b_re), to_lanes(b_im))

    eye = jnp.eye(gc, dtype=F32)

    def b_blocks(bb):
        bb = bb.reshape(n_ch, n_chunk, gc, n_st)
        blk = jnp.einsum("cjgp,gh->jgchp", bb, eye)
        return blk.reshape(n_chunk, gc * n_ch, gc * n_st).astype(BF16)

    def c_blocks(cc):
        cc = cc.reshape(n_chunk, gc, n_ch, n_st)
        blk = jnp.einsum("jgcp,gh->jgphc", cc, eye)
        return blk.reshape(n_chunk, gc * n_st, gc * n_ch).astype(BF16)

    b_cat = jnp.concatenate([b_blocks(bb_re), b_blocks(bb_im)], axis=2)
    c_cat = jnp.concatenate([c_blocks(c_re), c_blocks(c_im)], axis=1)
    return ptab, b_cat, c_cat


def _cmul_add(xr, xi, ar, ai, br, bi):
    return xr + ar * br - ai * bi, xi + ar * bi + ai * br


def _shift_plan(cache_shape, new_shape, n_steps):
    lb_n, buf = cache_shape[:2]
    keep = buf - new_shape[1]
    row_bytes = 4 * math.prod(cache_shape[2:])
    for per_row in range(1, keep + 1):
        n_jobs = lb_n * per_row
        if keep % per_row == 0 and keep // per_row * row_bytes <= KV_COPY_BYTES and n_steps % n_jobs == 0:
            return keep // per_row, per_row, n_steps // n_jobs
    raise ValueError("window shift does not fit the grid")


def _shift_lag(period):
    return 1 if period == 2 else 2


def _shift_slots(period):
    return 4 if period == 1 else 2


def _window_shift_step(step, n_steps, caches, news, outs, rings, nbufs, in_sem, out_sem, new_sem):
    last = step == n_steps - 1
    for g in range(N_DIL):
        cache, out, ring = caches[g], outs[g], rings[g]
        n_new = news[g].shape[1]
        keep = cache.shape[1] - n_new
        rows, per_row, period = _shift_plan(cache.shape, news[g].shape, n_steps)
        n_jobs = cache.shape[0] * per_row
        slots = ring.shape[0]
        lag = _shift_lag(period)
        assert n_jobs >= slots and (period == 1 or lag < period)

        def load(k, cache=cache, ring=ring, g=g, rows=rows, per_row=per_row, n_new=n_new, slots=slots):
            src = cache.at[lax.div(k, per_row), pl.ds(n_new + lax.rem(k, per_row) * rows, rows)]
            return pltpu.make_async_copy(src, ring.at[lax.rem(k, slots)], in_sem.at[g, lax.rem(k, slots)])

        def store(k, out=out, ring=ring, g=g, rows=rows, per_row=per_row, slots=slots):
            dst = out.at[lax.div(k, per_row), pl.ds(lax.rem(k, per_row) * rows, rows)]
            return pltpu.make_async_copy(ring.at[lax.rem(k, slots)], dst, out_sem.at[g, lax.rem(k, slots)])

        def issue(k, load=load, store=store, slots=slots):
            @pl.when(k >= slots)
            def _():
                store(k - slots).wait()
            load(k).start()

        def forward(k, load=load, store=store):
            load(k).wait()
            store(k).start()

        def drain(k, store=store, slots=slots):
            for back in range(slots - 1, -1, -1):
                store(k - back).wait()

        if period == 1:
            issue(step)

            @pl.when(step >= lag)
            def _(forward=forward, lag=lag):
                forward(step - lag)

            @pl.when(last)
            def _(forward=forward, drain=drain, lag=lag):
                for back in range(lag - 1, -1, -1):
                    forward(step - back)
                drain(step)
        else:
            k = lax.div(step, period)
            phase = lax.rem(step, period)

            @pl.when(phase == 0)
            def _(issue=issue, k=k):
                issue(k)

            @pl.when(phase == lag)
            def _(forward=forward, k=k):
                forward(k)

            @pl.when(last)
            def _(drain=drain, k=k):
                drain(k)

        new_in = pltpu.make_async_copy(news[g], nbufs[g], new_sem.at[g])
        new_out = pltpu.make_async_copy(nbufs[g], out.at[:, pl.ds(keep, n_new)], new_sem.at[N_DIL + g])

        @pl.when(step == 0)
        def _(new_in=new_in, new_out=new_out):
            new_in.start()
            new_in.wait()
            new_out.start()

        @pl.when(last)
        def _(new_out=new_out):
            new_out.wait()


def _s5_scan_kernel(u_ref, b_ref, ptab_ref, c_ref, d_ref, *rest, with_shift):
    if with_shift:
        n = N_DIL
        caches, news = rest[:n], rest[n:2 * n]
        y_ref, hfin_ref = rest[2 * n:2 * n + 2]
        outs = rest[2 * n + 2:3 * n + 2]
        up_ref, s_ref, hb_ref, end_ref = rest[3 * n + 2:3 * n + 6]
        rings, nbufs = rest[3 * n + 6:4 * n + 6], rest[4 * n + 6:5 * n + 6]
        in_sem, out_sem, new_sem = rest[5 * n + 6:]
        _, n_j, n_t = with_shift
        step = (pl.program_id(0) * n_j + pl.program_id(1)) * n_t + pl.program_id(2)
        _window_shift_step(step, math.prod(with_shift), caches, news, outs, rings, nbufs,
                           in_sem, out_sem, new_sem)
    else:
        y_ref, hfin_ref, up_ref, s_ref, hb_ref, end_ref = rest
    tc, sw2 = s_ref.shape
    sw = sw2 // 2
    seg = tc // ROW_TILE
    re, im = slice(0, sw), slice(sw, sw2)

    @pl.when(pl.program_id(2) == 0)
    def _():
        end_ref[...] = jnp.zeros_like(end_ref)

    for i in range(seg):
        up_ref[i * ROW_TILE:(i + 1) * ROW_TILE, :] = u_ref[0, pl.ds(i, ROW_TILE, stride=seg), :]
    s_ref[...] = jnp.dot(up_ref[...].astype(BF16), b_ref[0], preferred_element_type=F32)

    ar, ai = ptab_ref[0, 0], ptab_ref[1, 0]

    def local_step(i, carry):
        hr, hi = carry
        r0 = pl.multiple_of(i * ROW_TILE, ROW_TILE)
        hr, hi = _cmul_add(s_ref[pl.ds(r0, ROW_TILE), re], s_ref[pl.ds(r0, ROW_TILE), im], ar, ai, hr, hi)
        s_ref[pl.ds(r0, ROW_TILE), re] = hr
        s_ref[pl.ds(r0, ROW_TILE), im] = hi
        return hr, hi

    zero = jnp.zeros((ROW_TILE, sw), F32)
    er, ei = lax.fori_loop(0, seg, local_step, (zero, zero), unroll=8)

    row = lax.broadcasted_iota(jnp.int32, (ROW_TILE, sw), 0)
    cr = jnp.where(row == 0, pltpu.roll(end_ref[0], 1, axis=0), pltpu.roll(er, 1, axis=0))
    ci = jnp.where(row == 0, pltpu.roll(end_ref[1], 1, axis=0), pltpu.roll(ei, 1, axis=0))
    mr, mi = ptab_ref[0, seg - 1], ptab_ref[1, seg - 1]
    wr, wi = mr, mi
    for sh in (1, 2, 4):
        gr, gi = jnp.where(row >= sh, wr, 0.0), jnp.where(row >= sh, wi, 0.0)
        cr, ci = _cmul_add(cr, ci, gr, gi, pltpu.roll(cr, sh, axis=0), pltpu.roll(ci, sh, axis=0))
        wr, wi = wr * wr - wi * wi, 2.0 * wr * wi
    fr, fi = _cmul_add(er, ei, mr, mi, cr, ci)
    end_ref[0] = fr
    end_ref[1] = fi
    hfin_ref[0, 0] = fr
    hfin_ref[0, 1] = fi

    def fix_up(k, carry):
        r0 = pl.multiple_of(k * 2 * ROW_TILE, 2 * ROW_TILE)
        hs, ns = [], []
        for half in range(2):
            i = 2 * k + half
            rows = pl.ds(r0 + half * ROW_TILE, ROW_TILE)
            hr, hi = _cmul_add(s_ref[rows, re], s_ref[rows, im], ptab_ref[0, i], ptab_ref[1, i], cr, ci)
            hs.append(hr)
            ns.append(-hi)
        hb_ref[pl.ds(r0, 2 * ROW_TILE), re] = jnp.concatenate(hs, axis=0).astype(BF16)
        hb_ref[pl.ds(r0, 2 * ROW_TILE), im] = jnp.concatenate(ns, axis=0).astype(BF16)
        return carry

    lax.fori_loop(0, seg // 2, fix_up, 0, unroll=4)
    yp = jnp.dot(hb_ref[...], c_ref[0], preferred_element_type=F32) + d_ref[...] * up_ref[...]
    for i in range(seg):
        y_ref[0, pl.ds(i, ROW_TILE, stride=seg), :] = yp[i * ROW_TILE:(i + 1) * ROW_TILE, :]


def _s5_chunk(seq):
    return min(seq, 1024)


def _flat_window(a):
    return a.reshape(a.shape[0] * a.shape[1], a.shape[2], a.shape[3] * a.shape[4], a.shape[5])


def _s5_prompt(u, prm, ssm_d, nb, seq, shift=None):
    ptab, b_cat, c_cat = prm
    n_chunk, cw, sw2 = b_cat.shape
    sw = sw2 // 2
    width = u.shape[1]
    tc = _s5_chunk(seq)
    seg = tc // ROW_TILE
    assert seq % tc == 0 and seg % 2 == 0 and ptab.shape[1] == seg
    grid = (nb, n_chunk, seq // tc)
    in_specs = [pl.BlockSpec((1, tc, cw), lambda b, j, t: (b, t, j)),
                pl.BlockSpec((1, cw, sw2), lambda b, j, t: (j, 0, 0)),
                pl.BlockSpec((2, seg, ROW_TILE, sw), lambda b, j, t: (0, 0, 0, j)),
                pl.BlockSpec((1, sw2, cw), lambda b, j, t: (j, 0, 0)),
                pl.BlockSpec((1, cw), lambda b, j, t: (0, j))]
    out_specs = [pl.BlockSpec((1, tc, cw), lambda b, j, t: (b, t, j)),
                 pl.BlockSpec((1, 2, ROW_TILE, sw), lambda b, j, t: (b, 0, 0, j))]
    out_shape = [jax.ShapeDtypeStruct((nb, seq, width), F32),
                 jax.ShapeDtypeStruct((nb, 2, ROW_TILE, n_chunk * sw), F32)]
    scratch = [pltpu.VMEM((tc, cw), F32), pltpu.VMEM((tc, sw2), F32),
               pltpu.VMEM((tc, sw2), BF16), pltpu.VMEM((2, ROW_TILE, sw), F32)]
    args = [u.reshape(nb, seq, width), b_cat, ptab, c_cat, ssm_d.reshape(1, width)]
    if shift is not None:
        caches, news = shift
        cf, nf = [_flat_window(c) for c in caches], [_flat_window(w) for w in news]
        hbm = pl.BlockSpec(memory_space=pl.ANY)
        in_specs += [hbm] * (2 * N_DIL)
        out_specs += [hbm] * N_DIL
        out_shape += [jax.ShapeDtypeStruct(c.shape, c.dtype) for c in cf]
        args += cf + nf
        max_slots = 0
        for c, w in zip(cf, nf):
            rows, _, period = _shift_plan(c.shape, w.shape, math.prod(grid))
            scratch.append(pltpu.VMEM((_shift_slots(period), rows) + c.shape[2:], c.dtype))
            max_slots = max(max_slots, _shift_slots(period))
        scratch += [pltpu.VMEM(w.shape, w.dtype) for w in nf]
        scratch += [pltpu.SemaphoreType.DMA((N_DIL, max_slots)), pltpu.SemaphoreType.DMA((N_DIL, max_slots)),
                    pltpu.SemaphoreType.DMA((2 * N_DIL,))]
    res = pl.pallas_call(
        functools.partial(_s5_scan_kernel, with_shift=grid if shift is not None else None),
        grid=grid,
        in_specs=in_specs,
        out_specs=out_specs,
        out_shape=out_shape,
        scratch_shapes=scratch,
        compiler_params=_params("arbitrary", "arbitrary", "arbitrary") if shift is not None
        else _params("parallel", "parallel", "arbitrary"),
        name="s5_scan",
    )(*args)
    y, hfin = res[0], res[1]
    last = ROW_TILE - 1
    shifted = None if shift is None else [o.reshape(c.shape) for o, c in zip(res[2:], shift[0])]
    return y.reshape(nb * seq, width), hfin[:, 0, last], hfin[:, 1, last], shifted


def _s5_sample_kernel(u_ref, h0re_ref, h0im_ref, b_ref, ptab_ref, c_ref, d_ref, y_ref, hre_ref, him_ref):
    sw = h0re_ref.shape[1]
    ar, ai = ptab_ref[0, 0, 0:1, :], ptab_ref[1, 0, 0:1, :]
    hr, hi = h0re_ref[...], h0im_ref[...]
    for t in range(u_ref.shape[0]):
        u = u_ref[t]
        bu = jnp.dot(u.astype(BF16), b_ref[0], preferred_element_type=F32)
        hr, hi = _cmul_add(bu[:, :sw], bu[:, sw:], ar, ai, hr, hi)
        hb = jnp.concatenate([hr, -hi], axis=1).astype(BF16)
        y_ref[t] = jnp.dot(hb, c_ref[0], preferred_element_type=F32) + d_ref[...] * u
    hre_ref[...] = hr
    him_ref[...] = hi


def _s5_sample(u, h0re, h0im, prm, ssm_d):
    ptab, b_cat, c_cat = prm
    n_chunk, cw, sw2 = b_cat.shape
    sw = sw2 // 2
    n_new, nb, width = u.shape
    return pl.pallas_call(
        _s5_sample_kernel,
        grid=(n_chunk,),
        in_specs=[pl.BlockSpec((n_new, nb, cw), lambda j: (0, 0, j)),
                  pl.BlockSpec((nb, sw), lambda j: (0, j)),
                  pl.BlockSpec((nb, sw), lambda j: (0, j)),
                  pl.BlockSpec((1, cw, sw2), lambda j: (j, 0, 0)),
                  pl.BlockSpec((2, 1, ROW_TILE, sw), lambda j: (0, 0, 0, j)),
                  pl.BlockSpec((1, sw2, cw), lambda j: (j, 0, 0)),
                  pl.BlockSpec((1, cw), lambda j: (0, j))],
        out_specs=[pl.BlockSpec((n_new, nb, cw), lambda j: (0, 0, j)),
                   pl.BlockSpec((nb, sw), lambda j: (0, j)),
                   pl.BlockSpec((nb, sw), lambda j: (0, j))],
        out_shape=[jax.ShapeDtypeStruct((n_new, nb, width), F32),
                   jax.ShapeDtypeStruct((nb, n_chunk * sw), F32),
                   jax.ShapeDtypeStruct((nb, n_chunk * sw), F32)],
        compiler_params=_params("parallel"),
        name="s5_sample",
    )(u, h0re, h0im, b_cat, ptab, c_cat, ssm_d.reshape(1, width))


def _glu_kernel(y_ref, g_ref, w_ref, b_ref, o_ref):
    y = y_ref[...]
    s = 0.5 * y * (1.0 + jnp.tanh(math.sqrt(2.0 / math.pi) * (y + 0.044715 * (y * y * y))))
    z = jnp.dot(s.astype(BF16), w_ref[...], preferred_element_type=F32) + b_ref[...]
    o_ref[...] = (s * _sigmoid(z) * g_ref[...].astype(F32)).astype(o_ref.dtype)


def _glu(y, gate, w, b):
    m, width = y.shape
    tm = _row_tile(m, 1024)
    row = pl.BlockSpec((tm, width), lambda i: (i, 0))
    return pl.pallas_call(
        _glu_kernel,
        grid=(m // tm,),
        in_specs=[row, row, pl.BlockSpec((width, width), lambda i: (0, 0)),
                  pl.BlockSpec((1, width), lambda i: (0, 0))],
        out_specs=row,
        out_shape=jax.ShapeDtypeStruct((m, width), BF16),
        compiler_params=_params("parallel"),
        name="glu",
    )(y, gate, w, b.reshape(1, width))


def _merge_kernel(*refs):
    o_refs, l_refs = refs[:N_DIL], refs[N_DIL:2 * N_DIL]
    ga_ref, ys_ref, ma_ref, ms_ref, wa_ref, ws_ref, out_ref = refs[2 * N_DIL:2 * N_DIL + 7]
    a_ref, otok_ref, ltok_ref = refs[2 * N_DIL + 7:]

    def token_order(ref, cols, stage):
        dil, per = ref.shape[1], ref.shape[2]
        if dil == 1:
            return ref[0, 0, :, cols].astype(F32)
        for r in range(dil):
            stage[pl.ds(r, per, stride=dil), :] = ref[0, r, :, cols].astype(F32)
        return stage[...]

    ls = [token_order(l_refs[g], slice(None), ltok_ref.at[g]) for g in range(N_DIL)]
    mx = jnp.maximum(jnp.maximum(ls[0], ls[1]), ls[2])
    es = [jnp.exp(l - mx) for l in ls]
    den = es[0] + es[1] + es[2]
    wts = [e / den for e in es]
    tm, aw = a_ref.shape
    half = out_ref.shape[1] // 2
    halves = [slice(c * half, (c + 1) * half) for c in range(2)]
    ys = ys_ref[...]
    acc = [ms_ref[:, cs].astype(F32) * jnp.dot(ys, ws_ref[:, cs], preferred_element_type=F32) for cs in halves]
    group = min(aw, MXU_COLS) // HEAD_DIM
    ya = [None, None]
    for h0 in range(0, aw // HEAD_DIM, group):
        for h in range(h0, h0 + group):
            sl = slice(h * HEAD_DIM, (h + 1) * HEAD_DIM)
            attn = None
            for g in range(N_DIL):
                term = (jnp.broadcast_to(wts[g][:, h:h + 1], (tm, HEAD_DIM))
                        * token_order(o_refs[g], sl, otok_ref.at[g]))
                attn = term if attn is None else attn + term
            a_ref[:, sl] = (attn * ga_ref[:, sl].astype(F32)).astype(BF16)
        ks = slice(h0 * HEAD_DIM, (h0 + group) * HEAD_DIM)
        for c, cs in enumerate(halves):
            part = jnp.dot(a_ref[:, ks], wa_ref[ks, cs], preferred_element_type=F32)
            ya[c] = part if ya[c] is None else ya[c] + part
    for c, cs in enumerate(halves):
        out_ref[:, cs] = (ma_ref[:, cs].astype(F32) * ya[c] + acc[c]).astype(out_ref.dtype)


def _merge(outs, lses, g_attn, y_ssm, gates, w_a, w_s):
    m, aw = g_attn.shape
    d = w_a.shape[1]
    nb = outs[0].shape[0]
    seq = m // nb
    tm = _row_tile(seq, 512)
    bps = seq // tm
    row = pl.BlockSpec((tm, aw), lambda i: (i, 0))
    ol_specs = []
    for arr in list(outs) + list(lses):
        dil = arr.shape[1]
        assert tm % (dil * ROW_TILE) == 0
        ol_specs.append(pl.BlockSpec((1, dil, tm // dil, arr.shape[3]),
                                     lambda i: (lax.div(i, bps), 0, lax.rem(i, bps), 0)))
    scratch = [pltpu.VMEM((tm, aw), BF16), pltpu.VMEM((N_DIL, tm, HEAD_DIM), F32),
               pltpu.VMEM((N_DIL, tm, HEAD_DIM), F32)]
    once = pl.Buffered(1)
    return pl.pallas_call(
        _merge_kernel,
        grid=(m // tm,),
        in_specs=ol_specs + [row, row, pl.BlockSpec((tm, d), lambda i: (i, 0)),
                             pl.BlockSpec((tm, d), lambda i: (i, 1)),
                             pl.BlockSpec((aw, d), lambda i: (0, 0), pipeline_mode=once),
                             pl.BlockSpec((aw, d), lambda i: (0, 0), pipeline_mode=once)],
        out_specs=pl.BlockSpec((tm, d), lambda i: (i, 0)),
        out_shape=jax.ShapeDtypeStruct((m, d), BF16),
        scratch_shapes=scratch,
        compiler_params=_params("parallel"),
        name="merge",
    )(*outs, *lses, g_attn, y_ssm, gates, gates, w_a, w_s)


def _out_kernel(x_ref, m_ref, w_ref, o_ref):
    o_ref[...] = x_ref[...] + jnp.dot(m_ref[...], w_ref[...], preferred_element_type=F32)


def _out_proj(x, merged, w):
    m, d = x.shape
    tm = _row_tile(m, 1024)
    tn = min(d, 1024)
    return pl.pallas_call(
        _out_kernel,
        grid=(d // tn, m // tm),
        in_specs=[pl.BlockSpec((tm, tn), lambda j, i: (i, j)),
                  pl.BlockSpec((tm, d), lambda j, i: (i, 0)),
                  pl.BlockSpec((d, tn), lambda j, i: (0, j))],
        out_specs=pl.BlockSpec((tm, tn), lambda j, i: (i, j)),
        out_shape=jax.ShapeDtypeStruct((m, d), F32),
        compiler_params=_params("parallel", "parallel"),
        name="out_proj",
    )(x, merged, w)


def _sample_front_kernel(x_ref, gw_ref, w_ref, nwq_ref, nwk_ref, cos_ref, sin_ref, o_ref, nw_ref):
    j = pl.program_id(0)
    x = x_ref[...]
    xn = x * lax.rsqrt(jnp.mean(x * x, axis=-1, keepdims=True) + NORM_EPS) * gw_ref[...]
    acc = jnp.dot(xn.astype(BF16), w_ref[...], preferred_element_type=F32)
    qk, v_end = 2 * N_DIL, 3 * N_DIL

    @pl.when(j < qk)
    def _():
        nw_ref[...] = jnp.where(j < N_DIL, nwq_ref[...], nwk_ref[...])
        o_ref[...] = acc
        _norm_rope_inplace(o_ref, nw_ref, cos_ref, sin_ref)

    @pl.when(((j >= qk) & (j < v_end)) | (j == v_end + 1))
    def _():
        o_ref[...] = acc

    @pl.when((j == v_end) | (j == v_end + 2))
    def _():
        o_ref[...] = acc * _sigmoid(acc)

    @pl.when(j > v_end + 2)
    def _():
        o_ref[...] = _sigmoid(acc)


def _sample_front(x, norm_w, w, q_norm_w, k_norm_w, cos, sin):
    m, d = x.shape
    aw = d // 2
    n_cols = w.shape[1]
    row = pl.BlockSpec((m, HEAD_DIM), lambda j: (0, 0))
    one = pl.BlockSpec((1, HEAD_DIM), lambda j: (0, 0))
    return pl.pallas_call(
        _sample_front_kernel,
        grid=(n_cols // aw,),
        in_specs=[pl.BlockSpec((m, d), lambda j: (0, 0)), pl.BlockSpec((1, d), lambda j: (0, 0)),
                  pl.BlockSpec((d, aw), lambda j: (0, j)), one, one, row, row],
        out_specs=pl.BlockSpec((m, aw), lambda j: (0, j)),
        out_shape=jax.ShapeDtypeStruct((m, n_cols), F32),
        scratch_shapes=[pltpu.VMEM((1, HEAD_DIM), F32)],
        compiler_params=_params("arbitrary"),
        name="sample_front",
    )(x, norm_w.reshape(1, d), w, q_norm_w.reshape(1, HEAD_DIM), k_norm_w.reshape(1, HEAD_DIM), cos, sin)


def _layer(x, cos, sin, lw, nb, seq, layer, depth, kv_prev, sample, shift=None):
    (norm_w, w_in, q_norm_w, k_norm_w, w_glu, b_glu, w_br_attn, w_br_ssm, w_out, ssm_d, s5prm) = lw
    m, d = x.shape
    aw = d // 2
    qkv = N_DIL * aw
    hg = aw // HEAD_DIM

    new_kv = []
    if sample is None:
        xn = _rmsnorm(x, norm_w)
        g_attn = _proj(xn, w_in, 3 * qkv, aw, "silu", BF16)
        u = _proj(xn, w_in, 3 * qkv + aw, aw, "plain", F32)
        g_ssm = _proj(xn, w_in, 3 * qkv + 2 * aw, aw, "silu", BF16)
        gates = _proj(xn, w_in, 3 * qkv + 3 * aw, 2 * d, "sigmoid", BF16)
        qds = [_q(xn, w_in, g, nb, seq, q_norm_w, cos, sin) for g in range(N_DIL)]
        outs, lses = [], []
        for g in range(N_DIL):
            kd, vd, kv = _kv(xn, w_in, g, nb, seq, k_norm_w, cos, sin, layer, depth,
                             None if kv_prev is None else kv_prev[g])
            o, lse = _attn_prompt(qds[g], kd, vd)
            outs.append(o)
            lses.append(lse)
            new_kv.append(kv)
        y, hre, him, shifted = _s5_prompt(u, s5prm, ssm_d, nb, seq, shift)
    else:
        caches, h0re, h0im = sample
        front = _sample_front(x, norm_w, w_in, q_norm_w, k_norm_w, cos, sin)
        q, k, v = front[:, :qkv], front[:, qkv:2 * qkv], front[:, 2 * qkv:3 * qkv]
        g_attn, u = front[:, 3 * qkv:3 * qkv + aw], front[:, 3 * qkv + aw:3 * qkv + 2 * aw]
        g_ssm, gates = front[:, 3 * qkv + 2 * aw:3 * qkv + 3 * aw], front[:, 3 * qkv + 3 * aw:]
        tok = (nb, seq, N_DIL * hg, HEAD_DIM)
        o, lse = _attn_sample(q.reshape(tok), k.reshape(tok), v.reshape(tok), caches, layer)
        o = o.reshape(m, qkv)
        lse = jnp.pad(lse[..., 0].reshape(m, N_DIL, hg), ((0, 0), (0, 0), (0, HEAD_DIM - hg)))
        outs = [o[:, g * aw:(g + 1) * aw].reshape(1, 1, m, aw) for g in range(N_DIL)]
        lses = [lse[:, g].reshape(1, 1, m, HEAD_DIM) for g in range(N_DIL)]
        ut = u.reshape(nb, seq, aw).transpose(1, 0, 2)
        yt, hre, him = _s5_sample(ut, h0re, h0im, s5prm, ssm_d)
        y = yt.transpose(1, 0, 2).reshape(m, aw)
        k5, v5 = k.reshape(nb, seq, N_DIL, hg, HEAD_DIM), v.reshape(nb, seq, N_DIL, hg, HEAD_DIM)
        new_kv = [jnp.stack([k5[:, :, g], v5[:, :, g]], axis=2) for g in range(N_DIL)]
        shifted = None

    y_ssm = _glu(y, g_ssm, w_glu, b_glu)
    merged = _merge(outs, lses, g_attn, y_ssm, gates, w_br_attn, w_br_ssm)
    x_out = _out_proj(x, merged, w_out)

    n_grp = aw // SSM_GROUP_CH
    state = jnp.stack([hre, him], axis=-1).reshape(nb, n_grp, SSM_STATE, 2)
    return x_out, new_kv, state, shifted


def kernel(x_prompt, x_sample, cache_kv_d1, cache_kv_d4, cache_kv_d16, state_ssm, norm_w, w_in, q_norm_w, k_norm_w, ssm_lambda_re, ssm_lambda_im, ssm_log_dt, ssm_b_re, ssm_b_im, ssm_c_re, ssm_c_im, ssm_d, w_glu, b_glu, w_br_attn, w_br_ssm, w_out):
    nb_p, seq_p, d = x_prompt.shape
    nb_s, seq_s, _ = x_sample.shape
    depth = norm_w.shape[0]
    caches = (cache_kv_d1, cache_kv_d4, cache_kv_d16)

    cos_p, sin_p = _rope_tables(jnp.arange(seq_p, dtype=F32))
    cos_s, sin_s = _rope_tables(PAST_LEN + jnp.arange(seq_s, dtype=F32))
    cos_s, sin_s = jnp.tile(cos_s, (nb_s, 1)), jnp.tile(sin_s, (nb_s, 1))

    hp = x_prompt.reshape(nb_p * seq_p, d)
    hs = x_sample.reshape(nb_s * seq_s, d)
    lws = []
    for l in range(depth):
        s5prm = _s5_params(ssm_lambda_re[l], ssm_lambda_im[l], ssm_log_dt[l], ssm_b_re[l], ssm_b_im[l],
                           ssm_c_re[l], ssm_c_im[l], _s5_chunk(seq_p) // ROW_TILE)
        lws.append((norm_w[l], _layer_bf16(w_in, l), q_norm_w[l], k_norm_w[l], _layer_bf16(w_glu, l), b_glu[l],
                    _layer_bf16(w_br_attn, l), _layer_bf16(w_br_ssm, l), _layer_bf16(w_out, l), ssm_d[l], s5prm))

    kv_s_new = [[] for _ in range(N_DIL)]
    ssm_s = []
    n_state = state_ssm.shape[2] * state_ssm.shape[3]
    for l in range(depth):
        h0re = state_ssm[l, ..., 0].reshape(nb_s, n_state)
        h0im = state_ssm[l, ..., 1].reshape(nb_s, n_state)
        hs, new_kv_s, st_s, _ = _layer(hs, cos_s, sin_s, lws[l], nb_s, seq_s, l, depth, None,
                                       (caches, h0re, h0im))
        for g in range(N_DIL):
            kv_s_new[g].append(new_kv_s[g])
        ssm_s.append(st_s)

    kv_p, kv_s, ssm_p = None, None, []
    for l in range(depth):
        shift = (caches, [jnp.stack(n) for n in kv_s_new]) if l == 0 else None
        hp, kv_p, st, shifted = _layer(hp, cos_p, sin_p, lws[l], nb_p, seq_p, l, depth, kv_p, None, shift)
        kv_s = shifted if shifted is not None else kv_s
        ssm_p.append(st)

    return (hp.reshape(x_prompt.shape), hs.reshape(x_sample.shape), kv_p[0], kv_p[1], kv_p[2], jnp.stack(ssm_p),
            kv_s[0], kv_s[1], kv_s[2], jnp.stack(ssm_s))
```

```python
import functools
import math

import jax
import jax.numpy as jnp
from jax import lax
from jax.experimental import pallas as pl
from jax.experimental.pallas import tpu as pltpu

F32 = jnp.float32
BF16 = jnp.bfloat16

HEAD_DIM = 128
WINDOWS = (128, 512, 2048)
DILATIONS = (1, 4, 16)
N_DIL = 3
SPAN = 128
SSM_GROUP_CH = 16
SSM_STATE = 64
GROUPS_PER_CHUNK = 8
ROPE_THETA = 10000.0
NORM_EPS = 1e-6
PAST_LEN = 16384
NEG_BIG = -1e30
VMEM_LIMIT = 58 * 1024 * 1024
KV_ROWS_PER_COPY = 512
ROW_TILE = 8
MXU_COLS = 256
QK_ROW_CHUNK = 64
KV_COPY_BYTES = 4 * 1024 * 1024


def _params(*sem):
    return pltpu.CompilerParams(dimension_semantics=sem, vmem_limit_bytes=VMEM_LIMIT)


def _sigmoid(x):
    return 1.0 / (1.0 + jnp.exp(-x))


def _row_tile(m, cap):
    t = min(m, cap)
    assert m % t == 0, (m, t)
    return t


def _rmsnorm_kernel(x_ref, w_ref, o_ref):
    x = x_ref[...]
    y = x * lax.rsqrt(jnp.mean(x * x, axis=-1, keepdims=True) + NORM_EPS)
    o_ref[...] = (y * w_ref[...]).astype(o_ref.dtype)


def _rmsnorm(x, w):
    m, d = x.shape
    tm = _row_tile(m, 512)
    return pl.pallas_call(
        _rmsnorm_kernel,
        grid=(m // tm,),
        in_specs=[pl.BlockSpec((tm, d), lambda i: (i, 0)),
                  pl.BlockSpec((1, d), lambda i: (0, 0))],
        out_specs=pl.BlockSpec((tm, d), lambda i: (i, 0)),
        out_shape=jax.ShapeDtypeStruct((m, d), BF16),
        compiler_params=_params("parallel"),
        name="rmsnorm",
    )(x, w.reshape(1, d))


def _cast_kernel(w_ref, o_ref):
    o_ref[...] = w_ref[...].astype(o_ref.dtype)


def _layer_bf16(w, layer):
    _, rows, cols = w.shape
    tr, tc = min(rows, 512), min(cols, 2048)
    assert rows % tr == 0 and cols % tc == 0
    return pl.pallas_call(
        _cast_kernel,
        grid=(rows // tr, cols // tc),
        in_specs=[pl.BlockSpec((None, tr, tc), lambda i, j: (layer, i, j))],
        out_specs=pl.BlockSpec((tr, tc), lambda i, j: (i, j)),
        out_shape=jax.ShapeDtypeStruct((rows, cols), BF16),
        compiler_params=_params("parallel", "parallel"),
        name="cast_bf16",
    )(w)


def _norm_rope_inplace(t_ref, nw_ref, cos_ref, sin_ref, head_major=False):
    rows = cos_ref.shape[0]
    n_heads = t_ref.shape[0] // rows if head_major else t_ref.shape[1] // HEAD_DIM
    rc = min(rows, QK_ROW_CHUNK)
    half = HEAD_DIM // 2
    nw = nw_ref[...]
    nw_swapped = pltpu.roll(nw, half, axis=1)

    def chunk(c, carry):
        r0 = pl.multiple_of(c * rc, rc)
        cs = cos_ref[pl.ds(r0, rc), :] * nw
        ss = sin_ref[pl.ds(r0, rc), :] * nw_swapped
        for h in range(n_heads):
            idx = (pl.ds(h * rows + r0, rc), slice(None)) if head_major else (
                pl.ds(r0, rc), slice(h * HEAD_DIM, (h + 1) * HEAD_DIM))
            a = t_ref[idx]
            r = lax.rsqrt(jnp.mean(a * a, axis=-1, keepdims=True) + NORM_EPS)
            t_ref[idx] = (a * cs + pltpu.roll(a, half, axis=1) * ss) * r
        return carry

    lax.fori_loop(0, rows // rc, chunk, 0)


def _proj_kernel(x_ref, w_ref, o_ref, *, epilogue):
    acc = jnp.dot(x_ref[...], w_ref[...], preferred_element_type=F32)
    if epilogue == "silu":
        acc = acc * _sigmoid(acc)
    elif epilogue == "sigmoid":
        acc = _sigmoid(acc)
    o_ref[...] = acc.astype(o_ref.dtype)


def _proj(xn, w, col0, ncols, epilogue, out_dtype):
    m, d = xn.shape
    tm = _row_tile(m, 1024)
    tn = min(ncols, 1024)
    assert ncols % tn == 0 and col0 % tn == 0
    jb = col0 // tn
    return pl.pallas_call(
        functools.partial(_proj_kernel, epilogue=epilogue),
        grid=(m // tm, ncols // tn),
        in_specs=[pl.BlockSpec((tm, d), lambda i, j: (i, 0)),
                  pl.BlockSpec((d, tn), lambda i, j: (0, jb + j))],
        out_specs=pl.BlockSpec((tm, tn), lambda i, j: (i, j)),
        out_shape=jax.ShapeDtypeStruct((m, ncols), out_dtype),
        compiler_params=_params("parallel", "arbitrary"),
        name="proj_" + epilogue,
    )(xn, w)


def _norm_rope_head(a, ssq, nw_ref, cos_ref, sin_ref, store):
    rows = a.shape[0]
    rc = min(rows, QK_ROW_CHUNK)
    half = HEAD_DIM // 2
    nw = nw_ref[...]
    nw_swapped = pltpu.roll(nw, half, axis=1)
    for r0 in range(0, rows, rc):
        ac = a[r0:r0 + rc]
        cs = cos_ref[r0:r0 + rc, :] * nw
        ss = sin_ref[r0:r0 + rc, :] * nw_swapped
        r = lax.rsqrt(ssq[r0:r0 + rc] * (1.0 / HEAD_DIM) + NORM_EPS)
        store(r0, rc, (ac * cs + pltpu.roll(ac, half, axis=1) * ss) * r)


def _project_dilated(x, w_ref, nw_ref, cos_ref, sin_ref, hm_ref, d_ref, dil):
    tm = x.shape[0]
    n_heads = w_ref.shape[1] // HEAD_DIM
    per = tm // dil
    acc = jnp.dot(x, w_ref[...], preferred_element_type=F32)
    ones = jnp.ones((HEAD_DIM, HEAD_DIM), BF16)
    for h in range(n_heads):
        sl = slice(h * HEAD_DIM, (h + 1) * HEAD_DIM)
        a = acc[:, sl]
        if nw_ref is None:
            hm_ref[h * tm:(h + 1) * tm, :] = a
        else:
            def store(r0, rc, chunk, h=h):
                hm_ref[h * tm + r0:h * tm + r0 + rc, :] = chunk

            ssq = jnp.dot((a * a).astype(BF16), ones, preferred_element_type=F32)
            _norm_rope_head(a, ssq, nw_ref, cos_ref, sin_ref, store)
        for r in range(dil):
            rows = pl.ds(h * tm + r, per, stride=dil) if dil > 1 else pl.ds(h * tm, per)
            d_ref[0, r, :, sl] = hm_ref[rows, :].astype(d_ref.dtype)


def _qkv_kernel(x_ref, wq_ref, wk_ref, wv_ref, nwq_ref, nwk_ref, cos_ref, sin_ref, kv_in_ref,
                qd_ref, kd_ref, vd_ref, kv_ref, hk_ref, hv_ref, rows_ref, sem,
                *, dil, layer, first_kept, kept_rows):
    b, i = pl.program_id(0), pl.program_id(1)
    tm = x_ref.shape[0]
    n_heads = wk_ref.shape[1] // HEAD_DIM
    kb = rows_ref.shape[0]
    x = x_ref[...]
    _project_dilated(x, wq_ref, nwq_ref, cos_ref, sin_ref, hv_ref, qd_ref, dil)
    _project_dilated(x, wk_ref, nwk_ref, cos_ref, sin_ref, hk_ref, kd_ref, dil)
    _project_dilated(x, wv_ref, None, None, None, hv_ref, vd_ref, dil)

    n_parts = kept_rows // kb

    def copy_out(bb, ii, part):
        dst = kv_ref.at[layer, bb, pl.ds((ii - first_kept) * kept_rows + part * kb, kb)]
        return pltpu.make_async_copy(rows_ref, dst, sem.at[0])

    @pl.when(i >= first_kept)
    def _():
        for part in range(n_parts):
            if part == 0:
                @pl.when((i > first_kept) | (b > 0))
                def _():
                    prev_same = i > first_kept
                    copy_out(jnp.where(prev_same, b, b - 1),
                             jnp.where(prev_same, i - 1, pl.num_programs(1) - 1), n_parts - 1).wait()
            else:
                copy_out(b, i, part - 1).wait()
            base = tm - kept_rows + part * kb

            def tokens(j, carry, base=base):
                t0 = pl.multiple_of(j * ROW_TILE, ROW_TILE)
                for which, hm_ref in ((0, hk_ref), (1, hv_ref)):
                    tiles = [hm_ref[pl.ds(h * tm + base + t0, ROW_TILE), :] for h in range(n_heads)]
                    rows_ref[pl.ds(t0, ROW_TILE), which] = jnp.swapaxes(jnp.stack(tiles, axis=0), 0, 1)
                return carry

            lax.fori_loop(0, kb // ROW_TILE, tokens, 0, unroll=2)
            copy_out(b, i, part).start()

    @pl.when((b == pl.num_programs(0) - 1) & (i == pl.num_programs(1) - 1))
    def _():
        copy_out(b, i, n_parts - 1).wait()


def _qkv(xn, w, g, nb, seq, q_norm_w, k_norm_w, cos, sin, layer, depth, kv_prev):
    m, d = xn.shape
    aw = d // 2
    hg = aw // HEAD_DIM
    dil = DILATIONS[g]
    keep = min(WINDOWS[g], seq)
    tm = _row_tile(seq, 1024)
    per = tm // dil
    kept_rows = min(keep, tm)
    kb = min(kept_rows, KV_ROWS_PER_COPY)
    bps = seq // tm
    first_kept = (seq - keep) // tm if keep >= tm else bps - 1
    assert per % 16 == 0 and keep % kept_rows == 0 and (seq - keep) % kept_rows == 0
    assert kept_rows % kb == 0 and kb % ROW_TILE == 0
    once = pl.Buffered(1)
    nw = lambda a: a.reshape(1, HEAD_DIM)
    dshape = jax.ShapeDtypeStruct((nb, dil, seq // dil, aw), BF16)
    staging = pltpu.VMEM((hg * tm, HEAD_DIM), F32)

    wspec = lambda col: pl.BlockSpec((d, aw), lambda b, i: (0, col), pipeline_mode=once)
    tspec = pl.BlockSpec((tm, HEAD_DIM), lambda b, i: (i, 0))
    dspec = pl.BlockSpec((1, dil, per, aw), lambda b, i: (b, 0, i, 0))
    hbm = pl.BlockSpec(memory_space=pl.ANY)
    kv_shape = jax.ShapeDtypeStruct((depth, nb, keep, 2, hg, HEAD_DIM), F32)
    if kv_prev is None:
        kv_prev = jnp.zeros(kv_shape.shape, kv_shape.dtype)
    nspec = pl.BlockSpec((1, HEAD_DIM), lambda b, i: (0, 0))
    return pl.pallas_call(
        functools.partial(_qkv_kernel, dil=dil, layer=layer, first_kept=first_kept, kept_rows=kept_rows),
        grid=(nb, bps),
        in_specs=[pl.BlockSpec((tm, d), lambda b, i: (b * bps + i, 0)),
                  wspec(g), wspec(N_DIL + g), wspec(2 * N_DIL + g), nspec, nspec, tspec, tspec, hbm],
        out_specs=[dspec, dspec, dspec, hbm],
        out_shape=[dshape, dshape, dshape, kv_shape],
        scratch_shapes=[staging, staging, pltpu.VMEM((kb, 2, hg, HEAD_DIM), F32),
                        pltpu.SemaphoreType.DMA((1,))],
        input_output_aliases={8: 3},
        compiler_params=_params("arbitrary", "arbitrary"),
        name="qkv_d%d" % dil,
    )(xn, w, w, w, nw(q_norm_w), nw(k_norm_w), cos, sin, kv_prev)


def _rope_tables(pos):
    half = HEAD_DIM // 2
    inv_freq = jnp.power(ROPE_THETA, -jnp.arange(half, dtype=F32) * (2.0 / HEAD_DIM))
    ang = pos[:, None] * inv_freq[None, :]
    cos, sin = jnp.cos(ang), jnp.sin(ang)
    return jnp.concatenate([cos, cos], axis=-1), jnp.concatenate([-sin, sin], axis=-1)


def _attn_prompt_kernel(q_ref, kp_ref, kc_ref, vp_ref, vc_ref, o_ref, lse_ref):
    not_first = pl.program_id(2) > 0
    a = lax.broadcasted_iota(jnp.int32, (SPAN, 2 * SPAN), 0)
    c = lax.broadcasted_iota(jnp.int32, (SPAN, 2 * SPAN), 1)
    band = (c >= a) & (c <= a + SPAN)
    band_first = band & ((c >= SPAN) | not_first)
    lane = lax.broadcasted_iota(jnp.int32, (SPAN, HEAD_DIM), 1)
    scale = HEAD_DIM ** -0.5
    for j in range(q_ref.shape[2] // SPAN):
        rows = slice(j * SPAN, (j + 1) * SPAN)
        lse_tile = jnp.zeros((SPAN, HEAD_DIM), F32)
        for h in range(q_ref.shape[3] // HEAD_DIM):
            sl = slice(h * HEAD_DIM, (h + 1) * HEAD_DIM)
            q = q_ref[0, 0, rows, sl]
            if j == 0:
                k = jnp.concatenate([kp_ref[0, 0, :, sl], kc_ref[0, 0, rows, sl]], axis=0)
                v = jnp.concatenate([vp_ref[0, 0, :, sl], vc_ref[0, 0, rows, sl]], axis=0)
            else:
                k = kc_ref[0, 0, (j - 1) * SPAN:(j + 1) * SPAN, sl]
                v = vc_ref[0, 0, (j - 1) * SPAN:(j + 1) * SPAN, sl]
            s = lax.dot_general(q, k, (((1,), (1,)), ((), ())), preferred_element_type=F32) * scale
            s = jnp.where(band_first if j == 0 else band, s, NEG_BIG)
            m = jnp.max(s, axis=-1, keepdims=True)
            p = jnp.exp(s - m)
            l = jnp.sum(p, axis=-1, keepdims=True)
            o = jnp.dot(p.astype(BF16), v, preferred_element_type=F32) / l
            o_ref[0, 0, rows, sl] = o.astype(o_ref.dtype)
            lse_tile = jnp.where(lane == h, m + jnp.log(l), lse_tile)
        lse_ref[0, 0, rows, :] = lse_tile


def _attn_prompt(q, k, v):
    nb, dil, length, aw = q.shape
    n_blk = length // SPAN
    qb = min(4, n_blk)
    assert length % SPAN == 0 and n_blk % qb == 0 and aw // HEAD_DIM <= HEAD_DIM
    cur = pl.BlockSpec((1, 1, qb * SPAN, aw), lambda b, r, n: (b, r, n, 0))
    prev = pl.BlockSpec((1, 1, SPAN, aw), lambda b, r, n: (b, r, jnp.maximum(qb * n - 1, 0), 0))
    return pl.pallas_call(
        _attn_prompt_kernel,
        grid=(nb, dil, n_blk // qb),
        in_specs=[cur, prev, cur, prev, cur],
        out_specs=[cur, pl.BlockSpec((1, 1, qb * SPAN, HEAD_DIM), lambda b, r, n: (b, r, n, 0))],
        out_shape=[jax.ShapeDtypeStruct(q.shape, BF16),
                   jax.ShapeDtypeStruct((nb, dil, length, HEAD_DIM), F32)],
        compiler_params=_params("parallel", "parallel", "arbitrary"),
        name="attn_prompt_d%d" % dil,
    )(q, k, k, v, v)


def _attn_sample_kernel(q_ref, kn_ref, vn_ref, c1_ref, c4_ref, c16_ref, o_ref, lse_ref):
    n_new, n_heads, _ = q_ref.shape
    hg = n_heads // N_DIL
    scale = HEAD_DIM ** -0.5
    caches = (c1_ref, c4_ref, c16_ref)
    for g in range(N_DIL):
        hs = slice(g * hg, (g + 1) * hg)
        for t in range(n_new):
            q = q_ref[t, hs, :]
            if DILATIONS[g] == 1:
                kc, vc = caches[g][:, 0], caches[g][:, 1]
                rows = lax.broadcasted_iota(jnp.int32, (kc.shape[0], hg, 1), 0)
                s = jnp.sum(kc * q[None], axis=-1, keepdims=True) * scale
                s = jnp.where(rows >= t, s, NEG_BIG)
                new = range(t + 1)
            else:
                kc, vc = caches[g][:, t, 0], caches[g][:, t, 1]
                s = jnp.sum(kc * q[None], axis=-1, keepdims=True) * scale
                new = (t,)
            s_new = [jnp.sum(kn_ref[u, hs, :] * q, axis=-1, keepdims=True) * scale for u in new]
            m = jnp.max(s, axis=0)
            for sn in s_new:
                m = jnp.maximum(m, sn)
            p = jnp.exp(s - m[None])
            l = jnp.sum(p, axis=0)
            acc = jnp.sum(p * vc, axis=0)
            for u, sn in zip(new, s_new):
                pn = jnp.exp(sn - m)
                l = l + pn
                acc = acc + pn * vn_ref[u, hs, :]
            o_ref[t, hs, :] = acc / l
            lse_ref[t, hs, :] = jnp.broadcast_to(m + jnp.log(l), (hg, HEAD_DIM))


def _attn_sample(q, kn, vn, caches, layer):
    nb, n_new, n_heads, _ = q.shape
    hg = n_heads // N_DIL
    views, specs = [], []
    for g in range(N_DIL):
        dil = DILATIONS[g]
        depth, _, buf = caches[g].shape[:3]
        assert buf == WINDOWS[g] and (dil == 1 or dil % n_new == 0)
        if dil == 1:
            views.append(caches[g])
            specs.append(pl.BlockSpec((None, None, buf, 2, hg, HEAD_DIM),
                                      lambda b: (layer, b, 0, 0, 0, 0)))
        else:
            views.append(caches[g].reshape(depth, nb, SPAN, dil, 2, hg, HEAD_DIM))
            specs.append(pl.BlockSpec((None, None, SPAN, n_new, 2, hg, HEAD_DIM),
                                      lambda b: (layer, b, 0, 0, 0, 0, 0)))
    tok = pl.BlockSpec((None, n_new, n_heads, HEAD_DIM), lambda b: (b, 0, 0, 0))
    return pl.pallas_call(
        _attn_sample_kernel,
        grid=(nb,),
        in_specs=[tok, tok, tok] + specs,
        out_specs=[tok, tok],
        out_shape=[jax.ShapeDtypeStruct(q.shape, F32)] * 2,
        compiler_params=_params("parallel"),
        name="attn_sample",
    )(q, kn, vn, *views)


def _s5_params_kernel(lre_ref, lim_ref, ldt_ref, bre_ref, bim_ref, ptab_ref, bbre_ref, bbim_ref):
    dt = jnp.exp(ldt_ref[...])
    lre, lim = lre_ref[...], lim_ref[...]
    xr, xi = lre * dt, lim * dt
    mag = jnp.exp(xr)
    ar, ai = mag * jnp.cos(xi), mag * jnp.sin(xi)
    shape = (ROW_TILE, xr.shape[1])
    abr, abi = jnp.broadcast_to(ar, shape), jnp.broadcast_to(ai, shape)
    ptab_ref[0, 0] = abr
    ptab_ref[1, 0] = abi

    def next_power(i, carry):
        pr, pi = carry
        pr, pi = pr * abr - pi * abi, pr * abi + pi * abr
        ptab_ref[0, i] = pr
        ptab_ref[1, i] = pi
        return pr, pi

    lax.fori_loop(1, ptab_ref.shape[1], next_power, (abr, abi))
    nr, ni = ar - 1.0, ai
    den = lre * lre + lim * lim
    fr = (nr * lre + ni * lim) / den
    fi = (ni * lre - nr * lim) / den
    bre, bim = bre_ref[...], bim_ref[...]
    bbre_ref[...] = fr * bre - fi * bim
    bbim_ref[...] = fr * bim + fi * bre


def _s5_params(lam_re, lam_im, log_dt, b_re, b_im, c_re, c_im, seg_len):
    n_grp, n_st = lam_re.shape
    n_ch = b_re.shape[2]
    n = n_grp * n_st
    gc = GROUPS_PER_CHUNK
    n_chunk = n_grp // gc
    sw = gc * n_st
    to_lanes = lambda b: b.transpose(2, 0, 1).reshape(n_ch, n)
    lane = lambda rows: pl.BlockSpec((rows, sw), lambda j: (0, j))
    ptab, bb_re, bb_im = pl.pallas_call(
        _s5_params_kernel,
        grid=(n_chunk,),
        in_specs=[lane(1)] * 3 + [lane(n_ch)] * 2,
        out_specs=[pl.BlockSpec((2, seg_len, ROW_TILE, sw), lambda j: (0, 0, 0, j)), lane(n_ch), lane(n_ch)],
        out_shape=[jax.ShapeDtypeStruct((2, seg_len, ROW_TILE, n), F32),
                   jax.ShapeDtypeStruct((n_ch, n), F32), jax.ShapeDtypeStruct((n_ch, n), F32)],
        compiler_params=_params("parallel"),
        name="s5_params",
    )(lam_re.reshape(1, n), lam_im.reshape(1, n),
      jnp.broadcast_to(log_dt[:, None], (n_grp, n_st)).reshape(1, n), to_lanes(b_re), to_lanes(b_im))

    eye = jnp.eye(gc, dtype=F32)

    def b_blocks(bb):
        bb = bb.reshape(n_ch, n_chunk, gc, n_st)
        blk = jnp.einsum("cjgp,gh->jgchp", bb, eye)
        return blk.reshape(n_chunk, gc * n_ch, gc * n_st).astype(BF16)

    def c_blocks(cc):
        cc = cc.reshape(n_chunk, gc, n_ch, n_st)
        blk = jnp.einsum("jgcp,gh->jgphc", cc, eye)
        return blk.reshape(n_chunk, gc * n_st, gc * n_ch).astype(BF16)

    b_cat = jnp.concatenate([b_blocks(bb_re), b_blocks(bb_im)], axis=2)
    c_cat = jnp.concatenate([c_blocks(c_re), c_blocks(c_im)], axis=1)
    return ptab, b_cat, c_cat


def _cmul_add(xr, xi, ar, ai, br, bi):
    return xr + ar * br - ai * bi, xi + ar * bi + ai * br


def _shift_plan(cache_shape, new_shape, n_steps):
    lb_n, buf = cache_shape[:2]
    keep = buf - new_shape[1]
    row_bytes = 4 * math.prod(cache_shape[2:])
    for per_row in range(1, keep + 1):
        n_jobs = lb_n * per_row
        if keep % per_row == 0 and keep // per_row * row_bytes <= KV_COPY_BYTES and n_steps % n_jobs == 0:
            return keep // per_row, per_row, n_steps // n_jobs
    raise ValueError("window shift does not fit the grid")


def _shift_lag(period):
    return 1 if period == 2 else 2


def _shift_slots(period):
    return 4 if period == 1 else 2


def _window_shift_step(step, n_steps, caches, news, outs, rings, nbufs, in_sem, out_sem, new_sem):
    last = step == n_steps - 1
    for g in range(N_DIL):
        cache, out, ring = caches[g], outs[g], rings[g]
        n_new = news[g].shape[1]
        keep = cache.shape[1] - n_new
        rows, per_row, period = _shift_plan(cache.shape, news[g].shape, n_steps)
        n_jobs = cache.shape[0] * per_row
        slots = ring.shape[0]
        lag = _shift_lag(period)
        assert n_jobs >= slots and (period == 1 or lag < period)

        def load(k, cache=cache, ring=ring, g=g, rows=rows, per_row=per_row, n_new=n_new, slots=slots):
            src = cache.at[lax.div(k, per_row), pl.ds(n_new + lax.rem(k, per_row) * rows, rows)]
            return pltpu.make_async_copy(src, ring.at[lax.rem(k, slots)], in_sem.at[g, lax.rem(k, slots)])

        def store(k, out=out, ring=ring, g=g, rows=rows, per_row=per_row, slots=slots):
            dst = out.at[lax.div(k, per_row), pl.ds(lax.rem(k, per_row) * rows, rows)]
            return pltpu.make_async_copy(ring.at[lax.rem(k, slots)], dst, out_sem.at[g, lax.rem(k, slots)])

        def issue(k, load=load, store=store, slots=slots):
            @pl.when(k >= slots)
            def _():
                store(k - slots).wait()
            load(k).start()

        def forward(k, load=load, store=store):
            load(k).wait()
            store(k).start()

        def drain(k, store=store, slots=slots):
            for back in range(slots - 1, -1, -1):
                store(k - back).wait()

        if period == 1:
            issue(step)

            @pl.when(step >= lag)
            def _(forward=forward, lag=lag):
                forward(step - lag)

            @pl.when(last)
            def _(forward=forward, drain=drain, lag=lag):
                for back in range(lag - 1, -1, -1):
                    forward(step - back)
                drain(step)
        else:
            k = lax.div(step, period)
            phase = lax.rem(step, period)

            @pl.when(phase == 0)
            def _(issue=issue, k=k):
                issue(k)

            @pl.when(phase == lag)
            def _(forward=forward, k=k):
                forward(k)

            @pl.when(last)
            def _(drain=drain, k=k):
                drain(k)

        new_in = pltpu.make_async_copy(news[g], nbufs[g], new_sem.at[g])
        new_out = pltpu.make_async_copy(nbufs[g], out.at[:, pl.ds(keep, n_new)], new_sem.at[N_DIL + g])

        @pl.when(step == 0)
        def _(new_in=new_in, new_out=new_out):
            new_in.start()
            new_in.wait()
            new_out.start()

        @pl.when(last)
        def _(new_out=new_out):
            new_out.wait()


def _s5_scan_kernel(u_ref, b_ref, ptab_ref, c_ref, d_ref, *rest, with_shift):
    if with_shift:
        n = N_DIL
        caches, news = rest[:n], rest[n:2 * n]
        y_ref, hfin_ref = rest[2 * n:2 * n + 2]
        outs = rest[2 * n + 2:3 * n + 2]
        up_ref, s_ref, hb_ref, end_ref = rest[3 * n + 2:3 * n + 6]
        rings, nbufs = rest[3 * n + 6:4 * n + 6], rest[4 * n + 6:5 * n + 6]
        in_sem, out_sem, new_sem = rest[5 * n + 6:]
        _, n_j, n_t = with_shift
        step = (pl.program_id(0) * n_j + pl.program_id(1)) * n_t + pl.program_id(2)
        _window_shift_step(step, math.prod(with_shift), caches, news, outs, rings, nbufs,
                           in_sem, out_sem, new_sem)
    else:
        y_ref, hfin_ref, up_ref, s_ref, hb_ref, end_ref = rest
    tc, sw2 = s_ref.shape
    sw = sw2 // 2
    seg = tc // ROW_TILE
    re, im = slice(0, sw), slice(sw, sw2)

    @pl.when(pl.program_id(2) == 0)
    def _():
        end_ref[...] = jnp.zeros_like(end_ref)

    for i in range(seg):
        up_ref[i * ROW_TILE:(i + 1) * ROW_TILE, :] = u_ref[0, pl.ds(i, ROW_TILE, stride=seg), :]
    s_ref[...] = jnp.dot(up_ref[...].astype(BF16), b_ref[0], preferred_element_type=F32)

    ar, ai = ptab_ref[0, 0], ptab_ref[1, 0]

    def local_step(i, carry):
        hr, hi = carry
        r0 = pl.multiple_of(i * ROW_TILE, ROW_TILE)
        hr, hi = _cmul_add(s_ref[pl.ds(r0, ROW_TILE), re], s_ref[pl.ds(r0, ROW_TILE), im], ar, ai, hr, hi)
        s_ref[pl.ds(r0, ROW_TILE), re] = hr
        s_ref[pl.ds(r0, ROW_TILE), im] = hi
        return hr, hi

    zero = jnp.zeros((ROW_TILE, sw), F32)
    er, ei = lax.fori_loop(0, seg, local_step, (zero, zero), unroll=8)

    row = lax.broadcasted_iota(jnp.int32, (ROW_TILE, sw), 0)
    cr = jnp.where(row == 0, pltpu.roll(end_ref[0], 1, axis=0), pltpu.roll(er, 1, axis=0))
    ci = jnp.where(row == 0, pltpu.roll(end_ref[1], 1, axis=0), pltpu.roll(ei, 1, axis=0))
    mr, mi = ptab_ref[0, seg - 1], ptab_ref[1, seg - 1]
    wr, wi = mr, mi
    for sh in (1, 2, 4):
        gr, gi = jnp.where(row >= sh, wr, 0.0), jnp.where(row >= sh, wi, 0.0)
        cr, ci = _cmul_add(cr, ci, gr, gi, pltpu.roll(cr, sh, axis=0), pltpu.roll(ci, sh, axis=0))
        wr, wi = wr * wr - wi * wi, 2.0 * wr * wi
    fr, fi = _cmul_add(er, ei, mr, mi, cr, ci)
    end_ref[0] = fr
    end_ref[1] = fi
    hfin_ref[0, 0] = fr
    hfin_ref[0, 1] = fi

    def fix_up(k, carry):
        r0 = pl.multiple_of(k * 2 * ROW_TILE, 2 * ROW_TILE)
        hs, ns = [], []
        for half in range(2):
            i = 2 * k + half
            rows = pl.ds(r0 + half * ROW_TILE, ROW_TILE)
            hr, hi = _cmul_add(s_ref[rows, re], s_ref[rows, im], ptab_ref[0, i], ptab_ref[1, i], cr, ci)
            hs.append(hr)
            ns.append(-hi)
        hb_ref[pl.ds(r0, 2 * ROW_TILE), re] = jnp.concatenate(hs, axis=0).astype(BF16)
        hb_ref[pl.ds(r0, 2 * ROW_TILE), im] = jnp.concatenate(ns, axis=0).astype(BF16)
        return carry

    lax.fori_loop(0, seg // 2, fix_up, 0, unroll=4)
    yp = jnp.dot(hb_ref[...], c_ref[0], preferred_element_type=F32) + d_ref[...] * up_ref[...]
    for i in range(seg):
        y_ref[0, pl.ds(i, ROW_TILE, stride=seg), :] = yp[i * ROW_TILE:(i + 1) * ROW_TILE, :]


def _s5_chunk(seq):
    return min(seq, 1024)


def _flat_window(a):
    return a.reshape(a.shape[0] * a.shape[1], a.shape[2], a.shape[3] * a.shape[4], a.shape[5])


def _s5_prompt(u, prm, ssm_d, nb, seq, shift=None):
    ptab, b_cat, c_cat = prm
    n_chunk, cw, sw2 = b_cat.shape
    sw = sw2 // 2
    width = u.shape[1]
    tc = _s5_chunk(seq)
    seg = tc // ROW_TILE
    assert seq % tc == 0 and seg % 2 == 0 and ptab.shape[1] == seg
    grid = (nb, n_chunk, seq // tc)
    in_specs = [pl.BlockSpec((1, tc, cw), lambda b, j, t: (b, t, j)),
                pl.BlockSpec((1, cw, sw2), lambda b, j, t: (j, 0, 0)),
                pl.BlockSpec((2, seg, ROW_TILE, sw), lambda b, j, t: (0, 0, 0, j)),
                pl.BlockSpec((1, sw2, cw), lambda b, j, t: (j, 0, 0)),
                pl.BlockSpec((1, cw), lambda b, j, t: (0, j))]
    out_specs = [pl.BlockSpec((1, tc, cw), lambda b, j, t: (b, t, j)),
                 pl.BlockSpec((1, 2, ROW_TILE, sw), lambda b, j, t: (b, 0, 0, j))]
    out_shape = [jax.ShapeDtypeStruct((nb, seq, width), F32),
                 jax.ShapeDtypeStruct((nb, 2, ROW_TILE, n_chunk * sw), F32)]
    scratch = [pltpu.VMEM((tc, cw), F32), pltpu.VMEM((tc, sw2), F32),
               pltpu.VMEM((tc, sw2), BF16), pltpu.VMEM((2, ROW_TILE, sw), F32)]
    args = [u.reshape(nb, seq, width), b_cat, ptab, c_cat, ssm_d.reshape(1, width)]
    if shift is not None:
        caches, news = shift
        cf, nf = [_flat_window(c) for c in caches], [_flat_window(w) for w in news]
        hbm = pl.BlockSpec(memory_space=pl.ANY)
        in_specs += [hbm] * (2 * N_DIL)
        out_specs += [hbm] * N_DIL
        out_shape += [jax.ShapeDtypeStruct(c.shape, c.dtype) for c in cf]
        args += cf + nf
        max_slots = 0
        for c, w in zip(cf, nf):
            rows, _, period = _shift_plan(c.shape, w.shape, math.prod(grid))
            scratch.append(pltpu.VMEM((_shift_slots(period), rows) + c.shape[2:], c.dtype))
            max_slots = max(max_slots, _shift_slots(period))
        scratch += [pltpu.VMEM(w.shape, w.dtype) for w in nf]
        scratch += [pltpu.SemaphoreType.DMA((N_DIL, max_slots)), pltpu.SemaphoreType.DMA((N_DIL, max_slots)),
                    pltpu.SemaphoreType.DMA((2 * N_DIL,))]
    res = pl.pallas_call(
        functools.partial(_s5_scan_kernel, with_shift=grid if shift is not None else None),
        grid=grid,
        in_specs=in_specs,
        out_specs=out_specs,
        out_shape=out_shape,
        scratch_shapes=scratch,
        compiler_params=_params("arbitrary", "arbitrary", "arbitrary") if shift is not None
        else _params("parallel", "parallel", "arbitrary"),
        name="s5_scan",
    )(*args)
    y, hfin = res[0], res[1]
    last = ROW_TILE - 1
    shifted = None if shift is None else [o.reshape(c.shape) for o, c in zip(res[2:], shift[0])]
    return y.reshape(nb * seq, width), hfin[:, 0, last], hfin[:, 1, last], shifted


def _s5_sample_kernel(u_ref, h0re_ref, h0im_ref, b_ref, ptab_ref, c_ref, d_ref, y_ref, hre_ref, him_ref):
    sw = h0re_ref.shape[1]
    ar, ai = ptab_ref[0, 0, 0:1, :], ptab_ref[1, 0, 0:1, :]
    hr, hi = h0re_ref[...], h0im_ref[...]
    for t in range(u_ref.shape[0]):
        u = u_ref[t]
        bu = jnp.dot(u.astype(BF16), b_ref[0], preferred_element_type=F32)
        hr, hi = _cmul_add(bu[:, :sw], bu[:, sw:], ar, ai, hr, hi)
        hb = jnp.concatenate([hr, -hi], axis=1).astype(BF16)
        y_ref[t] = jnp.dot(hb, c_ref[0], preferred_element_type=F32) + d_ref[...] * u
    hre_ref[...] = hr
    him_ref[...] = hi


def _s5_sample(u, h0re, h0im, prm, ssm_d):
    ptab, b_cat, c_cat = prm
    n_chunk, cw, sw2 = b_cat.shape
    sw = sw2 // 2
    n_new, nb, width = u.shape
    return pl.pallas_call(
        _s5_sample_kernel,
        grid=(n_chunk,),
        in_specs=[pl.BlockSpec((n_new, nb, cw), lambda j: (0, 0, j)),
                  pl.BlockSpec((nb, sw), lambda j: (0, j)),
                  pl.BlockSpec((nb, sw), lambda j: (0, j)),
                  pl.BlockSpec((1, cw, sw2), lambda j: (j, 0, 0)),
                  pl.BlockSpec((2, 1, ROW_TILE, sw), lambda j: (0, 0, 0, j)),
                  pl.BlockSpec((1, sw2, cw), lambda j: (j, 0, 0)),
                  pl.BlockSpec((1, cw), lambda j: (0, j))],
        out_specs=[pl.BlockSpec((n_new, nb, cw), lambda j: (0, 0, j)),
                   pl.BlockSpec((nb, sw), lambda j: (0, j)),
                   pl.BlockSpec((nb, sw), lambda j: (0, j))],
        out_shape=[jax.ShapeDtypeStruct((n_new, nb, width), F32),
                   jax.ShapeDtypeStruct((nb, n_chunk * sw), F32),
                   jax.ShapeDtypeStruct((nb, n_chunk * sw), F32)],
        compiler_params=_params("parallel"),
        name="s5_sample",
    )(u, h0re, h0im, b_cat, ptab, c_cat, ssm_d.reshape(1, width))


def _glu_kernel(y_ref, g_ref, w_ref, b_ref, o_ref):
    y = y_ref[...]
    s = 0.5 * y * (1.0 + jnp.tanh(math.sqrt(2.0 / math.pi) * (y + 0.044715 * (y * y * y))))
    z = jnp.dot(s.astype(BF16), w_ref[...], preferred_element_type=F32) + b_ref[...]
    o_ref[...] = (s * _sigmoid(z) * g_ref[...].astype(F32)).astype(o_ref.dtype)


def _glu(y, gate, w, b):
    m, width = y.shape
    tm = _row_tile(m, 1024)
    row = pl.BlockSpec((tm, width), lambda i: (i, 0))
    return pl.pallas_call(
        _glu_kernel,
        grid=(m // tm,),
        in_specs=[row, row, pl.BlockSpec((width, width), lambda i: (0, 0)),
                  pl.BlockSpec((1, width), lambda i: (0, 0))],
        out_specs=row,
        out_shape=jax.ShapeDtypeStruct((m, width), BF16),
        compiler_params=_params("parallel"),
        name="glu",
    )(y, gate, w, b.reshape(1, width))


def _merge_kernel(*refs):
    o_refs, l_refs = refs[:N_DIL], refs[N_DIL:2 * N_DIL]
    ga_ref, ys_ref, ma_ref, ms_ref, wa_ref, ws_ref, out_ref = refs[2 * N_DIL:2 * N_DIL + 7]
    a_ref, otok_ref, ltok_ref = refs[2 * N_DIL + 7:]

    def token_order(ref, cols, stage):
        dil, per = ref.shape[1], ref.shape[2]
        if dil == 1:
            return ref[0, 0, :, cols].astype(F32)
        for r in range(dil):
            stage[pl.ds(r, per, stride=dil), :] = ref[0, r, :, cols].astype(F32)
        return stage[...]

    ls = [token_order(l_refs[g], slice(None), ltok_ref.at[g]) for g in range(N_DIL)]
    mx = jnp.maximum(jnp.maximum(ls[0], ls[1]), ls[2])
    es = [jnp.exp(l - mx) for l in ls]
    den = es[0] + es[1] + es[2]
    wts = [e / den for e in es]
    tm, aw = a_ref.shape
    half = out_ref.shape[1] // 2
    halves = [slice(c * half, (c + 1) * half) for c in range(2)]
    ys = ys_ref[...]
    acc = [ms_ref[:, cs].astype(F32) * jnp.dot(ys, ws_ref[:, cs], preferred_element_type=F32) for cs in halves]
    group = min(aw, MXU_COLS) // HEAD_DIM
    ya = [None, None]
    for h0 in range(0, aw // HEAD_DIM, group):
        for h in range(h0, h0 + group):
            sl = slice(h * HEAD_DIM, (h + 1) * HEAD_DIM)
            attn = None
            for g in range(N_DIL):
                term = (jnp.broadcast_to(wts[g][:, h:h + 1], (tm, HEAD_DIM))
                        * token_order(o_refs[g], sl, otok_ref.at[g]))
                attn = term if attn is None else attn + term
            a_ref[:, sl] = (attn * ga_ref[:, sl].astype(F32)).astype(BF16)
        ks = slice(h0 * HEAD_DIM, (h0 + group) * HEAD_DIM)
        for c, cs in enumerate(halves):
            part = jnp.dot(a_ref[:, ks], wa_ref[ks, cs], preferred_element_type=F32)
            ya[c] = part if ya[c] is None else ya[c] + part
    for c, cs in enumerate(halves):
        out_ref[:, cs] = (ma_ref[:, cs].astype(F32) * ya[c] + acc[c]).astype(out_ref.dtype)


def _merge(outs, lses, g_attn, y_ssm, gates, w_a, w_s):
    m, aw = g_attn.shape
    d = w_a.shape[1]
    nb = outs[0].shape[0]
    seq = m // nb
    tm = _row_tile(seq, 512)
    bps = seq // tm
    row = pl.BlockSpec((tm, aw), lambda i: (i, 0))
    ol_specs = []
    for arr in list(outs) + list(lses):
        dil = arr.shape[1]
        assert tm % (dil * ROW_TILE) == 0
        ol_specs.append(pl.BlockSpec((1, dil, tm // dil, arr.shape[3]),
                                     lambda i: (lax.div(i, bps), 0, lax.rem(i, bps), 0)))
    scratch = [pltpu.VMEM((tm, aw), BF16), pltpu.VMEM((N_DIL, tm, HEAD_DIM), F32),
               pltpu.VMEM((N_DIL, tm, HEAD_DIM), F32)]
    once = pl.Buffered(1)
    return pl.pallas_call(
        _merge_kernel,
        grid=(m // tm,),
        in_specs=ol_specs + [row, row, pl.BlockSpec((tm, d), lambda i: (i, 0)),
                             pl.BlockSpec((tm, d), lambda i: (i, 1)),
                             pl.BlockSpec((aw, d), lambda i: (0, 0), pipeline_mode=once),
                             pl.BlockSpec((aw, d), lambda i: (0, 0), pipeline_mode=once)],
        out_specs=pl.BlockSpec((tm, d), lambda i: (i, 0)),
        out_shape=jax.ShapeDtypeStruct((m, d), BF16),
        scratch_shapes=scratch,
        compiler_params=_params("parallel"),
        name="merge",
    )(*outs, *lses, g_attn, y_ssm, gates, gates, w_a, w_s)


def _out_kernel(x_ref, m_ref, w_ref, o_ref):
    o_ref[...] = x_ref[...] + jnp.dot(m_ref[...], w_ref[...], preferred_element_type=F32)


def _out_proj(x, merged, w):
    m, d = x.shape
    tm = _row_tile(m, 1024)
    tn = min(d, 1024)
    return pl.pallas_call(
        _out_kernel,
        grid=(d // tn, m // tm),
        in_specs=[pl.BlockSpec((tm, tn), lambda j, i: (i, j)),
                  pl.BlockSpec((tm, d), lambda j, i: (i, 0)),
                  pl.BlockSpec((d, tn), lambda j, i: (0, j))],
        out_specs=pl.BlockSpec((tm, tn), lambda j, i: (i, j)),
        out_shape=jax.ShapeDtypeStruct((m, d), F32),
        compiler_params=_params("parallel", "parallel"),
        name="out_proj",
    )(x, merged, w)


def _sample_front_kernel(x_ref, gw_ref, w_ref, nwq_ref, nwk_ref, cos_ref, sin_ref, o_ref, nw_ref):
    j = pl.program_id(0)
    x = x_ref[...]
    xn = x * lax.rsqrt(jnp.mean(x * x, axis=-1, keepdims=True) + NORM_EPS) * gw_ref[...]
    acc = jnp.dot(xn.astype(BF16), w_ref[...], preferred_element_type=F32)
    qk, v_end = 2 * N_DIL, 3 * N_DIL

    @pl.when(j < qk)
    def _():
        nw_ref[...] = jnp.where(j < N_DIL, nwq_ref[...], nwk_ref[...])
        o_ref[...] = acc
        _norm_rope_inplace(o_ref, nw_ref, cos_ref, sin_ref)

    @pl.when(((j >= qk) & (j < v_end)) | (j == v_end + 1))
    def _():
        o_ref[...] = acc

    @pl.when((j == v_end) | (j == v_end + 2))
    def _():
        o_ref[...] = acc * _sigmoid(acc)

    @pl.when(j > v_end + 2)
    def _():
        o_ref[...] = _sigmoid(acc)


def _sample_front(x, norm_w, w, q_norm_w, k_norm_w, cos, sin):
    m, d = x.shape
    aw = d // 2
    n_cols = w.shape[1]
    row = pl.BlockSpec((m, HEAD_DIM), lambda j: (0, 0))
    one = pl.BlockSpec((1, HEAD_DIM), lambda j: (0, 0))
    return pl.pallas_call(
        _sample_front_kernel,
        grid=(n_cols // aw,),
        in_specs=[pl.BlockSpec((m, d), lambda j: (0, 0)), pl.BlockSpec((1, d), lambda j: (0, 0)),
                  pl.BlockSpec((d, aw), lambda j: (0, j)), one, one, row, row],
        out_specs=pl.BlockSpec((m, aw), lambda j: (0, j)),
        out_shape=jax.ShapeDtypeStruct((m, n_cols), F32),
        scratch_shapes=[pltpu.VMEM((1, HEAD_DIM), F32)],
        compiler_params=_params("arbitrary"),
        name="sample_front",
    )(x, norm_w.reshape(1, d), w, q_norm_w.reshape(1, HEAD_DIM), k_norm_w.reshape(1, HEAD_DIM), cos, sin)


def _layer(x, cos, sin, lw, nb, seq, layer, depth, kv_prev, sample, shift=None):
    (norm_w, w_in, q_norm_w, k_norm_w, w_glu, b_glu, w_br_attn, w_br_ssm, w_out, ssm_d, s5prm) = lw
    m, d = x.shape
    aw = d // 2
    qkv = N_DIL * aw
    hg = aw // HEAD_DIM

    new_kv = []
    if sample is None:
        xn = _rmsnorm(x, norm_w)
        g_attn = _proj(xn, w_in, 3 * qkv, aw, "silu", BF16)
        u = _proj(xn, w_in, 3 * qkv + aw, aw, "plain", F32)
        g_ssm = _proj(xn, w_in, 3 * qkv + 2 * aw, aw, "silu", BF16)
        gates = _proj(xn, w_in, 3 * qkv + 3 * aw, 2 * d, "sigmoid", BF16)
        outs, lses = [], []
        for g in range(N_DIL):
            qd, kd, vd, kv = _qkv(xn, w_in, g, nb, seq, q_norm_w, k_norm_w, cos, sin, layer, depth,
                                  None if kv_prev is None else kv_prev[g])
            o, lse = _attn_prompt(qd, kd, vd)
            outs.append(o)
            lses.append(lse)
            new_kv.append(kv)
        y, hre, him, shifted = _s5_prompt(u, s5prm, ssm_d, nb, seq, shift)
    else:
        caches, h0re, h0im = sample
        front = _sample_front(x, norm_w, w_in, q_norm_w, k_norm_w, cos, sin)
        q, k, v = front[:, :qkv], front[:, qkv:2 * qkv], front[:, 2 * qkv:3 * qkv]
        g_attn, u = front[:, 3 * qkv:3 * qkv + aw], front[:, 3 * qkv + aw:3 * qkv + 2 * aw]
        g_ssm, gates = front[:, 3 * qkv + 2 * aw:3 * qkv + 3 * aw], front[:, 3 * qkv + 3 * aw:]
        tok = (nb, seq, N_DIL * hg, HEAD_DIM)
        o, lse = _attn_sample(q.reshape(tok), k.reshape(tok), v.reshape(tok), caches, layer)
        o = o.reshape(m, qkv)
        lse = jnp.pad(lse[..., 0].reshape(m, N_DIL, hg), ((0, 0), (0, 0), (0, HEAD_DIM - hg)))
        outs = [o[:, g * aw:(g + 1) * aw].reshape(1, 1, m, aw) for g in range(N_DIL)]
        lses = [lse[:, g].reshape(1, 1, m, HEAD_DIM) for g in range(N_DIL)]
        ut = u.reshape(nb, seq, aw).transpose(1, 0, 2)
        yt, hre, him = _s5_sample(ut, h0re, h0im, s5prm, ssm_d)
        y = yt.transpose(1, 0, 2).reshape(m, aw)
        k5, v5 = k.reshape(nb, seq, N_DIL, hg, HEAD_DIM), v.reshape(nb, seq, N_DIL, hg, HEAD_DIM)
        new_kv = [jnp.stack([k5[:, :, g], v5[:, :, g]], axis=2) for g in range(N_DIL)]
        shifted = None

    y_ssm = _glu(y, g_ssm, w_glu, b_glu)
    merged = _merge(outs, lses, g_attn, y_ssm, gates, w_br_attn, w_br_ssm)
    x_out = _out_proj(x, merged, w_out)

    n_grp = aw // SSM_GROUP_CH
    state = jnp.stack([hre, him], axis=-1).reshape(nb, n_grp, SSM_STATE, 2)
    return x_out, new_kv, state, shifted


def kernel(x_prompt, x_sample, cache_kv_d1, cache_kv_d4, cache_kv_d16, state_ssm, norm_w, w_in, q_norm_w, k_norm_w, ssm_lambda_re, ssm_lambda_im, ssm_log_dt, ssm_b_re, ssm_b_im, ssm_c_re, ssm_c_im, ssm_d, w_glu, b_glu, w_br_attn, w_br_ssm, w_out):
    nb_p, seq_p, d = x_prompt.shape
    nb_s, seq_s, _ = x_sample.shape
    depth = norm_w.shape[0]
    caches = (cache_kv_d1, cache_kv_d4, cache_kv_d16)

    cos_p, sin_p = _rope_tables(jnp.arange(seq_p, dtype=F32))
    cos_s, sin_s = _rope_tables(PAST_LEN + jnp.arange(seq_s, dtype=F32))
    cos_s, sin_s = jnp.tile(cos_s, (nb_s, 1)), jnp.tile(sin_s, (nb_s, 1))

    hp = x_prompt.reshape(nb_p * seq_p, d)
    hs = x_sample.reshape(nb_s * seq_s, d)
    lws = []
    for l in range(depth):
        s5prm = _s5_params(ssm_lambda_re[l], ssm_lambda_im[l], ssm_log_dt[l], ssm_b_re[l], ssm_b_im[l],
                           ssm_c_re[l], ssm_c_im[l], _s5_chunk(seq_p) // ROW_TILE)
        lws.append((norm_w[l], _layer_bf16(w_in, l), q_norm_w[l], k_norm_w[l], _layer_bf16(w_glu, l), b_glu[l],
                    _layer_bf16(w_br_attn, l), _layer_bf16(w_br_ssm, l), _layer_bf16(w_out, l), ssm_d[l], s5prm))

    kv_s_new = [[] for _ in range(N_DIL)]
    ssm_s = []
    n_state = state_ssm.shape[2] * state_ssm.shape[3]
    for l in range(depth):
        h0re = state_ssm[l, ..., 0].reshape(nb_s, n_state)
        h0im = state_ssm[l, ..., 1].reshape(nb_s, n_state)
        hs, new_kv_s, st_s, _ = _layer(hs, cos_s, sin_s, lws[l], nb_s, seq_s, l, depth, None,
                                       (caches, h0re, h0im))
        for g in range(N_DIL):
            kv_s_new[g].append(new_kv_s[g])
        ssm_s.append(st_s)

    kv_p, kv_s, ssm_p = None, None, []
    for l in range(depth):
        shift = (caches, [jnp.stack(n) for n in kv_s_new]) if l == 0 else None
        hp, kv_p, st, shifted = _layer(hp, cos_p, sin_p, lws[l], nb_p, seq_p, l, depth, kv_p, None, shift)
        kv_s = shifted if shifted is not None else kv_s
        ssm_p.append(st)

    return (hp.reshape(x_prompt.shape), hs.reshape(x_sample.shape), kv_p[0], kv_p[1], kv_p[2], jnp.stack(ssm_p),
            kv_s[0], kv_s[1], kv_s[2], jnp.stack(ssm_s))
```

```python
import functools
import math

import jax
import jax.numpy as jnp
from jax import lax
from jax.experimental import pallas as pl
from jax.experimental.pallas import tpu as pltpu

F32 = jnp.float32
BF16 = jnp.bfloat16

HEAD_DIM = 128
WINDOWS = (128, 512, 2048)
DILATIONS = (1, 4, 16)
N_DIL = 3
SPAN = 128
SSM_GROUP_CH = 16
SSM_STATE = 64
GROUPS_PER_CHUNK = 8
ROPE_THETA = 10000.0
NORM_EPS = 1e-6
PAST_LEN = 16384
NEG_BIG = -1e30
VMEM_LIMIT = 58 * 1024 * 1024
KV_ROWS_PER_COPY = 512
ROW_TILE = 8
MXU_COLS = 256
QK_ROW_CHUNK = 64
KV_COPY_BYTES = 4 * 1024 * 1024


def _params(*sem):
    return pltpu.CompilerParams(dimension_semantics=sem, vmem_limit_bytes=VMEM_LIMIT)


def _sigmoid(x):
    return 1.0 / (1.0 + jnp.exp(-x))


def _row_tile(m, cap):
    t = min(m, cap)
    assert m % t == 0, (m, t)
    return t


def _rmsnorm_kernel(x_ref, w_ref, o_ref):
    x = x_ref[...]
    y = x * lax.rsqrt(jnp.mean(x * x, axis=-1, keepdims=True) + NORM_EPS)
    o_ref[...] = (y * w_ref[...]).astype(o_ref.dtype)


def _rmsnorm(x, w):
    m, d = x.shape
    tm = _row_tile(m, 512)
    return pl.pallas_call(
        _rmsnorm_kernel,
        grid=(m // tm,),
        in_specs=[pl.BlockSpec((tm, d), lambda i: (i, 0)),
                  pl.BlockSpec((1, d), lambda i: (0, 0))],
        out_specs=pl.BlockSpec((tm, d), lambda i: (i, 0)),
        out_shape=jax.ShapeDtypeStruct((m, d), BF16),
        compiler_params=_params("parallel"),
        name="rmsnorm",
    )(x, w.reshape(1, d))


def _cast_kernel(w_ref, o_ref):
    o_ref[...] = w_ref[...].astype(o_ref.dtype)


def _layer_bf16(w, layer):
    _, rows, cols = w.shape
    tr, tc = min(rows, 512), min(cols, 2048)
    assert rows % tr == 0 and cols % tc == 0
    return pl.pallas_call(
        _cast_kernel,
        grid=(rows // tr, cols // tc),
        in_specs=[pl.BlockSpec((None, tr, tc), lambda i, j: (layer, i, j))],
        out_specs=pl.BlockSpec((tr, tc), lambda i, j: (i, j)),
        out_shape=jax.ShapeDtypeStruct((rows, cols), BF16),
        compiler_params=_params("parallel", "parallel"),
        name="cast_bf16",
    )(w)


def _norm_rope_inplace(t_ref, nw_ref, cos_ref, sin_ref, head_major=False):
    rows = cos_ref.shape[0]
    n_heads = t_ref.shape[0] // rows if head_major else t_ref.shape[1] // HEAD_DIM
    rc = min(rows, QK_ROW_CHUNK)
    half = HEAD_DIM // 2
    nw = nw_ref[...]
    nw_swapped = pltpu.roll(nw, half, axis=1)

    def chunk(c, carry):
        r0 = pl.multiple_of(c * rc, rc)
        cs = cos_ref[pl.ds(r0, rc), :] * nw
        ss = sin_ref[pl.ds(r0, rc), :] * nw_swapped
        for h in range(n_heads):
            idx = (pl.ds(h * rows + r0, rc), slice(None)) if head_major else (
                pl.ds(r0, rc), slice(h * HEAD_DIM, (h + 1) * HEAD_DIM))
            a = t_ref[idx]
            r = lax.rsqrt(jnp.mean(a * a, axis=-1, keepdims=True) + NORM_EPS)
            t_ref[idx] = (a * cs + pltpu.roll(a, half, axis=1) * ss) * r
        return carry

    lax.fori_loop(0, rows // rc, chunk, 0)


def _proj_kernel(x_ref, w_ref, o_ref, *, epilogue):
    acc = jnp.dot(x_ref[...], w_ref[...], preferred_element_type=F32)
    if epilogue == "silu":
        acc = acc * _sigmoid(acc)
    elif epilogue == "sigmoid":
        acc = _sigmoid(acc)
    o_ref[...] = acc.astype(o_ref.dtype)


def _proj(xn, w, col0, ncols, epilogue, out_dtype):
    m, d = xn.shape
    tm = _row_tile(m, 1024)
    tn = min(ncols, 1024)
    assert ncols % tn == 0 and col0 % tn == 0
    jb = col0 // tn
    return pl.pallas_call(
        functools.partial(_proj_kernel, epilogue=epilogue),
        grid=(m // tm, ncols // tn),
        in_specs=[pl.BlockSpec((tm, d), lambda i, j: (i, 0)),
                  pl.BlockSpec((d, tn), lambda i, j: (0, jb + j))],
        out_specs=pl.BlockSpec((tm, tn), lambda i, j: (i, j)),
        out_shape=jax.ShapeDtypeStruct((m, ncols), out_dtype),
        compiler_params=_params("parallel", "arbitrary"),
        name="proj_" + epilogue,
    )(xn, w)


def _norm_rope_head(a, ssq, nw_ref, cos_ref, sin_ref, store):
    rows = a.shape[0]
    rc = min(rows, QK_ROW_CHUNK)
    half = HEAD_DIM // 2
    nw = nw_ref[...]
    nw_swapped = pltpu.roll(nw, half, axis=1)
    for r0 in range(0, rows, rc):
        ac = a[r0:r0 + rc]
        cs = cos_ref[r0:r0 + rc, :] * nw
        ss = sin_ref[r0:r0 + rc, :] * nw_swapped
        r = lax.rsqrt(ssq[r0:r0 + rc] * (1.0 / HEAD_DIM) + NORM_EPS)
        store(r0, rc, (ac * cs + pltpu.roll(ac, half, axis=1) * ss) * r)


def _project_dilated(x, w_ref, nw_ref, cos_ref, sin_ref, hm_ref, d_ref, dil):
    tm = x.shape[0]
    n_heads = w_ref.shape[1] // HEAD_DIM
    per = tm // dil
    acc = jnp.dot(x, w_ref[...], preferred_element_type=F32)
    ones = jnp.ones((HEAD_DIM, HEAD_DIM), BF16)
    for h in range(n_heads):
        sl = slice(h * HEAD_DIM, (h + 1) * HEAD_DIM)
        a = acc[:, sl]
        if nw_ref is None:
            hm_ref[h * tm:(h + 1) * tm, :] = a
        else:
            def store(r0, rc, chunk, h=h):
                hm_ref[h * tm + r0:h * tm + r0 + rc, :] = chunk

            ssq = jnp.dot((a * a).astype(BF16), ones, preferred_element_type=F32)
            _norm_rope_head(a, ssq, nw_ref, cos_ref, sin_ref, store)
        for r in range(dil):
            rows = pl.ds(h * tm + r, per, stride=dil) if dil > 1 else pl.ds(h * tm, per)
            d_ref[0, r, :, sl] = hm_ref[rows, :].astype(d_ref.dtype)


def _qkv_kernel(x_ref, wq_ref, wk_ref, wv_ref, nwq_ref, nwk_ref, cos_ref, sin_ref, kv_in_ref,
                qd_ref, kd_ref, vd_ref, kv_ref, hk_ref, hv_ref, rows_ref, sem,
                *, dil, layer, first_kept, kept_rows):
    b, i = pl.program_id(0), pl.program_id(1)
    tm = x_ref.shape[0]
    n_heads = wk_ref.shape[1] // HEAD_DIM
    kb = rows_ref.shape[0]
    x = x_ref[...]
    _project_dilated(x, wq_ref, nwq_ref, cos_ref, sin_ref, hv_ref, qd_ref, dil)
    _project_dilated(x, wk_ref, nwk_ref, cos_ref, sin_ref, hk_ref, kd_ref, dil)
    _project_dilated(x, wv_ref, None, None, None, hv_ref, vd_ref, dil)

    n_parts = kept_rows // kb

    def copy_out(bb, ii, part):
        dst = kv_ref.at[layer, bb, pl.ds((ii - first_kept) * kept_rows + part * kb, kb)]
        return pltpu.make_async_copy(rows_ref, dst, sem.at[0])

    @pl.when(i >= first_kept)
    def _():
        for part in range(n_parts):
            if part == 0:
                @pl.when((i > first_kept) | (b > 0))
                def _():
                    prev_same = i > first_kept
                    copy_out(jnp.where(prev_same, b, b - 1),
                             jnp.where(prev_same, i - 1, pl.num_programs(1) - 1), n_parts - 1).wait()
            else:
                copy_out(b, i, part - 1).wait()
            base = tm - kept_rows + part * kb

            def tokens(j, carry, base=base):
                t0 = pl.multiple_of(j * ROW_TILE, ROW_TILE)
                for which, hm_ref in ((0, hk_ref), (1, hv_ref)):
                    tiles = [hm_ref[pl.ds(h * tm + base + t0, ROW_TILE), :] for h in range(n_heads)]
                    rows_ref[pl.ds(t0, ROW_TILE), which] = jnp.swapaxes(jnp.stack(tiles, axis=0), 0, 1)
                return carry

            lax.fori_loop(0, kb // ROW_TILE, tokens, 0, unroll=2)
            copy_out(b, i, part).start()

    @pl.when((b == pl.num_programs(0) - 1) & (i == pl.num_programs(1) - 1))
    def _():
        copy_out(b, i, n_parts - 1).wait()


def _qkv(xn, w, g, nb, seq, q_norm_w, k_norm_w, cos, sin, layer, depth, kv_prev):
    m, d = xn.shape
    aw = d // 2
    hg = aw // HEAD_DIM
    dil = DILATIONS[g]
    keep = min(WINDOWS[g], seq)
    tm = _row_tile(seq, 1024)
    per = tm // dil
    kept_rows = min(keep, tm)
    kb = min(kept_rows, KV_ROWS_PER_COPY)
    bps = seq // tm
    first_kept = (seq - keep) // tm if keep >= tm else bps - 1
    assert per % 16 == 0 and keep % kept_rows == 0 and (seq - keep) % kept_rows == 0
    assert kept_rows % kb == 0 and kb % ROW_TILE == 0
    once = pl.Buffered(1)
    nw = lambda a: a.reshape(1, HEAD_DIM)
    dshape = jax.ShapeDtypeStruct((nb, dil, seq // dil, aw), BF16)
    staging = pltpu.VMEM((hg * tm, HEAD_DIM), F32)

    wspec = lambda col: pl.BlockSpec((d, aw), lambda b, i: (0, col), pipeline_mode=once)
    tspec = pl.BlockSpec((tm, HEAD_DIM), lambda b, i: (i, 0))
    dspec = pl.BlockSpec((1, dil, per, aw), lambda b, i: (b, 0, i, 0))
    hbm = pl.BlockSpec(memory_space=pl.ANY)
    kv_shape = jax.ShapeDtypeStruct((depth, nb, keep, 2, hg, HEAD_DIM), F32)
    if kv_prev is None:
        kv_prev = jnp.zeros(kv_shape.shape, kv_shape.dtype)
    nspec = pl.BlockSpec((1, HEAD_DIM), lambda b, i: (0, 0))
    return pl.pallas_call(
        functools.partial(_qkv_kernel, dil=dil, layer=layer, first_kept=first_kept, kept_rows=kept_rows),
        grid=(nb, bps),
        in_specs=[pl.BlockSpec((tm, d), lambda b, i: (b * bps + i, 0)),
                  wspec(g), wspec(N_DIL + g), wspec(2 * N_DIL + g), nspec, nspec, tspec, tspec, hbm],
        out_specs=[dspec, dspec, dspec, hbm],
        out_shape=[dshape, dshape, dshape, kv_shape],
        scratch_shapes=[staging, staging, pltpu.VMEM((kb, 2, hg, HEAD_DIM), F32),
                        pltpu.SemaphoreType.DMA((1,))],
        input_output_aliases={8: 3},
        compiler_params=_params("arbitrary", "arbitrary"),
        name="qkv_d%d" % dil,
    )(xn, w, w, w, nw(q_norm_w), nw(k_norm_w), cos, sin, kv_prev)


def _rope_tables(pos):
    half = HEAD_DIM // 2
    inv_freq = jnp.power(ROPE_THETA, -jnp.arange(half, dtype=F32) * (2.0 / HEAD_DIM))
    ang = pos[:, None] * inv_freq[None, :]
    cos, sin = jnp.cos(ang), jnp.sin(ang)
    return jnp.concatenate([cos, cos], axis=-1), jnp.concatenate([-sin, sin], axis=-1)


def _attn_prompt_kernel(q_ref, kp_ref, kc_ref, vp_ref, vc_ref, o_ref, lse_ref):
    not_first = pl.program_id(2) > 0
    a = lax.broadcasted_iota(jnp.int32, (SPAN, 2 * SPAN), 0)
    c = lax.broadcasted_iota(jnp.int32, (SPAN, 2 * SPAN), 1)
    band = (c >= a) & (c <= a + SPAN)
    band_first = band & ((c >= SPAN) | not_first)
    lane = lax.broadcasted_iota(jnp.int32, (SPAN, HEAD_DIM), 1)
    scale = HEAD_DIM ** -0.5
    for j in range(q_ref.shape[2] // SPAN):
        rows = slice(j * SPAN, (j + 1) * SPAN)
        lse_tile = jnp.zeros((SPAN, HEAD_DIM), F32)
        for h in range(q_ref.shape[3] // HEAD_DIM):
            sl = slice(h * HEAD_DIM, (h + 1) * HEAD_DIM)
            q = q_ref[0, 0, rows, sl]
            if j == 0:
                k = jnp.concatenate([kp_ref[0, 0, :, sl], kc_ref[0, 0, rows, sl]], axis=0)
                v = jnp.concatenate([vp_ref[0, 0, :, sl], vc_ref[0, 0, rows, sl]], axis=0)
            else:
                k = kc_ref[0, 0, (j - 1) * SPAN:(j + 1) * SPAN, sl]
                v = vc_ref[0, 0, (j - 1) * SPAN:(j + 1) * SPAN, sl]
            s = lax.dot_general(q, k, (((1,), (1,)), ((), ())), preferred_element_type=F32) * scale
            s = jnp.where(band_first if j == 0 else band, s, NEG_BIG)
            m = jnp.max(s, axis=-1, keepdims=True)
            p = jnp.exp(s - m)
            l = jnp.sum(p, axis=-1, keepdims=True)
            o = jnp.dot(p.astype(BF16), v, preferred_element_type=F32) / l
            o_ref[0, 0, rows, sl] = o.astype(o_ref.dtype)
            lse_tile = jnp.where(lane == h, m + jnp.log(l), lse_tile)
        lse_ref[0, 0, rows, :] = lse_tile


def _attn_prompt(q, k, v):
    nb, dil, length, aw = q.shape
    n_blk = length // SPAN
    qb = min(4, n_blk)
    assert length % SPAN == 0 and n_blk % qb == 0 and aw // HEAD_DIM <= HEAD_DIM
    cur = pl.BlockSpec((1, 1, qb * SPAN, aw), lambda b, r, n: (b, r, n, 0))
    prev = pl.BlockSpec((1, 1, SPAN, aw), lambda b, r, n: (b, r, jnp.maximum(qb * n - 1, 0), 0))
    return pl.pallas_call(
        _attn_prompt_kernel,
        grid=(nb, dil, n_blk // qb),
        in_specs=[cur, prev, cur, prev, cur],
        out_specs=[cur, pl.BlockSpec((1, 1, qb * SPAN, HEAD_DIM), lambda b, r, n: (b, r, n, 0))],
        out_shape=[jax.ShapeDtypeStruct(q.shape, BF16),
                   jax.ShapeDtypeStruct((nb, dil, length, HEAD_DIM), F32)],
        compiler_params=_params("parallel", "parallel", "arbitrary"),
        name="attn_prompt_d%d" % dil,
    )(q, k, k, v, v)


def _attn_sample_kernel(q_ref, kn_ref, vn_ref, c1_ref, c4_ref, c16_ref, o_ref, lse_ref):
    n_new, n_heads, _ = q_ref.shape
    hg = n_heads // N_DIL
    scale = HEAD_DIM ** -0.5
    caches = (c1_ref, c4_ref, c16_ref)
    for g in range(N_DIL):
        hs = slice(g * hg, (g + 1) * hg)
        for t in range(n_new):
            q = q_ref[t, hs, :]
            if DILATIONS[g] == 1:
                kc, vc = caches[g][:, 0], caches[g][:, 1]
                rows = lax.broadcasted_iota(jnp.int32, (kc.shape[0], hg, 1), 0)
                s = jnp.sum(kc * q[None], axis=-1, keepdims=True) * scale
                s = jnp.where(rows >= t, s, NEG_BIG)
                new = range(t + 1)
            else:
                kc, vc = caches[g][:, t, 0], caches[g][:, t, 1]
                s = jnp.sum(kc * q[None], axis=-1, keepdims=True) * scale
                new = (t,)
            s_new = [jnp.sum(kn_ref[u, hs, :] * q, axis=-1, keepdims=True) * scale for u in new]
            m = jnp.max(s, axis=0)
            for sn in s_new:
                m = jnp.maximum(m, sn)
            p = jnp.exp(s - m[None])
            l = jnp.sum(p, axis=0)
            acc = jnp.sum(p * vc, axis=0)
            for u, sn in zip(new, s_new):
                pn = jnp.exp(sn - m)
                l = l + pn
                acc = acc + pn * vn_ref[u, hs, :]
            o_ref[t, hs, :] = acc / l
            lse_ref[t, hs, :] = jnp.broadcast_to(m + jnp.log(l), (hg, HEAD_DIM))


def _attn_sample(q, kn, vn, caches, layer):
    nb, n_new, n_heads, _ = q.shape
    hg = n_heads // N_DIL
    views, specs = [], []
    for g in range(N_DIL):
        dil = DILATIONS[g]
        depth, _, buf = caches[g].shape[:3]
        assert buf == WINDOWS[g] and (dil == 1 or dil % n_new == 0)
        if dil == 1:
            views.append(caches[g])
            specs.append(pl.BlockSpec((None, None, buf, 2, hg, HEAD_DIM),
                                      lambda b: (layer, b, 0, 0, 0, 0)))
        else:
            views.append(caches[g].reshape(depth, nb, SPAN, dil, 2, hg, HEAD_DIM))
            specs.append(pl.BlockSpec((None, None, SPAN, n_new, 2, hg, HEAD_DIM),
                                      lambda b: (layer, b, 0, 0, 0, 0, 0)))
    tok = pl.BlockSpec((None, n_new, n_heads, HEAD_DIM), lambda b: (b, 0, 0, 0))
    return pl.pallas_call(
        _attn_sample_kernel,
        grid=(nb,),
        in_specs=[tok, tok, tok] + specs,
        out_specs=[tok, tok],
        out_shape=[jax.ShapeDtypeStruct(q.shape, F32)] * 2,
        compiler_params=_params("parallel"),
        name="attn_sample",
    )(q, kn, vn, *views)


def _s5_params_kernel(lre_ref, lim_ref, ldt_ref, bre_ref, bim_ref, ptab_ref, bbre_ref, bbim_ref):
    dt = jnp.exp(ldt_ref[...])
    lre, lim = lre_ref[...], lim_ref[...]
    xr, xi = lre * dt, lim * dt
    mag = jnp.exp(xr)
    ar, ai = mag * jnp.cos(xi), mag * jnp.sin(xi)
    shape = (ROW_TILE, xr.shape[1])
    abr, abi = jnp.broadcast_to(ar, shape), jnp.broadcast_to(ai, shape)
    ptab_ref[0, 0] = abr
    ptab_ref[1, 0] = abi

    def next_power(i, carry):
        pr, pi = carry
        pr, pi = pr * abr - pi * abi, pr * abi + pi * abr
        ptab_ref[0, i] = pr
        ptab_ref[1, i] = pi
        return pr, pi

    lax.fori_loop(1, ptab_ref.shape[1], next_power, (abr, abi))
    nr, ni = ar - 1.0, ai
    den = lre * lre + lim * lim
    fr = (nr * lre + ni * lim) / den
    fi = (ni * lre - nr * lim) / den
    bre, bim = bre_ref[...], bim_ref[...]
    bbre_ref[...] = fr * bre - fi * bim
    bbim_ref[...] = fr * bim + fi * bre


def _s5_params(lam_re, lam_im, log_dt, b_re, b_im, c_re, c_im, seg_len):
    n_grp, n_st = lam_re.shape
    n_ch = b_re.shape[2]
    n = n_grp * n_st
    gc = GROUPS_PER_CHUNK
    n_chunk = n_grp // gc
    sw = gc * n_st
    to_lanes = lambda b: b.transpose(2, 0, 1).reshape(n_ch, n)
    lane = lambda rows: pl.BlockSpec((rows, sw), lambda j: (0, j))
    ptab, bb_re, bb_im = pl.pallas_call(
        _s5_params_kernel,
        grid=(n_chunk,),
        in_specs=[lane(1)] * 3 + [lane(n_ch)] * 2,
        out_specs=[pl.BlockSpec((2, seg_len, ROW_TILE, sw), lambda j: (0, 0, 0, j)), lane(n_ch), lane(n_ch)],
        out_shape=[jax.ShapeDtypeStruct((2, seg_len, ROW_TILE, n), F32),
                   jax.ShapeDtypeStruct((n_ch, n), F32), jax.ShapeDtypeStruct((n_ch, n), F32)],
        compiler_params=_params("parallel"),
        name="s5_params",
    )(lam_re.reshape(1, n), lam_im.reshape(1, n),
      jnp.broadcast_to(log_dt[:, None], (n_grp, n_st)).reshape(1, n), to_lanes(b_re), to_lanes(b_im))

    eye = jnp.eye(gc, dtype=F32)

    def b_blocks(bb):
        bb = bb.reshape(n_ch, n_chunk, gc, n_st)
        blk = jnp.einsum("cjgp,gh->jgchp", bb, eye)
        return blk.reshape(n_chunk, gc * n_ch, gc * n_st).astype(BF16)

    def c_blocks(cc):
        cc = cc.reshape(n_chunk, gc, n_ch, n_st)
        blk = jnp.einsum("jgcp,gh->jgphc", cc, eye)
        return blk.reshape(n_chunk, gc * n_st, gc * n_ch).astype(BF16)

    b_cat = jnp.concatenate([b_blocks(bb_re), b_blocks(bb_im)], axis=2)
    c_cat = jnp.concatenate([c_blocks(c_re), c_blocks(c_im)], axis=1)
    return ptab, b_cat, c_cat


def _cmul_add(xr, xi, ar, ai, br, bi):
    return xr + ar * br - ai * bi, xi + ar * bi + ai * br


def _shift_plan(cache_shape, new_shape, n_steps):
    lb_n, buf = cache_shape[:2]
    keep = buf - new_shape[1]
    row_bytes = 4 * math.prod(cache_shape[2:])
    for per_row in range(1, keep + 1):
        n_jobs = lb_n * per_row
        if keep % per_row == 0 and keep // per_row * row_bytes <= KV_COPY_BYTES and n_steps % n_jobs == 0:
            return keep // per_row, per_row, n_steps // n_jobs
    raise ValueError("window shift does not fit the grid")


def _shift_lag(period):
    return 1 if period == 2 else 2


def _shift_slots(period):
    return 4 if period == 1 else 2


def _window_shift_step(step, n_steps, caches, news, outs, rings, nbufs, in_sem, out_sem, new_sem):
    last = step == n_steps - 1
    for g in range(len(caches)):
        cache, out, ring = caches[g], outs[g], rings[g]
        n_new = news[g].shape[1]
        keep = cache.shape[1] - n_new
        rows, per_row, period = _shift_plan(cache.shape, news[g].shape, n_steps)
        n_jobs = cache.shape[0] * per_row
        slots = ring.shape[0]
        lag = _shift_lag(period)
        assert n_jobs >= slots and (period == 1 or lag < period)

        def load(k, cache=cache, ring=ring, g=g, rows=rows, per_row=per_row, n_new=n_new, slots=slots):
            src = cache.at[lax.div(k, per_row), pl.ds(n_new + lax.rem(k, per_row) * rows, rows)]
            return pltpu.make_async_copy(src, ring.at[lax.rem(k, slots)], in_sem.at[g, lax.rem(k, slots)])

        def store(k, out=out, ring=ring, g=g, rows=rows, per_row=per_row, slots=slots):
            dst = out.at[lax.div(k, per_row), pl.ds(lax.rem(k, per_row) * rows, rows)]
            return pltpu.make_async_copy(ring.at[lax.rem(k, slots)], dst, out_sem.at[g, lax.rem(k, slots)])

        def issue(k, load=load, store=store, slots=slots):
            @pl.when(k >= slots)
            def _():
                store(k - slots).wait()
            load(k).start()

        def forward(k, load=load, store=store):
            load(k).wait()
            store(k).start()

        def drain(k, store=store, slots=slots):
            for back in range(slots - 1, -1, -1):
                store(k - back).wait()

        if period == 1:
            issue(step)

            @pl.when(step >= lag)
            def _(forward=forward, lag=lag):
                forward(step - lag)

            @pl.when(last)
            def _(forward=forward, drain=drain, lag=lag):
                for back in range(lag - 1, -1, -1):
                    forward(step - back)
                drain(step)
        else:
            k = lax.div(step, period)
            phase = lax.rem(step, period)

            @pl.when(phase == 0)
            def _(issue=issue, k=k):
                issue(k)

            @pl.when(phase == lag)
            def _(forward=forward, k=k):
                forward(k)

            @pl.when(last)
            def _(drain=drain, k=k):
                drain(k)

        new_in = pltpu.make_async_copy(news[g], nbufs[g], new_sem.at[g])
        new_out = pltpu.make_async_copy(nbufs[g], out.at[:, pl.ds(keep, n_new)], new_sem.at[len(caches) + g])

        @pl.when(step == 0)
        def _(new_in=new_in, new_out=new_out):
            new_in.start()
            new_in.wait()
            new_out.start()

        @pl.when(last)
        def _(new_out=new_out):
            new_out.wait()


def _s5_scan_kernel(u_ref, b_ref, ptab_ref, c_ref, d_ref, *rest, with_shift):
    if with_shift:
        n, grid = with_shift
        caches, news = rest[:n], rest[n:2 * n]
        y_ref, hfin_ref = rest[2 * n:2 * n + 2]
        outs = rest[2 * n + 2:3 * n + 2]
        up_ref, s_ref, hb_ref, end_ref = rest[3 * n + 2:3 * n + 6]
        rings, nbufs = rest[3 * n + 6:4 * n + 6], rest[4 * n + 6:5 * n + 6]
        in_sem, out_sem, new_sem = rest[5 * n + 6:]
        step = (pl.program_id(0) * grid[1] + pl.program_id(1)) * grid[2] + pl.program_id(2)
        _window_shift_step(step, math.prod(grid), caches, news, outs, rings, nbufs, in_sem, out_sem, new_sem)
    else:
        y_ref, hfin_ref, up_ref, s_ref, hb_ref, end_ref = rest
    tc, sw2 = s_ref.shape
    sw = sw2 // 2
    seg = tc // ROW_TILE
    re, im = slice(0, sw), slice(sw, sw2)

    @pl.when(pl.program_id(2) == 0)
    def _():
        end_ref[...] = jnp.zeros_like(end_ref)

    for i in range(seg):
        up_ref[i * ROW_TILE:(i + 1) * ROW_TILE, :] = u_ref[0, pl.ds(i, ROW_TILE, stride=seg), :]
    s_ref[...] = jnp.dot(up_ref[...].astype(BF16), b_ref[0], preferred_element_type=F32)

    ar, ai = ptab_ref[0, 0], ptab_ref[1, 0]

    def local_step(i, carry):
        hr, hi = carry
        r0 = pl.multiple_of(i * ROW_TILE, ROW_TILE)
        hr, hi = _cmul_add(s_ref[pl.ds(r0, ROW_TILE), re], s_ref[pl.ds(r0, ROW_TILE), im], ar, ai, hr, hi)
        s_ref[pl.ds(r0, ROW_TILE), re] = hr
        s_ref[pl.ds(r0, ROW_TILE), im] = hi
        return hr, hi

    zero = jnp.zeros((ROW_TILE, sw), F32)
    er, ei = lax.fori_loop(0, seg, local_step, (zero, zero), unroll=8)

    row = lax.broadcasted_iota(jnp.int32, (ROW_TILE, sw), 0)
    cr = jnp.where(row == 0, pltpu.roll(end_ref[0], 1, axis=0), pltpu.roll(er, 1, axis=0))
    ci = jnp.where(row == 0, pltpu.roll(end_ref[1], 1, axis=0), pltpu.roll(ei, 1, axis=0))
    mr, mi = ptab_ref[0, seg - 1], ptab_ref[1, seg - 1]
    wr, wi = mr, mi
    for sh in (1, 2, 4):
        gr, gi = jnp.where(row >= sh, wr, 0.0), jnp.where(row >= sh, wi, 0.0)
        cr, ci = _cmul_add(cr, ci, gr, gi, pltpu.roll(cr, sh, axis=0), pltpu.roll(ci, sh, axis=0))
        wr, wi = wr * wr - wi * wi, 2.0 * wr * wi
    fr, fi = _cmul_add(er, ei, mr, mi, cr, ci)
    end_ref[0] = fr
    end_ref[1] = fi
    hfin_ref[0, 0] = fr
    hfin_ref[0, 1] = fi

    def fix_up(k, carry):
        r0 = pl.multiple_of(k * 2 * ROW_TILE, 2 * ROW_TILE)
        hs, ns = [], []
        for half in range(2):
            i = 2 * k + half
            rows = pl.ds(r0 + half * ROW_TILE, ROW_TILE)
            hr, hi = _cmul_add(s_ref[rows, re], s_ref[rows, im], ptab_ref[0, i], ptab_ref[1, i], cr, ci)
            hs.append(hr)
            ns.append(-hi)
        hb_ref[pl.ds(r0, 2 * ROW_TILE), re] = jnp.concatenate(hs, axis=0).astype(BF16)
        hb_ref[pl.ds(r0, 2 * ROW_TILE), im] = jnp.concatenate(ns, axis=0).astype(BF16)
        return carry

    lax.fori_loop(0, seg // 2, fix_up, 0, unroll=4)
    yp = jnp.dot(hb_ref[...], c_ref[0], preferred_element_type=F32) + d_ref[...] * up_ref[...]
    for i in range(seg):
        y_ref[0, pl.ds(i, ROW_TILE, stride=seg), :] = yp[i * ROW_TILE:(i + 1) * ROW_TILE, :]


def _s5_chunk(seq, hosts_big_shift):
    return min(seq, 1024 if hosts_big_shift else 2048)


def _flat_window(a):
    return a.reshape(a.shape[0] * a.shape[1], a.shape[2], a.shape[3] * a.shape[4], a.shape[5])


def _s5_prompt(u, prm, ssm_d, nb, seq, shift=None):
    ptab, b_cat, c_cat = prm
    n_chunk, cw, sw2 = b_cat.shape
    sw = sw2 // 2
    width = u.shape[1]
    seg = ptab.shape[1]
    tc = seg * ROW_TILE
    assert seq % tc == 0 and seg % 2 == 0
    grid = (nb, n_chunk, seq // tc)
    in_specs = [pl.BlockSpec((1, tc, cw), lambda b, j, t: (b, t, j)),
                pl.BlockSpec((1, cw, sw2), lambda b, j, t: (j, 0, 0)),
                pl.BlockSpec((2, seg, ROW_TILE, sw), lambda b, j, t: (0, 0, 0, j)),
                pl.BlockSpec((1, sw2, cw), lambda b, j, t: (j, 0, 0)),
                pl.BlockSpec((1, cw), lambda b, j, t: (0, j))]
    out_specs = [pl.BlockSpec((1, tc, cw), lambda b, j, t: (b, t, j)),
                 pl.BlockSpec((1, 2, ROW_TILE, sw), lambda b, j, t: (b, 0, 0, j))]
    out_shape = [jax.ShapeDtypeStruct((nb, seq, width), F32),
                 jax.ShapeDtypeStruct((nb, 2, ROW_TILE, n_chunk * sw), F32)]
    scratch = [pltpu.VMEM((tc, cw), F32), pltpu.VMEM((tc, sw2), F32),
               pltpu.VMEM((tc, sw2), BF16), pltpu.VMEM((2, ROW_TILE, sw), F32)]
    args = [u.reshape(nb, seq, width), b_cat, ptab, c_cat, ssm_d.reshape(1, width)]
    if shift is not None:
        caches, news = shift
        cf, nf = [_flat_window(c) for c in caches], [_flat_window(w) for w in news]
        hbm = pl.BlockSpec(memory_space=pl.ANY)
        n_shift = len(cf)
        in_specs += [hbm] * (2 * n_shift)
        out_specs += [hbm] * n_shift
        out_shape += [jax.ShapeDtypeStruct(c.shape, c.dtype) for c in cf]
        args += cf + nf
        max_slots = 0
        for c, w in zip(cf, nf):
            rows, _, period = _shift_plan(c.shape, w.shape, math.prod(grid))
            scratch.append(pltpu.VMEM((_shift_slots(period), rows) + c.shape[2:], c.dtype))
            max_slots = max(max_slots, _shift_slots(period))
        scratch += [pltpu.VMEM(w.shape, w.dtype) for w in nf]
        scratch += [pltpu.SemaphoreType.DMA((n_shift, max_slots)), pltpu.SemaphoreType.DMA((n_shift, max_slots)),
                    pltpu.SemaphoreType.DMA((2 * n_shift,))]
    res = pl.pallas_call(
        functools.partial(_s5_scan_kernel, with_shift=(n_shift, grid) if shift is not None else None),
        grid=grid,
        in_specs=in_specs,
        out_specs=out_specs,
        out_shape=out_shape,
        scratch_shapes=scratch,
        compiler_params=_params("arbitrary", "arbitrary", "arbitrary") if shift is not None
        else _params("parallel", "parallel", "arbitrary"),
        name="s5_scan",
    )(*args)
    y, hfin = res[0], res[1]
    last = ROW_TILE - 1
    shifted = None if shift is None else [o.reshape(c.shape) for o, c in zip(res[2:], shift[0])]
    return y.reshape(nb * seq, width), hfin[:, 0, last], hfin[:, 1, last], shifted


def _s5_sample_kernel(u_ref, h0re_ref, h0im_ref, b_ref, ptab_ref, c_ref, d_ref, y_ref, hre_ref, him_ref):
    sw = h0re_ref.shape[1]
    ar, ai = ptab_ref[0, 0, 0:1, :], ptab_ref[1, 0, 0:1, :]
    hr, hi = h0re_ref[...], h0im_ref[...]
    for t in range(u_ref.shape[0]):
        u = u_ref[t]
        bu = jnp.dot(u.astype(BF16), b_ref[0], preferred_element_type=F32)
        hr, hi = _cmul_add(bu[:, :sw], bu[:, sw:], ar, ai, hr, hi)
        hb = jnp.concatenate([hr, -hi], axis=1).astype(BF16)
        y_ref[t] = jnp.dot(hb, c_ref[0], preferred_element_type=F32) + d_ref[...] * u
    hre_ref[...] = hr
    him_ref[...] = hi


def _s5_sample(u, h0re, h0im, prm, ssm_d):
    ptab, b_cat, c_cat = prm
    n_chunk, cw, sw2 = b_cat.shape
    sw = sw2 // 2
    n_new, nb, width = u.shape
    return pl.pallas_call(
        _s5_sample_kernel,
        grid=(n_chunk,),
        in_specs=[pl.BlockSpec((n_new, nb, cw), lambda j: (0, 0, j)),
                  pl.BlockSpec((nb, sw), lambda j: (0, j)),
                  pl.BlockSpec((nb, sw), lambda j: (0, j)),
                  pl.BlockSpec((1, cw, sw2), lambda j: (j, 0, 0)),
                  pl.BlockSpec((2, 1, ROW_TILE, sw), lambda j: (0, 0, 0, j)),
                  pl.BlockSpec((1, sw2, cw), lambda j: (j, 0, 0)),
                  pl.BlockSpec((1, cw), lambda j: (0, j))],
        out_specs=[pl.BlockSpec((n_new, nb, cw), lambda j: (0, 0, j)),
                   pl.BlockSpec((nb, sw), lambda j: (0, j)),
                   pl.BlockSpec((nb, sw), lambda j: (0, j))],
        out_shape=[jax.ShapeDtypeStruct((n_new, nb, width), F32),
                   jax.ShapeDtypeStruct((nb, n_chunk * sw), F32),
                   jax.ShapeDtypeStruct((nb, n_chunk * sw), F32)],
        compiler_params=_params("parallel"),
        name="s5_sample",
    )(u, h0re, h0im, b_cat, ptab, c_cat, ssm_d.reshape(1, width))


def _glu_kernel(y_ref, g_ref, w_ref, b_ref, o_ref):
    y = y_ref[...]
    s = 0.5 * y * (1.0 + jnp.tanh(math.sqrt(2.0 / math.pi) * (y + 0.044715 * (y * y * y))))
    z = jnp.dot(s.astype(BF16), w_ref[...], preferred_element_type=F32) + b_ref[...]
    o_ref[...] = (s * _sigmoid(z) * g_ref[...].astype(F32)).astype(o_ref.dtype)


def _glu(y, gate, w, b):
    m, width = y.shape
    tm = _row_tile(m, 1024)
    row = pl.BlockSpec((tm, width), lambda i: (i, 0))
    return pl.pallas_call(
        _glu_kernel,
        grid=(m // tm,),
        in_specs=[row, row, pl.BlockSpec((width, width), lambda i: (0, 0)),
                  pl.BlockSpec((1, width), lambda i: (0, 0))],
        out_specs=row,
        out_shape=jax.ShapeDtypeStruct((m, width), BF16),
        compiler_params=_params("parallel"),
        name="glu",
    )(y, gate, w, b.reshape(1, width))


def _merge_kernel(*refs):
    o_refs, l_refs = refs[:N_DIL], refs[N_DIL:2 * N_DIL]
    ga_ref, ys_ref, ma_ref, ms_ref, wa_ref, ws_ref, out_ref = refs[2 * N_DIL:2 * N_DIL + 7]
    a_ref, otok_ref, ltok_ref = refs[2 * N_DIL + 7:]

    def token_order(ref, cols, stage):
        dil, per = ref.shape[1], ref.shape[2]
        if dil == 1:
            return ref[0, 0, :, cols].astype(F32)
        for r in range(dil):
            stage[pl.ds(r, per, stride=dil), :] = ref[0, r, :, cols].astype(F32)
        return stage[...]

    ls = [token_order(l_refs[g], slice(None), ltok_ref.at[g]) for g in range(N_DIL)]
    mx = jnp.maximum(jnp.maximum(ls[0], ls[1]), ls[2])
    es = [jnp.exp(l - mx) for l in ls]
    den = es[0] + es[1] + es[2]
    wts = [e / den for e in es]
    tm, aw = a_ref.shape
    half = out_ref.shape[1] // 2
    halves = [slice(c * half, (c + 1) * half) for c in range(2)]
    ys = ys_ref[...]
    acc = [ms_ref[:, cs].astype(F32) * jnp.dot(ys, ws_ref[:, cs], preferred_element_type=F32) for cs in halves]
    group = min(aw, MXU_COLS) // HEAD_DIM
    ya = [None, None]
    for h0 in range(0, aw // HEAD_DIM, group):
        for h in range(h0, h0 + group):
            sl = slice(h * HEAD_DIM, (h + 1) * HEAD_DIM)
            attn = None
            for g in range(N_DIL):
                term = (jnp.broadcast_to(wts[g][:, h:h + 1], (tm, HEAD_DIM))
                        * token_order(o_refs[g], sl, otok_ref.at[g]))
                attn = term if attn is None else attn + term
            a_ref[:, sl] = (attn * ga_ref[:, sl].astype(F32)).astype(BF16)
        ks = slice(h0 * HEAD_DIM, (h0 + group) * HEAD_DIM)
        for c, cs in enumerate(halves):
            part = jnp.dot(a_ref[:, ks], wa_ref[ks, cs], preferred_element_type=F32)
            ya[c] = part if ya[c] is None else ya[c] + part
    for c, cs in enumerate(halves):
        out_ref[:, cs] = (ma_ref[:, cs].astype(F32) * ya[c] + acc[c]).astype(out_ref.dtype)


def _merge(outs, lses, g_attn, y_ssm, gates, w_a, w_s):
    m, aw = g_attn.shape
    d = w_a.shape[1]
    nb = outs[0].shape[0]
    seq = m // nb
    tm = _row_tile(seq, 512)
    bps = seq // tm
    row = pl.BlockSpec((tm, aw), lambda i: (i, 0))
    ol_specs = []
    for arr in list(outs) + list(lses):
        dil = arr.shape[1]
        assert tm % (dil * ROW_TILE) == 0
        ol_specs.append(pl.BlockSpec((1, dil, tm // dil, arr.shape[3]),
                                     lambda i: (lax.div(i, bps), 0, lax.rem(i, bps), 0)))
    scratch = [pltpu.VMEM((tm, aw), BF16), pltpu.VMEM((N_DIL, tm, HEAD_DIM), F32),
               pltpu.VMEM((N_DIL, tm, HEAD_DIM), F32)]
    once = pl.Buffered(1)
    return pl.pallas_call(
        _merge_kernel,
        grid=(m // tm,),
        in_specs=ol_specs + [row, row, pl.BlockSpec((tm, d), lambda i: (i, 0)),
                             pl.BlockSpec((tm, d), lambda i: (i, 1)),
                             pl.BlockSpec((aw, d), lambda i: (0, 0), pipeline_mode=once),
                             pl.BlockSpec((aw, d), lambda i: (0, 0), pipeline_mode=once)],
        out_specs=pl.BlockSpec((tm, d), lambda i: (i, 0)),
        out_shape=jax.ShapeDtypeStruct((m, d), BF16),
        scratch_shapes=scratch,
        compiler_params=_params("parallel"),
        name="merge",
    )(*outs, *lses, g_attn, y_ssm, gates, gates, w_a, w_s)


def _out_kernel(x_ref, m_ref, w_ref, o_ref):
    o_ref[...] = x_ref[...] + jnp.dot(m_ref[...], w_ref[...], preferred_element_type=F32)


def _out_proj(x, merged, w):
    m, d = x.shape
    tm = _row_tile(m, 1024)
    tn = min(d, 1024)
    return pl.pallas_call(
        _out_kernel,
        grid=(d // tn, m // tm),
        in_specs=[pl.BlockSpec((tm, tn), lambda j, i: (i, j)),
                  pl.BlockSpec((tm, d), lambda j, i: (i, 0)),
                  pl.BlockSpec((d, tn), lambda j, i: (0, j))],
        out_specs=pl.BlockSpec((tm, tn), lambda j, i: (i, j)),
        out_shape=jax.ShapeDtypeStruct((m, d), F32),
        compiler_params=_params("parallel", "parallel"),
        name="out_proj",
    )(x, merged, w)


def _sample_front_kernel(x_ref, gw_ref, w_ref, nwq_ref, nwk_ref, cos_ref, sin_ref, o_ref, nw_ref):
    j = pl.program_id(0)
    x = x_ref[...]
    xn = x * lax.rsqrt(jnp.mean(x * x, axis=-1, keepdims=True) + NORM_EPS) * gw_ref[...]
    acc = jnp.dot(xn.astype(BF16), w_ref[...], preferred_element_type=F32)
    qk, v_end = 2 * N_DIL, 3 * N_DIL

    @pl.when(j < qk)
    def _():
        nw_ref[...] = jnp.where(j < N_DIL, nwq_ref[...], nwk_ref[...])
        o_ref[...] = acc
        _norm_rope_inplace(o_ref, nw_ref, cos_ref, sin_ref)

    @pl.when(((j >= qk) & (j < v_end)) | (j == v_end + 1))
    def _():
        o_ref[...] = acc

    @pl.when((j == v_end) | (j == v_end + 2))
    def _():
        o_ref[...] = acc * _sigmoid(acc)

    @pl.when(j > v_end + 2)
    def _():
        o_ref[...] = _sigmoid(acc)


def _sample_front(x, norm_w, w, q_norm_w, k_norm_w, cos, sin):
    m, d = x.shape
    aw = d // 2
    n_cols = w.shape[1]
    row = pl.BlockSpec((m, HEAD_DIM), lambda j: (0, 0))
    one = pl.BlockSpec((1, HEAD_DIM), lambda j: (0, 0))
    return pl.pallas_call(
        _sample_front_kernel,
        grid=(n_cols // aw,),
        in_specs=[pl.BlockSpec((m, d), lambda j: (0, 0)), pl.BlockSpec((1, d), lambda j: (0, 0)),
                  pl.BlockSpec((d, aw), lambda j: (0, j)), one, one, row, row],
        out_specs=pl.BlockSpec((m, aw), lambda j: (0, j)),
        out_shape=jax.ShapeDtypeStruct((m, n_cols), F32),
        scratch_shapes=[pltpu.VMEM((1, HEAD_DIM), F32)],
        compiler_params=_params("arbitrary"),
        name="sample_front",
    )(x, norm_w.reshape(1, d), w, q_norm_w.reshape(1, HEAD_DIM), k_norm_w.reshape(1, HEAD_DIM), cos, sin)


def _layer(x, cos, sin, lw, nb, seq, layer, depth, kv_prev, sample, shift=None):
    (norm_w, w_in, q_norm_w, k_norm_w, w_glu, b_glu, w_br_attn, w_br_ssm, w_out, ssm_d, s5prm) = lw
    m, d = x.shape
    aw = d // 2
    qkv = N_DIL * aw
    hg = aw // HEAD_DIM

    new_kv = []
    if sample is None:
        xn = _rmsnorm(x, norm_w)
        g_attn = _proj(xn, w_in, 3 * qkv, aw, "silu", BF16)
        u = _proj(xn, w_in, 3 * qkv + aw, aw, "plain", F32)
        g_ssm = _proj(xn, w_in, 3 * qkv + 2 * aw, aw, "silu", BF16)
        gates = _proj(xn, w_in, 3 * qkv + 3 * aw, 2 * d, "sigmoid", BF16)
        outs, lses = [], []
        for g in range(N_DIL):
            qd, kd, vd, kv = _qkv(xn, w_in, g, nb, seq, q_norm_w, k_norm_w, cos, sin, layer, depth,
                                  None if kv_prev is None else kv_prev[g])
            o, lse = _attn_prompt(qd, kd, vd)
            outs.append(o)
            lses.append(lse)
            new_kv.append(kv)
        y, hre, him, shifted = _s5_prompt(u, s5prm, ssm_d, nb, seq, shift)
    else:
        caches, h0re, h0im = sample
        front = _sample_front(x, norm_w, w_in, q_norm_w, k_norm_w, cos, sin)
        q, k, v = front[:, :qkv], front[:, qkv:2 * qkv], front[:, 2 * qkv:3 * qkv]
        g_attn, u = front[:, 3 * qkv:3 * qkv + aw], front[:, 3 * qkv + aw:3 * qkv + 2 * aw]
        g_ssm, gates = front[:, 3 * qkv + 2 * aw:3 * qkv + 3 * aw], front[:, 3 * qkv + 3 * aw:]
        tok = (nb, seq, N_DIL * hg, HEAD_DIM)
        o, lse = _attn_sample(q.reshape(tok), k.reshape(tok), v.reshape(tok), caches, layer)
        o = o.reshape(m, qkv)
        lse = jnp.pad(lse[..., 0].reshape(m, N_DIL, hg), ((0, 0), (0, 0), (0, HEAD_DIM - hg)))
        outs = [o[:, g * aw:(g + 1) * aw].reshape(1, 1, m, aw) for g in range(N_DIL)]
        lses = [lse[:, g].reshape(1, 1, m, HEAD_DIM) for g in range(N_DIL)]
        ut = u.reshape(nb, seq, aw).transpose(1, 0, 2)
        yt, hre, him = _s5_sample(ut, h0re, h0im, s5prm, ssm_d)
        y = yt.transpose(1, 0, 2).reshape(m, aw)
        k5, v5 = k.reshape(nb, seq, N_DIL, hg, HEAD_DIM), v.reshape(nb, seq, N_DIL, hg, HEAD_DIM)
        new_kv = [jnp.stack([k5[:, :, g], v5[:, :, g]], axis=2) for g in range(N_DIL)]
        shifted = None

    y_ssm = _glu(y, g_ssm, w_glu, b_glu)
    merged = _merge(outs, lses, g_attn, y_ssm, gates, w_br_attn, w_br_ssm)
    x_out = _out_proj(x, merged, w_out)

    n_grp = aw // SSM_GROUP_CH
    state = jnp.stack([hre, him], axis=-1).reshape(nb, n_grp, SSM_STATE, 2)
    return x_out, new_kv, state, shifted


def kernel(x_prompt, x_sample, cache_kv_d1, cache_kv_d4, cache_kv_d16, state_ssm, norm_w, w_in, q_norm_w, k_norm_w, ssm_lambda_re, ssm_lambda_im, ssm_log_dt, ssm_b_re, ssm_b_im, ssm_c_re, ssm_c_im, ssm_d, w_glu, b_glu, w_br_attn, w_br_ssm, w_out):
    nb_p, seq_p, d = x_prompt.shape
    nb_s, seq_s, _ = x_sample.shape
    depth = norm_w.shape[0]
    caches = (cache_kv_d1, cache_kv_d4, cache_kv_d16)

    cos_p, sin_p = _rope_tables(jnp.arange(seq_p, dtype=F32))
    cos_s, sin_s = _rope_tables(PAST_LEN + jnp.arange(seq_s, dtype=F32))
    cos_s, sin_s = jnp.tile(cos_s, (nb_s, 1)), jnp.tile(sin_s, (nb_s, 1))

    hp = x_prompt.reshape(nb_p * seq_p, d)
    hs = x_sample.reshape(nb_s * seq_s, d)
    lws = []
    for l in range(depth):
        s5prm = _s5_params(ssm_lambda_re[l], ssm_lambda_im[l], ssm_log_dt[l], ssm_b_re[l], ssm_b_im[l],
                           ssm_c_re[l], ssm_c_im[l], _s5_chunk(seq_p, l == 0) // ROW_TILE)
        lws.append((norm_w[l], _layer_bf16(w_in, l), q_norm_w[l], k_norm_w[l], _layer_bf16(w_glu, l), b_glu[l],
                    _layer_bf16(w_br_attn, l), _layer_bf16(w_br_ssm, l), _layer_bf16(w_out, l), ssm_d[l], s5prm))

    kv_s_new = [[] for _ in range(N_DIL)]
    ssm_s = []
    n_state = state_ssm.shape[2] * state_ssm.shape[3]
    for l in range(depth):
        h0re = state_ssm[l, ..., 0].reshape(nb_s, n_state)
        h0im = state_ssm[l, ..., 1].reshape(nb_s, n_state)
        hs, new_kv_s, st_s, _ = _layer(hs, cos_s, sin_s, lws[l], nb_s, seq_s, l, depth, None,
                                       (caches, h0re, h0im))
        for g in range(N_DIL):
            kv_s_new[g].append(new_kv_s[g])
        ssm_s.append(st_s)

    hosted = {0: list(range(N_DIL))} if depth == 1 else {0: [N_DIL - 1], 1: list(range(N_DIL - 1))}
    news = [jnp.stack(n) for n in kv_s_new]
    kv_p, kv_s, ssm_p = None, [None] * N_DIL, []
    for l in range(depth):
        gs = hosted.get(l)
        shift = ([caches[g] for g in gs], [news[g] for g in gs]) if gs else None
        hp, kv_p, st, shifted = _layer(hp, cos_p, sin_p, lws[l], nb_p, seq_p, l, depth, kv_p, None, shift)
        for g, arr in zip(gs or [], shifted or []):
            kv_s[g] = arr
        ssm_p.append(st)

    return (hp.reshape(x_prompt.shape), hs.reshape(x_sample.shape), kv_p[0], kv_p[1], kv_p[2], jnp.stack(ssm_p),
            kv_s[0], kv_s[1], kv_s[2], jnp.stack(ssm_s))
```

```python
import functools
import math

import jax
import jax.numpy as jnp
from jax import lax
from jax.experimental import pallas as pl
from jax.experimental.pallas import tpu as pltpu

F32 = jnp.float32
BF16 = jnp.bfloat16

HEAD_DIM = 128
WINDOWS = (128, 512, 2048)
DILATIONS = (1, 4, 16)
N_DIL = 3
SPAN = 128
SSM_GROUP_CH = 16
SSM_STATE = 64
GROUPS_PER_CHUNK = 8
ROPE_THETA = 10000.0
NORM_EPS = 1e-6
PAST_LEN = 16384
NEG_BIG = -1e30
VMEM_LIMIT = 58 * 1024 * 1024
KV_ROWS_PER_COPY = 512
ROW_TILE = 8
MXU_COLS = 256
QK_ROW_CHUNK = 64
KV_COPY_BYTES = 4 * 1024 * 1024
POWER_BLOCK = 8


def _params(*sem):
    return pltpu.CompilerParams(dimension_semantics=sem, vmem_limit_bytes=VMEM_LIMIT)


def _sigmoid(x):
    return 1.0 / (1.0 + jnp.exp(-x))


def _row_tile(m, cap):
    t = min(m, cap)
    assert m % t == 0, (m, t)
    return t


def _rmsnorm_kernel(x_ref, w_ref, o_ref):
    x = x_ref[...]
    y = x * lax.rsqrt(jnp.mean(x * x, axis=-1, keepdims=True) + NORM_EPS)
    o_ref[...] = (y * w_ref[...]).astype(o_ref.dtype)


def _rmsnorm(x, w):
    m, d = x.shape
    tm = _row_tile(m, 512)
    return pl.pallas_call(
        _rmsnorm_kernel,
        grid=(m // tm,),
        in_specs=[pl.BlockSpec((tm, d), lambda i: (i, 0)),
                  pl.BlockSpec((1, d), lambda i: (0, 0))],
        out_specs=pl.BlockSpec((tm, d), lambda i: (i, 0)),
        out_shape=jax.ShapeDtypeStruct((m, d), BF16),
        compiler_params=_params("parallel"),
        name="rmsnorm",
    )(x, w.reshape(1, d))


def _cast_kernel(w_ref, o_ref):
    o_ref[...] = w_ref[...].astype(o_ref.dtype)


def _layer_bf16(w, layer):
    _, rows, cols = w.shape
    tr, tc = min(rows, 512), min(cols, 2048)
    assert rows % tr == 0 and cols % tc == 0
    return pl.pallas_call(
        _cast_kernel,
        grid=(rows // tr, cols // tc),
        in_specs=[pl.BlockSpec((None, tr, tc), lambda i, j: (layer, i, j))],
        out_specs=pl.BlockSpec((tr, tc), lambda i, j: (i, j)),
        out_shape=jax.ShapeDtypeStruct((rows, cols), BF16),
        compiler_params=_params("parallel", "parallel"),
        name="cast_bf16",
    )(w)


def _norm_rope_inplace(t_ref, nw_ref, cos_ref, sin_ref, head_major=False):
    rows = cos_ref.shape[0]
    n_heads = t_ref.shape[0] // rows if head_major else t_ref.shape[1] // HEAD_DIM
    rc = min(rows, QK_ROW_CHUNK)
    half = HEAD_DIM // 2
    nw = nw_ref[...]
    nw_swapped = pltpu.roll(nw, half, axis=1)

    def chunk(c, carry):
        r0 = pl.multiple_of(c * rc, rc)
        cs = cos_ref[pl.ds(r0, rc), :] * nw
        ss = sin_ref[pl.ds(r0, rc), :] * nw_swapped
        for h in range(n_heads):
            idx = (pl.ds(h * rows + r0, rc), slice(None)) if head_major else (
                pl.ds(r0, rc), slice(h * HEAD_DIM, (h + 1) * HEAD_DIM))
            a = t_ref[idx]
            r = lax.rsqrt(jnp.mean(a * a, axis=-1, keepdims=True) + NORM_EPS)
            t_ref[idx] = (a * cs + pltpu.roll(a, half, axis=1) * ss) * r
        return carry

    lax.fori_loop(0, rows // rc, chunk, 0)


def _proj_kernel(x_ref, w_ref, o_ref, *, epilogue):
    acc = jnp.dot(x_ref[...], w_ref[...], preferred_element_type=F32)
    if epilogue == "silu":
        acc = acc * _sigmoid(acc)
    elif epilogue == "sigmoid":
        acc = _sigmoid(acc)
    o_ref[...] = acc.astype(o_ref.dtype)


def _proj(xn, w, col0, ncols, epilogue, out_dtype):
    m, d = xn.shape
    tm = _row_tile(m, 1024)
    tn = min(ncols, 1024)
    assert ncols % tn == 0 and col0 % tn == 0
    jb = col0 // tn
    return pl.pallas_call(
        functools.partial(_proj_kernel, epilogue=epilogue),
        grid=(m // tm, ncols // tn),
        in_specs=[pl.BlockSpec((tm, d), lambda i, j: (i, 0)),
                  pl.BlockSpec((d, tn), lambda i, j: (0, jb + j))],
        out_specs=pl.BlockSpec((tm, tn), lambda i, j: (i, j)),
        out_shape=jax.ShapeDtypeStruct((m, ncols), out_dtype),
        compiler_params=_params("parallel", "arbitrary"),
        name="proj_" + epilogue,
    )(xn, w)


def _norm_rope_head(a, ssq, nw_ref, cos_ref, sin_ref, store):
    rows = a.shape[0]
    rc = min(rows, QK_ROW_CHUNK)
    half = HEAD_DIM // 2
    nw = nw_ref[...]
    nw_swapped = pltpu.roll(nw, half, axis=1)
    for r0 in range(0, rows, rc):
        ac = a[r0:r0 + rc]
        cs = cos_ref[r0:r0 + rc, :] * nw
        ss = sin_ref[r0:r0 + rc, :] * nw_swapped
        r = lax.rsqrt(ssq[r0:r0 + rc] * (1.0 / HEAD_DIM) + NORM_EPS)
        store(r0, rc, (ac * cs + pltpu.roll(ac, half, axis=1) * ss) * r)


def _project_dilated(x, w_ref, nw_ref, cos_ref, sin_ref, hm_ref, d_ref, dil):
    tm = x.shape[0]
    n_heads = w_ref.shape[1] // HEAD_DIM
    per = tm // dil
    acc = jnp.dot(x, w_ref[...], preferred_element_type=F32)
    ones = jnp.ones((HEAD_DIM, HEAD_DIM), BF16)
    for h in range(n_heads):
        sl = slice(h * HEAD_DIM, (h + 1) * HEAD_DIM)
        a = acc[:, sl]
        if nw_ref is None:
            hm_ref[h * tm:(h + 1) * tm, :] = a
        else:
            def store(r0, rc, chunk, h=h):
                hm_ref[h * tm + r0:h * tm + r0 + rc, :] = chunk

            ssq = jnp.dot((a * a).astype(BF16), ones, preferred_element_type=F32)
            _norm_rope_head(a, ssq, nw_ref, cos_ref, sin_ref, store)
        for r in range(dil):
            rows = pl.ds(h * tm + r, per, stride=dil) if dil > 1 else pl.ds(h * tm, per)
            d_ref[0, r, :, sl] = hm_ref[rows, :].astype(d_ref.dtype)


def _qkv_kernel(x_ref, wq_ref, wk_ref, wv_ref, nwq_ref, nwk_ref, cos_ref, sin_ref, kv_in_ref,
                qd_ref, kd_ref, vd_ref, kv_ref, hk_ref, hv_ref, rows_ref, sem,
                *, dil, layer, first_kept, kept_rows):
    b, i = pl.program_id(0), pl.program_id(1)
    tm = x_ref.shape[0]
    n_heads = wk_ref.shape[1] // HEAD_DIM
    kb = rows_ref.shape[0]
    x = x_ref[...]
    _project_dilated(x, wq_ref, nwq_ref, cos_ref, sin_ref, hv_ref, qd_ref, dil)
    _project_dilated(x, wk_ref, nwk_ref, cos_ref, sin_ref, hk_ref, kd_ref, dil)
    _project_dilated(x, wv_ref, None, None, None, hv_ref, vd_ref, dil)

    n_parts = kept_rows // kb

    def copy_out(bb, ii, part):
        dst = kv_ref.at[layer, bb, pl.ds((ii - first_kept) * kept_rows + part * kb, kb)]
        return pltpu.make_async_copy(rows_ref, dst, sem.at[0])

    @pl.when(i >= first_kept)
    def _():
        for part in range(n_parts):
            if part == 0:
                @pl.when((i > first_kept) | (b > 0))
                def _():
                    prev_same = i > first_kept
                    copy_out(jnp.where(prev_same, b, b - 1),
                             jnp.where(prev_same, i - 1, pl.num_programs(1) - 1), n_parts - 1).wait()
            else:
                copy_out(b, i, part - 1).wait()
            base = tm - kept_rows + part * kb

            def tokens(j, carry, base=base):
                t0 = pl.multiple_of(j * ROW_TILE, ROW_TILE)
                for which, hm_ref in ((0, hk_ref), (1, hv_ref)):
                    tiles = [hm_ref[pl.ds(h * tm + base + t0, ROW_TILE), :] for h in range(n_heads)]
                    rows_ref[pl.ds(t0, ROW_TILE), which] = jnp.swapaxes(jnp.stack(tiles, axis=0), 0, 1)
                return carry

            lax.fori_loop(0, kb // ROW_TILE, tokens, 0, unroll=2)
            copy_out(b, i, part).start()

    @pl.when((b == pl.num_programs(0) - 1) & (i == pl.num_programs(1) - 1))
    def _():
        copy_out(b, i, n_parts - 1).wait()


def _qkv(xn, w, g, nb, seq, q_norm_w, k_norm_w, cos, sin, layer, depth, kv_prev):
    m, d = xn.shape
    aw = d // 2
    hg = aw // HEAD_DIM
    dil = DILATIONS[g]
    keep = min(WINDOWS[g], seq)
    tm = _row_tile(seq, 1024)
    per = tm // dil
    kept_rows = min(keep, tm)
    kb = min(kept_rows, KV_ROWS_PER_COPY)
    bps = seq // tm
    first_kept = (seq - keep) // tm if keep >= tm else bps - 1
    assert per % 16 == 0 and keep % kept_rows == 0 and (seq - keep) % kept_rows == 0
    assert kept_rows % kb == 0 and kb % ROW_TILE == 0
    once = pl.Buffered(1)
    nw = lambda a: a.reshape(1, HEAD_DIM)
    dshape = jax.ShapeDtypeStruct((nb, dil, seq // dil, aw), BF16)
    staging = pltpu.VMEM((hg * tm, HEAD_DIM), F32)

    wspec = lambda col: pl.BlockSpec((d, aw), lambda b, i: (0, col), pipeline_mode=once)
    tspec = pl.BlockSpec((tm, HEAD_DIM), lambda b, i: (i, 0))
    dspec = pl.BlockSpec((1, dil, per, aw), lambda b, i: (b, 0, i, 0))
    hbm = pl.BlockSpec(memory_space=pl.ANY)
    kv_shape = jax.ShapeDtypeStruct((depth, nb, keep, 2, hg, HEAD_DIM), F32)
    if kv_prev is None:
        kv_prev = jnp.zeros(kv_shape.shape, kv_shape.dtype)
    nspec = pl.BlockSpec((1, HEAD_DIM), lambda b, i: (0, 0))
    return pl.pallas_call(
        functools.partial(_qkv_kernel, dil=dil, layer=layer, first_kept=first_kept, kept_rows=kept_rows),
        grid=(nb, bps),
        in_specs=[pl.BlockSpec((tm, d), lambda b, i: (b * bps + i, 0)),
                  wspec(g), wspec(N_DIL + g), wspec(2 * N_DIL + g), nspec, nspec, tspec, tspec, hbm],
        out_specs=[dspec, dspec, dspec, hbm],
        out_shape=[dshape, dshape, dshape, kv_shape],
        scratch_shapes=[staging, staging, pltpu.VMEM((kb, 2, hg, HEAD_DIM), F32),
                        pltpu.SemaphoreType.DMA((1,))],
        input_output_aliases={8: 3},
        compiler_params=_params("arbitrary", "arbitrary"),
        name="qkv_d%d" % dil,
    )(xn, w, w, w, nw(q_norm_w), nw(k_norm_w), cos, sin, kv_prev)


def _rope_tables(pos):
    half = HEAD_DIM // 2
    inv_freq = jnp.power(ROPE_THETA, -jnp.arange(half, dtype=F32) * (2.0 / HEAD_DIM))
    ang = pos[:, None] * inv_freq[None, :]
    cos, sin = jnp.cos(ang), jnp.sin(ang)
    return jnp.concatenate([cos, cos], axis=-1), jnp.concatenate([-sin, sin], axis=-1)


def _attn_prompt_kernel(q_ref, kp_ref, kc_ref, vp_ref, vc_ref, o_ref, lse_ref):
    not_first = pl.program_id(2) > 0
    a = lax.broadcasted_iota(jnp.int32, (SPAN, 2 * SPAN), 0)
    c = lax.broadcasted_iota(jnp.int32, (SPAN, 2 * SPAN), 1)
    band = (c >= a) & (c <= a + SPAN)
    band_first = band & ((c >= SPAN) | not_first)
    lane = lax.broadcasted_iota(jnp.int32, (SPAN, HEAD_DIM), 1)
    scale = HEAD_DIM ** -0.5
    for j in range(q_ref.shape[2] // SPAN):
        rows = slice(j * SPAN, (j + 1) * SPAN)
        lse_tile = jnp.zeros((SPAN, HEAD_DIM), F32)
        for h in range(q_ref.shape[3] // HEAD_DIM):
            sl = slice(h * HEAD_DIM, (h + 1) * HEAD_DIM)
            q = q_ref[0, 0, rows, sl]
            if j == 0:
                k = jnp.concatenate([kp_ref[0, 0, :, sl], kc_ref[0, 0, rows, sl]], axis=0)
                v = jnp.concatenate([vp_ref[0, 0, :, sl], vc_ref[0, 0, rows, sl]], axis=0)
            else:
                k = kc_ref[0, 0, (j - 1) * SPAN:(j + 1) * SPAN, sl]
                v = vc_ref[0, 0, (j - 1) * SPAN:(j + 1) * SPAN, sl]
            s = lax.dot_general(q, k, (((1,), (1,)), ((), ())), preferred_element_type=F32) * scale
            s = jnp.where(band_first if j == 0 else band, s, NEG_BIG)
            m = jnp.max(s, axis=-1, keepdims=True)
            p = jnp.exp(s - m)
            l = jnp.sum(p, axis=-1, keepdims=True)
            o = jnp.dot(p.astype(BF16), v, preferred_element_type=F32) / l
            o_ref[0, 0, rows, sl] = o.astype(o_ref.dtype)
            lse_tile = jnp.where(lane == h, m + jnp.log(l), lse_tile)
        lse_ref[0, 0, rows, :] = lse_tile


def _attn_prompt(q, k, v):
    nb, dil, length, aw = q.shape
    n_blk = length // SPAN
    qb = min(4, n_blk)
    assert length % SPAN == 0 and n_blk % qb == 0 and aw // HEAD_DIM <= HEAD_DIM
    cur = pl.BlockSpec((1, 1, qb * SPAN, aw), lambda b, r, n: (b, r, n, 0))
    prev = pl.BlockSpec((1, 1, SPAN, aw), lambda b, r, n: (b, r, jnp.maximum(qb * n - 1, 0), 0))
    return pl.pallas_call(
        _attn_prompt_kernel,
        grid=(nb, dil, n_blk // qb),
        in_specs=[cur, prev, cur, prev, cur],
        out_specs=[cur, pl.BlockSpec((1, 1, qb * SPAN, HEAD_DIM), lambda b, r, n: (b, r, n, 0))],
        out_shape=[jax.ShapeDtypeStruct(q.shape, BF16),
                   jax.ShapeDtypeStruct((nb, dil, length, HEAD_DIM), F32)],
        compiler_params=_params("parallel", "parallel", "arbitrary"),
        name="attn_prompt_d%d" % dil,
    )(q, k, k, v, v)


def _attn_sample_kernel(q_ref, kn_ref, vn_ref, c1_ref, c4_ref, c16_ref, o_ref, lse_ref):
    n_new, n_heads, _ = q_ref.shape
    hg = n_heads // N_DIL
    scale = HEAD_DIM ** -0.5
    caches = (c1_ref, c4_ref, c16_ref)
    for g in range(N_DIL):
        hs = slice(g * hg, (g + 1) * hg)
        for t in range(n_new):
            q = q_ref[t, hs, :]
            if DILATIONS[g] == 1:
                kc, vc = caches[g][:, 0], caches[g][:, 1]
                rows = lax.broadcasted_iota(jnp.int32, (kc.shape[0], hg, 1), 0)
                s = jnp.sum(kc * q[None], axis=-1, keepdims=True) * scale
                s = jnp.where(rows >= t, s, NEG_BIG)
                new = range(t + 1)
            else:
                kc, vc = caches[g][:, t, 0], caches[g][:, t, 1]
                s = jnp.sum(kc * q[None], axis=-1, keepdims=True) * scale
                new = (t,)
            s_new = [jnp.sum(kn_ref[u, hs, :] * q, axis=-1, keepdims=True) * scale for u in new]
            m = jnp.max(s, axis=0)
            for sn in s_new:
                m = jnp.maximum(m, sn)
            p = jnp.exp(s - m[None])
            l = jnp.sum(p, axis=0)
            acc = jnp.sum(p * vc, axis=0)
            for u, sn in zip(new, s_new):
                pn = jnp.exp(sn - m)
                l = l + pn
                acc = acc + pn * vn_ref[u, hs, :]
            o_ref[t, hs, :] = acc / l
            lse_ref[t, hs, :] = jnp.broadcast_to(m + jnp.log(l), (hg, HEAD_DIM))


def _attn_sample(q, kn, vn, caches, layer):
    nb, n_new, n_heads, _ = q.shape
    hg = n_heads // N_DIL
    views, specs = [], []
    for g in range(N_DIL):
        dil = DILATIONS[g]
        depth, _, buf = caches[g].shape[:3]
        assert buf == WINDOWS[g] and (dil == 1 or dil % n_new == 0)
        if dil == 1:
            views.append(caches[g])
            specs.append(pl.BlockSpec((None, None, buf, 2, hg, HEAD_DIM),
                                      lambda b: (layer, b, 0, 0, 0, 0)))
        else:
            views.append(caches[g].reshape(depth, nb, SPAN, dil, 2, hg, HEAD_DIM))
            specs.append(pl.BlockSpec((None, None, SPAN, n_new, 2, hg, HEAD_DIM),
                                      lambda b: (layer, b, 0, 0, 0, 0, 0)))
    tok = pl.BlockSpec((None, n_new, n_heads, HEAD_DIM), lambda b: (b, 0, 0, 0))
    return pl.pallas_call(
        _attn_sample_kernel,
        grid=(nb,),
        in_specs=[tok, tok, tok] + specs,
        out_specs=[tok, tok],
        out_shape=[jax.ShapeDtypeStruct(q.shape, F32)] * 2,
        compiler_params=_params("parallel"),
        name="attn_sample",
    )(q, kn, vn, *views)


def _s5_params_kernel(lre_ref, lim_ref, ldt_ref, bre_ref, bim_ref, ptab_ref, bbre_ref, bbim_ref):
    dt = jnp.exp(ldt_ref[...])
    lre, lim = lre_ref[...], lim_ref[...]
    xr, xi = lre * dt, lim * dt
    mag = jnp.exp(xr)
    ar, ai = mag * jnp.cos(xi), mag * jnp.sin(xi)
    shape = (ROW_TILE, xr.shape[1])
    abr, abi = jnp.broadcast_to(ar, shape), jnp.broadcast_to(ai, shape)
    pr, pi = abr, abi
    for i in range(POWER_BLOCK):
        if i:
            pr, pi = pr * abr - pi * abi, pr * abi + pi * abr
        ptab_ref[0, i] = pr
        ptab_ref[1, i] = pi
    mr, mi = pr, pi

    def next_block(k, carry):
        for r in range(POWER_BLOCK):
            src, dst = (k - 1) * POWER_BLOCK + r, k * POWER_BLOCK + r
            qr, qi = ptab_ref[0, src], ptab_ref[1, src]
            ptab_ref[0, dst] = qr * mr - qi * mi
            ptab_ref[1, dst] = qr * mi + qi * mr
        return carry

    assert ptab_ref.shape[1] % POWER_BLOCK == 0
    lax.fori_loop(1, ptab_ref.shape[1] // POWER_BLOCK, next_block, 0)
    nr, ni = ar - 1.0, ai
    den = lre * lre + lim * lim
    fr = (nr * lre + ni * lim) / den
    fi = (ni * lre - nr * lim) / den
    bre, bim = bre_ref[...], bim_ref[...]
    bbre_ref[...] = fr * bre - fi * bim
    bbim_ref[...] = fr * bim + fi * bre


def _s5_params(lam_re, lam_im, log_dt, b_re, b_im, c_re, c_im, seg_len):
    n_grp, n_st = lam_re.shape
    n_ch = b_re.shape[2]
    n = n_grp * n_st
    gc = GROUPS_PER_CHUNK
    n_chunk = n_grp // gc
    sw = gc * n_st
    to_lanes = lambda b: b.transpose(2, 0, 1).reshape(n_ch, n)
    lane = lambda rows: pl.BlockSpec((rows, sw), lambda j: (0, j))
    ptab, bb_re, bb_im = pl.pallas_call(
        _s5_params_kernel,
        grid=(n_chunk,),
        in_specs=[lane(1)] * 3 + [lane(n_ch)] * 2,
        out_specs=[pl.BlockSpec((2, seg_len, ROW_TILE, sw), lambda j: (0, 0, 0, j)), lane(n_ch), lane(n_ch)],
        out_shape=[jax.ShapeDtypeStruct((2, seg_len, ROW_TILE, n), F32),
                   jax.ShapeDtypeStruct((n_ch, n), F32), jax.ShapeDtypeStruct((n_ch, n), F32)],
        compiler_params=_params("parallel"),
        name="s5_params",
    )(lam_re.reshape(1, n), lam_im.reshape(1, n),
      jnp.broadcast_to(log_dt[:, None], (n_grp, n_st)).reshape(1, n), to_lanes(b_re), to_lanes(b_im))

    eye = jnp.eye(gc, dtype=F32)

    def b_blocks(bb):
        bb = bb.reshape(n_ch, n_chunk, gc, n_st)
        blk = jnp.einsum("cjgp,gh->jgchp", bb, eye)
        return blk.reshape(n_chunk, gc * n_ch, gc * n_st).astype(BF16)

    def c_blocks(cc):
        cc = cc.reshape(n_chunk, gc, n_ch, n_st)
        blk = jnp.einsum("jgcp,gh->jgphc", cc, eye)
        return blk.reshape(n_chunk, gc * n_st, gc * n_ch).astype(BF16)

    b_cat = jnp.concatenate([b_blocks(bb_re), b_blocks(bb_im)], axis=2)
    c_cat = jnp.concatenate([c_blocks(c_re), c_blocks(c_im)], axis=1)
    return ptab, b_cat, c_cat


def _cmul_add(xr, xi, ar, ai, br, bi):
    return xr + ar * br - ai * bi, xi + ar * bi + ai * br


def _shift_plan(cache_shape, new_shape, n_steps):
    lb_n, buf = cache_shape[:2]
    keep = buf - new_shape[1]
    row_bytes = 4 * math.prod(cache_shape[2:])
    for per_row in range(1, keep + 1):
        n_jobs = lb_n * per_row
        if keep % per_row == 0 and keep // per_row * row_bytes <= KV_COPY_BYTES and n_steps % n_jobs == 0:
            return keep // per_row, per_row, n_steps // n_jobs
    raise ValueError("window shift does not fit the grid")


def _shift_lag(period):
    return 1 if period == 2 else 2


def _shift_slots(period):
    return 4 if period == 1 else 2


def _window_shift_step(step, n_steps, caches, news, outs, rings, nbufs, in_sem, out_sem, new_sem):
    last = step == n_steps - 1
    for g in range(len(caches)):
        cache, out, ring = caches[g], outs[g], rings[g]
        n_new = news[g].shape[1]
        keep = cache.shape[1] - n_new
        rows, per_row, period = _shift_plan(cache.shape, news[g].shape, n_steps)
        n_jobs = cache.shape[0] * per_row
        slots = ring.shape[0]
        lag = _shift_lag(period)
        assert n_jobs >= slots and (period == 1 or lag < period)

        def load(k, cache=cache, ring=ring, g=g, rows=rows, per_row=per_row, n_new=n_new, slots=slots):
            src = cache.at[lax.div(k, per_row), pl.ds(n_new + lax.rem(k, per_row) * rows, rows)]
            return pltpu.make_async_copy(src, ring.at[lax.rem(k, slots)], in_sem.at[g, lax.rem(k, slots)])

        def store(k, out=out, ring=ring, g=g, rows=rows, per_row=per_row, slots=slots):
            dst = out.at[lax.div(k, per_row), pl.ds(lax.rem(k, per_row) * rows, rows)]
            return pltpu.make_async_copy(ring.at[lax.rem(k, slots)], dst, out_sem.at[g, lax.rem(k, slots)])

        def issue(k, load=load, store=store, slots=slots):
            @pl.when(k >= slots)
            def _():
                store(k - slots).wait()
            load(k).start()

        def forward(k, load=load, store=store):
            load(k).wait()
            store(k).start()

        def drain(k, store=store, slots=slots):
            for back in range(slots - 1, -1, -1):
                store(k - back).wait()

        if period == 1:
            issue(step)

            @pl.when(step >= lag)
            def _(forward=forward, lag=lag):
                forward(step - lag)

            @pl.when(last)
            def _(forward=forward, drain=drain, lag=lag):
                for back in range(lag - 1, -1, -1):
                    forward(step - back)
                drain(step)
        else:
            k = lax.div(step, period)
            phase = lax.rem(step, period)

            @pl.when(phase == 0)
            def _(issue=issue, k=k):
                issue(k)

            @pl.when(phase == lag)
            def _(forward=forward, k=k):
                forward(k)

            @pl.when(last)
            def _(drain=drain, k=k):
                drain(k)

        new_in = pltpu.make_async_copy(news[g], nbufs[g], new_sem.at[g])
        new_out = pltpu.make_async_copy(nbufs[g], out.at[:, pl.ds(keep, n_new)], new_sem.at[len(caches) + g])

        @pl.when(step == 0)
        def _(new_in=new_in, new_out=new_out):
            new_in.start()
            new_in.wait()
            new_out.start()

        @pl.when(last)
        def _(new_out=new_out):
            new_out.wait()


def _s5_scan_kernel(u_ref, b_ref, ptab_ref, c_ref, d_ref, *rest, with_shift):
    if with_shift:
        n, grid = with_shift
        caches, news = rest[:n], rest[n:2 * n]
        y_ref, hfin_ref = rest[2 * n:2 * n + 2]
        outs = rest[2 * n + 2:3 * n + 2]
        up_ref, s_ref, hb_ref, end_ref = rest[3 * n + 2:3 * n + 6]
        rings, nbufs = rest[3 * n + 6:4 * n + 6], rest[4 * n + 6:5 * n + 6]
        in_sem, out_sem, new_sem = rest[5 * n + 6:]
        step = (pl.program_id(0) * grid[1] + pl.program_id(1)) * grid[2] + pl.program_id(2)
        _window_shift_step(step, math.prod(grid), caches, news, outs, rings, nbufs, in_sem, out_sem, new_sem)
    else:
        y_ref, hfin_ref, up_ref, s_ref, hb_ref, end_ref = rest
    tc, sw2 = s_ref.shape
    sw = sw2 // 2
    seg = tc // ROW_TILE
    re, im = slice(0, sw), slice(sw, sw2)

    @pl.when(pl.program_id(2) == 0)
    def _():
        end_ref[...] = jnp.zeros_like(end_ref)

    for i in range(seg):
        up_ref[i * ROW_TILE:(i + 1) * ROW_TILE, :] = u_ref[0, pl.ds(i, ROW_TILE, stride=seg), :]
    s_ref[...] = jnp.dot(up_ref[...].astype(BF16), b_ref[0], preferred_element_type=F32)

    ar, ai = ptab_ref[0, 0], ptab_ref[1, 0]

    def local_step(i, carry):
        hr, hi = carry
        r0 = pl.multiple_of(i * ROW_TILE, ROW_TILE)
        hr, hi = _cmul_add(s_ref[pl.ds(r0, ROW_TILE), re], s_ref[pl.ds(r0, ROW_TILE), im], ar, ai, hr, hi)
        s_ref[pl.ds(r0, ROW_TILE), re] = hr
        s_ref[pl.ds(r0, ROW_TILE), im] = hi
        return hr, hi

    zero = jnp.zeros((ROW_TILE, sw), F32)
    er, ei = lax.fori_loop(0, seg, local_step, (zero, zero), unroll=8)

    row = lax.broadcasted_iota(jnp.int32, (ROW_TILE, sw), 0)
    cr = jnp.where(row == 0, pltpu.roll(end_ref[0], 1, axis=0), pltpu.roll(er, 1, axis=0))
    ci = jnp.where(row == 0, pltpu.roll(end_ref[1], 1, axis=0), pltpu.roll(ei, 1, axis=0))
    mr, mi = ptab_ref[0, seg - 1], ptab_ref[1, seg - 1]
    wr, wi = mr, mi
    for sh in (1, 2, 4):
        gr, gi = jnp.where(row >= sh, wr, 0.0), jnp.where(row >= sh, wi, 0.0)
        cr, ci = _cmul_add(cr, ci, gr, gi, pltpu.roll(cr, sh, axis=0), pltpu.roll(ci, sh, axis=0))
        wr, wi = wr * wr - wi * wi, 2.0 * wr * wi
    fr, fi = _cmul_add(er, ei, mr, mi, cr, ci)
    end_ref[0] = fr
    end_ref[1] = fi
    hfin_ref[0, 0] = fr
    hfin_ref[0, 1] = fi

    def fix_up(k, carry):
        r0 = pl.multiple_of(k * 2 * ROW_TILE, 2 * ROW_TILE)
        hs, ns = [], []
        for half in range(2):
            i = 2 * k + half
            rows = pl.ds(r0 + half * ROW_TILE, ROW_TILE)
            hr, hi = _cmul_add(s_ref[rows, re], s_ref[rows, im], ptab_ref[0, i], ptab_ref[1, i], cr, ci)
            hs.append(hr)
            ns.append(-hi)
        hb_ref[pl.ds(r0, 2 * ROW_TILE), re] = jnp.concatenate(hs, axis=0).astype(BF16)
        hb_ref[pl.ds(r0, 2 * ROW_TILE), im] = jnp.concatenate(ns, axis=0).astype(BF16)
        return carry

    lax.fori_loop(0, seg // 2, fix_up, 0, unroll=4)
    yp = jnp.dot(hb_ref[...], c_ref[0], preferred_element_type=F32) + d_ref[...] * up_ref[...]
    for i in range(seg):
        y_ref[0, pl.ds(i, ROW_TILE, stride=seg), :] = yp[i * ROW_TILE:(i + 1) * ROW_TILE, :]


def _s5_chunk(seq, hosts_big_shift):
    return min(seq, 1024 if hosts_big_shift else 2048)


def _flat_window(a):
    return a.reshape(a.shape[0] * a.shape[1], a.shape[2], a.shape[3] * a.shape[4], a.shape[5])


def _s5_prompt(u, prm, ssm_d, nb, seq, shift=None):
    ptab, b_cat, c_cat = prm
    n_chunk, cw, sw2 = b_cat.shape
    sw = sw2 // 2
    width = u.shape[1]
    seg = ptab.shape[1]
    tc = seg * ROW_TILE
    assert seq % tc == 0 and seg % 2 == 0
    grid = (nb, n_chunk, seq // tc)
    in_specs = [pl.BlockSpec((1, tc, cw), lambda b, j, t: (b, t, j)),
                pl.BlockSpec((1, cw, sw2), lambda b, j, t: (j, 0, 0)),
                pl.BlockSpec((2, seg, ROW_TILE, sw), lambda b, j, t: (0, 0, 0, j)),
                pl.BlockSpec((1, sw2, cw), lambda b, j, t: (j, 0, 0)),
                pl.BlockSpec((1, cw), lambda b, j, t: (0, j))]
    out_specs = [pl.BlockSpec((1, tc, cw), lambda b, j, t: (b, t, j)),
                 pl.BlockSpec((1, 2, ROW_TILE, sw), lambda b, j, t: (b, 0, 0, j))]
    out_shape = [jax.ShapeDtypeStruct((nb, seq, width), F32),
                 jax.ShapeDtypeStruct((nb, 2, ROW_TILE, n_chunk * sw), F32)]
    scratch = [pltpu.VMEM((tc, cw), F32), pltpu.VMEM((tc, sw2), F32),
               pltpu.VMEM((tc, sw2), BF16), pltpu.VMEM((2, ROW_TILE, sw), F32)]
    args = [u.reshape(nb, seq, width), b_cat, ptab, c_cat, ssm_d.reshape(1, width)]
    if shift is not None:
        caches, news = shift
        cf, nf = [_flat_window(c) for c in caches], [_flat_window(w) for w in news]
        hbm = pl.BlockSpec(memory_space=pl.ANY)
        n_shift = len(cf)
        in_specs += [hbm] * (2 * n_shift)
        out_specs += [hbm] * n_shift
        out_shape += [jax.ShapeDtypeStruct(c.shape, c.dtype) for c in cf]
        args += cf + nf
        max_slots = 0
        for c, w in zip(cf, nf):
            rows, _, period = _shift_plan(c.shape, w.shape, math.prod(grid))
            scratch.append(pltpu.VMEM((_shift_slots(period), rows) + c.shape[2:], c.dtype))
            max_slots = max(max_slots, _shift_slots(period))
        scratch += [pltpu.VMEM(w.shape, w.dtype) for w in nf]
        scratch += [pltpu.SemaphoreType.DMA((n_shift, max_slots)), pltpu.SemaphoreType.DMA((n_shift, max_slots)),
                    pltpu.SemaphoreType.DMA((2 * n_shift,))]
    res = pl.pallas_call(
        functools.partial(_s5_scan_kernel, with_shift=(n_shift, grid) if shift is not None else None),
        grid=grid,
        in_specs=in_specs,
        out_specs=out_specs,
        out_shape=out_shape,
        scratch_shapes=scratch,
        compiler_params=_params("arbitrary", "arbitrary", "arbitrary") if shift is not None
        else _params("parallel", "parallel", "arbitrary"),
        name="s5_scan",
    )(*args)
    y, hfin = res[0], res[1]
    last = ROW_TILE - 1
    shifted = None if shift is None else [o.reshape(c.shape) for o, c in zip(res[2:], shift[0])]
    return y.reshape(nb * seq, width), hfin[:, 0, last], hfin[:, 1, last], shifted


def _s5_sample_kernel(u_ref, h0re_ref, h0im_ref, b_ref, ptab_ref, c_ref, d_ref, y_ref, hre_ref, him_ref):
    sw = h0re_ref.shape[1]
    ar, ai = ptab_ref[0, 0, 0:1, :], ptab_ref[1, 0, 0:1, :]
    hr, hi = h0re_ref[...], h0im_ref[...]
    for t in range(u_ref.shape[0]):
        u = u_ref[t]
        bu = jnp.dot(u.astype(BF16), b_ref[0], preferred_element_type=F32)
        hr, hi = _cmul_add(bu[:, :sw], bu[:, sw:], ar, ai, hr, hi)
        hb = jnp.concatenate([hr, -hi], axis=1).astype(BF16)
        y_ref[t] = jnp.dot(hb, c_ref[0], preferred_element_type=F32) + d_ref[...] * u
    hre_ref[...] = hr
    him_ref[...] = hi


def _s5_sample(u, h0re, h0im, prm, ssm_d):
    ptab, b_cat, c_cat = prm
    n_chunk, cw, sw2 = b_cat.shape
    sw = sw2 // 2
    n_new, nb, width = u.shape
    return pl.pallas_call(
        _s5_sample_kernel,
        grid=(n_chunk,),
        in_specs=[pl.BlockSpec((n_new, nb, cw), lambda j: (0, 0, j)),
                  pl.BlockSpec((nb, sw), lambda j: (0, j)),
                  pl.BlockSpec((nb, sw), lambda j: (0, j)),
                  pl.BlockSpec((1, cw, sw2), lambda j: (j, 0, 0)),
                  pl.BlockSpec((2, 1, ROW_TILE, sw), lambda j: (0, 0, 0, j)),
                  pl.BlockSpec((1, sw2, cw), lambda j: (j, 0, 0)),
                  pl.BlockSpec((1, cw), lambda j: (0, j))],
        out_specs=[pl.BlockSpec((n_new, nb, cw), lambda j: (0, 0, j)),
                   pl.BlockSpec((nb, sw), lambda j: (0, j)),
                   pl.BlockSpec((nb, sw), lambda j: (0, j))],
        out_shape=[jax.ShapeDtypeStruct((n_new, nb, width), F32),
                   jax.ShapeDtypeStruct((nb, n_chunk * sw), F32),
                   jax.ShapeDtypeStruct((nb, n_chunk * sw), F32)],
        compiler_params=_params("parallel"),
        name="s5_sample",
    )(u, h0re, h0im, b_cat, ptab, c_cat, ssm_d.reshape(1, width))


def _glu_kernel(y_ref, g_ref, w_ref, b_ref, o_ref):
    y = y_ref[...]
    s = 0.5 * y * (1.0 + jnp.tanh(math.sqrt(2.0 / math.pi) * (y + 0.044715 * (y * y * y))))
    z = jnp.dot(s.astype(BF16), w_ref[...], preferred_element_type=F32) + b_ref[...]
    o_ref[...] = (s * _sigmoid(z) * g_ref[...].astype(F32)).astype(o_ref.dtype)


def _glu(y, gate, w, b):
    m, width = y.shape
    tm = _row_tile(m, 1024)
    row = pl.BlockSpec((tm, width), lambda i: (i, 0))
    return pl.pallas_call(
        _glu_kernel,
        grid=(m // tm,),
        in_specs=[row, row, pl.BlockSpec((width, width), lambda i: (0, 0)),
                  pl.BlockSpec((1, width), lambda i: (0, 0))],
        out_specs=row,
        out_shape=jax.ShapeDtypeStruct((m, width), BF16),
        compiler_params=_params("parallel"),
        name="glu",
    )(y, gate, w, b.reshape(1, width))


def _merge_kernel(*refs):
    o_refs, l_refs = refs[:N_DIL], refs[N_DIL:2 * N_DIL]
    ga_ref, ys_ref, ma_ref, ms_ref, wa_ref, ws_ref, out_ref = refs[2 * N_DIL:2 * N_DIL + 7]
    a_ref, otok_ref, ltok_ref = refs[2 * N_DIL + 7:]

    def token_order(ref, cols, stage):
        dil, per = ref.shape[1], ref.shape[2]
        if dil == 1:
            return ref[0, 0, :, cols].astype(F32)
        for r in range(dil):
            stage[pl.ds(r, per, stride=dil), :] = ref[0, r, :, cols].astype(F32)
        return stage[...]

    ls = [token_order(l_refs[g], slice(None), ltok_ref.at[g]) for g in range(N_DIL)]
    mx = jnp.maximum(jnp.maximum(ls[0], ls[1]), ls[2])
    es = [jnp.exp(l - mx) for l in ls]
    den = es[0] + es[1] + es[2]
    wts = [e / den for e in es]
    tm, aw = a_ref.shape
    half = out_ref.shape[1] // 2
    halves = [slice(c * half, (c + 1) * half) for c in range(2)]
    ys = ys_ref[...]
    acc = [ms_ref[:, cs].astype(F32) * jnp.dot(ys, ws_ref[:, cs], preferred_element_type=F32) for cs in halves]
    group = min(aw, MXU_COLS) // HEAD_DIM
    ya = [None, None]
    for h0 in range(0, aw // HEAD_DIM, group):
        for h in range(h0, h0 + group):
            sl = slice(h * HEAD_DIM, (h + 1) * HEAD_DIM)
            attn = None
            for g in range(N_DIL):
                term = (jnp.broadcast_to(wts[g][:, h:h + 1], (tm, HEAD_DIM))
                        * token_order(o_refs[g], sl, otok_ref.at[g]))
                attn = term if attn is None else attn + term
            a_ref[:, sl] = (attn * ga_ref[:, sl].astype(F32)).astype(BF16)
        ks = slice(h0 * HEAD_DIM, (h0 + group) * HEAD_DIM)
        for c, cs in enumerate(halves):
            part = jnp.dot(a_ref[:, ks], wa_ref[ks, cs], preferred_element_type=F32)
            ya[c] = part if ya[c] is None else ya[c] + part
    for c, cs in enumerate(halves):
        out_ref[:, cs] = (ma_ref[:, cs].astype(F32) * ya[c] + acc[c]).astype(out_ref.dtype)


def _merge(outs, lses, g_attn, y_ssm, gates, w_a, w_s):
    m, aw = g_attn.shape
    d = w_a.shape[1]
    nb = outs[0].shape[0]
    seq = m // nb
    tm = _row_tile(seq, 512)
    bps = seq // tm
    row = pl.BlockSpec((tm, aw), lambda i: (i, 0))
    ol_specs = []
    for arr in list(outs) + list(lses):
        dil = arr.shape[1]
        assert tm % (dil * ROW_TILE) == 0
        ol_specs.append(pl.BlockSpec((1, dil, tm // dil, arr.shape[3]),
                                     lambda i: (lax.div(i, bps), 0, lax.rem(i, bps), 0)))
    scratch = [pltpu.VMEM((tm, aw), BF16), pltpu.VMEM((N_DIL, tm, HEAD_DIM), F32),
               pltpu.VMEM((N_DIL, tm, HEAD_DIM), F32)]
    once = pl.Buffered(1)
    return pl.pallas_call(
        _merge_kernel,
        grid=(m // tm,),
        in_specs=ol_specs + [row, row, pl.BlockSpec((tm, d), lambda i: (i, 0)),
                             pl.BlockSpec((tm, d), lambda i: (i, 1)),
                             pl.BlockSpec((aw, d), lambda i: (0, 0), pipeline_mode=once),
                             pl.BlockSpec((aw, d), lambda i: (0, 0), pipeline_mode=once)],
        out_specs=pl.BlockSpec((tm, d), lambda i: (i, 0)),
        out_shape=jax.ShapeDtypeStruct((m, d), BF16),
        scratch_shapes=scratch,
        compiler_params=_params("parallel"),
        name="merge",
    )(*outs, *lses, g_attn, y_ssm, gates, gates, w_a, w_s)


def _out_kernel(x_ref, m_ref, w_ref, o_ref):
    o_ref[...] = x_ref[...] + jnp.dot(m_ref[...], w_ref[...], preferred_element_type=F32)


def _out_proj(x, merged, w):
    m, d = x.shape
    tm = _row_tile(m, 1024)
    tn = min(d, 1024)
    return pl.pallas_call(
        _out_kernel,
        grid=(d // tn, m // tm),
        in_specs=[pl.BlockSpec((tm, tn), lambda j, i: (i, j)),
                  pl.BlockSpec((tm, d), lambda j, i: (i, 0)),
                  pl.BlockSpec((d, tn), lambda j, i: (0, j))],
        out_specs=pl.BlockSpec((tm, tn), lambda j, i: (i, j)),
        out_shape=jax.ShapeDtypeStruct((m, d), F32),
        compiler_params=_params("parallel", "parallel"),
        name="out_proj",
    )(x, merged, w)


def _sample_front_kernel(x_ref, gw_ref, w_ref, nwq_ref, nwk_ref, cos_ref, sin_ref, o_ref, nw_ref):
    j = pl.program_id(0)
    x = x_ref[...]
    xn = x * lax.rsqrt(jnp.mean(x * x, axis=-1, keepdims=True) + NORM_EPS) * gw_ref[...]
    acc = jnp.dot(xn.astype(BF16), w_ref[...], preferred_element_type=F32)
    qk, v_end = 2 * N_DIL, 3 * N_DIL

    @pl.when(j < qk)
    def _():
        nw_ref[...] = jnp.where(j < N_DIL, nwq_ref[...], nwk_ref[...])
        o_ref[...] = acc
        _norm_rope_inplace(o_ref, nw_ref, cos_ref, sin_ref)

    @pl.when(((j >= qk) & (j < v_end)) | (j == v_end + 1))
    def _():
        o_ref[...] = acc

    @pl.when((j == v_end) | (j == v_end + 2))
    def _():
        o_ref[...] = acc * _sigmoid(acc)

    @pl.when(j > v_end + 2)
    def _():
        o_ref[...] = _sigmoid(acc)


def _sample_front(x, norm_w, w, q_norm_w, k_norm_w, cos, sin):
    m, d = x.shape
    aw = d // 2
    n_cols = w.shape[1]
    row = pl.BlockSpec((m, HEAD_DIM), lambda j: (0, 0))
    one = pl.BlockSpec((1, HEAD_DIM), lambda j: (0, 0))
    return pl.pallas_call(
        _sample_front_kernel,
        grid=(n_cols // aw,),
        in_specs=[pl.BlockSpec((m, d), lambda j: (0, 0)), pl.BlockSpec((1, d), lambda j: (0, 0)),
                  pl.BlockSpec((d, aw), lambda j: (0, j)), one, one, row, row],
        out_specs=pl.BlockSpec((m, aw), lambda j: (0, j)),
        out_shape=jax.ShapeDtypeStruct((m, n_cols), F32),
        scratch_shapes=[pltpu.VMEM((1, HEAD_DIM), F32)],
        compiler_params=_params("arbitrary"),
        name="sample_front",
    )(x, norm_w.reshape(1, d), w, q_norm_w.reshape(1, HEAD_DIM), k_norm_w.reshape(1, HEAD_DIM), cos, sin)


def _layer(x, cos, sin, lw, nb, seq, layer, depth, kv_prev, sample, shift=None):
    (norm_w, w_in, q_norm_w, k_norm_w, w_glu, b_glu, w_br_attn, w_br_ssm, w_out, ssm_d, s5prm) = lw
    m, d = x.shape
    aw = d // 2
    qkv = N_DIL * aw
    hg = aw // HEAD_DIM

    new_kv = []
    if sample is None:
        xn = _rmsnorm(x, norm_w)
        g_attn = _proj(xn, w_in, 3 * qkv, aw, "silu", BF16)
        u = _proj(xn, w_in, 3 * qkv + aw, aw, "plain", F32)
        g_ssm = _proj(xn, w_in, 3 * qkv + 2 * aw, aw, "silu", BF16)
        gates = _proj(xn, w_in, 3 * qkv + 3 * aw, 2 * d, "sigmoid", BF16)
        outs, lses = [], []
        for g in range(N_DIL):
            qd, kd, vd, kv = _qkv(xn, w_in, g, nb, seq, q_norm_w, k_norm_w, cos, sin, layer, depth,
                                  None if kv_prev is None else kv_prev[g])
            o, lse = _attn_prompt(qd, kd, vd)
            outs.append(o)
            lses.append(lse)
            new_kv.append(kv)
        y, hre, him, shifted = _s5_prompt(u, s5prm, ssm_d, nb, seq, shift)
    else:
        caches, h0re, h0im = sample
        front = _sample_front(x, norm_w, w_in, q_norm_w, k_norm_w, cos, sin)
        q, k, v = front[:, :qkv], front[:, qkv:2 * qkv], front[:, 2 * qkv:3 * qkv]
        g_attn, u = front[:, 3 * qkv:3 * qkv + aw], front[:, 3 * qkv + aw:3 * qkv + 2 * aw]
        g_ssm, gates = front[:, 3 * qkv + 2 * aw:3 * qkv + 3 * aw], front[:, 3 * qkv + 3 * aw:]
        tok = (nb, seq, N_DIL * hg, HEAD_DIM)
        o, lse = _attn_sample(q.reshape(tok), k.reshape(tok), v.reshape(tok), caches, layer)
        o = o.reshape(m, qkv)
        lse = jnp.pad(lse[..., 0].reshape(m, N_DIL, hg), ((0, 0), (0, 0), (0, HEAD_DIM - hg)))
        outs = [o[:, g * aw:(g + 1) * aw].reshape(1, 1, m, aw) for g in range(N_DIL)]
        lses = [lse[:, g].reshape(1, 1, m, HEAD_DIM) for g in range(N_DIL)]
        ut = u.reshape(nb, seq, aw).transpose(1, 0, 2)
        yt, hre, him = _s5_sample(ut, h0re, h0im, s5prm, ssm_d)
        y = yt.transpose(1, 0, 2).reshape(m, aw)
        k5, v5 = k.reshape(nb, seq, N_DIL, hg, HEAD_DIM), v.reshape(nb, seq, N_DIL, hg, HEAD_DIM)
        new_kv = [jnp.stack([k5[:, :, g], v5[:, :, g]], axis=2) for g in range(N_DIL)]
        shifted = None

    y_ssm = _glu(y, g_ssm, w_glu, b_glu)
    merged = _merge(outs, lses, g_attn, y_ssm, gates, w_br_attn, w_br_ssm)
    x_out = _out_proj(x, merged, w_out)

    n_grp = aw // SSM_GROUP_CH
    state = jnp.stack([hre, him], axis=-1).reshape(nb, n_grp, SSM_STATE, 2)
    return x_out, new_kv, state, shifted


def kernel(x_prompt, x_sample, cache_kv_d1, cache_kv_d4, cache_kv_d16, state_ssm, norm_w, w_in, q_norm_w, k_norm_w, ssm_lambda_re, ssm_lambda_im, ssm_log_dt, ssm_b_re, ssm_b_im, ssm_c_re, ssm_c_im, ssm_d, w_glu, b_glu, w_br_attn, w_br_ssm, w_out):
    nb_p, seq_p, d = x_prompt.shape
    nb_s, seq_s, _ = x_sample.shape
    depth = norm_w.shape[0]
    caches = (cache_kv_d1, cache_kv_d4, cache_kv_d16)

    cos_p, sin_p = _rope_tables(jnp.arange(seq_p, dtype=F32))
    cos_s, sin_s = _rope_tables(PAST_LEN + jnp.arange(seq_s, dtype=F32))
    cos_s, sin_s = jnp.tile(cos_s, (nb_s, 1)), jnp.tile(sin_s, (nb_s, 1))

    hp = x_prompt.reshape(nb_p * seq_p, d)
    hs = x_sample.reshape(nb_s * seq_s, d)
    lws = []
    for l in range(depth):
        s5prm = _s5_params(ssm_lambda_re[l], ssm_lambda_im[l], ssm_log_dt[l], ssm_b_re[l], ssm_b_im[l],
                           ssm_c_re[l], ssm_c_im[l], _s5_chunk(seq_p, l == 0) // ROW_TILE)
        lws.append((norm_w[l], _layer_bf16(w_in, l), q_norm_w[l], k_norm_w[l], _layer_bf16(w_glu, l), b_glu[l],
                    _layer_bf16(w_br_attn, l), _layer_bf16(w_br_ssm, l), _layer_bf16(w_out, l), ssm_d[l], s5prm))

    kv_s_new = [[] for _ in range(N_DIL)]
    ssm_s = []
    n_state = state_ssm.shape[2] * state_ssm.shape[3]
    for l in range(depth):
        h0re = state_ssm[l, ..., 0].reshape(nb_s, n_state)
        h0im = state_ssm[l, ..., 1].reshape(nb_s, n_state)
        hs, new_kv_s, st_s, _ = _layer(hs, cos_s, sin_s, lws[l], nb_s, seq_s, l, depth, None,
                                       (caches, h0re, h0im))
        for g in range(N_DIL):
            kv_s_new[g].append(new_kv_s[g])
        ssm_s.append(st_s)

    hosted = {0: list(range(N_DIL))} if depth == 1 else {0: [N_DIL - 1], 1: list(range(N_DIL - 1))}
    news = [jnp.stack(n) for n in kv_s_new]
    kv_p, kv_s, ssm_p = None, [None] * N_DIL, []
    for l in range(depth):
        gs = hosted.get(l)
        shift = ([caches[g] for g in gs], [news[g] for g in gs]) if gs else None
        hp, kv_p, st, shifted = _layer(hp, cos_p, sin_p, lws[l], nb_p, seq_p, l, depth, kv_p, None, shift)
        for g, arr in zip(gs or [], shifted or []):
            kv_s[g] = arr
        ssm_p.append(st)

    return (hp.reshape(x_prompt.shape), hs.reshape(x_sample.shape), kv_p[0], kv_p[1], kv_p[2], jnp.stack(ssm_p),
            kv_s[0], kv_s[1], kv_s[2], jnp.stack(ssm_s))
```

```python
import functools
import math

import jax
import jax.numpy as jnp
from jax import lax
from jax.experimental import pallas as pl
from jax.experimental.pallas import tpu as pltpu

F32 = jnp.float32
BF16 = jnp.bfloat16

HEAD_DIM = 128
WINDOWS = (128, 512, 2048)
DILATIONS = (1, 4, 16)
N_DIL = 3
SPAN = 128
SSM_GROUP_CH = 16
SSM_STATE = 64
GROUPS_PER_CHUNK = 8
ROPE_THETA = 10000.0
NORM_EPS = 1e-6
PAST_LEN = 16384
NEG_BIG = -1e30
VMEM_LIMIT = 58 * 1024 * 1024
KV_ROWS_PER_COPY = 512
ROW_TILE = 8
MXU_COLS = 256
QK_ROW_CHUNK = 64
KV_COPY_BYTES = 4 * 1024 * 1024


def _params(*sem):
    return pltpu.CompilerParams(dimension_semantics=sem, vmem_limit_bytes=VMEM_LIMIT)


def _sigmoid(x):
    return 1.0 / (1.0 + jnp.exp(-x))


def _row_tile(m, cap):
    t = min(m, cap)
    assert m % t == 0, (m, t)
    return t


def _rmsnorm_kernel(x_ref, w_ref, o_ref):
    x = x_ref[...]
    y = x * lax.rsqrt(jnp.mean(x * x, axis=-1, keepdims=True) + NORM_EPS)
    o_ref[...] = (y * w_ref[...]).astype(o_ref.dtype)


def _rmsnorm(x, w):
    m, d = x.shape
    tm = _row_tile(m, 512)
    return pl.pallas_call(
        _rmsnorm_kernel,
        grid=(m // tm,),
        in_specs=[pl.BlockSpec((tm, d), lambda i: (i, 0)),
                  pl.BlockSpec((1, d), lambda i: (0, 0))],
        out_specs=pl.BlockSpec((tm, d), lambda i: (i, 0)),
        out_shape=jax.ShapeDtypeStruct((m, d), BF16),
        compiler_params=_params("parallel"),
        name="rmsnorm",
    )(x, w.reshape(1, d))


def _cast_kernel(w_ref, o_ref):
    o_ref[...] = w_ref[...].astype(o_ref.dtype)


def _layer_bf16(w, layer):
    _, rows, cols = w.shape
    tr, tc = min(rows, 512), min(cols, 2048)
    assert rows % tr == 0 and cols % tc == 0
    return pl.pallas_call(
        _cast_kernel,
        grid=(rows // tr, cols // tc),
        in_specs=[pl.BlockSpec((None, tr, tc), lambda i, j: (layer, i, j))],
        out_specs=pl.BlockSpec((tr, tc), lambda i, j: (i, j)),
        out_shape=jax.ShapeDtypeStruct((rows, cols), BF16),
        compiler_params=_params("parallel", "parallel"),
        name="cast_bf16",
    )(w)


def _norm_rope_inplace(t_ref, nw_ref, cos_ref, sin_ref, head_major=False):
    rows = cos_ref.shape[0]
    n_heads = t_ref.shape[0] // rows if head_major else t_ref.shape[1] // HEAD_DIM
    rc = min(rows, QK_ROW_CHUNK)
    half = HEAD_DIM // 2
    nw = nw_ref[...]
    nw_swapped = pltpu.roll(nw, half, axis=1)

    def chunk(c, carry):
        r0 = pl.multiple_of(c * rc, rc)
        cs = cos_ref[pl.ds(r0, rc), :] * nw
        ss = sin_ref[pl.ds(r0, rc), :] * nw_swapped
        for h in range(n_heads):
            idx = (pl.ds(h * rows + r0, rc), slice(None)) if head_major else (
                pl.ds(r0, rc), slice(h * HEAD_DIM, (h + 1) * HEAD_DIM))
            a = t_ref[idx]
            r = lax.rsqrt(jnp.mean(a * a, axis=-1, keepdims=True) + NORM_EPS)
            t_ref[idx] = (a * cs + pltpu.roll(a, half, axis=1) * ss) * r
        return carry

    lax.fori_loop(0, rows // rc, chunk, 0)


def _proj_kernel(x_ref, w_ref, o_ref, *, epilogue):
    acc = jnp.dot(x_ref[...], w_ref[...], preferred_element_type=F32)
    if epilogue == "silu":
        acc = acc * _sigmoid(acc)
    elif epilogue == "sigmoid":
        acc = _sigmoid(acc)
    o_ref[...] = acc.astype(o_ref.dtype)


def _proj(xn, w, col0, ncols, epilogue, out_dtype):
    m, d = xn.shape
    tm = _row_tile(m, 1024)
    tn = min(ncols, 1024)
    assert ncols % tn == 0 and col0 % tn == 0
    jb = col0 // tn
    return pl.pallas_call(
        functools.partial(_proj_kernel, epilogue=epilogue),
        grid=(m // tm, ncols // tn),
        in_specs=[pl.BlockSpec((tm, d), lambda i, j: (i, 0)),
                  pl.BlockSpec((d, tn), lambda i, j: (0, jb + j))],
        out_specs=pl.BlockSpec((tm, tn), lambda i, j: (i, j)),
        out_shape=jax.ShapeDtypeStruct((m, ncols), out_dtype),
        compiler_params=_params("parallel", "arbitrary"),
        name="proj_" + epilogue,
    )(xn, w)


def _norm_rope_head(a, ssq, nw_ref, cos_ref, sin_ref, store):
    rows = a.shape[0]
    rc = min(rows, QK_ROW_CHUNK)
    half = HEAD_DIM // 2
    nw = nw_ref[...]
    nw_swapped = pltpu.roll(nw, half, axis=1)
    for r0 in range(0, rows, rc):
        ac = a[r0:r0 + rc]
        cs = cos_ref[r0:r0 + rc, :] * nw
        ss = sin_ref[r0:r0 + rc, :] * nw_swapped
        r = lax.rsqrt(ssq[r0:r0 + rc] * (1.0 / HEAD_DIM) + NORM_EPS)
        store(r0, rc, (ac * cs + pltpu.roll(ac, half, axis=1) * ss) * r)


def _project_dilated(x, w_ref, nw_ref, cos_ref, sin_ref, hm_ref, d_ref, dil):
    tm = x.shape[0]
    n_heads = w_ref.shape[1] // HEAD_DIM
    per = tm // dil
    acc = jnp.dot(x, w_ref[...], preferred_element_type=F32)
    ones = jnp.ones((HEAD_DIM, HEAD_DIM), BF16)
    for h in range(n_heads):
        sl = slice(h * HEAD_DIM, (h + 1) * HEAD_DIM)
        a = acc[:, sl]
        if nw_ref is None:
            hm_ref[h * tm:(h + 1) * tm, :] = a
        else:
            def store(r0, rc, chunk, h=h):
                hm_ref[h * tm + r0:h * tm + r0 + rc, :] = chunk

            ssq = jnp.dot((a * a).astype(BF16), ones, preferred_element_type=F32)
            _norm_rope_head(a, ssq, nw_ref, cos_ref, sin_ref, store)
        for r in range(dil):
            rows = pl.ds(h * tm + r, per, stride=dil) if dil > 1 else pl.ds(h * tm, per)
            d_ref[0, r, :, sl] = hm_ref[rows, :].astype(d_ref.dtype)


def _qkv_kernel(x_ref, wq_ref, wk_ref, wv_ref, nwq_ref, nwk_ref, cos_ref, sin_ref, kv_in_ref,
                qd_ref, kd_ref, vd_ref, kv_ref, hk_ref, hv_ref, rows_ref, sem,
                *, dil, layer, first_kept, kept_rows):
    b, i = pl.program_id(0), pl.program_id(1)
    tm = x_ref.shape[0]
    n_heads = wk_ref.shape[1] // HEAD_DIM
    kb = rows_ref.shape[0]
    x = x_ref[...]
    _project_dilated(x, wq_ref, nwq_ref, cos_ref, sin_ref, hv_ref, qd_ref, dil)
    _project_dilated(x, wk_ref, nwk_ref, cos_ref, sin_ref, hk_ref, kd_ref, dil)
    _project_dilated(x, wv_ref, None, None, None, hv_ref, vd_ref, dil)

    n_parts = kept_rows // kb

    def copy_out(bb, ii, part):
        dst = kv_ref.at[layer, bb, pl.ds((ii - first_kept) * kept_rows + part * kb, kb)]
        return pltpu.make_async_copy(rows_ref, dst, sem.at[0])

    @pl.when(i >= first_kept)
    def _():
        for part in range(n_parts):
            if part == 0:
                @pl.when((i > first_kept) | (b > 0))
                def _():
                    prev_same = i > first_kept
                    copy_out(jnp.where(prev_same, b, b - 1),
                             jnp.where(prev_same, i - 1, pl.num_programs(1) - 1), n_parts - 1).wait()
            else:
                copy_out(b, i, part - 1).wait()
            base = tm - kept_rows + part * kb

            def tokens(j, carry, base=base):
                t0 = pl.multiple_of(j * ROW_TILE, ROW_TILE)
                for which, hm_ref in ((0, hk_ref), (1, hv_ref)):
                    tiles = [hm_ref[pl.ds(h * tm + base + t0, ROW_TILE), :] for h in range(n_heads)]
                    rows_ref[pl.ds(t0, ROW_TILE), which] = jnp.swapaxes(jnp.stack(tiles, axis=0), 0, 1)
                return carry

            lax.fori_loop(0, kb // ROW_TILE, tokens, 0, unroll=2)
            copy_out(b, i, part).start()

    @pl.when((b == pl.num_programs(0) - 1) & (i == pl.num_programs(1) - 1))
    def _():
        copy_out(b, i, n_parts - 1).wait()


def _qkv(xn, w, g, nb, seq, q_norm_w, k_norm_w, cos, sin, layer, depth, kv_prev):
    m, d = xn.shape
    aw = d // 2
    hg = aw // HEAD_DIM
    dil = DILATIONS[g]
    keep = min(WINDOWS[g], seq)
    tm = _row_tile(seq, 1024)
    per = tm // dil
    kept_rows = min(keep, tm)
    kb = min(kept_rows, KV_ROWS_PER_COPY)
    bps = seq // tm
    first_kept = (seq - keep) // tm if keep >= tm else bps - 1
    assert per % 16 == 0 and keep % kept_rows == 0 and (seq - keep) % kept_rows == 0
    assert kept_rows % kb == 0 and kb % ROW_TILE == 0
    once = pl.Buffered(1)
    nw = lambda a: a.reshape(1, HEAD_DIM)
    dshape = jax.ShapeDtypeStruct((nb, dil, seq // dil, aw), BF16)
    staging = pltpu.VMEM((hg * tm, HEAD_DIM), F32)

    wspec = lambda col: pl.BlockSpec((d, aw), lambda b, i: (0, col), pipeline_mode=once)
    tspec = pl.BlockSpec((tm, HEAD_DIM), lambda b, i: (i, 0))
    dspec = pl.BlockSpec((1, dil, per, aw), lambda b, i: (b, 0, i, 0))
    hbm = pl.BlockSpec(memory_space=pl.ANY)
    kv_shape = jax.ShapeDtypeStruct((depth, nb, keep, 2, hg, HEAD_DIM), F32)
    if kv_prev is None:
        kv_prev = jnp.zeros(kv_shape.shape, kv_shape.dtype)
    nspec = pl.BlockSpec((1, HEAD_DIM), lambda b, i: (0, 0))
    return pl.pallas_call(
        functools.partial(_qkv_kernel, dil=dil, layer=layer, first_kept=first_kept, kept_rows=kept_rows),
        grid=(nb, bps),
        in_specs=[pl.BlockSpec((tm, d), lambda b, i: (b * bps + i, 0)),
                  wspec(g), wspec(N_DIL + g), wspec(2 * N_DIL + g), nspec, nspec, tspec, tspec, hbm],
        out_specs=[dspec, dspec, dspec, hbm],
        out_shape=[dshape, dshape, dshape, kv_shape],
        scratch_shapes=[staging, staging, pltpu.VMEM((kb, 2, hg, HEAD_DIM), F32),
                        pltpu.SemaphoreType.DMA((1,))],
        input_output_aliases={8: 3},
        compiler_params=_params("arbitrary", "arbitrary"),
        name="qkv_d%d" % dil,
    )(xn, w, w, w, nw(q_norm_w), nw(k_norm_w), cos, sin, kv_prev)


def _rope_tables(pos):
    half = HEAD_DIM // 2
    inv_freq = jnp.power(ROPE_THETA, -jnp.arange(half, dtype=F32) * (2.0 / HEAD_DIM))
    ang = pos[:, None] * inv_freq[None, :]
    cos, sin = jnp.cos(ang), jnp.sin(ang)
    return jnp.concatenate([cos, cos], axis=-1), jnp.concatenate([-sin, sin], axis=-1)


def _attn_prompt_kernel(q_ref, kp_ref, kc_ref, vp_ref, vc_ref, o_ref, lse_ref):
    not_first = pl.program_id(2) > 0
    a = lax.broadcasted_iota(jnp.int32, (SPAN, 2 * SPAN), 0)
    c = lax.broadcasted_iota(jnp.int32, (SPAN, 2 * SPAN), 1)
    band = (c >= a) & (c <= a + SPAN)
    band_first = band & ((c >= SPAN) | not_first)
    lane = lax.broadcasted_iota(jnp.int32, (SPAN, HEAD_DIM), 1)
    scale = HEAD_DIM ** -0.5
    for j in range(q_ref.shape[2] // SPAN):
        rows = slice(j * SPAN, (j + 1) * SPAN)
        lse_tile = jnp.zeros((SPAN, HEAD_DIM), F32)
        for h in range(q_ref.shape[3] // HEAD_DIM):
            sl = slice(h * HEAD_DIM, (h + 1) * HEAD_DIM)
            q = q_ref[0, 0, rows, sl]
            if j == 0:
                k = jnp.concatenate([kp_ref[0, 0, :, sl], kc_ref[0, 0, rows, sl]], axis=0)
                v = jnp.concatenate([vp_ref[0, 0, :, sl], vc_ref[0, 0, rows, sl]], axis=0)
            else:
                k = kc_ref[0, 0, (j - 1) * SPAN:(j + 1) * SPAN, sl]
                v = vc_ref[0, 0, (j - 1) * SPAN:(j + 1) * SPAN, sl]
            s = lax.dot_general(q, k, (((1,), (1,)), ((), ())), preferred_element_type=F32) * scale
            s = jnp.where(band_first if j == 0 else band, s, NEG_BIG)
            m = jnp.max(s, axis=-1, keepdims=True)
            p = jnp.exp(s - m)
            l = jnp.sum(p, axis=-1, keepdims=True)
            o = jnp.dot(p.astype(BF16), v, preferred_element_type=F32) / l
            o_ref[0, 0, rows, sl] = o.astype(o_ref.dtype)
            lse_tile = jnp.where(lane == h, m + jnp.log(l), lse_tile)
        lse_ref[0, 0, rows, :] = lse_tile


def _attn_prompt(q, k, v):
    nb, dil, length, aw = q.shape
    n_blk = length // SPAN
    qb = min(4, n_blk)
    assert length % SPAN == 0 and n_blk % qb == 0 and aw // HEAD_DIM <= HEAD_DIM
    cur = pl.BlockSpec((1, 1, qb * SPAN, aw), lambda b, r, n: (b, r, n, 0))
    prev = pl.BlockSpec((1, 1, SPAN, aw), lambda b, r, n: (b, r, jnp.maximum(qb * n - 1, 0), 0))
    return pl.pallas_call(
        _attn_prompt_kernel,
        grid=(nb, dil, n_blk // qb),
        in_specs=[cur, prev, cur, prev, cur],
        out_specs=[cur, pl.BlockSpec((1, 1, qb * SPAN, HEAD_DIM), lambda b, r, n: (b, r, n, 0))],
        out_shape=[jax.ShapeDtypeStruct(q.shape, BF16),
                   jax.ShapeDtypeStruct((nb, dil, length, HEAD_DIM), F32)],
        compiler_params=_params("parallel", "parallel", "arbitrary"),
        name="attn_prompt_d%d" % dil,
    )(q, k, k, v, v)


def _attn_sample_kernel(q_ref, kn_ref, vn_ref, c1_ref, c4_ref, c16_ref, o_ref, lse_ref):
    n_new, n_heads, _ = q_ref.shape
    hg = n_heads // N_DIL
    scale = HEAD_DIM ** -0.5
    caches = (c1_ref, c4_ref, c16_ref)
    for g in range(N_DIL):
        hs = slice(g * hg, (g + 1) * hg)
        for t in range(n_new):
            q = q_ref[t, hs, :]
            if DILATIONS[g] == 1:
                kc, vc = caches[g][:, 0], caches[g][:, 1]
                rows = lax.broadcasted_iota(jnp.int32, (kc.shape[0], hg, 1), 0)
                s = jnp.sum(kc * q[None], axis=-1, keepdims=True) * scale
                s = jnp.where(rows >= t, s, NEG_BIG)
                new = range(t + 1)
            else:
                kc, vc = caches[g][:, t, 0], caches[g][:, t, 1]
                s = jnp.sum(kc * q[None], axis=-1, keepdims=True) * scale
                new = (t,)
            s_new = [jnp.sum(kn_ref[u, hs, :] * q, axis=-1, keepdims=True) * scale for u in new]
            m = jnp.max(s, axis=0)
            for sn in s_new:
                m = jnp.maximum(m, sn)
            p = jnp.exp(s - m[None])
            l = jnp.sum(p, axis=0)
            acc = jnp.sum(p * vc, axis=0)
            for u, sn in zip(new, s_new):
                pn = jnp.exp(sn - m)
                l = l + pn
                acc = acc + pn * vn_ref[u, hs, :]
            o_ref[t, hs, :] = acc / l
            lse_ref[t, hs, :] = jnp.broadcast_to(m + jnp.log(l), (hg, HEAD_DIM))


def _attn_sample(q, kn, vn, caches, layer):
    nb, n_new, n_heads, _ = q.shape
    hg = n_heads // N_DIL
    views, specs = [], []
    for g in range(N_DIL):
        dil = DILATIONS[g]
        depth, _, buf = caches[g].shape[:3]
        assert buf == WINDOWS[g] and (dil == 1 or dil % n_new == 0)
        if dil == 1:
            views.append(caches[g])
            specs.append(pl.BlockSpec((None, None, buf, 2, hg, HEAD_DIM),
                                      lambda b: (layer, b, 0, 0, 0, 0)))
        else:
            views.append(caches[g].reshape(depth, nb, SPAN, dil, 2, hg, HEAD_DIM))
            specs.append(pl.BlockSpec((None, None, SPAN, n_new, 2, hg, HEAD_DIM),
                                      lambda b: (layer, b, 0, 0, 0, 0, 0)))
    tok = pl.BlockSpec((None, n_new, n_heads, HEAD_DIM), lambda b: (b, 0, 0, 0))
    return pl.pallas_call(
        _attn_sample_kernel,
        grid=(nb,),
        in_specs=[tok, tok, tok] + specs,
        out_specs=[tok, tok],
        out_shape=[jax.ShapeDtypeStruct(q.shape, F32)] * 2,
        compiler_params=_params("parallel"),
        name="attn_sample",
    )(q, kn, vn, *views)


def _s5_params_kernel(lre_ref, lim_ref, ldt_ref, bre_ref, bim_ref, abar_ref, bbre_ref, bbim_ref):
    dt = jnp.exp(ldt_ref[...])
    lre, lim = lre_ref[...], lim_ref[...]
    xr, xi = lre * dt, lim * dt
    mag = jnp.exp(xr)
    ar, ai = mag * jnp.cos(xi), mag * jnp.sin(xi)
    shape = (ROW_TILE, xr.shape[1])
    abar_ref[0] = jnp.broadcast_to(ar, shape)
    abar_ref[1] = jnp.broadcast_to(ai, shape)
    nr, ni = ar - 1.0, ai
    den = lre * lre + lim * lim
    fr = (nr * lre + ni * lim) / den
    fi = (ni * lre - nr * lim) / den
    bre, bim = bre_ref[...], bim_ref[...]
    bbre_ref[...] = fr * bre - fi * bim
    bbim_ref[...] = fr * bim + fi * bre


def _s5_params(lam_re, lam_im, log_dt, b_re, b_im, c_re, c_im):
    n_grp, n_st = lam_re.shape
    n_ch = b_re.shape[2]
    n = n_grp * n_st
    gc = GROUPS_PER_CHUNK
    n_chunk = n_grp // gc
    sw = gc * n_st
    to_lanes = lambda b: b.transpose(2, 0, 1).reshape(n_ch, n)
    lane = lambda rows: pl.BlockSpec((rows, sw), lambda j: (0, j))
    abar, bb_re, bb_im = pl.pallas_call(
        _s5_params_kernel,
        grid=(n_chunk,),
        in_specs=[lane(1)] * 3 + [lane(n_ch)] * 2,
        out_specs=[pl.BlockSpec((2, ROW_TILE, sw), lambda j: (0, 0, j)), lane(n_ch), lane(n_ch)],
        out_shape=[jax.ShapeDtypeStruct((2, ROW_TILE, n), F32),
                   jax.ShapeDtypeStruct((n_ch, n), F32), jax.ShapeDtypeStruct((n_ch, n), F32)],
        compiler_params=_params("parallel"),
        name="s5_params",
    )(lam_re.reshape(1, n), lam_im.reshape(1, n),
      jnp.broadcast_to(log_dt[:, None], (n_grp, n_st)).reshape(1, n), to_lanes(b_re), to_lanes(b_im))

    eye = jnp.eye(gc, dtype=F32)

    def b_blocks(bb):
        bb = bb.reshape(n_ch, n_chunk, gc, n_st)
        blk = jnp.einsum("cjgp,gh->jgchp", bb, eye)
        return blk.reshape(n_chunk, gc * n_ch, gc * n_st).astype(BF16)

    def c_blocks(cc):
        cc = cc.reshape(n_chunk, gc, n_ch, n_st)
        blk = jnp.einsum("jgcp,gh->jgphc", cc, eye)
        return blk.reshape(n_chunk, gc * n_st, gc * n_ch).astype(BF16)

    b_cat = jnp.concatenate([b_blocks(bb_re), b_blocks(bb_im)], axis=2)
    c_cat = jnp.concatenate([c_blocks(c_re), c_blocks(c_im)], axis=1)
    return abar, b_cat, c_cat


def _cmul_add(xr, xi, ar, ai, br, bi):
    return xr + ar * br - ai * bi, xi + ar * bi + ai * br


def _shift_plan(cache_shape, new_shape, n_steps):
    lb_n, buf = cache_shape[:2]
    keep = buf - new_shape[1]
    row_bytes = 4 * math.prod(cache_shape[2:])
    for per_row in range(1, keep + 1):
        n_jobs = lb_n * per_row
        if keep % per_row == 0 and keep // per_row * row_bytes <= KV_COPY_BYTES and n_steps % n_jobs == 0:
            return keep // per_row, per_row, n_steps // n_jobs
    raise ValueError("window shift does not fit the grid")


def _shift_lag(period):
    return 1 if period == 2 else 2


def _shift_slots(period):
    return 4 if period == 1 else 2


def _window_shift_step(step, n_steps, caches, news, outs, rings, nbufs, in_sem, out_sem, new_sem):
    last = step == n_steps - 1
    for g in range(N_DIL):
        cache, out, ring = caches[g], outs[g], rings[g]
        n_new = news[g].shape[1]
        keep = cache.shape[1] - n_new
        rows, per_row, period = _shift_plan(cache.shape, news[g].shape, n_steps)
        n_jobs = cache.shape[0] * per_row
        slots = ring.shape[0]
        lag = _shift_lag(period)
        assert n_jobs >= slots and (period == 1 or lag < period)

        def load(k, cache=cache, ring=ring, g=g, rows=rows, per_row=per_row, n_new=n_new, slots=slots):
            src = cache.at[lax.div(k, per_row), pl.ds(n_new + lax.rem(k, per_row) * rows, rows)]
            return pltpu.make_async_copy(src, ring.at[lax.rem(k, slots)], in_sem.at[g, lax.rem(k, slots)])

        def store(k, out=out, ring=ring, g=g, rows=rows, per_row=per_row, slots=slots):
            dst = out.at[lax.div(k, per_row), pl.ds(lax.rem(k, per_row) * rows, rows)]
            return pltpu.make_async_copy(ring.at[lax.rem(k, slots)], dst, out_sem.at[g, lax.rem(k, slots)])

        def issue(k, load=load, store=store, slots=slots):
            @pl.when(k >= slots)
            def _():
                store(k - slots).wait()
            load(k).start()

        def forward(k, load=load, store=store):
            load(k).wait()
            store(k).start()

        def drain(k, store=store, slots=slots):
            for back in range(slots - 1, -1, -1):
                store(k - back).wait()

        if period == 1:
            issue(step)

            @pl.when(step >= lag)
            def _(forward=forward, lag=lag):
                forward(step - lag)

            @pl.when(last)
            def _(forward=forward, drain=drain, lag=lag):
                for back in range(lag - 1, -1, -1):
                    forward(step - back)
                drain(step)
        else:
            k = lax.div(step, period)
            phase = lax.rem(step, period)

            @pl.when(phase == 0)
            def _(issue=issue, k=k):
                issue(k)

            @pl.when(phase == lag)
            def _(forward=forward, k=k):
                forward(k)

            @pl.when(last)
            def _(drain=drain, k=k):
                drain(k)

        new_in = pltpu.make_async_copy(news[g], nbufs[g], new_sem.at[g])
        new_out = pltpu.make_async_copy(nbufs[g], out.at[:, pl.ds(keep, n_new)], new_sem.at[N_DIL + g])

        @pl.when(step == 0)
        def _(new_in=new_in, new_out=new_out):
            new_in.start()
            new_in.wait()
            new_out.start()

        @pl.when(last)
        def _(new_out=new_out):
            new_out.wait()


def _s5_scan_kernel(u_ref, b_ref, abar_ref, c_ref, d_ref, *rest, with_shift):
    rest, ptab_ref = rest[:-1], rest[-1]
    if with_shift:
        n = N_DIL
        caches, news = rest[:n], rest[n:2 * n]
        y_ref, hfin_ref = rest[2 * n:2 * n + 2]
        outs = rest[2 * n + 2:3 * n + 2]
        up_ref, s_ref, hb_ref, end_ref = rest[3 * n + 2:3 * n + 6]
        rings, nbufs = rest[3 * n + 6:4 * n + 6], rest[4 * n + 6:5 * n + 6]
        in_sem, out_sem, new_sem = rest[5 * n + 6:]
        _, n_j, n_t = with_shift
        step = (pl.program_id(0) * n_j + pl.program_id(1)) * n_t + pl.program_id(2)
        _window_shift_step(step, math.prod(with_shift), caches, news, outs, rings, nbufs,
                           in_sem, out_sem, new_sem)
    else:
        y_ref, hfin_ref, up_ref, s_ref, hb_ref, end_ref = rest
    tc, sw2 = s_ref.shape
    sw = sw2 // 2
    seg = tc // ROW_TILE
    re, im = slice(0, sw), slice(sw, sw2)

    @pl.when(pl.program_id(2) == 0)
    def _():
        end_ref[...] = jnp.zeros_like(end_ref)
        abr, abi = abar_ref[0], abar_ref[1]
        ptab_ref[0, 0] = abr
        ptab_ref[1, 0] = abi

        def next_power(i, carry):
            pr, pi = carry
            pr, pi = pr * abr - pi * abi, pr * abi + pi * abr
            ptab_ref[0, i] = pr
            ptab_ref[1, i] = pi
            return pr, pi

        lax.fori_loop(1, seg, next_power, (abr, abi))

    for i in range(seg):
        up_ref[i * ROW_TILE:(i + 1) * ROW_TILE, :] = u_ref[0, pl.ds(i, ROW_TILE, stride=seg), :]
    s_ref[...] = jnp.dot(up_ref[...].astype(BF16), b_ref[0], preferred_element_type=F32)

    ar, ai = ptab_ref[0, 0], ptab_ref[1, 0]

    def local_step(i, carry):
        hr, hi = carry
        r0 = pl.multiple_of(i * ROW_TILE, ROW_TILE)
        hr, hi = _cmul_add(s_ref[pl.ds(r0, ROW_TILE), re], s_ref[pl.ds(r0, ROW_TILE), im], ar, ai, hr, hi)
        s_ref[pl.ds(r0, ROW_TILE), re] = hr
        s_ref[pl.ds(r0, ROW_TILE), im] = hi
        return hr, hi

    zero = jnp.zeros((ROW_TILE, sw), F32)
    er, ei = lax.fori_loop(0, seg, local_step, (zero, zero), unroll=8)

    row = lax.broadcasted_iota(jnp.int32, (ROW_TILE, sw), 0)
    cr = jnp.where(row == 0, pltpu.roll(end_ref[0], 1, axis=0), pltpu.roll(er, 1, axis=0))
    ci = jnp.where(row == 0, pltpu.roll(end_ref[1], 1, axis=0), pltpu.roll(ei, 1, axis=0))
    mr, mi = ptab_ref[0, seg - 1], ptab_ref[1, seg - 1]
    wr, wi = mr, mi
    for sh in (1, 2, 4):
        gr, gi = jnp.where(row >= sh, wr, 0.0), jnp.where(row >= sh, wi, 0.0)
        cr, ci = _cmul_add(cr, ci, gr, gi, pltpu.roll(cr, sh, axis=0), pltpu.roll(ci, sh, axis=0))
        wr, wi = wr * wr - wi * wi, 2.0 * wr * wi
    fr, fi = _cmul_add(er, ei, mr, mi, cr, ci)
    end_ref[0] = fr
    end_ref[1] = fi
    hfin_ref[0, 0] = fr
    hfin_ref[0, 1] = fi

    def fix_up(k, carry):
        r0 = pl.multiple_of(k * 2 * ROW_TILE, 2 * ROW_TILE)
        hs, ns = [], []
        for half in range(2):
            i = 2 * k + half
            rows = pl.ds(r0 + half * ROW_TILE, ROW_TILE)
            hr, hi = _cmul_add(s_ref[rows, re], s_ref[rows, im], ptab_ref[0, i], ptab_ref[1, i], cr, ci)
            hs.append(hr)
            ns.append(-hi)
        hb_ref[pl.ds(r0, 2 * ROW_TILE), re] = jnp.concatenate(hs, axis=0).astype(BF16)
        hb_ref[pl.ds(r0, 2 * ROW_TILE), im] = jnp.concatenate(ns, axis=0).astype(BF16)
        return carry

    lax.fori_loop(0, seg // 2, fix_up, 0, unroll=4)
    yp = jnp.dot(hb_ref[...], c_ref[0], preferred_element_type=F32) + d_ref[...] * up_ref[...]
    for i in range(seg):
        y_ref[0, pl.ds(i, ROW_TILE, stride=seg), :] = yp[i * ROW_TILE:(i + 1) * ROW_TILE, :]


def _s5_chunk(seq):
    return min(seq, 1024)


def _flat_window(a):
    return a.reshape(a.shape[0] * a.shape[1], a.shape[2], a.shape[3] * a.shape[4], a.shape[5])


def _s5_prompt(u, prm, ssm_d, nb, seq, shift=None):
    abar, b_cat, c_cat = prm
    n_chunk, cw, sw2 = b_cat.shape
    sw = sw2 // 2
    width = u.shape[1]
    tc = _s5_chunk(seq)
    seg = tc // ROW_TILE
    assert seq % tc == 0 and seg % 2 == 0
    grid = (nb, n_chunk, seq // tc)
    in_specs = [pl.BlockSpec((1, tc, cw), lambda b, j, t: (b, t, j)),
                pl.BlockSpec((1, cw, sw2), lambda b, j, t: (j, 0, 0)),
                pl.BlockSpec((2, ROW_TILE, sw), lambda b, j, t: (0, 0, j)),
                pl.BlockSpec((1, sw2, cw), lambda b, j, t: (j, 0, 0)),
                pl.BlockSpec((1, cw), lambda b, j, t: (0, j))]
    out_specs = [pl.BlockSpec((1, tc, cw), lambda b, j, t: (b, t, j)),
                 pl.BlockSpec((1, 2, ROW_TILE, sw), lambda b, j, t: (b, 0, 0, j))]
    out_shape = [jax.ShapeDtypeStruct((nb, seq, width), F32),
                 jax.ShapeDtypeStruct((nb, 2, ROW_TILE, n_chunk * sw), F32)]
    scratch = [pltpu.VMEM((tc, cw), F32), pltpu.VMEM((tc, sw2), F32),
               pltpu.VMEM((tc, sw2), BF16), pltpu.VMEM((2, ROW_TILE, sw), F32)]
    args = [u.reshape(nb, seq, width), b_cat, abar, c_cat, ssm_d.reshape(1, width)]
    if shift is not None:
        caches, news = shift
        cf, nf = [_flat_window(c) for c in caches], [_flat_window(w) for w in news]
        hbm = pl.BlockSpec(memory_space=pl.ANY)
        in_specs += [hbm] * (2 * N_DIL)
        out_specs += [hbm] * N_DIL
        out_shape += [jax.ShapeDtypeStruct(c.shape, c.dtype) for c in cf]
        args += cf + nf
        max_slots = 0
        for c, w in zip(cf, nf):
            rows, _, period = _shift_plan(c.shape, w.shape, math.prod(grid))
            scratch.append(pltpu.VMEM((_shift_slots(period), rows) + c.shape[2:], c.dtype))
            max_slots = max(max_slots, _shift_slots(period))
        scratch += [pltpu.VMEM(w.shape, w.dtype) for w in nf]
        scratch += [pltpu.SemaphoreType.DMA((N_DIL, max_slots)), pltpu.SemaphoreType.DMA((N_DIL, max_slots)),
                    pltpu.SemaphoreType.DMA((2 * N_DIL,))]
    scratch.append(pltpu.VMEM((2, seg, ROW_TILE, sw), F32))
    res = pl.pallas_call(
        functools.partial(_s5_scan_kernel, with_shift=grid if shift is not None else None),
        grid=grid,
        in_specs=in_specs,
        out_specs=out_specs,
        out_shape=out_shape,
        scratch_shapes=scratch,
        compiler_params=_params("arbitrary", "arbitrary", "arbitrary") if shift is not None
        else _params("parallel", "parallel", "arbitrary"),
        name="s5_scan",
    )(*args)
    y, hfin = res[0], res[1]
    last = ROW_TILE - 1
    shifted = None if shift is None else [o.reshape(c.shape) for o, c in zip(res[2:], shift[0])]
    return y.reshape(nb * seq, width), hfin[:, 0, last], hfin[:, 1, last], shifted


def _s5_sample_kernel(u_ref, h0re_ref, h0im_ref, b_ref, abar_ref, c_ref, d_ref, y_ref, hre_ref, him_ref):
    sw = h0re_ref.shape[1]
    ar, ai = abar_ref[0, 0:1, :], abar_ref[1, 0:1, :]
    hr, hi = h0re_ref[...], h0im_ref[...]
    for t in range(u_ref.shape[0]):
        u = u_ref[t]
        bu = jnp.dot(u.astype(BF16), b_ref[0], preferred_element_type=F32)
        hr, hi = _cmul_add(bu[:, :sw], bu[:, sw:], ar, ai, hr, hi)
        hb = jnp.concatenate([hr, -hi], axis=1).astype(BF16)
        y_ref[t] = jnp.dot(hb, c_ref[0], preferred_element_type=F32) + d_ref[...] * u
    hre_ref[...] = hr
    him_ref[...] = hi


def _s5_sample(u, h0re, h0im, prm, ssm_d):
    abar, b_cat, c_cat = prm
    n_chunk, cw, sw2 = b_cat.shape
    sw = sw2 // 2
    n_new, nb, width = u.shape
    return pl.pallas_call(
        _s5_sample_kernel,
        grid=(n_chunk,),
        in_specs=[pl.BlockSpec((n_new, nb, cw), lambda j: (0, 0, j)),
                  pl.BlockSpec((nb, sw), lambda j: (0, j)),
                  pl.BlockSpec((nb, sw), lambda j: (0, j)),
                  pl.BlockSpec((1, cw, sw2), lambda j: (j, 0, 0)),
                  pl.BlockSpec((2, ROW_TILE, sw), lambda j: (0, 0, j)),
                  pl.BlockSpec((1, sw2, cw), lambda j: (j, 0, 0)),
                  pl.BlockSpec((1, cw), lambda j: (0, j))],
        out_specs=[pl.BlockSpec((n_new, nb, cw), lambda j: (0, 0, j)),
                   pl.BlockSpec((nb, sw), lambda j: (0, j)),
                   pl.BlockSpec((nb, sw), lambda j: (0, j))],
        out_shape=[jax.ShapeDtypeStruct((n_new, nb, width), F32),
                   jax.ShapeDtypeStruct((nb, n_chunk * sw), F32),
                   jax.ShapeDtypeStruct((nb, n_chunk * sw), F32)],
        compiler_params=_params("parallel"),
        name="s5_sample",
    )(u, h0re, h0im, b_cat, abar, c_cat, ssm_d.reshape(1, width))


def _glu_kernel(y_ref, g_ref, w_ref, b_ref, o_ref):
    y = y_ref[...]
    s = 0.5 * y * (1.0 + jnp.tanh(math.sqrt(2.0 / math.pi) * (y + 0.044715 * (y * y * y))))
    z = jnp.dot(s.astype(BF16), w_ref[...], preferred_element_type=F32) + b_ref[...]
    o_ref[...] = (s * _sigmoid(z) * g_ref[...].astype(F32)).astype(o_ref.dtype)


def _glu(y, gate, w, b):
    m, width = y.shape
    tm = _row_tile(m, 1024)
    row = pl.BlockSpec((tm, width), lambda i: (i, 0))
    return pl.pallas_call(
        _glu_kernel,
        grid=(m // tm,),
        in_specs=[row, row, pl.BlockSpec((width, width), lambda i: (0, 0)),
                  pl.BlockSpec((1, width), lambda i: (0, 0))],
        out_specs=row,
        out_shape=jax.ShapeDtypeStruct((m, width), BF16),
        compiler_params=_params("parallel"),
        name="glu",
    )(y, gate, w, b.reshape(1, width))


def _merge_kernel(*refs):
    o_refs, l_refs = refs[:N_DIL], refs[N_DIL:2 * N_DIL]
    ga_ref, ys_ref, ma_ref, ms_ref, wa_ref, ws_ref, out_ref = refs[2 * N_DIL:2 * N_DIL + 7]
    a_ref, otok_ref, ltok_ref = refs[2 * N_DIL + 7:]

    def token_order(ref, cols, stage):
        dil, per = ref.shape[1], ref.shape[2]
        if dil == 1:
            return ref[0, 0, :, cols].astype(F32)
        for r in range(dil):
            stage[pl.ds(r, per, stride=dil), :] = ref[0, r, :, cols].astype(F32)
        return stage[...]

    ls = [token_order(l_refs[g], slice(None), ltok_ref.at[g]) for g in range(N_DIL)]
    mx = jnp.maximum(jnp.maximum(ls[0], ls[1]), ls[2])
    es = [jnp.exp(l - mx) for l in ls]
    den = es[0] + es[1] + es[2]
    wts = [e / den for e in es]
    tm, aw = a_ref.shape
    half = out_ref.shape[1] // 2
    halves = [slice(c * half, (c + 1) * half) for c in range(2)]
    ys = ys_ref[...]
    acc = [ms_ref[:, cs].astype(F32) * jnp.dot(ys, ws_ref[:, cs], preferred_element_type=F32) for cs in halves]
    group = min(aw, MXU_COLS) // HEAD_DIM
    ya = [None, None]
    for h0 in range(0, aw // HEAD_DIM, group):
        for h in range(h0, h0 + group):
            sl = slice(h * HEAD_DIM, (h + 1) * HEAD_DIM)
            attn = None
            for g in range(N_DIL):
                term = (jnp.broadcast_to(wts[g][:, h:h + 1], (tm, HEAD_DIM))
                        * token_order(o_refs[g], sl, otok_ref.at[g]))
                attn = term if attn is None else attn + term
            a_ref[:, sl] = (attn * ga_ref[:, sl].astype(F32)).astype(BF16)
        ks = slice(h0 * HEAD_DIM, (h0 + group) * HEAD_DIM)
        for c, cs in enumerate(halves):
            part = jnp.dot(a_ref[:, ks], wa_ref[ks, cs], preferred_element_type=F32)
            ya[c] = part if ya[c] is None else ya[c] + part
    for c, cs in enumerate(halves):
        out_ref[:, cs] = (ma_ref[:, cs].astype(F32) * ya[c] + acc[c]).astype(out_ref.dtype)


def _merge(outs, lses, g_attn, y_ssm, gates, w_a, w_s):
    m, aw = g_attn.shape
    d = w_a.shape[1]
    nb = outs[0].shape[0]
    seq = m // nb
    tm = _row_tile(seq, 512)
    bps = seq // tm
    row = pl.BlockSpec((tm, aw), lambda i: (i, 0))
    ol_specs = []
    for arr in list(outs) + list(lses):
        dil = arr.shape[1]
        assert tm % (dil * ROW_TILE) == 0
        ol_specs.append(pl.BlockSpec((1, dil, tm // dil, arr.shape[3]),
                                     lambda i: (lax.div(i, bps), 0, lax.rem(i, bps), 0)))
    scratch = [pltpu.VMEM((tm, aw), BF16), pltpu.VMEM((N_DIL, tm, HEAD_DIM), F32),
               pltpu.VMEM((N_DIL, tm, HEAD_DIM), F32)]
    once = pl.Buffered(1)
    return pl.pallas_call(
        _merge_kernel,
        grid=(m // tm,),
        in_specs=ol_specs + [row, row, pl.BlockSpec((tm, d), lambda i: (i, 0)),
                             pl.BlockSpec((tm, d), lambda i: (i, 1)),
                             pl.BlockSpec((aw, d), lambda i: (0, 0), pipeline_mode=once),
                             pl.BlockSpec((aw, d), lambda i: (0, 0), pipeline_mode=once)],
        out_specs=pl.BlockSpec((tm, d), lambda i: (i, 0)),
        out_shape=jax.ShapeDtypeStruct((m, d), BF16),
        scratch_shapes=scratch,
        compiler_params=_params("parallel"),
        name="merge",
    )(*outs, *lses, g_attn, y_ssm, gates, gates, w_a, w_s)


def _out_kernel(x_ref, m_ref, w_ref, o_ref):
    o_ref[...] = x_ref[...] + jnp.dot(m_ref[...], w_ref[...], preferred_element_type=F32)


def _out_proj(x, merged, w):
    m, d = x.shape
    tm = _row_tile(m, 1024)
    tn = min(d, 1024)
    return pl.pallas_call(
        _out_kernel,
        grid=(d // tn, m // tm),
        in_specs=[pl.BlockSpec((tm, tn), lambda j, i: (i, j)),
                  pl.BlockSpec((tm, d), lambda j, i: (i, 0)),
                  pl.BlockSpec((d, tn), lambda j, i: (0, j))],
        out_specs=pl.BlockSpec((tm, tn), lambda j, i: (i, j)),
        out_shape=jax.ShapeDtypeStruct((m, d), F32),
        compiler_params=_params("parallel", "parallel"),
        name="out_proj",
    )(x, merged, w)


def _sample_front_kernel(x_ref, gw_ref, w_ref, nwq_ref, nwk_ref, cos_ref, sin_ref, o_ref, nw_ref):
    j = pl.program_id(0)
    x = x_ref[...]
    xn = x * lax.rsqrt(jnp.mean(x * x, axis=-1, keepdims=True) + NORM_EPS) * gw_ref[...]
    acc = jnp.dot(xn.astype(BF16), w_ref[...], preferred_element_type=F32)
    qk, v_end = 2 * N_DIL, 3 * N_DIL

    @pl.when(j < qk)
    def _():
        nw_ref[...] = jnp.where(j < N_DIL, nwq_ref[...], nwk_ref[...])
        o_ref[...] = acc
        _norm_rope_inplace(o_ref, nw_ref, cos_ref, sin_ref)

    @pl.when(((j >= qk) & (j < v_end)) | (j == v_end + 1))
    def _():
        o_ref[...] = acc

    @pl.when((j == v_end) | (j == v_end + 2))
    def _():
        o_ref[...] = acc * _sigmoid(acc)

    @pl.when(j > v_end + 2)
    def _():
        o_ref[...] = _sigmoid(acc)


def _sample_front(x, norm_w, w, q_norm_w, k_norm_w, cos, sin):
    m, d = x.shape
    aw = d // 2
    n_cols = w.shape[1]
    row = pl.BlockSpec((m, HEAD_DIM), lambda j: (0, 0))
    one = pl.BlockSpec((1, HEAD_DIM), lambda j: (0, 0))
    return pl.pallas_call(
        _sample_front_kernel,
        grid=(n_cols // aw,),
        in_specs=[pl.BlockSpec((m, d), lambda j: (0, 0)), pl.BlockSpec((1, d), lambda j: (0, 0)),
                  pl.BlockSpec((d, aw), lambda j: (0, j)), one, one, row, row],
        out_specs=pl.BlockSpec((m, aw), lambda j: (0, j)),
        out_shape=jax.ShapeDtypeStruct((m, n_cols), F32),
        scratch_shapes=[pltpu.VMEM((1, HEAD_DIM), F32)],
        compiler_params=_params("arbitrary"),
        name="sample_front",
    )(x, norm_w.reshape(1, d), w, q_norm_w.reshape(1, HEAD_DIM), k_norm_w.reshape(1, HEAD_DIM), cos, sin)


def _layer(x, cos, sin, lw, nb, seq, layer, depth, kv_prev, sample, shift=None):
    (norm_w, w_in, q_norm_w, k_norm_w, w_glu, b_glu, w_br_attn, w_br_ssm, w_out, ssm_d, s5prm) = lw
    m, d = x.shape
    aw = d // 2
    qkv = N_DIL * aw
    hg = aw // HEAD_DIM

    new_kv = []
    if sample is None:
        xn = _rmsnorm(x, norm_w)
        g_attn = _proj(xn, w_in, 3 * qkv, aw, "silu", BF16)
        u = _proj(xn, w_in, 3 * qkv + aw, aw, "plain", F32)
        g_ssm = _proj(xn, w_in, 3 * qkv + 2 * aw, aw, "silu", BF16)
        gates = _proj(xn, w_in, 3 * qkv + 3 * aw, 2 * d, "sigmoid", BF16)
        outs, lses = [], []
        for g in range(N_DIL):
            qd, kd, vd, kv = _qkv(xn, w_in, g, nb, seq, q_norm_w, k_norm_w, cos, sin, layer, depth,
                                  None if kv_prev is None else kv_prev[g])
            o, lse = _attn_prompt(qd, kd, vd)
            outs.append(o)
            lses.append(lse)
            new_kv.append(kv)
        y, hre, him, shifted = _s5_prompt(u, s5prm, ssm_d, nb, seq, shift)
    else:
        caches, h0re, h0im = sample
        front = _sample_front(x, norm_w, w_in, q_norm_w, k_norm_w, cos, sin)
        q, k, v = front[:, :qkv], front[:, qkv:2 * qkv], front[:, 2 * qkv:3 * qkv]
        g_attn, u = front[:, 3 * qkv:3 * qkv + aw], front[:, 3 * qkv + aw:3 * qkv + 2 * aw]
        g_ssm, gates = front[:, 3 * qkv + 2 * aw:3 * qkv + 3 * aw], front[:, 3 * qkv + 3 * aw:]
        tok = (nb, seq, N_DIL * hg, HEAD_DIM)
        o, lse = _attn_sample(q.reshape(tok), k.reshape(tok), v.reshape(tok), caches, layer)
        o = o.reshape(m, qkv)
        lse = jnp.pad(lse[..., 0].reshape(m, N_DIL, hg), ((0, 0), (0, 0), (0, HEAD_DIM - hg)))
        outs = [o[:, g * aw:(g + 1) * aw].reshape(1, 1, m, aw) for g in range(N_DIL)]
        lses = [lse[:, g].reshape(1, 1, m, HEAD_DIM) for g in range(N_DIL)]
        ut = u.reshape(nb, seq, aw).transpose(1, 0, 2)
        yt, hre, him = _s5_sample(ut, h0re, h0im, s5prm, ssm_d)
        y = yt.transpose(1, 0, 2).reshape(m, aw)
        k5, v5 = k.reshape(nb, seq, N_DIL, hg, HEAD_DIM), v.reshape(nb, seq, N_DIL, hg, HEAD_DIM)
        new_kv = [jnp.stack([k5[:, :, g], v5[:, :, g]], axis=2) for g in range(N_DIL)]
        shifted = None

    y_ssm = _glu(y, g_ssm, w_glu, b_glu)
    merged = _merge(outs, lses, g_attn, y_ssm, gates, w_br_attn, w_br_ssm)
    x_out = _out_proj(x, merged, w_out)

    n_grp = aw // SSM_GROUP_CH
    state = jnp.stack([hre, him], axis=-1).reshape(nb, n_grp, SSM_STATE, 2)
    return x_out, new_kv, state, shifted


def kernel(x_prompt, x_sample, cache_kv_d1, cache_kv_d4, cache_kv_d16, state_ssm, norm_w, w_in, q_norm_w, k_norm_w, ssm_lambda_re, ssm_lambda_im, ssm_log_dt, ssm_b_re, ssm_b_im, ssm_c_re, ssm_c_im, ssm_d, w_glu, b_glu, w_br_attn, w_br_ssm, w_out):
    nb_p, seq_p, d = x_prompt.shape
    nb_s, seq_s, _ = x_sample.shape
    depth = norm_w.shape[0]
    caches = (cache_kv_d1, cache_kv_d4, cache_kv_d16)

    cos_p, sin_p = _rope_tables(jnp.arange(seq_p, dtype=F32))
    cos_s, sin_s = _rope_tables(PAST_LEN + jnp.arange(seq_s, dtype=F32))
    cos_s, sin_s = jnp.tile(cos_s, (nb_s, 1)), jnp.tile(sin_s, (nb_s, 1))

    hp = x_prompt.reshape(nb_p * seq_p, d)
    hs = x_sample.reshape(nb_s * seq_s, d)
    lws = []
    for l in range(depth):
        s5prm = _s5_params(ssm_lambda_re[l], ssm_lambda_im[l], ssm_log_dt[l], ssm_b_re[l], ssm_b_im[l],
                           ssm_c_re[l], ssm_c_im[l])
        lws.append((norm_w[l], _layer_bf16(w_in, l), q_norm_w[l], k_norm_w[l], _layer_bf16(w_glu, l), b_glu[l],
                    _layer_bf16(w_br_attn, l), _layer_bf16(w_br_ssm, l), _layer_bf16(w_out, l), ssm_d[l], s5prm))

    kv_s_new = [[] for _ in range(N_DIL)]
    ssm_s = []
    n_state = state_ssm.shape[2] * state_ssm.shape[3]
    for l in range(depth):
        h0re = state_ssm[l, ..., 0].reshape(nb_s, n_state)
        h0im = state_ssm[l, ..., 1].reshape(nb_s, n_state)
        hs, new_kv_s, st_s, _ = _layer(hs, cos_s, sin_s, lws[l], nb_s, seq_s, l, depth, None,
                                       (caches, h0re, h0im))
        for g in range(N_DIL):
            kv_s_new[g].append(new_kv_s[g])
        ssm_s.append(st_s)

    kv_p, kv_s, ssm_p = None, None, []
    for l in range(depth):
        shift = (caches, [jnp.stack(n) for n in kv_s_new]) if l == 0 else None
        hp, kv_p, st, shifted = _layer(hp, cos_p, sin_p, lws[l], nb_p, seq_p, l, depth, kv_p, None, shift)
        kv_s = shifted if shifted is not None else kv_s
        ssm_p.append(st)

    return (hp.reshape(x_prompt.shape), hs.reshape(x_sample.shape), kv_p[0], kv_p[1], kv_p[2], jnp.stack(ssm_p),
            kv_s[0], kv_s[1], kv_s[2], jnp.stack(ssm_s))
```
